```python
import math
import jax, jax.numpy as jnp
from jax import lax
import numpy as np

D_MODEL = 1024
BATCH = 4
SEQ = 4096
DEPTH = 4

CHUNK = 64
Q_BLOCK = 128
HEAD_DIM = 64
A_HEADS = 4
B_HEADS = 8
C_HEADS = 8
IDX_HEADS = 4
IDX_DIM = 64
TOPK_MAX = 256
N_BRANCHES = 3
D_FF = 4 * D_MODEL
PLE_DIM = 256
ROPE_THETA = 10000.0
EPS = 1e-6

A_QK = A_HEADS * 2 * HEAD_DIM
A_V = A_HEADS * 2 * HEAD_DIM
A_WIDTH = A_V
B_WIDTH = B_HEADS * HEAD_DIM
C_WIDTH = C_HEADS * HEAD_DIM
IN_SPLITS = (A_QK, A_QK, A_V, B_WIDTH, B_WIDTH, B_WIDTH, C_WIDTH, C_WIDTH, C_WIDTH,
             IDX_HEADS * IDX_DIM, IDX_DIM, IDX_HEADS, N_BRANCHES * D_MODEL)
W_IN_COLS = (2 * A_QK + A_V + 3 * B_WIDTH + 3 * C_WIDTH + IDX_HEADS * IDX_DIM + IDX_DIM
             + IDX_HEADS + N_BRANCHES * D_MODEL)

kernel_name = "hybrid_diff_stickbreak_dsa_block"


def rmsnorm(x, g):
    xf = x.astype(jnp.float32)
    y = xf * lax.rsqrt(jnp.mean(xf * xf, axis=-1, keepdims=True) + EPS)
    return (y * g.astype(jnp.float32)).astype(x.dtype)


def rope_tables(positions, dim):
    inv = ROPE_THETA ** (-jnp.arange(0, dim, 2, dtype=jnp.float32) / dim)
    ang = positions.astype(jnp.float32)[..., None] * inv
    return jnp.cos(ang), jnp.sin(ang)


def apply_rope(x, cos, sin):
    c = cos[:, :, None, :]
    s = sin[:, :, None, :]
    xf = x.astype(jnp.float32)
    x1, x2 = jnp.split(xf, 2, axis=-1)
    return jnp.concatenate([x1 * c - x2 * s, x2 * c + x1 * s], axis=-1).astype(x.dtype)


def sweep_query_blocks(fn, *arrays):
    b, s = arrays[0].shape[:2]
    nb = s // Q_BLOCK
    blocks = tuple(jnp.moveaxis(a.reshape((b, nb, Q_BLOCK) + a.shape[2:]), 1, 0) for a in arrays)
    out = lax.map(lambda args: fn(args[0], *args[1:]), (jnp.arange(nb), *blocks))
    return jnp.moveaxis(out, 0, 1).reshape((b, s) + out.shape[3:])


def block_positions(blk, n_keys):
    q_pos = blk * Q_BLOCK + jnp.arange(Q_BLOCK)
    k_pos = jnp.arange(n_keys)
    chunk_ok = (k_pos[None, :] // CHUNK) <= (q_pos[:, None] // CHUNK)
    return q_pos, k_pos, chunk_ok


def diff_attention(q, k, v, lam, lam_init, subln_g):
    n_keys = k.shape[1]
    scale = HEAD_DIM ** -0.5

    def block(blk, qb):
        _, _, chunk_ok = block_positions(blk, n_keys)
        s = jnp.einsum('bqhcd,bkhcd->bhcqk', qb, k).astype(jnp.float32) * scale
        s = jnp.where(chunk_ok, s, -jnp.inf)
        pr = jax.nn.softmax(s, axis=-1)
        attn = (pr[:, :, 0] - lam * pr[:, :, 1]).astype(v.dtype)
        return jnp.einsum('bhqk,bkhe->bqhe', attn, v)

    o = sweep_query_blocks(block, q)
    return rmsnorm(o, subln_g) * (1.0 - lam_init)


def stick_breaking_attention(q, k, v):
    n_keys = k.shape[1]
    scale = HEAD_DIM ** -0.5

    def block(blk, qb):
        q_pos, k_pos, _ = block_positions(blk, n_keys)
        strict = k_pos[None, :] < q_pos[:, None]
        z = jnp.einsum('bqhd,bkhd->bhqk', qb, k).astype(jnp.float32) * scale
        log_beta = jax.nn.log_sigmoid(z)
        log_one_minus = jnp.where(strict, jax.nn.log_sigmoid(-z), 0.0)
        key_axis = log_one_minus.ndim - 1
        later = lax.cumsum(log_one_minus, axis=key_axis, reverse=True) - log_one_minus
        a = jnp.where(strict, jnp.exp(log_beta + later), 0.0).astype(v.dtype)
        return jnp.einsum('bhqk,bkhd->bqhd', a, v)

    return sweep_query_blocks(block, q)


def dsa_attention(q, k, v, qi, ki, wi):
    n_keys = k.shape[1]
    topk = min(TOPK_MAX, n_keys // 4)
    scale = HEAD_DIM ** -0.5

    def block(blk, qb, qib, wib):
        q_pos, _, chunk_ok = block_positions(blk, n_keys)
        idx_s = jnp.einsum('bqhd,bkd->bqhk', qib, ki).astype(jnp.float32) * (IDX_DIM ** -0.5)
        score = jnp.einsum('bqh,bqhk->bqk', wib.astype(jnp.float32) * (IDX_HEADS ** -0.5),
                           jax.nn.relu(idx_s))
        score = jnp.where(chunk_ok, score, -jnp.inf)
        _, sel = lax.top_k(score, topk)
        ok = (sel // CHUNK) <= (q_pos[None, :, None] // CHUNK)
        kg = jax.vmap(lambda kb, ib: kb[ib])(k, sel)
        vg = jax.vmap(lambda vb, ib: vb[ib])(v, sel)
        s = jnp.einsum('bqhd,bqkhd->bhqk', qb, kg).astype(jnp.float32) * scale
        s = jnp.where(ok[:, None], s, -jnp.inf)
        pr = jax.nn.softmax(s, axis=-1).astype(v.dtype)
        return jnp.einsum('bhqk,bqkhd->bqhd', pr, vg)

    return sweep_query_blocks(block, q, qi, wi)


def setup_inputs(seed: int = 0) -> dict:
    key = jax.random.key(seed)
    ks = iter(jax.random.split(key, 32))

    def dense(shape, fan_in):
        return jax.random.normal(next(ks), shape, jnp.float32) * (fan_in ** -0.5)

    def gain(shape):
        return 1.0 + 0.05 * jax.random.normal(next(ks), shape, jnp.float32)

    x = jax.random.normal(next(ks), (BATCH, SEQ, D_MODEL), jnp.float32)
    p = jax.random.normal(next(ks), (DEPTH, BATCH, SEQ, PLE_DIM), jnp.float32)
    offset = jax.random.randint(next(ks), (BATCH, 1), 0, 16, dtype=jnp.int32) * CHUNK
    positions = (offset + jnp.arange(SEQ, dtype=jnp.int32)[None, :]).astype(jnp.int32)
    return {
        "x": x,
        "p": p,
        "positions": positions,
        "attn_norm": gain((DEPTH, D_MODEL)),
        "w_in": dense((DEPTH, D_MODEL, W_IN_COLS), D_MODEL),
        "a_q_norm": gain((DEPTH, HEAD_DIM)),
        "a_k_norm": gain((DEPTH, HEAD_DIM)),
        "a_lambda": 0.1 * jax.random.normal(next(ks), (DEPTH, 4, HEAD_DIM), jnp.float32),
        "a_subln": gain((DEPTH, 2 * HEAD_DIM)),
        "c_q_norm": gain((DEPTH, HEAD_DIM)),
        "c_k_norm": gain((DEPTH, HEAD_DIM)),
        "idx_k_norm": gain((DEPTH, IDX_DIM)),
        "w_br_a": dense((DEPTH, A_WIDTH, D_MODEL), A_WIDTH),
        "w_br_b": dense((DEPTH, B_WIDTH, D_MODEL), B_WIDTH),
        "w_br_c": dense((DEPTH, C_WIDTH, D_MODEL), C_WIDTH),
        "w_out": dense((DEPTH, D_MODEL, D_MODEL), D_MODEL),
        "mlp_norm": gain((DEPTH, D_MODEL)),
        "w_up": dense((DEPTH, D_MODEL, D_FF), D_MODEL),
        "w_down": dense((DEPTH, D_FF, D_MODEL), D_FF),
        "ple_norm": gain((DEPTH, D_MODEL)),
        "w_ple_gate": dense((DEPTH, D_MODEL, D_MODEL), D_MODEL),
        "w_ple_proj": dense((DEPTH, PLE_DIM, D_MODEL), PLE_DIM),
    }


def reference(x, p, positions, attn_norm, w_in, a_q_norm, a_k_norm, a_lambda, a_subln,
              c_q_norm, c_k_norm, idx_k_norm, w_br_a, w_br_b, w_br_c, w_out,
              mlp_norm, w_up, w_down, ple_norm, w_ple_gate, w_ple_proj):
    b, s = x.shape[:2]
    cos, sin = rope_tables(positions, HEAD_DIM)
    offsets = [int(o) for o in np.cumsum(IN_SPLITS)[:-1]]
    for i in range(DEPTH):
        lam_init = 0.8 - 0.6 * math.exp(-0.3 * i)
        h = rmsnorm(x, attn_norm[i])
        proj = h @ w_in[i]
        (aq, ak, av, bq, bk, bv, cq, ck, cv, iq, ik, iw, gl) = jnp.split(proj, offsets, axis=-1)

        aq = apply_rope(rmsnorm(aq.reshape(b, s, 2 * A_HEADS, HEAD_DIM), a_q_norm[i]), cos, sin)
        ak = apply_rope(rmsnorm(ak.reshape(b, s, 2 * A_HEADS, HEAD_DIM), a_k_norm[i]), cos, sin)
        lp = a_lambda[i].astype(jnp.float32)
        lam = jnp.exp(jnp.sum(lp[0] * lp[1])) - jnp.exp(jnp.sum(lp[2] * lp[3])) + lam_init
        y_a = diff_attention(aq.reshape(b, s, A_HEADS, 2, HEAD_DIM),
                             ak.reshape(b, s, A_HEADS, 2, HEAD_DIM),
                             av.reshape(b, s, A_HEADS, 2 * HEAD_DIM),
                             lam, lam_init, a_subln[i]).reshape(b, s, A_WIDTH)

        y_b = stick_breaking_attention(bq.reshape(b, s, B_HEADS, HEAD_DIM),
                                       bk.reshape(b, s, B_HEADS, HEAD_DIM),
                                       bv.reshape(b, s, B_HEADS, HEAD_DIM)).reshape(b, s, B_WIDTH)

        cq = apply_rope(rmsnorm(cq.reshape(b, s, C_HEADS, HEAD_DIM), c_q_norm[i]), cos, sin)
        ck = apply_rope(rmsnorm(ck.reshape(b, s, C_HEADS, HEAD_DIM), c_k_norm[i]), cos, sin)
        iq = apply_rope(iq.reshape(b, s, IDX_HEADS, IDX_DIM), cos, sin)
        ik = apply_rope(rmsnorm(ik, idx_k_norm[i])[:, :, None, :], cos, sin)[:, :, 0]
        y_c = dsa_attention(cq, ck, cv.reshape(b, s, C_HEADS, HEAD_DIM),
                            iq, ik, iw).reshape(b, s, C_WIDTH)

        gates = jax.nn.sigmoid(gl.reshape(b, s, N_BRANCHES, D_MODEL).astype(jnp.float32)).astype(x.dtype)
        merged = (gates[:, :, 0] * (y_a @ w_br_a[i])
                  + gates[:, :, 1] * (y_b @ w_br_b[i])
                  + gates[:, :, 2] * (y_c @ w_br_c[i]))
        x = x + merged @ w_out[i]

        hm = rmsnorm(x, mlp_norm[i])
        x = x + jnp.square(jax.nn.relu(hm @ w_up[i])) @ w_down[i]

        hp = rmsnorm(x, ple_norm[i])
        x = x + jax.nn.sigmoid(hp @ w_ple_gate[i]) * (p[i] @ w_ple_proj[i])
    return x
```

```python
import functools
import math

import jax
import jax.numpy as jnp
from jax import lax
from jax.experimental import pallas as pl
from jax.experimental.pallas import tpu as pltpu

F32 = jnp.float32
BF16 = jnp.bfloat16

HEAD_DIM = 64
CHUNK = 64
A_HEADS = 4
B_HEADS = 8
C_HEADS = 8
IDX_HEADS = 4
IDX_DIM = 64
TOPK_MAX = 256
N_BRANCHES = 3
ROPE_THETA = 10000.0
EPS = 1e-6

LANES = 128
HALF_HEAD = HEAD_DIM // 2
NEG = -1e30
INT_MIN = -(2 ** 31)
VMEM_LIMIT = 56 * 1024 * 1024

ROW_TILE = 512
ATTN_TILE = 256
DSA_TQ = 256
DSA_TK = 512
TRI = 256
FF_TILE = 512


def _params(*sem):
    return pltpu.CompilerParams(dimension_semantics=sem, vmem_limit_bytes=VMEM_LIMIT)


def _dot(a, b):
    return jnp.dot(a, b, preferred_element_type=F32)


def _dot_nt(a, b):
    return lax.dot_general(a, b, (((1,), (1,)), ((), ())), preferred_element_type=F32)


def _split_dot(x, w):
    hi = x.astype(BF16)
    lo = (x - hi.astype(F32)).astype(BF16)
    return _dot(hi, w) + _dot(lo, w)


def _swap_halves(y):
    lane = lax.broadcasted_iota(jnp.int32, y.shape, 1)
    return jnp.where((lane & HALF_HEAD) != 0, pltpu.roll(y, HALF_HEAD, 1),
                     pltpu.roll(y, LANES - HALF_HEAD, 1))


def _rope_slab(y, cos, sin):
    return y * cos + _swap_halves(y) * sin


def _norm_kernel(x_ref, g_ref, o_ref):
    x = x_ref[...]
    ms = jnp.mean(x * x, axis=-1, keepdims=True)
    o_ref[...] = (x * lax.rsqrt(ms + EPS) * g_ref[0]).astype(o_ref.dtype)


def _norm_call(x2, gains, layer):
    n, d = x2.shape
    return pl.pallas_call(
        _norm_kernel,
        grid=(n // ROW_TILE,),
        in_specs=[pl.BlockSpec((ROW_TILE, d), lambda i: (i, 0)),
                  pl.BlockSpec((1, 1, d), lambda i: (layer, 0, 0))],
        out_specs=pl.BlockSpec((ROW_TILE, d), lambda i: (i, 0)),
        out_shape=jax.ShapeDtypeStruct((n, d), BF16),
        compiler_params=_params("parallel"),
    )(x2, gains)


def _proj_qk_kernel(h_ref, w_ref, gain_ref, cos_ref, sin_ref, gmat_ref, o_ref):
    t = _dot(h_ref[...], w_ref[0])
    ms = _split_dot(t * t, gmat_ref[...]) * (1.0 / HEAD_DIM)
    y = t * lax.rsqrt(ms + EPS) * gain_ref[0]
    cos = cos_ref[...]
    sin = sin_ref[...]
    for s in range(y.shape[1] // LANES):
        sl = slice(s * LANES, (s + 1) * LANES)
        o_ref[:, sl] = _rope_slab(y[:, sl], cos, sin).astype(o_ref.dtype)


def _proj_qk_call(h, w, gain, cos_t, sin_t, gmat, layer):
    n, d = h.shape
    cols = w.shape[2]
    tn = gmat.shape[0]
    return pl.pallas_call(
        _proj_qk_kernel,
        grid=(n // ROW_TILE, cols // tn),
        in_specs=[pl.BlockSpec((ROW_TILE, d), lambda i, j: (i, 0)),
                  pl.BlockSpec((1, d, tn), lambda i, j: (layer, 0, j)),
                  pl.BlockSpec((1, 1, tn), lambda i, j: (layer, 0, j)),
                  pl.BlockSpec((ROW_TILE, LANES), lambda i, j: (i, 0)),
                  pl.BlockSpec((ROW_TILE, LANES), lambda i, j: (i, 0)),
                  pl.BlockSpec((tn, tn), lambda i, j: (0, 0))],
        out_specs=pl.BlockSpec((ROW_TILE, tn), lambda i, j: (i, j)),
        out_shape=jax.ShapeDtypeStruct((n, cols), BF16),
        compiler_params=_params("parallel", "arbitrary"),
    )(h, w, gain, cos_t, sin_t, gmat)


def _proj_plain_kernel(h_ref, w_ref, o_ref):
    o_ref[...] = _dot(h_ref[...], w_ref[0]).astype(o_ref.dtype)


def _proj_plain_call(h, w, layer, tn=512):
    n, d = h.shape
    cols = w.shape[2]
    return pl.pallas_call(
        _proj_plain_kernel,
        grid=(n // ROW_TILE, cols // tn),
        in_specs=[pl.BlockSpec((ROW_TILE, d), lambda i, j: (i, 0)),
                  pl.BlockSpec((1, d, tn), lambda i, j: (layer, 0, j))],
        out_specs=pl.BlockSpec((ROW_TILE, tn), lambda i, j: (i, j)),
        out_shape=jax.ShapeDtypeStruct((n, cols), BF16),
        compiler_params=_params("parallel", "arbitrary"),
    )(h, w)


IDX_Q = IDX_HEADS * IDX_DIM
IDX_COLS = IDX_Q + LANES


def _proj_idx_kernel(h_ref, w_ref, gain_ref, scale_ref, cos_ref, sin_ref, o_ref):
    t = _dot(h_ref[...], w_ref[0])
    cos = cos_ref[...]
    sin = sin_ref[...]
    scale = scale_ref[...]
    for s in range(IDX_Q // LANES):
        sl = slice(s * LANES, (s + 1) * LANES)
        o_ref[:, sl] = _rope_slab(t[:, sl], cos, sin) * scale[:, sl]
    sl = slice(IDX_Q, IDX_COLS)
    tk = t[:, sl]
    lane = lax.broadcasted_iota(jnp.int32, tk.shape, 1)
    is_k = lane < IDX_DIM
    ms = jnp.sum(jnp.where(is_k, tk * tk, 0.0), axis=-1, keepdims=True) * (1.0 / IDX_DIM)
    yk = tk * lax.rsqrt(ms + EPS) * gain_ref[0]
    o_ref[:, sl] = jnp.where(is_k, _rope_slab(yk, cos, sin), tk) * scale[:, sl]


def _proj_idx_call(h, w, gain, scale, cos_t, sin_t, layer):
    n, d = h.shape
    return pl.pallas_call(
        _proj_idx_kernel,
        grid=(n // ROW_TILE,),
        in_specs=[pl.BlockSpec((ROW_TILE, d), lambda i: (i, 0)),
                  pl.BlockSpec((1, d, IDX_COLS), lambda i: (layer, 0, 0)),
                  pl.BlockSpec((1, 1, LANES), lambda i: (layer, 0, 0)),
                  pl.BlockSpec((1, IDX_COLS), lambda i: (0, 0)),
                  pl.BlockSpec((ROW_TILE, LANES), lambda i: (i, 0)),
                  pl.BlockSpec((ROW_TILE, LANES), lambda i: (i, 0))],
        out_specs=pl.BlockSpec((ROW_TILE, IDX_COLS), lambda i: (i, 0)),
        out_shape=jax.ShapeDtypeStruct((n, IDX_COLS), F32),
        compiler_params=_params("parallel"),
    )(h, w, gain, scale, cos_t, sin_t)


def _head_halves(q2):
    lane = lax.broadcasted_iota(jnp.int32, q2.shape, 1)
    zero = jnp.zeros_like(q2)
    return [jnp.where(lane < HEAD_DIM, q2, zero), jnp.where(lane >= HEAD_DIM, q2, zero)]


def _softmax_step(c, s, vb, m_ref, l_ref, acc_ref):
    m_old = m_ref[c]
    m_new = jnp.maximum(m_old, jnp.max(s, axis=-1, keepdims=True))
    p = jnp.exp(s - m_new)
    alpha = jnp.exp(m_old - m_new)
    l_ref[c] = alpha * l_ref[c] + jnp.sum(p, axis=-1, keepdims=True)
    acc_ref[c] = alpha * acc_ref[c] + _dot(p.astype(BF16), vb)
    m_ref[c] = m_new


def _local_chunk_ids(t):
    row = lax.broadcasted_iota(jnp.int32, (t, t), 0)
    col = lax.broadcasted_iota(jnp.int32, (t, t), 1)
    return row, col


def _attn_a_kernel(lam_ref, q_ref, k_ref, v_ref, g_ref, o_ref, m_ref, l_ref, acc_ref, *, lam_init):
    t = q_ref.shape[1]
    i = pl.program_id(1)
    lp = lam_ref[0]
    lam = (jnp.exp(jnp.sum(lp[0:1] * lp[1:2], axis=-1, keepdims=True))
           - jnp.exp(jnp.sum(lp[2:3] * lp[3:4], axis=-1, keepdims=True)) + lam_init)
    row, col = _local_chunk_ids(t)
    chunk_ok = (col // CHUNK) <= (row // CHUNK)

    for h in range(A_HEADS):
        sl = slice(h * LANES, (h + 1) * LANES)
        qs = _head_halves(q_ref[0, :, sl])
        m_ref[...] = jnp.full(m_ref.shape, NEG, F32)
        l_ref[...] = jnp.zeros(l_ref.shape, F32)
        acc_ref[...] = jnp.zeros(acc_ref.shape, F32)

        def step(j, masked):
            rows = pl.ds(pl.multiple_of(j * t, t), t)
            kb = k_ref[0, rows, sl]
            vb = v_ref[0, rows, sl]
            for c in range(2):
                s = _dot_nt(qs[c], kb)
                if masked:
                    s = jnp.where(chunk_ok, s, NEG)
                _softmax_step(c, s, vb, m_ref, l_ref, acc_ref)

        def body(j, carry):
            step(j, False)
            return carry

        lax.fori_loop(0, i, body, 0)
        step(i, True)
        o = acc_ref[0] / l_ref[0] - lam * (acc_ref[1] / l_ref[1])
        ms = jnp.mean(o * o, axis=-1, keepdims=True)
        o_ref[0, :, sl] = (o * lax.rsqrt(ms + EPS) * g_ref[0] * (1.0 - lam_init)).astype(o_ref.dtype)


def _attn_a_call(qk, pv, lam_p, subln, layer, lam_init):
    b, s, _ = qk.shape
    t = min(ATTN_TILE, s)
    w = A_HEADS * LANES
    return pl.pallas_call(
        functools.partial(_attn_a_kernel, lam_init=lam_init),
        grid=(b, s // t),
        in_specs=[pl.BlockSpec((1, 4, HEAD_DIM), lambda bi, i: (layer, 0, 0)),
                  pl.BlockSpec((1, t, w), lambda bi, i: (bi, i, 0)),
                  pl.BlockSpec((1, s, w), lambda bi, i: (bi, 0, 1)),
                  pl.BlockSpec((1, s, w), lambda bi, i: (bi, 0, 0)),
                  pl.BlockSpec((1, 1, LANES), lambda bi, i: (layer, 0, 0))],
        out_specs=pl.BlockSpec((1, t, w), lambda bi, i: (bi, i, 0)),
        out_shape=jax.ShapeDtypeStruct((b, s, w), BF16),
        scratch_shapes=[pltpu.VMEM((2, t, 1), F32), pltpu.VMEM((2, t, 1), F32),
                        pltpu.VMEM((2, t, LANES), F32)],
        compiler_params=_params("parallel", "arbitrary"),
    )(lam_p, qk, qk, pv, subln)


def _attn_b_kernel(q_ref, k_ref, v_ref, u_ref, o_ref, r_ref, acc_ref):
    t = q_ref.shape[1]
    i = pl.program_id(1)
    row, col = _local_chunk_ids(t)
    strict = col < row
    upper = u_ref[...]

    for hp in range(B_HEADS // 2):
        sl = slice(hp * LANES, (hp + 1) * LANES)
        qs = _head_halves(q_ref[0, :, sl])
        r_ref[...] = jnp.zeros(r_ref.shape, F32)
        acc_ref[...] = jnp.zeros(acc_ref.shape, F32)

        def step(j, diag):
            rows = pl.ds(pl.multiple_of(j * t, t), t)
            kb = k_ref[0, rows, sl]
            vb = v_ref[0, rows, sl]
            for c in range(2):
                z = _dot_nt(qs[c], kb)
                soft = jnp.log1p(jnp.exp(-jnp.abs(z)))
                log_beta = jnp.minimum(z, 0.0) - soft
                log_om = jnp.minimum(-z, 0.0) - soft
                if diag:
                    log_om = jnp.where(strict, log_om, 0.0)
                later = _split_dot(log_om, upper) + r_ref[c]
                a = jnp.exp(log_beta + later)
                if diag:
                    a = jnp.where(strict, a, 0.0)
                acc_ref[c] = acc_ref[c] + _dot(a.astype(BF16), vb)
                r_ref[c] = r_ref[c] + jnp.sum(log_om, axis=-1, keepdims=True)

        def body(n, carry):
            step(i - 1 - n, False)
            return carry

        step(i, True)
        lax.fori_loop(0, i, body, 0)
        lane = lax.broadcasted_iota(jnp.int32, (t, LANES), 1)
        o_ref[0, :, sl] = jnp.where(lane < HEAD_DIM, acc_ref[0], acc_ref[1]).astype(o_ref.dtype)


def _attn_b_call(pv, upper):
    b, s, _ = pv.shape
    t = upper.shape[0]
    w = B_HEADS * HEAD_DIM
    return pl.pallas_call(
        _attn_b_kernel,
        grid=(b, s // t),
        in_specs=[pl.BlockSpec((1, t, w), lambda bi, i: (bi, i, 1)),
                  pl.BlockSpec((1, s, w), lambda bi, i: (bi, 0, 2)),
                  pl.BlockSpec((1, s, w), lambda bi, i: (bi, 0, 3)),
                  pl.BlockSpec((t, t), lambda bi, i: (0, 0))],
        out_specs=pl.BlockSpec((1, t, w), lambda bi, i: (bi, i, 0)),
        out_shape=jax.ShapeDtypeStruct((b, s, w), BF16),
        scratch_shapes=[pltpu.VMEM((2, t, 1), F32), pltpu.VMEM((2, t, LANES), F32)],
        compiler_params=_params("parallel", "arbitrary"),
    )(pv, pv, pv, upper)


def _dsa_mask_kernel(q_ref, kw_ref, wq_ref, tri_ref, o_ref, key_ref, *, topk):
    tq = q_ref.shape[1]
    nj, _, tk = key_ref.shape
    i = pl.program_id(1)
    nch = ((i + 1) * tq + tk - 1) // tk
    lane = lax.broadcasted_iota(jnp.int32, (tq, LANES), 1)
    q = q_ref[0]
    qh = []
    for h in range(IDX_HEADS):
        x = q[:, (h // 2) * LANES:(h // 2 + 1) * LANES]
        if h % 2:
            x = pltpu.roll(x, HEAD_DIM, 1)
        qh.append(jnp.where(lane < IDX_DIM, x, 0.0).astype(BF16))
    wv = wq_ref[0]
    wh = [wv[:, IDX_DIM + h:IDX_DIM + h + 1] for h in range(IDX_HEADS)]
    row = i * tq + lax.broadcasted_iota(jnp.int32, (tq, tk), 0)
    col0 = lax.broadcasted_iota(jnp.int32, (tq, tk), 1)

    def chunk_ok(c):
        return ((col0 + c * tk) // CHUNK) <= (row // CHUNK)

    def fill(c, carry):
        kb = kw_ref[0, pl.ds(pl.multiple_of(c * tk, tk), tk), :].astype(BF16)
        score = jnp.zeros((tq, tk), F32)
        for h in range(IDX_HEADS):
            score = score + wh[h] * jnp.maximum(_dot_nt(qh[h], kb), 0.0)
        score = jnp.where(chunk_ok(c), score, -jnp.inf)
        bits = pltpu.bitcast(score, jnp.int32)
        key = jnp.where(bits < 0, bits ^ jnp.int32(0x7FFFFFFF), bits)
        key_ref[c] = jnp.where(score == 0.0, 0, key)
        return carry

    lax.fori_loop(0, nch, fill, 0)

    def count(pred, thr):
        def body(c, acc):
            hit = jnp.where(pred(key_ref[c], thr), 1.0, 0.0)
            for s in range(tk // LANES):
                acc = acc + hit[:, s * LANES:(s + 1) * LANES]
            return acc
        acc = lax.fori_loop(0, nch, body, jnp.zeros((tq, LANES), F32))
        return jnp.sum(acc, axis=-1, keepdims=True)

    def bit_body(n, lo):
        cand = lo + jnp.left_shift(jnp.int32(1), 31 - n)
        cnt = count(lambda k, t: k >= t, cand)
        return jnp.where(cnt >= topk, cand, lo)

    thr = lax.fori_loop(0, 32, bit_body, jnp.full((tq, 1), INT_MIN, jnp.int32))
    need = topk - count(lambda k, t: k > t, thr)
    tri = tri_ref[...]

    def emit(c, run):
        key = key_ref[c]
        gt = key > thr
        eq = key == thr
        eqf = jnp.where(eq, 1.0, 0.0)
        ok = chunk_ok(c)
        for s in range(tk // TRI):
            sl = slice(s * TRI, (s + 1) * TRI)
            before = _dot(eqf[:, sl].astype(BF16), tri) + run
            sel = (gt[:, sl] | (eq[:, sl] & (before < need))) & ok[:, sl]
            o_ref[0, 0, c, :, sl] = jnp.where(sel, 0.0, NEG).astype(o_ref.dtype)
            run = run + jnp.sum(eqf[:, sl], axis=-1, keepdims=True)
        return run

    lax.fori_loop(0, nch, emit, jnp.zeros((tq, 1), F32))

    def blank(c, carry):
        o_ref[0, 0, c] = jnp.full((tq, tk), NEG, o_ref.dtype)
        return carry

    lax.fori_loop(nch, nj, blank, 0)


def _dsa_mask_call(idx, tri, topk):
    b, s, _ = idx.shape
    tq = min(DSA_TQ, s)
    tk = min(DSA_TK, s)
    assert tk >= topk and tk % TRI == 0
    ni, nj = s // tq, s // tk
    return pl.pallas_call(
        functools.partial(_dsa_mask_kernel, topk=topk),
        grid=(b, ni),
        in_specs=[pl.BlockSpec((1, tq, IDX_Q), lambda bi, i: (bi, i, 0)),
                  pl.BlockSpec((1, s, LANES), lambda bi, i: (bi, 0, IDX_Q // LANES)),
                  pl.BlockSpec((1, tq, LANES), lambda bi, i: (bi, i, IDX_Q // LANES)),
                  pl.BlockSpec((TRI, TRI), lambda bi, i: (0, 0))],
        out_specs=pl.BlockSpec((1, 1, nj, tq, tk), lambda bi, i: (bi, i, 0, 0, 0)),
        out_shape=jax.ShapeDtypeStruct((b, ni, nj, tq, tk), BF16),
        scratch_shapes=[pltpu.VMEM((nj, tq, tk), jnp.int32)],
        compiler_params=_params("parallel", "arbitrary"),
    )(idx, idx, idx, tri)


def _attn_c_kernel(q_ref, k_ref, v_ref, b_ref, o_ref, m_ref, l_ref, acc_ref):
    tq = q_ref.shape[1]
    tk = b_ref.shape[4]
    i = pl.program_id(1)
    nb = ((i + 1) * tq + tk - 1) // tk
    lane = lax.broadcasted_iota(jnp.int32, (tq, LANES), 1)

    for hp in range(C_HEADS // 2):
        sl = slice(hp * LANES, (hp + 1) * LANES)
        qs = _head_halves(q_ref[0, :, sl])
        m_ref[...] = jnp.full(m_ref.shape, NEG, F32)
        l_ref[...] = jnp.zeros(l_ref.shape, F32)
        acc_ref[...] = jnp.zeros(acc_ref.shape, F32)

        def body(j, carry):
            rows = pl.ds(pl.multiple_of(j * tk, tk), tk)
            kb = k_ref[0, rows, sl]
            vb = v_ref[0, rows, sl]
            bias = b_ref[0, 0, j].astype(F32)
            for c in range(2):
                _softmax_step(c, _dot_nt(qs[c], kb) + bias, vb, m_ref, l_ref, acc_ref)
            return carry

        lax.fori_loop(0, nb, body, 0)
        o = jnp.where(lane < HEAD_DIM, acc_ref[0] / l_ref[0], acc_ref[1] / l_ref[1])
        o_ref[0, :, sl] = o.astype(o_ref.dtype)


def _attn_c_call(qk, pv, bias):
    b, s, _ = qk.shape
    _, ni, nj, tq, tk = bias.shape
    w = C_HEADS * HEAD_DIM
    return pl.pallas_call(
        _attn_c_kernel,
        grid=(b, ni),
        in_specs=[pl.BlockSpec((1, tq, w), lambda bi, i: (bi, i, 2)),
                  pl.BlockSpec((1, s, w), lambda bi, i: (bi, 0, 3)),
                  pl.BlockSpec((1, s, w), lambda bi, i: (bi, 0, 4)),
                  pl.BlockSpec((1, 1, nj, tq, tk), lambda bi, i: (bi, i, 0, 0, 0))],
        out_specs=pl.BlockSpec((1, tq, w), lambda bi, i: (bi, i, 0)),
        out_shape=jax.ShapeDtypeStruct((b, s, w), BF16),
        scratch_shapes=[pltpu.VMEM((2, tq, 1), F32), pltpu.VMEM((2, tq, 1), F32),
                        pltpu.VMEM((2, tq, LANES), F32)],
        compiler_params=_params("parallel", "arbitrary"),
    )(qk, qk, pv, bias)


def _merge_kernel(x_ref, h_ref, ya_ref, yb_ref, yc_ref, wg_ref, wa_ref, wb_ref, wc_ref, wo_ref, o_ref):
    d = x_ref.shape[1]
    h = h_ref[...]
    merged = jnp.zeros(x_ref.shape, F32)
    for n, (y_ref, w_ref) in enumerate(((ya_ref, wa_ref), (yb_ref, wb_ref), (yc_ref, wc_ref))):
        gate = jax.nn.sigmoid(_dot(h, wg_ref[0, :, n * d:(n + 1) * d]))
        merged = merged + gate * _dot(y_ref[...], w_ref[0])
    o_ref[...] = x_ref[...] + _dot(merged.astype(BF16), wo_ref[0])


def _merge_call(x2, h, ya, yb, yc, wg, wa, wb, wc, wo, layer):
    n, d = x2.shape
    tm = ROW_TILE
    row = lambda cols: pl.BlockSpec((tm, cols), lambda i: (i, 0))
    full = lambda w: pl.BlockSpec((1,) + w.shape[1:], lambda i: (layer, 0, 0))
    return pl.pallas_call(
        _merge_kernel,
        grid=(n // tm,),
        in_specs=[row(d), row(d), row(ya.shape[1]), row(yb.shape[1]), row(yc.shape[1]),
                  full(wg), full(wa), full(wb), full(wc), full(wo)],
        out_specs=row(d),
        out_shape=jax.ShapeDtypeStruct((n, d), F32),
        compiler_params=_params("parallel"),
    )(x2, h, ya, yb, yc, wg, wa, wb, wc, wo)


def _mlp_kernel(x_ref, p_ref, gm_ref, wu_ref, wd_ref, gp_ref, wpg_ref, wpp_ref, o_ref, h_ref, acc_ref):
    f = pl.program_id(1)

    @pl.when(f == 0)
    def _():
        x = x_ref[...]
        ms = jnp.mean(x * x, axis=-1, keepdims=True)
        h_ref[...] = (x * lax.rsqrt(ms + EPS) * gm_ref[0]).astype(BF16)
        acc_ref[...] = jnp.zeros(acc_ref.shape, F32)

    u = jnp.maximum(_dot(h_ref[...], wu_ref[0]), 0.0)
    acc_ref[...] += _dot((u * u).astype(BF16), wd_ref[0])

    @pl.when(f == pl.num_programs(1) - 1)
    def _():
        x = x_ref[...] + acc_ref[...]
        ms = jnp.mean(x * x, axis=-1, keepdims=True)
        hp = (x * lax.rsqrt(ms + EPS) * gp_ref[0]).astype(BF16)
        gate = jax.nn.sigmoid(_dot(hp, wpg_ref[0]))
        o_ref[...] = x + gate * _dot(p_ref[0].astype(BF16), wpp_ref[0])


def _mlp_call(x2, p3, g_mlp, w_up, w_down, g_ple, w_pg, w_pp, layer):
    n, d = x2.shape
    ff = w_up.shape[2]
    pd = p3.shape[2]
    tm, tf = ROW_TILE, FF_TILE
    return pl.pallas_call(
        _mlp_kernel,
        grid=(n // tm, ff // tf),
        in_specs=[pl.BlockSpec((tm, d), lambda i, f: (i, 0)),
                  pl.BlockSpec((1, tm, pd), lambda i, f: (layer, i, 0)),
                  pl.BlockSpec((1, 1, d), lambda i, f: (layer, 0, 0)),
                  pl.BlockSpec((1, d, tf), lambda i, f: (layer, 0, f)),
                  pl.BlockSpec((1, tf, d), lambda i, f: (layer, f, 0)),
                  pl.BlockSpec((1, 1, d), lambda i, f: (layer, 0, 0)),
                  pl.BlockSpec((1, d, d), lambda i, f: (layer, 0, 0)),
                  pl.BlockSpec((1, pd, d), lambda i, f: (layer, 0, 0))],
        out_specs=pl.BlockSpec((tm, d), lambda i, f: (i, 0)),
        out_shape=jax.ShapeDtypeStruct((n, d), F32),
        scratch_shapes=[pltpu.VMEM((tm, d), BF16), pltpu.VMEM((tm, d), F32)],
        compiler_params=_params("parallel", "arbitrary"),
    )(x2, p3, g_mlp, w_up, w_down, g_ple, w_pg, w_pp)


def _tile_gain(g, reps, scale=1.0):
    return jnp.tile(g.astype(F32), (1, reps)) * scale


def kernel(x, p, positions, attn_norm, w_in, a_q_norm, a_k_norm, a_lambda, a_subln,
           c_q_norm, c_k_norm, idx_k_norm, w_br_a, w_br_b, w_br_c, w_out,
           mlp_norm, w_up, w_down, ple_norm, w_ple_gate, w_ple_proj):
    b, s, d = x.shape
    depth = w_in.shape[0]
    n = b * s
    topk = min(TOPK_MAX, s // 4)
    scale = HEAD_DIM ** -0.5

    inv = ROPE_THETA ** (-jnp.arange(0, HEAD_DIM, 2, dtype=F32) / HEAD_DIM)
    ang = positions.astype(F32)[..., None] * inv
    cos, sin = jnp.cos(ang), jnp.sin(ang)
    cos_t = jnp.concatenate([cos, cos, cos, cos], axis=-1).reshape(n, LANES)
    sin_t = jnp.concatenate([-sin, sin, -sin, sin], axis=-1).reshape(n, LANES)

    qa, ka, va = 0, 512, 1024
    qb, kb, vb = 1536, 2048, 2560
    qc, kc, vc = 3072, 3584, 4096
    qi = 4608
    gl = 4932
    col = lambda a, width=512: w_in[:, :, a:a + width]
    w_qk = jnp.concatenate([col(qa), col(ka), col(qc), col(kc)], axis=-1).astype(BF16)
    w_pv = jnp.concatenate([col(va), col(qb) * scale, col(kb), col(vb), col(vc)], axis=-1).astype(BF16)
    w_idx = jnp.pad(col(qi, gl - qi), ((0, 0), (0, 0), (0, IDX_COLS - (gl - qi)))).astype(BF16)
    w_gl = col(gl, N_BRANCHES * d).astype(BF16)
    heads = 512 // HEAD_DIM
    gain_qk = jnp.concatenate([_tile_gain(a_q_norm, heads, scale), _tile_gain(a_k_norm, heads),
                               _tile_gain(c_q_norm, heads, scale), _tile_gain(c_k_norm, heads)],
                              axis=-1)[:, None, :]
    gain_idx = jnp.pad(idx_k_norm.astype(F32), ((0, 0), (0, LANES - IDX_DIM)))[:, None, :]
    scale_idx = jnp.concatenate([jnp.full((IDX_Q,), IDX_DIM ** -0.5, F32), jnp.ones((IDX_DIM,), F32),
                                 jnp.full((IDX_HEADS,), IDX_HEADS ** -0.5, F32),
                                 jnp.zeros((LANES - IDX_DIM - IDX_HEADS,), F32)])[None, :]
    gid = jnp.arange(512) // HEAD_DIM
    gmat = (gid[:, None] == gid[None, :]).astype(BF16)
    t_attn = min(ATTN_TILE, s)
    ar = jnp.arange(t_attn)
    upper = (ar[:, None] > ar[None, :]).astype(BF16)
    at = jnp.arange(TRI)
    tri = (at[:, None] < at[None, :]).astype(BF16)

    wa, wb, wc, wo = (w.astype(BF16) for w in (w_br_a, w_br_b, w_br_c, w_out))
    wu, wd, wpg, wpp = (w.astype(BF16) for w in (w_up, w_down, w_ple_gate, w_ple_proj))
    g_attn, g_mlp, g_ple = (g.astype(F32)[:, None, :] for g in (attn_norm, mlp_norm, ple_norm))
    subln = a_subln.astype(F32)[:, None, :]
    lam_p = a_lambda.astype(F32)
    p3 = p.reshape(depth, n, p.shape[-1])

    x2 = x.reshape(n, d)
    for i in range(depth):
        lam_init = 0.8 - 0.6 * math.exp(-0.3 * i)
        h = _norm_call(x2, g_attn, i)
        qk = _proj_qk_call(h, w_qk, gain_qk, cos_t, sin_t, gmat, i).reshape(b, s, -1)
        pv = _proj_plain_call(h, w_pv, i).reshape(b, s, -1)
        idx = _proj_idx_call(h, w_idx, gain_idx, scale_idx, cos_t, sin_t, i).reshape(b, s, -1)
        ya = _attn_a_call(qk, pv, lam_p, subln, i, lam_init)
        yb = _attn_b_call(pv, upper)
        bias = _dsa_mask_call(idx, tri, topk)
        yc = _attn_c_call(qk, pv, bias)
        x2 = _merge_call(x2, h, ya.reshape(n, -1), yb.reshape(n, -1), yc.reshape(n, -1),
                         w_gl, wa, wb, wc, wo, i)
        x2 = _mlp_call(x2, p3, g_mlp, wu, wd, g_ple, wpg, wpp, i)
    return x2.reshape(b, s, d)
```

```python
import functools
import math

import jax
import jax.numpy as jnp
from jax import lax
from jax.experimental import pallas as pl
from jax.experimental.pallas import tpu as pltpu

F32 = jnp.float32
BF16 = jnp.bfloat16

HEAD_DIM = 64
CHUNK = 64
A_HEADS = 4
B_HEADS = 8
C_HEADS = 8
IDX_HEADS = 4
IDX_DIM = 64
TOPK_MAX = 256
N_BRANCHES = 3
ROPE_THETA = 10000.0
EPS = 1e-6

LANES = 128
SUBLANES = 8
HALF_HEAD = HEAD_DIM // 2
NEG = -1e30
INT_MIN = -(2 ** 31)
LOG2E = 1.4426950408889634
VMEM_LIMIT = 56 * 1024 * 1024

ROW_TILE = 512
ATTN_TILE = 256
KEY_TILE = 128
FF_TILE = 512
GROUP = 512


def _params(*sem):
    return pltpu.CompilerParams(dimension_semantics=sem, vmem_limit_bytes=VMEM_LIMIT)


def _dot(a, b):
    return jnp.dot(a, b, preferred_element_type=F32)


def _split(x):
    hi = x.astype(BF16)
    return hi, (x - hi.astype(F32)).astype(BF16)


def _swap_halves(y):
    lane = lax.broadcasted_iota(jnp.int32, y.shape, 1)
    return jnp.where((lane & HALF_HEAD) != 0, pltpu.roll(y, HALF_HEAD, 1),
                     pltpu.roll(y, LANES - HALF_HEAD, 1))


def _rope_slab(y, cos, sin):
    return y * cos + _swap_halves(y) * sin


def _norm_kernel(x_ref, g_ref, o_ref):
    x = x_ref[...]
    ms = jnp.mean(x * x, axis=-1, keepdims=True)
    o_ref[...] = (x * lax.rsqrt(ms + EPS) * g_ref[0]).astype(o_ref.dtype)


def _norm_call(x2, gains, layer):
    n, d = x2.shape
    return pl.pallas_call(
        _norm_kernel,
        grid=(n // ROW_TILE,),
        in_specs=[pl.BlockSpec((ROW_TILE, d), lambda i: (i, 0)),
                  pl.BlockSpec((1, 1, d), lambda i: (layer, 0, 0))],
        out_specs=pl.BlockSpec((ROW_TILE, d), lambda i: (i, 0)),
        out_shape=jax.ShapeDtypeStruct((n, d), BF16),
        compiler_params=_params("parallel"),
    )(x2, gains)


def _proj_qk_kernel(h_ref, w_ref, gain_ref, cos_ref, sin_ref, gmat_ref, o_ref):
    t = _dot(h_ref[...], w_ref[0])
    hi, lo = _split(t * t)
    gmat = gmat_ref[...]
    ms = (_dot(hi, gmat) + _dot(lo, gmat)) * (1.0 / HEAD_DIM)
    y = t * lax.rsqrt(ms + EPS) * gain_ref[0]
    cos = cos_ref[...]
    sin = sin_ref[...]
    for s in range(y.shape[1] // LANES):
        sl = slice(s * LANES, (s + 1) * LANES)
        o_ref[:, sl] = _rope_slab(y[:, sl], cos, sin).astype(o_ref.dtype)


def _proj_qk_call(h, w, gain, cos_t, sin_t, gmat, layer):
    n, d = h.shape
    cols = w.shape[2]
    tn = gmat.shape[0]
    return pl.pallas_call(
        _proj_qk_kernel,
        grid=(n // ROW_TILE, cols // tn),
        in_specs=[pl.BlockSpec((ROW_TILE, d), lambda i, j: (i, 0)),
                  pl.BlockSpec((1, d, tn), lambda i, j: (layer, 0, j)),
                  pl.BlockSpec((1, 1, tn), lambda i, j: (layer, 0, j)),
                  pl.BlockSpec((ROW_TILE, LANES), lambda i, j: (i, 0)),
                  pl.BlockSpec((ROW_TILE, LANES), lambda i, j: (i, 0)),
                  pl.BlockSpec((tn, tn), lambda i, j: (0, 0))],
        out_specs=pl.BlockSpec((ROW_TILE, tn), lambda i, j: (i, j)),
        out_shape=jax.ShapeDtypeStruct((n, cols), BF16),
        compiler_params=_params("parallel", "arbitrary"),
    )(h, w, gain, cos_t, sin_t, gmat)


def _proj_plain_kernel(h_ref, w_ref, o_ref):
    o_ref[...] = _dot(h_ref[...], w_ref[0]).astype(o_ref.dtype)


def _proj_plain_call(h, w, layer):
    n, d = h.shape
    cols = w.shape[2]
    return pl.pallas_call(
        _proj_plain_kernel,
        grid=(n // ROW_TILE, cols // GROUP),
        in_specs=[pl.BlockSpec((ROW_TILE, d), lambda i, j: (i, 0)),
                  pl.BlockSpec((1, d, GROUP), lambda i, j: (layer, 0, j))],
        out_specs=pl.BlockSpec((ROW_TILE, GROUP), lambda i, j: (i, j)),
        out_shape=jax.ShapeDtypeStruct((n, cols), BF16),
        compiler_params=_params("parallel", "arbitrary"),
    )(h, w)


def _proj_vt_kernel(h_ref, w_ref, o_ref):
    t = _dot(h_ref[...], w_ref[0])
    tt = o_ref.shape[2]
    for r in range(o_ref.shape[0]):
        o_ref[r] = t[r * tt:(r + 1) * tt, :].T.astype(o_ref.dtype)


def _proj_vt_call(h, w, layer, tt):
    n, d = h.shape
    cols = w.shape[2]
    return pl.pallas_call(
        _proj_vt_kernel,
        grid=(n // ROW_TILE, cols // GROUP),
        in_specs=[pl.BlockSpec((ROW_TILE, d), lambda i, j: (i, 0)),
                  pl.BlockSpec((1, d, GROUP), lambda i, j: (layer, 0, j))],
        out_specs=pl.BlockSpec((ROW_TILE // tt, GROUP, tt), lambda i, j: (i, j, 0)),
        out_shape=jax.ShapeDtypeStruct((n // tt, cols, tt), BF16),
        compiler_params=_params("parallel", "arbitrary"),
    )(h, w)


IDX_Q = IDX_HEADS * IDX_DIM
IDX_COLS = IDX_Q + LANES


def _proj_idx_kernel(h_ref, w_ref, gain_ref, scale_ref, cos_ref, sin_ref, o_ref):
    t = _dot(h_ref[...], w_ref[0])
    cos = cos_ref[...]
    sin = sin_ref[...]
    scale = scale_ref[...]
    for s in range(IDX_Q // LANES):
        sl = slice(s * LANES, (s + 1) * LANES)
        o_ref[:, sl] = _rope_slab(t[:, sl], cos, sin) * scale[:, sl]
    sl = slice(IDX_Q, IDX_COLS)
    tk = t[:, sl]
    lane = lax.broadcasted_iota(jnp.int32, tk.shape, 1)
    is_k = lane < IDX_DIM
    ms = jnp.sum(jnp.where(is_k, tk * tk, 0.0), axis=-1, keepdims=True) * (1.0 / IDX_DIM)
    yk = tk * lax.rsqrt(ms + EPS) * gain_ref[0]
    o_ref[:, sl] = jnp.where(is_k, _rope_slab(yk, cos, sin), tk) * scale[:, sl]


def _proj_idx_call(h, w, gain, scale, cos_t, sin_t, layer):
    n, d = h.shape
    return pl.pallas_call(
        _proj_idx_kernel,
        grid=(n // ROW_TILE,),
        in_specs=[pl.BlockSpec((ROW_TILE, d), lambda i: (i, 0)),
                  pl.BlockSpec((1, d, IDX_COLS), lambda i: (layer, 0, 0)),
                  pl.BlockSpec((1, 1, LANES), lambda i: (layer, 0, 0)),
                  pl.BlockSpec((1, IDX_COLS), lambda i: (0, 0)),
                  pl.BlockSpec((ROW_TILE, LANES), lambda i: (i, 0)),
                  pl.BlockSpec((ROW_TILE, LANES), lambda i: (i, 0))],
        out_specs=pl.BlockSpec((ROW_TILE, IDX_COLS), lambda i: (i, 0)),
        out_shape=jax.ShapeDtypeStruct((n, IDX_COLS), F32),
        compiler_params=_params("parallel"),
    )(h, w, gain, scale, cos_t, sin_t)


def _load_qt(q_ref, qz_ref):
    qt = q_ref[0].astype(F32).T
    rows = lax.broadcasted_iota(jnp.int32, (LANES, qt.shape[1]), 0)
    for s in range(qt.shape[0] // LANES):
        slab = qt[s * LANES:(s + 1) * LANES, :]
        qz_ref[2 * s] = jnp.where(rows < HEAD_DIM, slab, 0.0).astype(BF16)
        qz_ref[2 * s + 1] = jnp.where(rows >= HEAD_DIM, slab, 0.0).astype(BF16)


def _softmax_step(n, s, vt, m_ref, l_ref, acc_ref):
    m_old = m_ref[n]
    m_new = jnp.maximum(m_old, jnp.max(s, axis=0, keepdims=True))
    p = jnp.exp2(s - m_new)
    alpha = jnp.exp2(m_old - m_new)
    l_ref[n] = alpha * l_ref[n] + jnp.sum(p, axis=0, keepdims=True)
    acc_ref[n] = alpha * acc_ref[n] + _dot(vt, p.astype(BF16))
    m_ref[n] = m_new


def _tile_ids(t):
    key = lax.broadcasted_iota(jnp.int32, (KEY_TILE, t), 0)
    qry = lax.broadcasted_iota(jnp.int32, (KEY_TILE, t), 1)
    return key, qry


def _key_rows(u):
    return pl.ds(pl.multiple_of(u * KEY_TILE, KEY_TILE), KEY_TILE)


def _attn_a_kernel(lam_ref, q_ref, k_ref, vt_ref, g_ref, o_ref, qz_ref, m_ref, l_ref, acc_ref, *,
                   lam_init):
    t = q_ref.shape[1]
    i = pl.program_id(1)
    lp = lam_ref[0]
    lam = (jnp.exp(jnp.sum(lp[0:1] * lp[1:2], axis=-1, keepdims=True))
           - jnp.exp(jnp.sum(lp[2:3] * lp[3:4], axis=-1, keepdims=True)) + lam_init)
    key, qry = _tile_ids(t)
    sub = t // KEY_TILE
    _load_qt(q_ref, qz_ref)
    m_ref[...] = jnp.full(m_ref.shape, NEG, F32)
    l_ref[...] = jnp.zeros(l_ref.shape, F32)
    acc_ref[...] = jnp.zeros(acc_ref.shape, F32)

    def step(u, diag):
        scores = []
        for n in range(2 * A_HEADS):
            kb = k_ref[0, _key_rows(u), (n // 2) * LANES:(n // 2 + 1) * LANES]
            scores.append(_dot(kb, qz_ref[n]))
        for n, s in enumerate(scores):
            if diag is not None:
                s = jnp.where(((key + diag * KEY_TILE) // CHUNK) <= (qry // CHUNK), s, NEG)
            vt = vt_ref[0, u, (n // 2) * LANES:(n // 2 + 1) * LANES, :]
            _softmax_step(n, s, vt, m_ref, l_ref, acc_ref)

    def body(u, carry):
        step(u, None)
        return carry

    lax.fori_loop(0, i * sub, body, 0)
    for r in range(sub):
        step(i * sub + r, r)
    gain = g_ref[0] * (1.0 - lam_init)
    for h in range(A_HEADS):
        o = acc_ref[2 * h] / l_ref[2 * h] - lam * (acc_ref[2 * h + 1] / l_ref[2 * h + 1])
        ms = jnp.mean(o * o, axis=0, keepdims=True)
        y = (o * lax.rsqrt(ms + EPS)).T * gain
        o_ref[0, :, h * LANES:(h + 1) * LANES] = y.astype(o_ref.dtype)


def _attn_a_call(qk, vt, lam_p, subln, layer, lam_init, t):
    b, s, _ = qk.shape
    nt, _, kt = vt.shape[1:]
    ns = 2 * A_HEADS
    return pl.pallas_call(
        functools.partial(_attn_a_kernel, lam_init=lam_init),
        grid=(b, s // t),
        in_specs=[pl.BlockSpec((1, 4, HEAD_DIM), lambda bi, i: (layer, 0, 0)),
                  pl.BlockSpec((1, t, GROUP), lambda bi, i: (bi, i, 0)),
                  pl.BlockSpec((1, s, GROUP), lambda bi, i: (bi, 0, 1)),
                  pl.BlockSpec((1, nt, GROUP, kt), lambda bi, i: (bi, 0, 0, 0)),
                  pl.BlockSpec((1, 1, LANES), lambda bi, i: (layer, 0, 0))],
        out_specs=pl.BlockSpec((1, t, GROUP), lambda bi, i: (bi, i, 0)),
        out_shape=jax.ShapeDtypeStruct((b, s, GROUP), BF16),
        scratch_shapes=[pltpu.VMEM((ns, LANES, t), BF16), pltpu.VMEM((ns, 1, t), F32),
                        pltpu.VMEM((ns, 1, t), F32), pltpu.VMEM((ns, LANES, t), F32)],
        compiler_params=_params("parallel", "arbitrary"),
    )(lam_p, qk, qk, vt, subln)


def _attn_b_kernel(q_ref, k_ref, vt_ref, lmat_ref, o_ref, qz_ref, r_ref, acc_ref):
    t = q_ref.shape[1]
    i = pl.program_id(1)
    key, qry = _tile_ids(t)
    sub = t // KEY_TILE
    lmat = lmat_ref[...]
    _load_qt(q_ref, qz_ref)
    r_ref[...] = jnp.zeros(r_ref.shape, F32)
    acc_ref[...] = jnp.zeros(acc_ref.shape, F32)

    def step(u, diag):
        if diag is not None:
            strict = (key + diag * KEY_TILE) < qry
        logits = []
        for n in range(B_HEADS):
            kb = k_ref[0, _key_rows(u), (n // 2) * LANES:(n // 2 + 1) * LANES]
            logits.append(_dot(kb, qz_ref[n]))
        log_betas, laters = [], []
        for n, z in enumerate(logits):
            soft = jnp.log2(1.0 + jnp.exp2(-jnp.abs(z)))
            log_beta = jnp.minimum(z, 0.0) - soft
            log_om = log_beta - z
            if diag is not None:
                log_om = jnp.where(strict, log_om, 0.0)
            hi, lo = _split(log_om)
            laters.append(_dot(lmat, hi) + _dot(lmat, lo) + r_ref[n])
            log_betas.append(log_beta)
            r_ref[n] = r_ref[n] + jnp.sum(log_om, axis=0, keepdims=True)
        for n in range(B_HEADS):
            a = jnp.exp2(log_betas[n] + laters[n])
            if diag is not None:
                a = jnp.where(strict, a, 0.0)
            vt = vt_ref[0, u, n * HEAD_DIM:(n + 1) * HEAD_DIM, :]
            acc_ref[n] = acc_ref[n] + _dot(vt, a.astype(BF16))

    def body(n, carry):
        step(i * sub - 1 - n, None)
        return carry

    for r in reversed(range(sub)):
        step(i * sub + r, r)
    lax.fori_loop(0, i * sub, body, 0)
    o_ref[0] = acc_ref[...].reshape(B_HEADS * HEAD_DIM, t).T.astype(o_ref.dtype)


def _attn_b_call(bqk, vt, lmat, t):
    b, s, _ = bqk.shape
    nt, _, kt = vt.shape[1:]
    return pl.pallas_call(
        _attn_b_kernel,
        grid=(b, s // t),
        in_specs=[pl.BlockSpec((1, t, GROUP), lambda bi, i: (bi, i, 0)),
                  pl.BlockSpec((1, s, GROUP), lambda bi, i: (bi, 0, 1)),
                  pl.BlockSpec((1, nt, GROUP, kt), lambda bi, i: (bi, 0, 1, 0)),
                  pl.BlockSpec((kt, kt), lambda bi, i: (0, 0))],
        out_specs=pl.BlockSpec((1, t, GROUP), lambda bi, i: (bi, i, 0)),
        out_shape=jax.ShapeDtypeStruct((b, s, GROUP), BF16),
        scratch_shapes=[pltpu.VMEM((B_HEADS, LANES, t), BF16), pltpu.VMEM((B_HEADS, 1, t), F32),
                        pltpu.VMEM((B_HEADS, HEAD_DIM, t), F32)],
        compiler_params=_params("parallel", "arbitrary"),
    )(bqk, bqk, vt, lmat)


def _fold_rows(x):
    parts = [x[r * SUBLANES:(r + 1) * SUBLANES] for r in range(x.shape[0] // SUBLANES)]
    while len(parts) > 1:
        parts = [a + b for a, b in zip(parts[0::2], parts[1::2])]
    return parts[0]


def _dsa_mask_kernel(q_ref, kw_ref, wq_ref, lt_ref, o_ref, key_ref, qz_ref, *, topk):
    t = q_ref.shape[1]
    nj = key_ref.shape[0]
    i = pl.program_id(1)
    nch = (i + 1) * (t // KEY_TILE)
    qt = q_ref[0].T
    pad = jnp.zeros((LANES - IDX_DIM, t), F32)
    for h in range(IDX_HEADS):
        qz_ref[h] = jnp.concatenate([qt[h * IDX_DIM:(h + 1) * IDX_DIM], pad], axis=0).astype(BF16)
    wt = wq_ref[0].T
    wh = [wt[IDX_DIM + h:IDX_DIM + h + 1, :] for h in range(IDX_HEADS)]
    key0, qry0 = _tile_ids(t)
    qry = qry0 + i * t

    def chunk_ok(c):
        return ((key0 + c * KEY_TILE) // CHUNK) <= (qry // CHUNK)

    def fill(c, carry):
        kb = kw_ref[0, _key_rows(c), :].astype(BF16)
        score = jnp.zeros((KEY_TILE, t), F32)
        for h in range(IDX_HEADS):
            score = score + wh[h] * jnp.maximum(_dot(kb, qz_ref[h]), 0.0)
        score = jnp.where(chunk_ok(c), score, -jnp.inf)
        bits = pltpu.bitcast(score, jnp.int32)
        okey = jnp.where(bits < 0, bits ^ jnp.int32(0x7FFFFFFF), bits)
        key_ref[c] = jnp.where(score == 0.0, 0, okey)
        return carry

    lax.fori_loop(0, nch, fill, 0)

    def count(pred, thr):
        def body(c, acc):
            return acc + _fold_rows(jnp.where(pred(key_ref[c], thr), 1.0, 0.0))
        acc = lax.fori_loop(0, nch, body, jnp.zeros((SUBLANES, t), F32))
        return jnp.sum(acc, axis=0, keepdims=True)

    def bit_body(n, lo):
        cand = lo + jnp.left_shift(jnp.int32(1), 31 - n)
        cnt = count(lambda k, th: k >= th, cand)
        return jnp.where(cnt >= topk, cand, lo)

    thr = lax.fori_loop(0, 32, bit_body, jnp.full((1, t), INT_MIN, jnp.int32))
    need = topk - count(lambda k, th: k > th, thr)
    lt = lt_ref[...]

    def emit(c, run):
        okey = key_ref[c]
        gt = okey > thr
        eq = okey == thr
        eqf = jnp.where(eq, 1.0, 0.0)
        before = _dot(lt, eqf.astype(BF16)) + run
        sel = (gt | (eq & (before < need))) & chunk_ok(c)
        o_ref[0, 0, c] = jnp.where(sel, 0.0, NEG).astype(o_ref.dtype)
        return run + jnp.sum(eqf, axis=0, keepdims=True)

    lax.fori_loop(0, nch, emit, jnp.zeros((1, t), F32))

    def blank(c, carry):
        o_ref[0, 0, c] = jnp.full((KEY_TILE, t), NEG, o_ref.dtype)
        return carry

    lax.fori_loop(nch, nj, blank, 0)


def _dsa_mask_call(idx, lt, topk, t):
    b, s, _ = idx.shape
    kt = lt.shape[0]
    assert t >= topk
    nt, nk = s // t, s // kt
    return pl.pallas_call(
        functools.partial(_dsa_mask_kernel, topk=topk),
        grid=(b, nt),
        in_specs=[pl.BlockSpec((1, t, IDX_Q), lambda bi, i: (bi, i, 0)),
                  pl.BlockSpec((1, s, LANES), lambda bi, i: (bi, 0, IDX_Q // LANES)),
                  pl.BlockSpec((1, t, LANES), lambda bi, i: (bi, i, IDX_Q // LANES)),
                  pl.BlockSpec((kt, kt), lambda bi, i: (0, 0))],
        out_specs=pl.BlockSpec((1, 1, nk, kt, t), lambda bi, i: (bi, i, 0, 0, 0)),
        out_shape=jax.ShapeDtypeStruct((b, nt, nk, kt, t), BF16),
        scratch_shapes=[pltpu.VMEM((nk, kt, t), jnp.int32), pltpu.VMEM((IDX_HEADS, LANES, t), BF16)],
        compiler_params=_params("parallel", "arbitrary"),
    )(idx, idx, idx, lt)


def _attn_c_kernel(q_ref, k_ref, vt_ref, b_ref, o_ref, qz_ref, m_ref, l_ref, acc_ref):
    t = q_ref.shape[1]
    i = pl.program_id(1)
    _load_qt(q_ref, qz_ref)
    m_ref[...] = jnp.full(m_ref.shape, NEG, F32)
    l_ref[...] = jnp.zeros(l_ref.shape, F32)
    acc_ref[...] = jnp.zeros(acc_ref.shape, F32)

    key, qry = _tile_ids(KEY_TILE)
    ident = jnp.where(key == qry, 1.0, 0.0).astype(BF16)

    def body(u, carry):
        bias = b_ref[0, 0, u]
        scores = []
        for n in range(C_HEADS):
            kb = k_ref[0, _key_rows(u), (n // 2) * LANES:(n // 2 + 1) * LANES]
            lhs = jnp.concatenate([kb, ident], axis=1)
            rhs = jnp.concatenate([qz_ref[n], bias], axis=0)
            scores.append(_dot(lhs, rhs))
        for n, s in enumerate(scores):
            vt = vt_ref[0, u, n * HEAD_DIM:(n + 1) * HEAD_DIM, :]
            _softmax_step(n, s, vt, m_ref, l_ref, acc_ref)
        return carry

    lax.fori_loop(0, (i + 1) * (t // KEY_TILE), body, 0)
    o = acc_ref[...] / l_ref[...]
    o_ref[0] = o.reshape(C_HEADS * HEAD_DIM, t).T.astype(o_ref.dtype)


def _attn_c_call(qk, vt, bias):
    b, s, _ = qk.shape
    nk, _, kt = vt.shape[1:]
    nt, t = bias.shape[1], bias.shape[4]
    return pl.pallas_call(
        _attn_c_kernel,
        grid=(b, nt),
        in_specs=[pl.BlockSpec((1, t, GROUP), lambda bi, i: (bi, i, 2)),
                  pl.BlockSpec((1, s, GROUP), lambda bi, i: (bi, 0, 3)),
                  pl.BlockSpec((1, nk, GROUP, kt), lambda bi, i: (bi, 0, 2, 0)),
                  pl.BlockSpec((1, 1, nk, kt, t), lambda bi, i: (bi, i, 0, 0, 0))],
        out_specs=pl.BlockSpec((1, t, GROUP), lambda bi, i: (bi, i, 0)),
        out_shape=jax.ShapeDtypeStruct((b, s, GROUP), BF16),
        scratch_shapes=[pltpu.VMEM((C_HEADS, LANES, t), BF16), pltpu.VMEM((C_HEADS, 1, t), F32),
                        pltpu.VMEM((C_HEADS, 1, t), F32), pltpu.VMEM((C_HEADS, HEAD_DIM, t), F32)],
        compiler_params=_params("parallel", "arbitrary"),
    )(qk, qk, vt, bias)


def _merge_kernel(x_ref, h_ref, ya_ref, yb_ref, yc_ref, wg_ref, wa_ref, wb_ref, wc_ref, wo_ref, o_ref):
    d = x_ref.shape[1]
    h = h_ref[...]
    merged = jnp.zeros(x_ref.shape, F32)
    for n, (y_ref, w_ref) in enumerate(((ya_ref, wa_ref), (yb_ref, wb_ref), (yc_ref, wc_ref))):
        gate = jax.nn.sigmoid(_dot(h, wg_ref[0, :, n * d:(n + 1) * d]))
        merged = merged + gate * _dot(y_ref[...], w_ref[0])
    o_ref[...] = x_ref[...] + _dot(merged.astype(BF16), wo_ref[0])


def _merge_call(x2, h, ya, yb, yc, wg, wa, wb, wc, wo, layer):
    n, d = x2.shape
    tm = ROW_TILE
    row = lambda cols: pl.BlockSpec((tm, cols), lambda i: (i, 0))
    full = lambda w: pl.BlockSpec((1,) + w.shape[1:], lambda i: (layer, 0, 0))
    return pl.pallas_call(
        _merge_kernel,
        grid=(n // tm,),
        in_specs=[row(d), row(d), row(ya.shape[1]), row(yb.shape[1]), row(yc.shape[1]),
                  full(wg), full(wa), full(wb), full(wc), full(wo)],
        out_specs=row(d),
        out_shape=jax.ShapeDtypeStruct((n, d), F32),
        compiler_params=_params("parallel"),
    )(x2, h, ya, yb, yc, wg, wa, wb, wc, wo)


def _mlp_kernel(x_ref, p_ref, gm_ref, wu_ref, wd_ref, gp_ref, wpg_ref, wpp_ref, o_ref, h_ref, acc_ref):
    f = pl.program_id(1)

    @pl.when(f == 0)
    def _():
        x = x_ref[...]
        ms = jnp.mean(x * x, axis=-1, keepdims=True)
        h_ref[...] = (x * lax.rsqrt(ms + EPS) * gm_ref[0]).astype(BF16)
        acc_ref[...] = jnp.zeros(acc_ref.shape, F32)

    u = jnp.maximum(_dot(h_ref[...], wu_ref[0]), 0.0)
    acc_ref[...] += _dot((u * u).astype(BF16), wd_ref[0])

    @pl.when(f == pl.num_programs(1) - 1)
    def _():
        x = x_ref[...] + acc_ref[...]
        ms = jnp.mean(x * x, axis=-1, keepdims=True)
        hp = (x * lax.rsqrt(ms + EPS) * gp_ref[0]).astype(BF16)
        gate = jax.nn.sigmoid(_dot(hp, wpg_ref[0]))
        o_ref[...] = x + gate * _dot(p_ref[0].astype(BF16), wpp_ref[0])


def _mlp_call(x2, p3, g_mlp, w_up, w_down, g_ple, w_pg, w_pp, layer):
    n, d = x2.shape
    ff = w_up.shape[2]
    pd = p3.shape[2]
    tm, tf = ROW_TILE, FF_TILE
    return pl.pallas_call(
        _mlp_kernel,
        grid=(n // tm, ff // tf),
        in_specs=[pl.BlockSpec((tm, d), lambda i, f: (i, 0)),
                  pl.BlockSpec((1, tm, pd), lambda i, f: (layer, i, 0)),
                  pl.BlockSpec((1, 1, d), lambda i, f: (layer, 0, 0)),
                  pl.BlockSpec((1, d, tf), lambda i, f: (layer, 0, f)),
                  pl.BlockSpec((1, tf, d), lambda i, f: (layer, f, 0)),
                  pl.BlockSpec((1, 1, d), lambda i, f: (layer, 0, 0)),
                  pl.BlockSpec((1, d, d), lambda i, f: (layer, 0, 0)),
                  pl.BlockSpec((1, pd, d), lambda i, f: (layer, 0, 0))],
        out_specs=pl.BlockSpec((tm, d), lambda i, f: (i, 0)),
        out_shape=jax.ShapeDtypeStruct((n, d), F32),
        scratch_shapes=[pltpu.VMEM((tm, d), BF16), pltpu.VMEM((tm, d), F32)],
        compiler_params=_params("parallel", "arbitrary"),
    )(x2, p3, g_mlp, w_up, w_down, g_ple, w_pg, w_pp)


def _tile_gain(g, reps, scale=1.0):
    return jnp.tile(g.astype(F32), (1, reps)) * scale


def kernel(x, p, positions, attn_norm, w_in, a_q_norm, a_k_norm, a_lambda, a_subln,
           c_q_norm, c_k_norm, idx_k_norm, w_br_a, w_br_b, w_br_c, w_out,
           mlp_norm, w_up, w_down, ple_norm, w_ple_gate, w_ple_proj):
    b, s, d = x.shape
    depth = w_in.shape[0]
    n = b * s
    topk = min(TOPK_MAX, s // 4)
    t = min(ATTN_TILE, s)
    scale = HEAD_DIM ** -0.5
    qscale = scale * LOG2E

    inv = ROPE_THETA ** (-jnp.arange(0, HEAD_DIM, 2, dtype=F32) / HEAD_DIM)
    ang = positions.astype(F32)[..., None] * inv
    cos, sin = jnp.cos(ang), jnp.sin(ang)
    cos_t = jnp.concatenate([cos, cos, cos, cos], axis=-1).reshape(n, LANES)
    sin_t = jnp.concatenate([-sin, sin, -sin, sin], axis=-1).reshape(n, LANES)

    qa, ka, va = 0, 512, 1024
    qb, kb, vb = 1536, 2048, 2560
    qc, kc, vc = 3072, 3584, 4096
    qi = 4608
    gl = 4932
    col = lambda a, width=GROUP: w_in[:, :, a:a + width]
    w_qk = jnp.concatenate([col(qa), col(ka), col(qc), col(kc)], axis=-1).astype(BF16)
    w_bqk = jnp.concatenate([col(qb) * qscale, col(kb)], axis=-1).astype(BF16)
    w_vt = jnp.concatenate([col(va), col(vb), col(vc)], axis=-1).astype(BF16)
    w_idx = jnp.pad(col(qi, gl - qi), ((0, 0), (0, 0), (0, IDX_COLS - (gl - qi)))).astype(BF16)
    w_gl = col(gl, N_BRANCHES * d).astype(BF16)
    heads = GROUP // HEAD_DIM
    gain_qk = jnp.concatenate([_tile_gain(a_q_norm, heads, qscale), _tile_gain(a_k_norm, heads),
                               _tile_gain(c_q_norm, heads, qscale), _tile_gain(c_k_norm, heads)],
                              axis=-1)[:, None, :]
    gain_idx = jnp.pad(idx_k_norm.astype(F32), ((0, 0), (0, LANES - IDX_DIM)))[:, None, :]
    scale_idx = jnp.concatenate([jnp.full((IDX_Q,), IDX_DIM ** -0.5, F32), jnp.ones((IDX_DIM,), F32),
                                 jnp.full((IDX_HEADS,), IDX_HEADS ** -0.5, F32),
                                 jnp.zeros((LANES - IDX_DIM - IDX_HEADS,), F32)])[None, :]
    gid = jnp.arange(GROUP) // HEAD_DIM
    gmat = (gid[:, None] == gid[None, :]).astype(BF16)
    kt = min(KEY_TILE, s)
    ar = jnp.arange(kt)
    later_mat = (ar[None, :] > ar[:, None]).astype(BF16)
    before_mat = (ar[None, :] < ar[:, None]).astype(BF16)

    wa, wb, wc, wo = (w.astype(BF16) for w in (w_br_a, w_br_b, w_br_c, w_out))
    wu, wd, wpg, wpp = (w.astype(BF16) for w in (w_up, w_down, w_ple_gate, w_ple_proj))
    g_attn, g_mlp, g_ple = (g.astype(F32)[:, None, :] for g in (attn_norm, mlp_norm, ple_norm))
    subln = a_subln.astype(F32)[:, None, :]
    lam_p = a_lambda.astype(F32)
    p3 = p.reshape(depth, n, p.shape[-1])

    x2 = x.reshape(n, d)
    for i in range(depth):
        lam_init = 0.8 - 0.6 * math.exp(-0.3 * i)
        h = _norm_call(x2, g_attn, i)
        qk = _proj_qk_call(h, w_qk, gain_qk, cos_t, sin_t, gmat, i).reshape(b, s, -1)
        bqk = _proj_plain_call(h, w_bqk, i).reshape(b, s, -1)
        vt = _proj_vt_call(h, w_vt, i, kt).reshape(b, s // kt, -1, kt)
        idx = _proj_idx_call(h, w_idx, gain_idx, scale_idx, cos_t, sin_t, i).reshape(b, s, -1)
        ya = _attn_a_call(qk, vt, lam_p, subln, i, lam_init, t)
        yb = _attn_b_call(bqk, vt, later_mat, t)
        bias = _dsa_mask_call(idx, before_mat, topk, t)
        yc = _attn_c_call(qk, vt, bias)
        x2 = _merge_call(x2, h, ya.reshape(n, -1), yb.reshape(n, -1), yc.reshape(n, -1),
                         w_gl, wa, wb, wc, wo, i)
        x2 = _mlp_call(x2, p3, g_mlp, wu, wd, g_ple, wpg, wpp, i)
    return x2.reshape(b, s, d)
```

```python
import functools
import math

import jax
import jax.numpy as jnp
from jax import lax
from jax.experimental import pallas as pl
from jax.experimental.pallas import tpu as pltpu

F32 = jnp.float32
BF16 = jnp.bfloat16

HEAD_DIM = 64
CHUNK = 64
A_HEADS = 4
B_HEADS = 8
C_HEADS = 8
IDX_HEADS = 4
IDX_DIM = 64
TOPK_MAX = 256
N_BRANCHES = 3
ROPE_THETA = 10000.0
EPS = 1e-6

LANES = 128
SUBLANES = 8
HALF_HEAD = HEAD_DIM // 2
NEG = -1e30
INT_MIN = -(2 ** 31)
LOG2E = 1.4426950408889634
VMEM_LIMIT = 56 * 1024 * 1024

ROW_TILE = 512
ATTN_TILE = 256
KEY_TILE = 128
FF_TILE = 512
GROUP = 512


def _params(*sem):
    return pltpu.CompilerParams(dimension_semantics=sem, vmem_limit_bytes=VMEM_LIMIT)


def _dot(a, b):
    return jnp.dot(a, b, preferred_element_type=F32)


def _split(x):
    hi = x.astype(BF16)
    return hi, (x - hi.astype(F32)).astype(BF16)


def _swap_halves(y):
    lane = lax.broadcasted_iota(jnp.int32, y.shape, 1)
    return jnp.where((lane & HALF_HEAD) != 0, pltpu.roll(y, HALF_HEAD, 1),
                     pltpu.roll(y, LANES - HALF_HEAD, 1))


def _rope_slab(y, cos, sin):
    return y * cos + _swap_halves(y) * sin


def _norm_kernel(x_ref, g_ref, o_ref):
    x = x_ref[...]
    ms = jnp.mean(x * x, axis=-1, keepdims=True)
    o_ref[...] = (x * lax.rsqrt(ms + EPS) * g_ref[0]).astype(o_ref.dtype)


def _norm_call(x2, gains, layer):
    n, d = x2.shape
    return pl.pallas_call(
        _norm_kernel,
        grid=(n // ROW_TILE,),
        in_specs=[pl.BlockSpec((ROW_TILE, d), lambda i: (i, 0)),
                  pl.BlockSpec((1, 1, d), lambda i: (layer, 0, 0))],
        out_specs=pl.BlockSpec((ROW_TILE, d), lambda i: (i, 0)),
        out_shape=jax.ShapeDtypeStruct((n, d), BF16),
        compiler_params=_params("parallel"),
    )(x2, gains)


def _proj_qk_kernel(h_ref, w_ref, gain_ref, cos_ref, sin_ref, gmat_ref, o_ref):
    t = _dot(h_ref[...], w_ref[0])
    hi, lo = _split(t * t)
    gmat = gmat_ref[...]
    ms = (_dot(hi, gmat) + _dot(lo, gmat)) * (1.0 / HEAD_DIM)
    y = t * lax.rsqrt(ms + EPS) * gain_ref[0]
    cos = cos_ref[...]
    sin = sin_ref[...]
    for s in range(y.shape[1] // LANES):
        sl = slice(s * LANES, (s + 1) * LANES)
        o_ref[:, sl] = _rope_slab(y[:, sl], cos, sin).astype(o_ref.dtype)


def _proj_qk_call(h, w, gain, cos_t, sin_t, gmat, layer):
    n, d = h.shape
    cols = w.shape[2]
    tn = gmat.shape[0]
    return pl.pallas_call(
        _proj_qk_kernel,
        grid=(n // ROW_TILE, cols // tn),
        in_specs=[pl.BlockSpec((ROW_TILE, d), lambda i, j: (i, 0)),
                  pl.BlockSpec((1, d, tn), lambda i, j: (layer, 0, j)),
                  pl.BlockSpec((1, 1, tn), lambda i, j: (layer, 0, j)),
                  pl.BlockSpec((ROW_TILE, LANES), lambda i, j: (i, 0)),
                  pl.BlockSpec((ROW_TILE, LANES), lambda i, j: (i, 0)),
                  pl.BlockSpec((tn, tn), lambda i, j: (0, 0))],
        out_specs=pl.BlockSpec((ROW_TILE, tn), lambda i, j: (i, j)),
        out_shape=jax.ShapeDtypeStruct((n, cols), BF16),
        compiler_params=_params("parallel", "arbitrary"),
    )(h, w, gain, cos_t, sin_t, gmat)


def _proj_plain_kernel(h_ref, w_ref, o_ref):
    o_ref[...] = _dot(h_ref[...], w_ref[0]).astype(o_ref.dtype)


def _proj_plain_call(h, w, layer):
    n, d = h.shape
    cols = w.shape[2]
    return pl.pallas_call(
        _proj_plain_kernel,
        grid=(n // ROW_TILE, cols // GROUP),
        in_specs=[pl.BlockSpec((ROW_TILE, d), lambda i, j: (i, 0)),
                  pl.BlockSpec((1, d, GROUP), lambda i, j: (layer, 0, j))],
        out_specs=pl.BlockSpec((ROW_TILE, GROUP), lambda i, j: (i, j)),
        out_shape=jax.ShapeDtypeStruct((n, cols), BF16),
        compiler_params=_params("parallel", "arbitrary"),
    )(h, w)


def _proj_vt_kernel(h_ref, w_ref, o_ref):
    t = _dot(h_ref[...], w_ref[0])
    tt = o_ref.shape[2]
    for r in range(o_ref.shape[0]):
        o_ref[r] = t[r * tt:(r + 1) * tt, :].T.astype(o_ref.dtype)


def _proj_vt_call(h, w, layer, tt):
    n, d = h.shape
    cols = w.shape[2]
    return pl.pallas_call(
        _proj_vt_kernel,
        grid=(n // ROW_TILE, cols // GROUP),
        in_specs=[pl.BlockSpec((ROW_TILE, d), lambda i, j: (i, 0)),
                  pl.BlockSpec((1, d, GROUP), lambda i, j: (layer, 0, j))],
        out_specs=pl.BlockSpec((ROW_TILE // tt, GROUP, tt), lambda i, j: (i, j, 0)),
        out_shape=jax.ShapeDtypeStruct((n // tt, cols, tt), BF16),
        compiler_params=_params("parallel", "arbitrary"),
    )(h, w)


IDX_Q = IDX_HEADS * IDX_DIM
IDX_COLS = IDX_Q + LANES


def _proj_idx_kernel(h_ref, w_ref, gain_ref, scale_ref, cos_ref, sin_ref, o_ref):
    t = _dot(h_ref[...], w_ref[0])
    cos = cos_ref[...]
    sin = sin_ref[...]
    scale = scale_ref[...]
    for s in range(IDX_Q // LANES):
        sl = slice(s * LANES, (s + 1) * LANES)
        o_ref[:, sl] = _rope_slab(t[:, sl], cos, sin) * scale[:, sl]
    sl = slice(IDX_Q, IDX_COLS)
    tk = t[:, sl]
    lane = lax.broadcasted_iota(jnp.int32, tk.shape, 1)
    is_k = lane < IDX_DIM
    ms = jnp.sum(jnp.where(is_k, tk * tk, 0.0), axis=-1, keepdims=True) * (1.0 / IDX_DIM)
    yk = tk * lax.rsqrt(ms + EPS) * gain_ref[0]
    o_ref[:, sl] = jnp.where(is_k, _rope_slab(yk, cos, sin), tk) * scale[:, sl]


def _proj_idx_call(h, w, gain, scale, cos_t, sin_t, layer):
    n, d = h.shape
    return pl.pallas_call(
        _proj_idx_kernel,
        grid=(n // ROW_TILE,),
        in_specs=[pl.BlockSpec((ROW_TILE, d), lambda i: (i, 0)),
                  pl.BlockSpec((1, d, IDX_COLS), lambda i: (layer, 0, 0)),
                  pl.BlockSpec((1, 1, LANES), lambda i: (layer, 0, 0)),
                  pl.BlockSpec((1, IDX_COLS), lambda i: (0, 0)),
                  pl.BlockSpec((ROW_TILE, LANES), lambda i: (i, 0)),
                  pl.BlockSpec((ROW_TILE, LANES), lambda i: (i, 0))],
        out_specs=pl.BlockSpec((ROW_TILE, IDX_COLS), lambda i: (i, 0)),
        out_shape=jax.ShapeDtypeStruct((n, IDX_COLS), F32),
        compiler_params=_params("parallel"),
    )(h, w, gain, scale, cos_t, sin_t)


def _load_qt(q_ref, qz_ref):
    qt = q_ref[0].astype(F32).T
    rows = lax.broadcasted_iota(jnp.int32, (LANES, qt.shape[1]), 0)
    for s in range(qt.shape[0] // LANES):
        slab = qt[s * LANES:(s + 1) * LANES, :]
        qz_ref[2 * s] = jnp.where(rows < HEAD_DIM, slab, 0.0).astype(BF16)
        qz_ref[2 * s + 1] = jnp.where(rows >= HEAD_DIM, slab, 0.0).astype(BF16)


def _softmax_step(n, s, vt, m_ref, l_ref, acc_ref):
    m_old = m_ref[n]
    m_new = jnp.maximum(m_old, jnp.max(s, axis=0, keepdims=True))
    p = jnp.exp2(s - m_new)
    alpha = jnp.exp2(m_old - m_new)
    l_ref[n] = alpha * l_ref[n] + jnp.sum(p, axis=0, keepdims=True)
    acc_ref[n] = alpha * acc_ref[n] + _dot(vt, p.astype(BF16))
    m_ref[n] = m_new


def _tile_ids(t):
    key = lax.broadcasted_iota(jnp.int32, (KEY_TILE, t), 0)
    qry = lax.broadcasted_iota(jnp.int32, (KEY_TILE, t), 1)
    return key, qry


def _key_rows(u):
    return pl.ds(pl.multiple_of(u * KEY_TILE, KEY_TILE), KEY_TILE)


def _attn_a_kernel(lam_ref, q_ref, k_ref, vt_ref, g_ref, o_ref, qz_ref, m_ref, l_ref, acc_ref, *,
                   lam_init):
    t = q_ref.shape[1]
    i = pl.program_id(1)
    lp = lam_ref[0]
    lam = (jnp.exp(jnp.sum(lp[0:1] * lp[1:2], axis=-1, keepdims=True))
           - jnp.exp(jnp.sum(lp[2:3] * lp[3:4], axis=-1, keepdims=True)) + lam_init)
    key, qry = _tile_ids(t)
    sub = t // KEY_TILE
    _load_qt(q_ref, qz_ref)
    m_ref[...] = jnp.full(m_ref.shape, NEG, F32)
    l_ref[...] = jnp.zeros(l_ref.shape, F32)
    acc_ref[...] = jnp.zeros(acc_ref.shape, F32)

    def step(u, diag):
        scores = []
        for n in range(2 * A_HEADS):
            kb = k_ref[0, _key_rows(u), (n // 2) * LANES:(n // 2 + 1) * LANES]
            scores.append(_dot(kb, qz_ref[n]))
        for n, s in enumerate(scores):
            if diag is not None:
                s = jnp.where(((key + diag * KEY_TILE) // CHUNK) <= (qry // CHUNK), s, NEG)
            vt = vt_ref[0, u, (n // 2) * LANES:(n // 2 + 1) * LANES, :]
            _softmax_step(n, s, vt, m_ref, l_ref, acc_ref)

    def body(u, carry):
        step(u, None)
        return carry

    lax.fori_loop(0, i * sub, body, 0)
    for r in range(sub):
        step(i * sub + r, r)
    gain = g_ref[0] * (1.0 - lam_init)
    for h in range(A_HEADS):
        o = acc_ref[2 * h] / l_ref[2 * h] - lam * (acc_ref[2 * h + 1] / l_ref[2 * h + 1])
        ms = jnp.mean(o * o, axis=0, keepdims=True)
        y = (o * lax.rsqrt(ms + EPS)).T * gain
        o_ref[0, :, h * LANES:(h + 1) * LANES] = y.astype(o_ref.dtype)


def _attn_a_call(qk, vt, lam_p, subln, layer, lam_init, t):
    b, s, _ = qk.shape
    nt, _, kt = vt.shape[1:]
    ns = 2 * A_HEADS
    return pl.pallas_call(
        functools.partial(_attn_a_kernel, lam_init=lam_init),
        grid=(b, s // t),
        in_specs=[pl.BlockSpec((1, 4, HEAD_DIM), lambda bi, i: (layer, 0, 0)),
                  pl.BlockSpec((1, t, GROUP), lambda bi, i: (bi, i, 0)),
                  pl.BlockSpec((1, s, GROUP), lambda bi, i: (bi, 0, 1)),
                  pl.BlockSpec((1, nt, GROUP, kt), lambda bi, i: (bi, 0, 0, 0)),
                  pl.BlockSpec((1, 1, LANES), lambda bi, i: (layer, 0, 0))],
        out_specs=pl.BlockSpec((1, t, GROUP), lambda bi, i: (bi, i, 0)),
        out_shape=jax.ShapeDtypeStruct((b, s, GROUP), BF16),
        scratch_shapes=[pltpu.VMEM((ns, LANES, t), BF16), pltpu.VMEM((ns, 1, t), F32),
                        pltpu.VMEM((ns, 1, t), F32), pltpu.VMEM((ns, LANES, t), F32)],
        compiler_params=_params("parallel", "arbitrary"),
    )(lam_p, qk, qk, vt, subln)


def _attn_b_kernel(q_ref, k_ref, vt_ref, lmat_ref, o_ref, qz_ref, r_ref, acc_ref):
    t = q_ref.shape[1]
    i = pl.program_id(1)
    key, qry = _tile_ids(t)
    sub = t // KEY_TILE
    lmat = lmat_ref[...]
    _load_qt(q_ref, qz_ref)
    r_ref[...] = jnp.zeros(r_ref.shape, F32)
    acc_ref[...] = jnp.zeros(acc_ref.shape, F32)

    def step(u, diag):
        if diag is not None:
            strict = (key + diag * KEY_TILE) < qry
        logits = []
        for n in range(B_HEADS):
            kb = k_ref[0, _key_rows(u), (n // 2) * LANES:(n // 2 + 1) * LANES]
            logits.append(_dot(kb, qz_ref[n]))
        log_betas, laters, sums = [], [], []
        for n, z in enumerate(logits):
            soft = jnp.log2(1.0 + jnp.exp2(jnp.abs(z) * -1.0))
            log_beta = jnp.minimum(z, 0.0) - soft
            log_om = log_beta - z
            if diag is not None:
                log_om = jnp.where(strict, log_om, 0.0)
            laters.append(_dot(lmat, log_om.astype(BF16)))
            log_betas.append(log_beta)
            sums.append(jnp.sum(log_om, axis=0, keepdims=True))
        for n in range(B_HEADS):
            a = jnp.exp2(log_betas[n] + laters[n])
            if diag is not None:
                a = jnp.where(strict, a, 0.0)
            vt = vt_ref[0, u, n * HEAD_DIM:(n + 1) * HEAD_DIM, :]
            r = r_ref[n]
            acc_ref[n] = acc_ref[n] + _dot(vt, a.astype(BF16)) * jnp.exp2(r)
            r_ref[n] = r + sums[n]

    def body(n, carry):
        step(i * sub - 1 - n, None)
        return carry

    for r in reversed(range(sub)):
        step(i * sub + r, r)
    lax.fori_loop(0, i * sub, body, 0)
    o_ref[0] = acc_ref[...].reshape(B_HEADS * HEAD_DIM, t).T.astype(o_ref.dtype)


def _attn_b_call(bqk, vt, lmat, t):
    b, s, _ = bqk.shape
    nt, _, kt = vt.shape[1:]
    return pl.pallas_call(
        _attn_b_kernel,
        grid=(b, s // t),
        in_specs=[pl.BlockSpec((1, t, GROUP), lambda bi, i: (bi, i, 0)),
                  pl.BlockSpec((1, s, GROUP), lambda bi, i: (bi, 0, 1)),
                  pl.BlockSpec((1, nt, GROUP, kt), lambda bi, i: (bi, 0, 1, 0)),
                  pl.BlockSpec((kt, kt), lambda bi, i: (0, 0))],
        out_specs=pl.BlockSpec((1, t, GROUP), lambda bi, i: (bi, i, 0)),
        out_shape=jax.ShapeDtypeStruct((b, s, GROUP), BF16),
        scratch_shapes=[pltpu.VMEM((B_HEADS, LANES, t), BF16), pltpu.VMEM((B_HEADS, 1, t), F32),
                        pltpu.VMEM((B_HEADS, HEAD_DIM, t), F32)],
        compiler_params=_params("parallel", "arbitrary"),
    )(bqk, bqk, vt, lmat)


def _fold_rows(x):
    parts = [x[r * SUBLANES:(r + 1) * SUBLANES] for r in range(x.shape[0] // SUBLANES)]
    while len(parts) > 1:
        parts = [a + b for a, b in zip(parts[0::2], parts[1::2])]
    return parts[0]


def _dsa_mask_kernel(q_ref, kw_ref, wq_ref, lt_ref, o_ref, key_ref, qz_ref, *, topk):
    t = q_ref.shape[1]
    nj = key_ref.shape[0]
    i = pl.program_id(1)
    nch = (i + 1) * (t // KEY_TILE)
    qt = q_ref[0].T
    pad = jnp.zeros((LANES - IDX_DIM, t), F32)
    for h in range(IDX_HEADS):
        qz_ref[h] = jnp.concatenate([qt[h * IDX_DIM:(h + 1) * IDX_DIM], pad], axis=0).astype(BF16)
    wt = wq_ref[0].T
    wh = [wt[IDX_DIM + h:IDX_DIM + h + 1, :] for h in range(IDX_HEADS)]
    key0, qry0 = _tile_ids(t)
    sub = t // KEY_TILE

    def chunk_ok(r):
        return ((key0 + r * KEY_TILE) // CHUNK) <= (qry0 // CHUNK)

    def fill(g, diag):
        for r in range(sub):
            c = g * sub + r
            kb = kw_ref[0, _key_rows(c), :].astype(BF16)
            score = jnp.zeros((KEY_TILE, t), F32)
            for h in range(IDX_HEADS):
                score = score + wh[h] * jnp.maximum(_dot(kb, qz_ref[h]), 0.0)
            if diag:
                score = jnp.where(chunk_ok(r), score, -jnp.inf)
            bits = pltpu.bitcast(score, jnp.int32)
            okey = jnp.where(bits < 0, bits ^ jnp.int32(0x7FFFFFFF), bits)
            key_ref[c] = jnp.where(score == 0.0, 0, okey)

    def fill_body(g, carry):
        fill(g, False)
        return carry

    lax.fori_loop(0, i, fill_body, 0)
    fill(i, True)

    def count(pred, thr):
        def body(g, acc):
            for r in range(sub):
                acc = acc + _fold_rows(jnp.where(pred(key_ref[g * sub + r], thr), 1.0, 0.0))
            return acc
        acc = lax.fori_loop(0, i + 1, body, jnp.zeros((SUBLANES, t), F32))
        return jnp.sum(acc, axis=0, keepdims=True)

    def bit_body(n, lo):
        cand = lo + jnp.left_shift(jnp.int32(1), 31 - n)
        cnt = count(lambda k, th: k >= th, cand)
        return jnp.where(cnt >= topk, cand, lo)

    thr = lax.fori_loop(0, 32, bit_body, jnp.full((1, t), INT_MIN, jnp.int32))
    need = topk - count(lambda k, th: k > th, thr)
    lt = lt_ref[...]

    def emit(g, run, diag):
        for r in range(sub):
            c = g * sub + r
            okey = key_ref[c]
            eq = okey == thr
            eqf = jnp.where(eq, 1.0, 0.0)
            before = _dot(lt, eqf.astype(BF16)) + run
            sel = (okey > thr) | (eq & (before < need))
            if diag:
                sel = sel & chunk_ok(r)
            o_ref[0, 0, c] = jnp.where(sel, 0.0, NEG).astype(o_ref.dtype)
            run = run + jnp.sum(eqf, axis=0, keepdims=True)
        return run

    run = lax.fori_loop(0, i, lambda g, run: emit(g, run, False), jnp.zeros((1, t), F32))
    emit(i, run, True)

    def blank(c, carry):
        o_ref[0, 0, c] = jnp.full((KEY_TILE, t), NEG, o_ref.dtype)
        return carry

    lax.fori_loop(nch, nj, blank, 0)


def _dsa_mask_call(idx, lt, topk, t):
    b, s, _ = idx.shape
    kt = lt.shape[0]
    assert t >= topk
    nt, nk = s // t, s // kt
    return pl.pallas_call(
        functools.partial(_dsa_mask_kernel, topk=topk),
        grid=(b, nt),
        in_specs=[pl.BlockSpec((1, t, IDX_Q), lambda bi, i: (bi, i, 0)),
                  pl.BlockSpec((1, s, LANES), lambda bi, i: (bi, 0, IDX_Q // LANES)),
                  pl.BlockSpec((1, t, LANES), lambda bi, i: (bi, i, IDX_Q // LANES)),
                  pl.BlockSpec((kt, kt), lambda bi, i: (0, 0))],
        out_specs=pl.BlockSpec((1, 1, nk, kt, t), lambda bi, i: (bi, i, 0, 0, 0)),
        out_shape=jax.ShapeDtypeStruct((b, nt, nk, kt, t), BF16),
        scratch_shapes=[pltpu.VMEM((nk, kt, t), jnp.int32), pltpu.VMEM((IDX_HEADS, LANES, t), BF16)],
        compiler_params=_params("parallel", "arbitrary"),
    )(idx, idx, idx, lt)


def _attn_c_kernel(q_ref, k_ref, vt_ref, b_ref, o_ref, qz_ref, m_ref, l_ref, acc_ref):
    t = q_ref.shape[1]
    i = pl.program_id(1)
    _load_qt(q_ref, qz_ref)
    m_ref[...] = jnp.full(m_ref.shape, NEG, F32)
    l_ref[...] = jnp.zeros(l_ref.shape, F32)
    acc_ref[...] = jnp.zeros(acc_ref.shape, F32)

    key, qry = _tile_ids(KEY_TILE)
    ident = jnp.where(key == qry, 1.0, 0.0).astype(BF16)

    def body(u, carry):
        bias = b_ref[0, 0, u]
        scores = []
        for n in range(C_HEADS):
            kb = k_ref[0, _key_rows(u), (n // 2) * LANES:(n // 2 + 1) * LANES]
            lhs = jnp.concatenate([kb, ident], axis=1)
            rhs = jnp.concatenate([qz_ref[n], bias], axis=0)
            scores.append(_dot(lhs, rhs))
        for n, s in enumerate(scores):
            vt = vt_ref[0, u, n * HEAD_DIM:(n + 1) * HEAD_DIM, :]
            _softmax_step(n, s, vt, m_ref, l_ref, acc_ref)
        return carry

    lax.fori_loop(0, (i + 1) * (t // KEY_TILE), body, 0)
    o = acc_ref[...] / l_ref[...]
    o_ref[0] = o.reshape(C_HEADS * HEAD_DIM, t).T.astype(o_ref.dtype)


def _attn_c_call(qk, vt, bias):
    b, s, _ = qk.shape
    nk, _, kt = vt.shape[1:]
    nt, t = bias.shape[1], bias.shape[4]
    return pl.pallas_call(
        _attn_c_kernel,
        grid=(b, nt),
        in_specs=[pl.BlockSpec((1, t, GROUP), lambda bi, i: (bi, i, 2)),
                  pl.BlockSpec((1, s, GROUP), lambda bi, i: (bi, 0, 3)),
                  pl.BlockSpec((1, nk, GROUP, kt), lambda bi, i: (bi, 0, 2, 0)),
                  pl.BlockSpec((1, 1, nk, kt, t), lambda bi, i: (bi, i, 0, 0, 0))],
        out_specs=pl.BlockSpec((1, t, GROUP), lambda bi, i: (bi, i, 0)),
        out_shape=jax.ShapeDtypeStruct((b, s, GROUP), BF16),
        scratch_shapes=[pltpu.VMEM((C_HEADS, LANES, t), BF16), pltpu.VMEM((C_HEADS, 1, t), F32),
                        pltpu.VMEM((C_HEADS, 1, t), F32), pltpu.VMEM((C_HEADS, HEAD_DIM, t), F32)],
        compiler_params=_params("parallel", "arbitrary"),
    )(qk, qk, vt, bias)


def _merge_kernel(x_ref, h_ref, ya_ref, yb_ref, yc_ref, wg_ref, wa_ref, wb_ref, wc_ref, wo_ref, o_ref):
    d = x_ref.shape[1]
    h = h_ref[...]
    merged = jnp.zeros(x_ref.shape, F32)
    for n, (y_ref, w_ref) in enumerate(((ya_ref, wa_ref), (yb_ref, wb_ref), (yc_ref, wc_ref))):
        gate = jax.nn.sigmoid(_dot(h, wg_ref[0, :, n * d:(n + 1) * d]))
        merged = merged + gate * _dot(y_ref[...], w_ref[0])
    o_ref[...] = x_ref[...] + _dot(merged.astype(BF16), wo_ref[0])


def _merge_call(x2, h, ya, yb, yc, wg, wa, wb, wc, wo, layer):
    n, d = x2.shape
    tm = ROW_TILE
    row = lambda cols: pl.BlockSpec((tm, cols), lambda i: (i, 0))
    full = lambda w: pl.BlockSpec((1,) + w.shape[1:], lambda i: (layer, 0, 0))
    return pl.pallas_call(
        _merge_kernel,
        grid=(n // tm,),
        in_specs=[row(d), row(d), row(ya.shape[1]), row(yb.shape[1]), row(yc.shape[1]),
                  full(wg), full(wa), full(wb), full(wc), full(wo)],
        out_specs=row(d),
        out_shape=jax.ShapeDtypeStruct((n, d), F32),
        compiler_params=_params("parallel"),
    )(x2, h, ya, yb, yc, wg, wa, wb, wc, wo)


def _mlp_kernel(x_ref, p_ref, gm_ref, wu_ref, wd_ref, gp_ref, wpg_ref, wpp_ref, o_ref, h_ref, acc_ref):
    f = pl.program_id(1)

    @pl.when(f == 0)
    def _():
        x = x_ref[...]
        ms = jnp.mean(x * x, axis=-1, keepdims=True)
        h_ref[...] = (x * lax.rsqrt(ms + EPS) * gm_ref[0]).astype(BF16)
        acc_ref[...] = jnp.zeros(acc_ref.shape, F32)

    u = jnp.maximum(_dot(h_ref[...], wu_ref[0]), 0.0)
    acc_ref[...] += _dot((u * u).astype(BF16), wd_ref[0])

    @pl.when(f == pl.num_programs(1) - 1)
    def _():
        x = x_ref[...] + acc_ref[...]
        ms = jnp.mean(x * x, axis=-1, keepdims=True)
        hp = (x * lax.rsqrt(ms + EPS) * gp_ref[0]).astype(BF16)
        gate = jax.nn.sigmoid(_dot(hp, wpg_ref[0]))
        o_ref[...] = x + gate * _dot(p_ref[0].astype(BF16), wpp_ref[0])


def _mlp_call(x2, p3, g_mlp, w_up, w_down, g_ple, w_pg, w_pp, layer):
    n, d = x2.shape
    ff = w_up.shape[2]
    pd = p3.shape[2]
    tm, tf = ROW_TILE, FF_TILE
    return pl.pallas_call(
        _mlp_kernel,
        grid=(n // tm, ff // tf),
        in_specs=[pl.BlockSpec((tm, d), lambda i, f: (i, 0)),
                  pl.BlockSpec((1, tm, pd), lambda i, f: (layer, i, 0)),
                  pl.BlockSpec((1, 1, d), lambda i, f: (layer, 0, 0)),
                  pl.BlockSpec((1, d, tf), lambda i, f: (layer, 0, f)),
                  pl.BlockSpec((1, tf, d), lambda i, f: (layer, f, 0)),
                  pl.BlockSpec((1, 1, d), lambda i, f: (layer, 0, 0)),
                  pl.BlockSpec((1, d, d), lambda i, f: (layer, 0, 0)),
                  pl.BlockSpec((1, pd, d), lambda i, f: (layer, 0, 0))],
        out_specs=pl.BlockSpec((tm, d), lambda i, f: (i, 0)),
        out_shape=jax.ShapeDtypeStruct((n, d), F32),
        scratch_shapes=[pltpu.VMEM((tm, d), BF16), pltpu.VMEM((tm, d), F32)],
        compiler_params=_params("parallel", "arbitrary"),
    )(x2, p3, g_mlp, w_up, w_down, g_ple, w_pg, w_pp)


def _tile_gain(g, reps, scale=1.0):
    return jnp.tile(g.astype(F32), (1, reps)) * scale


def kernel(x, p, positions, attn_norm, w_in, a_q_norm, a_k_norm, a_lambda, a_subln,
           c_q_norm, c_k_norm, idx_k_norm, w_br_a, w_br_b, w_br_c, w_out,
           mlp_norm, w_up, w_down, ple_norm, w_ple_gate, w_ple_proj):
    b, s, d = x.shape
    depth = w_in.shape[0]
    n = b * s
    topk = min(TOPK_MAX, s // 4)
    t = min(ATTN_TILE, s)
    scale = HEAD_DIM ** -0.5
    qscale = scale * LOG2E

    inv = ROPE_THETA ** (-jnp.arange(0, HEAD_DIM, 2, dtype=F32) / HEAD_DIM)
    ang = positions.astype(F32)[..., None] * inv
    cos, sin = jnp.cos(ang), jnp.sin(ang)
    cos_t = jnp.concatenate([cos, cos, cos, cos], axis=-1).reshape(n, LANES)
    sin_t = jnp.concatenate([-sin, sin, -sin, sin], axis=-1).reshape(n, LANES)

    qa, ka, va = 0, 512, 1024
    qb, kb, vb = 1536, 2048, 2560
    qc, kc, vc = 3072, 3584, 4096
    qi = 4608
    gl = 4932
    col = lambda a, width=GROUP: w_in[:, :, a:a + width]
    w_qk = jnp.concatenate([col(qa), col(ka), col(qc), col(kc)], axis=-1).astype(BF16)
    w_bqk = jnp.concatenate([col(qb) * qscale, col(kb)], axis=-1).astype(BF16)
    w_vt = jnp.concatenate([col(va), col(vb), col(vc)], axis=-1).astype(BF16)
    w_idx = jnp.pad(col(qi, gl - qi), ((0, 0), (0, 0), (0, IDX_COLS - (gl - qi)))).astype(BF16)
    w_gl = col(gl, N_BRANCHES * d).astype(BF16)
    heads = GROUP // HEAD_DIM
    gain_qk = jnp.concatenate([_tile_gain(a_q_norm, heads, qscale), _tile_gain(a_k_norm, heads),
                               _tile_gain(c_q_norm, heads, qscale), _tile_gain(c_k_norm, heads)],
                              axis=-1)[:, None, :]
    gain_idx = jnp.pad(idx_k_norm.astype(F32), ((0, 0), (0, LANES - IDX_DIM)))[:, None, :]
    scale_idx = jnp.concatenate([jnp.full((IDX_Q,), IDX_DIM ** -0.5, F32), jnp.ones((IDX_DIM,), F32),
                                 jnp.full((IDX_HEADS,), IDX_HEADS ** -0.5, F32),
                                 jnp.zeros((LANES - IDX_DIM - IDX_HEADS,), F32)])[None, :]
    gid = jnp.arange(GROUP) // HEAD_DIM
    gmat = (gid[:, None] == gid[None, :]).astype(BF16)
    kt = min(KEY_TILE, s)
    ar = jnp.arange(kt)
    later_mat = (ar[None, :] > ar[:, None]).astype(BF16)
    before_mat = (ar[None, :] < ar[:, None]).astype(BF16)

    wa, wb, wc, wo = (w.astype(BF16) for w in (w_br_a, w_br_b, w_br_c, w_out))
    wu, wd, wpg, wpp = (w.astype(BF16) for w in (w_up, w_down, w_ple_gate, w_ple_proj))
    g_attn, g_mlp, g_ple = (g.astype(F32)[:, None, :] for g in (attn_norm, mlp_norm, ple_norm))
    subln = a_subln.astype(F32)[:, None, :]
    lam_p = a_lambda.astype(F32)
    p3 = p.reshape(depth, n, p.shape[-1])

    x2 = x.reshape(n, d)
    for i in range(depth):
        lam_init = 0.8 - 0.6 * math.exp(-0.3 * i)
        h = _norm_call(x2, g_attn, i)
        qk = _proj_qk_call(h, w_qk, gain_qk, cos_t, sin_t, gmat, i).reshape(b, s, -1)
        bqk = _proj_plain_call(h, w_bqk, i).reshape(b, s, -1)
        vt = _proj_vt_call(h, w_vt, i, kt).reshape(b, s // kt, -1, kt)
        idx = _proj_idx_call(h, w_idx, gain_idx, scale_idx, cos_t, sin_t, i).reshape(b, s, -1)
        ya = _attn_a_call(qk, vt, lam_p, subln, i, lam_init, t)
        yb = _attn_b_call(bqk, vt, later_mat, t)
        bias = _dsa_mask_call(idx, before_mat, topk, t)
        yc = _attn_c_call(qk, vt, bias)
        x2 = _merge_call(x2, h, ya.reshape(n, -1), yb.reshape(n, -1), yc.reshape(n, -1),
                         w_gl, wa, wb, wc, wo, i)
        x2 = _mlp_call(x2, p3, g_mlp, wu, wd, g_ple, wpg, wpp, i)
    return x2.reshape(b, s, d)
```

```python
import functools
import math

import jax
import jax.numpy as jnp
from jax import lax
from jax.experimental import pallas as pl
from jax.experimental.pallas import tpu as pltpu

F32 = jnp.float32
BF16 = jnp.bfloat16

HEAD_DIM = 64
CHUNK = 64
A_HEADS = 4
B_HEADS = 8
C_HEADS = 8
IDX_HEADS = 4
IDX_DIM = 64
TOPK_MAX = 256
N_BRANCHES = 3
ROPE_THETA = 10000.0
EPS = 1e-6

LANES = 128
SUBLANES = 8
HALF_HEAD = HEAD_DIM // 2
NEG = -1e30
INT_MIN = -(2 ** 31)
LOG2E = 1.4426950408889634
VMEM_LIMIT = 56 * 1024 * 1024

ROW_TILE = 512
ATTN_TILE = 256
KEY_TILE = 128
FF_TILE = 512
GROUP = 512


def _params(*sem):
    return pltpu.CompilerParams(dimension_semantics=sem, vmem_limit_bytes=VMEM_LIMIT)


def _dot(a, b):
    return jnp.dot(a, b, preferred_element_type=F32)


def _split(x):
    hi = x.astype(BF16)
    return hi, (x - hi.astype(F32)).astype(BF16)


def _swap_halves(y):
    lane = lax.broadcasted_iota(jnp.int32, y.shape, 1)
    return jnp.where((lane & HALF_HEAD) != 0, pltpu.roll(y, HALF_HEAD, 1),
                     pltpu.roll(y, LANES - HALF_HEAD, 1))


def _rope_slab(y, cos, sin):
    return y * cos + _swap_halves(y) * sin


def _norm_kernel(x_ref, g_ref, o_ref):
    x = x_ref[...]
    ms = jnp.mean(x * x, axis=-1, keepdims=True)
    o_ref[...] = (x * lax.rsqrt(ms + EPS) * g_ref[0]).astype(o_ref.dtype)


def _norm_call(x2, gains, layer):
    n, d = x2.shape
    return pl.pallas_call(
        _norm_kernel,
        grid=(n // ROW_TILE,),
        in_specs=[pl.BlockSpec((ROW_TILE, d), lambda i: (i, 0)),
                  pl.BlockSpec((1, 1, d), lambda i: (layer, 0, 0))],
        out_specs=pl.BlockSpec((ROW_TILE, d), lambda i: (i, 0)),
        out_shape=jax.ShapeDtypeStruct((n, d), BF16),
        compiler_params=_params("parallel"),
    )(x2, gains)


def _proj_qk_kernel(h_ref, w_ref, gain_ref, cos_ref, sin_ref, gmat_ref, o_ref):
    t = _dot(h_ref[...], w_ref[0])
    hi, lo = _split(t * t)
    gmat = gmat_ref[...]
    ms = (_dot(hi, gmat) + _dot(lo, gmat)) * (1.0 / HEAD_DIM)
    y = t * lax.rsqrt(ms + EPS) * gain_ref[0]
    cos = cos_ref[...]
    sin = sin_ref[...]
    for s in range(y.shape[1] // LANES):
        sl = slice(s * LANES, (s + 1) * LANES)
        o_ref[:, sl] = _rope_slab(y[:, sl], cos, sin).astype(o_ref.dtype)


def _proj_qk_call(h, w, gain, cos_t, sin_t, gmat, layer):
    n, d = h.shape
    cols = w.shape[2]
    tn = gmat.shape[0]
    return pl.pallas_call(
        _proj_qk_kernel,
        grid=(n // ROW_TILE, cols // tn),
        in_specs=[pl.BlockSpec((ROW_TILE, d), lambda i, j: (i, 0)),
                  pl.BlockSpec((1, d, tn), lambda i, j: (layer, 0, j)),
                  pl.BlockSpec((1, 1, tn), lambda i, j: (layer, 0, j)),
                  pl.BlockSpec((ROW_TILE, LANES), lambda i, j: (i, 0)),
                  pl.BlockSpec((ROW_TILE, LANES), lambda i, j: (i, 0)),
                  pl.BlockSpec((tn, tn), lambda i, j: (0, 0))],
        out_specs=pl.BlockSpec((ROW_TILE, tn), lambda i, j: (i, j)),
        out_shape=jax.ShapeDtypeStruct((n, cols), BF16),
        compiler_params=_params("parallel", "arbitrary"),
    )(h, w, gain, cos_t, sin_t, gmat)


def _proj_plain_kernel(h_ref, w_ref, o_ref):
    o_ref[...] = _dot(h_ref[...], w_ref[0]).astype(o_ref.dtype)


def _proj_plain_call(h, w, layer):
    n, d = h.shape
    cols = w.shape[2]
    return pl.pallas_call(
        _proj_plain_kernel,
        grid=(n // ROW_TILE, cols // GROUP),
        in_specs=[pl.BlockSpec((ROW_TILE, d), lambda i, j: (i, 0)),
                  pl.BlockSpec((1, d, GROUP), lambda i, j: (layer, 0, j))],
        out_specs=pl.BlockSpec((ROW_TILE, GROUP), lambda i, j: (i, j)),
        out_shape=jax.ShapeDtypeStruct((n, cols), BF16),
        compiler_params=_params("parallel", "arbitrary"),
    )(h, w)


def _proj_vt_kernel(h_ref, w_ref, o_ref):
    t = _dot(h_ref[...], w_ref[0])
    tt = o_ref.shape[2]
    for r in range(o_ref.shape[0]):
        o_ref[r] = t[r * tt:(r + 1) * tt, :].T.astype(o_ref.dtype)


def _proj_vt_call(h, w, layer, tt):
    n, d = h.shape
    cols = w.shape[2]
    return pl.pallas_call(
        _proj_vt_kernel,
        grid=(n // ROW_TILE, cols // GROUP),
        in_specs=[pl.BlockSpec((ROW_TILE, d), lambda i, j: (i, 0)),
                  pl.BlockSpec((1, d, GROUP), lambda i, j: (layer, 0, j))],
        out_specs=pl.BlockSpec((ROW_TILE // tt, GROUP, tt), lambda i, j: (i, j, 0)),
        out_shape=jax.ShapeDtypeStruct((n // tt, cols, tt), BF16),
        compiler_params=_params("parallel", "arbitrary"),
    )(h, w)


IDX_Q = IDX_HEADS * IDX_DIM
IDX_COLS = IDX_Q + LANES


def _proj_idx_kernel(h_ref, w_ref, gain_ref, scale_ref, cos_ref, sin_ref, o_ref):
    t = _dot(h_ref[...], w_ref[0])
    cos = cos_ref[...]
    sin = sin_ref[...]
    scale = scale_ref[...]
    for s in range(IDX_Q // LANES):
        sl = slice(s * LANES, (s + 1) * LANES)
        o_ref[:, sl] = _rope_slab(t[:, sl], cos, sin) * scale[:, sl]
    sl = slice(IDX_Q, IDX_COLS)
    tk = t[:, sl]
    lane = lax.broadcasted_iota(jnp.int32, tk.shape, 1)
    is_k = lane < IDX_DIM
    ms = jnp.sum(jnp.where(is_k, tk * tk, 0.0), axis=-1, keepdims=True) * (1.0 / IDX_DIM)
    yk = tk * lax.rsqrt(ms + EPS) * gain_ref[0]
    o_ref[:, sl] = jnp.where(is_k, _rope_slab(yk, cos, sin), tk) * scale[:, sl]


def _proj_idx_call(h, w, gain, scale, cos_t, sin_t, layer):
    n, d = h.shape
    return pl.pallas_call(
        _proj_idx_kernel,
        grid=(n // ROW_TILE,),
        in_specs=[pl.BlockSpec((ROW_TILE, d), lambda i: (i, 0)),
                  pl.BlockSpec((1, d, IDX_COLS), lambda i: (layer, 0, 0)),
                  pl.BlockSpec((1, 1, LANES), lambda i: (layer, 0, 0)),
                  pl.BlockSpec((1, IDX_COLS), lambda i: (0, 0)),
                  pl.BlockSpec((ROW_TILE, LANES), lambda i: (i, 0)),
                  pl.BlockSpec((ROW_TILE, LANES), lambda i: (i, 0))],
        out_specs=pl.BlockSpec((ROW_TILE, IDX_COLS), lambda i: (i, 0)),
        out_shape=jax.ShapeDtypeStruct((n, IDX_COLS), F32),
        compiler_params=_params("parallel"),
    )(h, w, gain, scale, cos_t, sin_t)


def _load_qt(q_ref, qz_ref):
    qt = q_ref[0].astype(F32).T
    rows = lax.broadcasted_iota(jnp.int32, (LANES, qt.shape[1]), 0)
    for s in range(qt.shape[0] // LANES):
        slab = qt[s * LANES:(s + 1) * LANES, :]
        qz_ref[2 * s] = jnp.where(rows < HEAD_DIM, slab, 0.0).astype(BF16)
        qz_ref[2 * s + 1] = jnp.where(rows >= HEAD_DIM, slab, 0.0).astype(BF16)


def _softmax_step(n, s, vt, m_ref, l_ref, acc_ref):
    m_old = m_ref[n]
    m_new = jnp.maximum(m_old, jnp.max(s, axis=0, keepdims=True))
    p = jnp.exp2(s - m_new)
    alpha = jnp.exp2(m_old - m_new)
    l_ref[n] = alpha * l_ref[n] + jnp.sum(p, axis=0, keepdims=True)
    acc_ref[n] = alpha * acc_ref[n] + _dot(vt, p.astype(BF16))
    m_ref[n] = m_new


def _tile_ids(t):
    key = lax.broadcasted_iota(jnp.int32, (KEY_TILE, t), 0)
    qry = lax.broadcasted_iota(jnp.int32, (KEY_TILE, t), 1)
    return key, qry


def _key_rows(u):
    return pl.ds(pl.multiple_of(u * KEY_TILE, KEY_TILE), KEY_TILE)


def _attn_a_kernel(lam_ref, q_ref, k_ref, vt_ref, g_ref, o_ref, qz_ref, m_ref, l_ref, acc_ref, *,
                   lam_init):
    t = q_ref.shape[1]
    i = pl.program_id(1)
    lp = lam_ref[0]
    lam = (jnp.exp(jnp.sum(lp[0:1] * lp[1:2], axis=-1, keepdims=True))
           - jnp.exp(jnp.sum(lp[2:3] * lp[3:4], axis=-1, keepdims=True)) + lam_init)
    key, qry = _tile_ids(t)
    sub = t // KEY_TILE
    _load_qt(q_ref, qz_ref)
    m_ref[...] = jnp.full(m_ref.shape, NEG, F32)
    l_ref[...] = jnp.zeros(l_ref.shape, F32)
    acc_ref[...] = jnp.zeros(acc_ref.shape, F32)

    def group(g, diag):
        scores = []
        for r in range(sub):
            for n in range(2 * A_HEADS):
                kb = k_ref[0, _key_rows(g * sub + r), (n // 2) * LANES:(n // 2 + 1) * LANES]
                scores.append(_dot(kb, qz_ref[n]))
        for r in range(sub):
            for n in range(2 * A_HEADS):
                s = scores[r * 2 * A_HEADS + n]
                if diag:
                    s = jnp.where(((key + r * KEY_TILE) // CHUNK) <= (qry // CHUNK), s, NEG)
                vt = vt_ref[0, g * sub + r, (n // 2) * LANES:(n // 2 + 1) * LANES, :]
                _softmax_step(n, s, vt, m_ref, l_ref, acc_ref)

    def body(g, carry):
        group(g, False)
        return carry

    lax.fori_loop(0, i, body, 0)
    group(i, True)
    gain = g_ref[0] * (1.0 - lam_init)
    for h in range(A_HEADS):
        o = acc_ref[2 * h] / l_ref[2 * h] - lam * (acc_ref[2 * h + 1] / l_ref[2 * h + 1])
        ms = jnp.mean(o * o, axis=0, keepdims=True)
        y = (o * lax.rsqrt(ms + EPS)).T * gain
        o_ref[0, :, h * LANES:(h + 1) * LANES] = y.astype(o_ref.dtype)


def _attn_a_call(qk, vt, lam_p, subln, layer, lam_init, t):
    b, s, _ = qk.shape
    nt, _, kt = vt.shape[1:]
    ns = 2 * A_HEADS
    return pl.pallas_call(
        functools.partial(_attn_a_kernel, lam_init=lam_init),
        grid=(b, s // t),
        in_specs=[pl.BlockSpec((1, 4, HEAD_DIM), lambda bi, i: (layer, 0, 0)),
                  pl.BlockSpec((1, t, GROUP), lambda bi, i: (bi, i, 0)),
                  pl.BlockSpec((1, s, GROUP), lambda bi, i: (bi, 0, 1)),
                  pl.BlockSpec((1, nt, GROUP, kt), lambda bi, i: (bi, 0, 0, 0)),
                  pl.BlockSpec((1, 1, LANES), lambda bi, i: (layer, 0, 0))],
        out_specs=pl.BlockSpec((1, t, GROUP), lambda bi, i: (bi, i, 0)),
        out_shape=jax.ShapeDtypeStruct((b, s, GROUP), BF16),
        scratch_shapes=[pltpu.VMEM((ns, LANES, t), BF16), pltpu.VMEM((ns, 1, t), F32),
                        pltpu.VMEM((ns, 1, t), F32), pltpu.VMEM((ns, LANES, t), F32)],
        compiler_params=_params("parallel", "arbitrary"),
    )(lam_p, qk, qk, vt, subln)


def _attn_b_kernel(q_ref, k_ref, vt_ref, lmat_ref, o_ref, qz_ref, r_ref, acc_ref):
    t = q_ref.shape[1]
    i = pl.program_id(1)
    key, qry = _tile_ids(t)
    sub = t // KEY_TILE
    lmat = lmat_ref[...]
    _load_qt(q_ref, qz_ref)
    r_ref[...] = jnp.zeros(r_ref.shape, F32)
    acc_ref[...] = jnp.zeros(acc_ref.shape, F32)

    def group(g, diag):
        order = [(r, n) for r in reversed(range(sub)) for n in range(B_HEADS)]
        strict = {r: (key + r * KEY_TILE) < qry for r in range(sub)} if diag else None
        logits = {}
        for r, n in order:
            kb = k_ref[0, _key_rows(g * sub + r), (n // 2) * LANES:(n // 2 + 1) * LANES]
            logits[r, n] = _dot(kb, qz_ref[n])
        log_betas, laters, sums = {}, {}, {}
        for r, n in order:
            z = logits[r, n]
            soft = jnp.log2(1.0 + jnp.exp2(jnp.abs(z) * -1.0))
            log_beta = jnp.minimum(z, 0.0) - soft
            log_om = log_beta - z
            if diag:
                log_om = jnp.where(strict[r], log_om, 0.0)
            laters[r, n] = _dot(lmat, log_om.astype(BF16))
            log_betas[r, n] = log_beta
            sums[r, n] = jnp.sum(log_om, axis=0, keepdims=True)
        for r, n in order:
            a = jnp.exp2(log_betas[r, n] + laters[r, n])
            if diag:
                a = jnp.where(strict[r], a, 0.0)
            vt = vt_ref[0, g * sub + r, n * HEAD_DIM:(n + 1) * HEAD_DIM, :]
            visited = r_ref[n]
            acc_ref[n] = acc_ref[n] + _dot(vt, a.astype(BF16)) * jnp.exp2(visited)
            r_ref[n] = visited + sums[r, n]

    def body(n, carry):
        group(i - 1 - n, False)
        return carry

    group(i, True)
    lax.fori_loop(0, i, body, 0)
    o_ref[0] = acc_ref[...].reshape(B_HEADS * HEAD_DIM, t).T.astype(o_ref.dtype)


def _attn_b_call(bqk, vt, lmat, t):
    b, s, _ = bqk.shape
    nt, _, kt = vt.shape[1:]
    return pl.pallas_call(
        _attn_b_kernel,
        grid=(b, s // t),
        in_specs=[pl.BlockSpec((1, t, GROUP), lambda bi, i: (bi, i, 0)),
                  pl.BlockSpec((1, s, GROUP), lambda bi, i: (bi, 0, 1)),
                  pl.BlockSpec((1, nt, GROUP, kt), lambda bi, i: (bi, 0, 1, 0)),
                  pl.BlockSpec((kt, kt), lambda bi, i: (0, 0))],
        out_specs=pl.BlockSpec((1, t, GROUP), lambda bi, i: (bi, i, 0)),
        out_shape=jax.ShapeDtypeStruct((b, s, GROUP), BF16),
        scratch_shapes=[pltpu.VMEM((B_HEADS, LANES, t), BF16), pltpu.VMEM((B_HEADS, 1, t), F32),
                        pltpu.VMEM((B_HEADS, HEAD_DIM, t), F32)],
        compiler_params=_params("parallel", "arbitrary"),
    )(bqk, bqk, vt, lmat)


def _fold_rows(x):
    parts = [x[r * SUBLANES:(r + 1) * SUBLANES] for r in range(x.shape[0] // SUBLANES)]
    while len(parts) > 1:
        parts = [a + b for a, b in zip(parts[0::2], parts[1::2])]
    return parts[0]


def _dsa_mask_kernel(q_ref, kw_ref, wq_ref, lt_ref, o_ref, key_ref, qz_ref, *, topk):
    t = q_ref.shape[1]
    nj = key_ref.shape[0]
    i = pl.program_id(1)
    nch = (i + 1) * (t // KEY_TILE)
    qt = q_ref[0].T
    pad = jnp.zeros((LANES - IDX_DIM, t), F32)
    for h in range(IDX_HEADS):
        qz_ref[h] = jnp.concatenate([qt[h * IDX_DIM:(h + 1) * IDX_DIM], pad], axis=0).astype(BF16)
    wt = wq_ref[0].T
    wh = [wt[IDX_DIM + h:IDX_DIM + h + 1, :] for h in range(IDX_HEADS)]
    key0, qry0 = _tile_ids(t)
    sub = t // KEY_TILE

    def chunk_ok(r):
        return ((key0 + r * KEY_TILE) // CHUNK) <= (qry0 // CHUNK)

    def fill(g, diag):
        for r in range(sub):
            c = g * sub + r
            kb = kw_ref[0, _key_rows(c), :].astype(BF16)
            score = jnp.zeros((KEY_TILE, t), F32)
            for h in range(IDX_HEADS):
                score = score + wh[h] * jnp.maximum(_dot(kb, qz_ref[h]), 0.0)
            if diag:
                score = jnp.where(chunk_ok(r), score, -jnp.inf)
            bits = pltpu.bitcast(score, jnp.int32)
            okey = jnp.where(bits < 0, bits ^ jnp.int32(0x7FFFFFFF), bits)
            key_ref[c] = jnp.where(score == 0.0, 0, okey)

    def fill_body(g, carry):
        fill(g, False)
        return carry

    lax.fori_loop(0, i, fill_body, 0)
    fill(i, True)

    def count(pred, thr):
        def body(g, acc):
            for r in range(sub):
                acc = acc + _fold_rows(jnp.where(pred(key_ref[g * sub + r], thr), 1.0, 0.0))
            return acc
        acc = lax.fori_loop(0, i + 1, body, jnp.zeros((SUBLANES, t), F32))
        return jnp.sum(acc, axis=0, keepdims=True)

    def bit_body(n, lo):
        cand = lo + jnp.left_shift(jnp.int32(1), 31 - n)
        cnt = count(lambda k, th: k >= th, cand)
        return jnp.where(cnt >= topk, cand, lo)

    thr = lax.fori_loop(0, 32, bit_body, jnp.full((1, t), INT_MIN, jnp.int32))
    need = topk - count(lambda k, th: k > th, thr)
    lt = lt_ref[...]

    def emit(g, run, diag):
        for r in range(sub):
            c = g * sub + r
            okey = key_ref[c]
            eq = okey == thr
            eqf = jnp.where(eq, 1.0, 0.0)
            before = _dot(lt, eqf.astype(BF16)) + run
            sel = (okey > thr) | (eq & (before < need))
            if diag:
                sel = sel & chunk_ok(r)
            o_ref[0, 0, c] = jnp.where(sel, 0.0, NEG).astype(o_ref.dtype)
            run = run + jnp.sum(eqf, axis=0, keepdims=True)
        return run

    run = lax.fori_loop(0, i, lambda g, run: emit(g, run, False), jnp.zeros((1, t), F32))
    emit(i, run, True)

    def blank(c, carry):
        o_ref[0, 0, c] = jnp.full((KEY_TILE, t), NEG, o_ref.dtype)
        return carry

    lax.fori_loop(nch, nj, blank, 0)


def _dsa_mask_call(idx, lt, topk, t):
    b, s, _ = idx.shape
    kt = lt.shape[0]
    assert t >= topk
    nt, nk = s // t, s // kt
    return pl.pallas_call(
        functools.partial(_dsa_mask_kernel, topk=topk),
        grid=(b, nt),
        in_specs=[pl.BlockSpec((1, t, IDX_Q), lambda bi, i: (bi, i, 0)),
                  pl.BlockSpec((1, s, LANES), lambda bi, i: (bi, 0, IDX_Q // LANES)),
                  pl.BlockSpec((1, t, LANES), lambda bi, i: (bi, i, IDX_Q // LANES)),
                  pl.BlockSpec((kt, kt), lambda bi, i: (0, 0))],
        out_specs=pl.BlockSpec((1, 1, nk, kt, t), lambda bi, i: (bi, i, 0, 0, 0)),
        out_shape=jax.ShapeDtypeStruct((b, nt, nk, kt, t), BF16),
        scratch_shapes=[pltpu.VMEM((nk, kt, t), jnp.int32), pltpu.VMEM((IDX_HEADS, LANES, t), BF16)],
        compiler_params=_params("parallel", "arbitrary"),
    )(idx, idx, idx, lt)


def _attn_c_kernel(q_ref, k_ref, vt_ref, b_ref, o_ref, qz_ref, m_ref, l_ref, acc_ref):
    t = q_ref.shape[1]
    i = pl.program_id(1)
    _load_qt(q_ref, qz_ref)
    m_ref[...] = jnp.full(m_ref.shape, NEG, F32)
    l_ref[...] = jnp.zeros(l_ref.shape, F32)
    acc_ref[...] = jnp.zeros(acc_ref.shape, F32)

    key, qry = _tile_ids(KEY_TILE)
    ident = jnp.where(key == qry, 1.0, 0.0).astype(BF16)

    sub = t // KEY_TILE

    def body(g, carry):
        scores = []
        for r in range(sub):
            u = g * sub + r
            bias = b_ref[0, 0, u]
            for n in range(C_HEADS):
                kb = k_ref[0, _key_rows(u), (n // 2) * LANES:(n // 2 + 1) * LANES]
                lhs = jnp.concatenate([kb, ident], axis=1)
                rhs = jnp.concatenate([qz_ref[n], bias], axis=0)
                scores.append(_dot(lhs, rhs))
        for r in range(sub):
            for n in range(C_HEADS):
                vt = vt_ref[0, g * sub + r, n * HEAD_DIM:(n + 1) * HEAD_DIM, :]
                _softmax_step(n, scores[r * C_HEADS + n], vt, m_ref, l_ref, acc_ref)
        return carry

    lax.fori_loop(0, i + 1, body, 0)
    o = acc_ref[...] / l_ref[...]
    o_ref[0] = o.reshape(C_HEADS * HEAD_DIM, t).T.astype(o_ref.dtype)


def _attn_c_call(qk, vt, bias):
    b, s, _ = qk.shape
    nk, _, kt = vt.shape[1:]
    nt, t = bias.shape[1], bias.shape[4]
    return pl.pallas_call(
        _attn_c_kernel,
        grid=(b, nt),
        in_specs=[pl.BlockSpec((1, t, GROUP), lambda bi, i: (bi, i, 2)),
                  pl.BlockSpec((1, s, GROUP), lambda bi, i: (bi, 0, 3)),
                  pl.BlockSpec((1, nk, GROUP, kt), lambda bi, i: (bi, 0, 2, 0)),
                  pl.BlockSpec((1, 1, nk, kt, t), lambda bi, i: (bi, i, 0, 0, 0))],
        out_specs=pl.BlockSpec((1, t, GROUP), lambda bi, i: (bi, i, 0)),
        out_shape=jax.ShapeDtypeStruct((b, s, GROUP), BF16),
        scratch_shapes=[pltpu.VMEM((C_HEADS, LANES, t), BF16), pltpu.VMEM((C_HEADS, 1, t), F32),
                        pltpu.VMEM((C_HEADS, 1, t), F32), pltpu.VMEM((C_HEADS, HEAD_DIM, t), F32)],
        compiler_params=_params("parallel", "arbitrary"),
    )(qk, qk, vt, bias)


def _merge_kernel(x_ref, h_ref, ya_ref, yb_ref, yc_ref, wg_ref, wa_ref, wb_ref, wc_ref, wo_ref, o_ref):
    d = x_ref.shape[1]
    h = h_ref[...]
    merged = jnp.zeros(x_ref.shape, F32)
    for n, (y_ref, w_ref) in enumerate(((ya_ref, wa_ref), (yb_ref, wb_ref), (yc_ref, wc_ref))):
        gate = jax.nn.sigmoid(_dot(h, wg_ref[0, :, n * d:(n + 1) * d]))
        merged = merged + gate * _dot(y_ref[...], w_ref[0])
    o_ref[...] = x_ref[...] + _dot(merged.astype(BF16), wo_ref[0])


def _merge_call(x2, h, ya, yb, yc, wg, wa, wb, wc, wo, layer):
    n, d = x2.shape
    tm = ROW_TILE
    row = lambda cols: pl.BlockSpec((tm, cols), lambda i: (i, 0))
    full = lambda w: pl.BlockSpec((1,) + w.shape[1:], lambda i: (layer, 0, 0))
    return pl.pallas_call(
        _merge_kernel,
        grid=(n // tm,),
        in_specs=[row(d), row(d), row(ya.shape[1]), row(yb.shape[1]), row(yc.shape[1]),
                  full(wg), full(wa), full(wb), full(wc), full(wo)],
        out_specs=row(d),
        out_shape=jax.ShapeDtypeStruct((n, d), F32),
        compiler_params=_params("parallel"),
    )(x2, h, ya, yb, yc, wg, wa, wb, wc, wo)


def _mlp_kernel(x_ref, p_ref, gm_ref, wu_ref, wd_ref, gp_ref, wpg_ref, wpp_ref, o_ref, h_ref, acc_ref):
    f = pl.program_id(1)

    @pl.when(f == 0)
    def _():
        x = x_ref[...]
        ms = jnp.mean(x * x, axis=-1, keepdims=True)
        h_ref[...] = (x * lax.rsqrt(ms + EPS) * gm_ref[0]).astype(BF16)
        acc_ref[...] = jnp.zeros(acc_ref.shape, F32)

    u = jnp.maximum(_dot(h_ref[...], wu_ref[0]), 0.0)
    acc_ref[...] += _dot((u * u).astype(BF16), wd_ref[0])

    @pl.when(f == pl.num_programs(1) - 1)
    def _():
        x = x_ref[...] + acc_ref[...]
        ms = jnp.mean(x * x, axis=-1, keepdims=True)
        hp = (x * lax.rsqrt(ms + EPS) * gp_ref[0]).astype(BF16)
        gate = jax.nn.sigmoid(_dot(hp, wpg_ref[0]))
        o_ref[...] = x + gate * _dot(p_ref[0].astype(BF16), wpp_ref[0])


def _mlp_call(x2, p3, g_mlp, w_up, w_down, g_ple, w_pg, w_pp, layer):
    n, d = x2.shape
    ff = w_up.shape[2]
    pd = p3.shape[2]
    tm, tf = ROW_TILE, FF_TILE
    return pl.pallas_call(
        _mlp_kernel,
        grid=(n // tm, ff // tf),
        in_specs=[pl.BlockSpec((tm, d), lambda i, f: (i, 0)),
                  pl.BlockSpec((1, tm, pd), lambda i, f: (layer, i, 0)),
                  pl.BlockSpec((1, 1, d), lambda i, f: (layer, 0, 0)),
                  pl.BlockSpec((1, d, tf), lambda i, f: (layer, 0, f)),
                  pl.BlockSpec((1, tf, d), lambda i, f: (layer, f, 0)),
                  pl.BlockSpec((1, 1, d), lambda i, f: (layer, 0, 0)),
                  pl.BlockSpec((1, d, d), lambda i, f: (layer, 0, 0)),
                  pl.BlockSpec((1, pd, d), lambda i, f: (layer, 0, 0))],
        out_specs=pl.BlockSpec((tm, d), lambda i, f: (i, 0)),
        out_shape=jax.ShapeDtypeStruct((n, d), F32),
        scratch_shapes=[pltpu.VMEM((tm, d), BF16), pltpu.VMEM((tm, d), F32)],
        compiler_params=_params("parallel", "arbitrary"),
    )(x2, p3, g_mlp, w_up, w_down, g_ple, w_pg, w_pp)


def _tile_gain(g, reps, scale=1.0):
    return jnp.tile(g.astype(F32), (1, reps)) * scale


def kernel(x, p, positions, attn_norm, w_in, a_q_norm, a_k_norm, a_lambda, a_subln,
           c_q_norm, c_k_norm, idx_k_norm, w_br_a, w_br_b, w_br_c, w_out,
           mlp_norm, w_up, w_down, ple_norm, w_ple_gate, w_ple_proj):
    b, s, d = x.shape
    depth = w_in.shape[0]
    n = b * s
    topk = min(TOPK_MAX, s // 4)
    t = min(ATTN_TILE, s)
    scale = HEAD_DIM ** -0.5
    qscale = scale * LOG2E

    inv = ROPE_THETA ** (-jnp.arange(0, HEAD_DIM, 2, dtype=F32) / HEAD_DIM)
    ang = positions.astype(F32)[..., None] * inv
    cos, sin = jnp.cos(ang), jnp.sin(ang)
    cos_t = jnp.concatenate([cos, cos, cos, cos], axis=-1).reshape(n, LANES)
    sin_t = jnp.concatenate([-sin, sin, -sin, sin], axis=-1).reshape(n, LANES)

    qa, ka, va = 0, 512, 1024
    qb, kb, vb = 1536, 2048, 2560
    qc, kc, vc = 3072, 3584, 4096
    qi = 4608
    gl = 4932
    col = lambda a, width=GROUP: w_in[:, :, a:a + width]
    w_qk = jnp.concatenate([col(qa), col(ka), col(qc), col(kc)], axis=-1).astype(BF16)
    w_bqk = jnp.concatenate([col(qb) * qscale, col(kb)], axis=-1).astype(BF16)
    w_vt = jnp.concatenate([col(va), col(vb), col(vc)], axis=-1).astype(BF16)
    w_idx = jnp.pad(col(qi, gl - qi), ((0, 0), (0, 0), (0, IDX_COLS - (gl - qi)))).astype(BF16)
    w_gl = col(gl, N_BRANCHES * d).astype(BF16)
    heads = GROUP // HEAD_DIM
    gain_qk = jnp.concatenate([_tile_gain(a_q_norm, heads, qscale), _tile_gain(a_k_norm, heads),
                               _tile_gain(c_q_norm, heads, qscale), _tile_gain(c_k_norm, heads)],
                              axis=-1)[:, None, :]
    gain_idx = jnp.pad(idx_k_norm.astype(F32), ((0, 0), (0, LANES - IDX_DIM)))[:, None, :]
    scale_idx = jnp.concatenate([jnp.full((IDX_Q,), IDX_DIM ** -0.5, F32), jnp.ones((IDX_DIM,), F32),
                                 jnp.full((IDX_HEADS,), IDX_HEADS ** -0.5, F32),
                                 jnp.zeros((LANES - IDX_DIM - IDX_HEADS,), F32)])[None, :]
    gid = jnp.arange(GROUP) // HEAD_DIM
    gmat = (gid[:, None] == gid[None, :]).astype(BF16)
    kt = min(KEY_TILE, s)
    ar = jnp.arange(kt)
    later_mat = (ar[None, :] > ar[:, None]).astype(BF16)
    before_mat = (ar[None, :] < ar[:, None]).astype(BF16)

    wa, wb, wc, wo = (w.astype(BF16) for w in (w_br_a, w_br_b, w_br_c, w_out))
    wu, wd, wpg, wpp = (w.astype(BF16) for w in (w_up, w_down, w_ple_gate, w_ple_proj))
    g_attn, g_mlp, g_ple = (g.astype(F32)[:, None, :] for g in (attn_norm, mlp_norm, ple_norm))
    subln = a_subln.astype(F32)[:, None, :]
    lam_p = a_lambda.astype(F32)
    p3 = p.reshape(depth, n, p.shape[-1])

    x2 = x.reshape(n, d)
    for i in range(depth):
        lam_init = 0.8 - 0.6 * math.exp(-0.3 * i)
        h = _norm_call(x2, g_attn, i)
        qk = _proj_qk_call(h, w_qk, gain_qk, cos_t, sin_t, gmat, i).reshape(b, s, -1)
        bqk = _proj_plain_call(h, w_bqk, i).reshape(b, s, -1)
        vt = _proj_vt_call(h, w_vt, i, kt).reshape(b, s // kt, -1, kt)
        idx = _proj_idx_call(h, w_idx, gain_idx, scale_idx, cos_t, sin_t, i).reshape(b, s, -1)
        ya = _attn_a_call(qk, vt, lam_p, subln, i, lam_init, t)
        yb = _attn_b_call(bqk, vt, later_mat, t)
        bias = _dsa_mask_call(idx, before_mat, topk, t)
        yc = _attn_c_call(qk, vt, bias)
        x2 = _merge_call(x2, h, ya.reshape(n, -1), yb.reshape(n, -1), yc.reshape(n, -1),
                         w_gl, wa, wb, wc, wo, i)
        x2 = _mlp_call(x2, p3, g_mlp, wu, wd, g_ple, wpg, wpp, i)
    return x2.reshape(b, s, d)
```

```python
import functools
import math

import jax
import jax.numpy as jnp
from jax import lax
from jax.experimental import pallas as pl
from jax.experimental.pallas import tpu as pltpu

F32 = jnp.float32
BF16 = jnp.bfloat16

HEAD_DIM = 64
CHUNK = 64
A_HEADS = 4
B_HEADS = 8
C_HEADS = 8
IDX_HEADS = 4
IDX_DIM = 64
TOPK_MAX = 256
N_BRANCHES = 3
ROPE_THETA = 10000.0
EPS = 1e-6

LANES = 128
SUBLANES = 8
HALF_HEAD = HEAD_DIM // 2
NEG = -1e30
INT_MIN = -(2 ** 31)
LOG2E = 1.4426950408889634
VMEM_LIMIT = 56 * 1024 * 1024

ROW_TILE = 512
ATTN_TILE = 256
KEY_TILE = 128
FF_TILE = 1024
GROUP = 512


def _params(*sem):
    return pltpu.CompilerParams(dimension_semantics=sem, vmem_limit_bytes=VMEM_LIMIT)


def _dot(a, b):
    return jnp.dot(a, b, preferred_element_type=F32)


def _split(x):
    hi = x.astype(BF16)
    return hi, (x - hi.astype(F32)).astype(BF16)


def _swap_halves(y):
    lane = lax.broadcasted_iota(jnp.int32, y.shape, 1)
    return jnp.where((lane & HALF_HEAD) != 0, pltpu.roll(y, HALF_HEAD, 1),
                     pltpu.roll(y, LANES - HALF_HEAD, 1))


def _rope_slab(y, cos, sin):
    return y * cos + _swap_halves(y) * sin


IDX_Q = IDX_HEADS * IDX_DIM
IDX_COLS = IDX_Q + LANES


def _front_kernel(x_ref, g_ref, wqk_ref, wbqk_ref, wvt_ref, widx_ref, gqk_ref, gidx_ref, sidx_ref,
                  cos_ref, sin_ref, gmat_ref, h_ref, qk_ref, bqk_ref, vt_ref, idx_ref):
    x = x_ref[...]
    ms = jnp.mean(x * x, axis=-1, keepdims=True)
    h = (x * lax.rsqrt(ms + EPS) * g_ref[0]).astype(BF16)
    h_ref[...] = h
    cos = cos_ref[...]
    sin = sin_ref[...]

    gmat = gmat_ref[...]
    for j in range(qk_ref.shape[1] // GROUP):
        cols = slice(j * GROUP, (j + 1) * GROUP)
        t = _dot(h, wqk_ref[0, :, cols])
        hi, lo = _split(t * t)
        ms = (_dot(hi, gmat) + _dot(lo, gmat)) * (1.0 / HEAD_DIM)
        y = t * lax.rsqrt(ms + EPS) * gqk_ref[0, :, cols]
        for s in range(GROUP // LANES):
            sl = slice(s * LANES, (s + 1) * LANES)
            qk_ref[:, j * GROUP + s * LANES:j * GROUP + (s + 1) * LANES] = (
                _rope_slab(y[:, sl], cos, sin).astype(qk_ref.dtype))

    for j in range(bqk_ref.shape[1] // GROUP):
        cols = slice(j * GROUP, (j + 1) * GROUP)
        bqk_ref[:, cols] = _dot(h, wbqk_ref[0, :, cols]).astype(bqk_ref.dtype)

    tt = vt_ref.shape[2]
    for j in range(vt_ref.shape[1] // GROUP):
        cols = slice(j * GROUP, (j + 1) * GROUP)
        t = _dot(h, wvt_ref[0, :, cols])
        for r in range(vt_ref.shape[0]):
            vt_ref[r, cols, :] = t[r * tt:(r + 1) * tt, :].T.astype(vt_ref.dtype)

    t = _dot(h, widx_ref[0])
    scale = sidx_ref[...]
    for s in range(IDX_Q // LANES):
        sl = slice(s * LANES, (s + 1) * LANES)
        idx_ref[:, sl] = _rope_slab(t[:, sl], cos, sin) * scale[:, sl]
    sl = slice(IDX_Q, IDX_COLS)
    tk = t[:, sl]
    lane = lax.broadcasted_iota(jnp.int32, tk.shape, 1)
    is_k = lane < IDX_DIM
    ms = jnp.sum(jnp.where(is_k, tk * tk, 0.0), axis=-1, keepdims=True) * (1.0 / IDX_DIM)
    yk = tk * lax.rsqrt(ms + EPS) * gidx_ref[0]
    idx_ref[:, sl] = jnp.where(is_k, _rope_slab(yk, cos, sin), tk) * scale[:, sl]


def _front_call(x2, g_attn, w_qk, w_bqk, w_vt, w_idx, gain_qk, gain_idx, scale_idx, cos_t, sin_t,
                gmat, layer, tt):
    n, d = x2.shape
    tm = ROW_TILE
    row = lambda cols: pl.BlockSpec((tm, cols), lambda i: (i, 0))
    full = lambda w: pl.BlockSpec((1,) + w.shape[1:], lambda i: (layer, 0, 0))
    fixed = lambda a: pl.BlockSpec(a.shape, lambda i: (0, 0))
    cqk, cbqk, cvt = w_qk.shape[2], w_bqk.shape[2], w_vt.shape[2]
    return pl.pallas_call(
        _front_kernel,
        grid=(n // tm,),
        in_specs=[row(d), full(g_attn), full(w_qk), full(w_bqk), full(w_vt), full(w_idx),
                  full(gain_qk), full(gain_idx), fixed(scale_idx), row(LANES), row(LANES),
                  fixed(gmat)],
        out_specs=[row(d), row(cqk), row(cbqk),
                   pl.BlockSpec((tm // tt, cvt, tt), lambda i: (i, 0, 0)), row(IDX_COLS)],
        out_shape=[jax.ShapeDtypeStruct((n, d), BF16), jax.ShapeDtypeStruct((n, cqk), BF16),
                   jax.ShapeDtypeStruct((n, cbqk), BF16),
                   jax.ShapeDtypeStruct((n // tt, cvt, tt), BF16),
                   jax.ShapeDtypeStruct((n, IDX_COLS), F32)],
        compiler_params=_params("parallel"),
    )(x2, g_attn, w_qk, w_bqk, w_vt, w_idx, gain_qk, gain_idx, scale_idx, cos_t, sin_t, gmat)


def _load_qt(q_ref, qz_ref):
    qt = q_ref[0].astype(F32).T
    rows = lax.broadcasted_iota(jnp.int32, (LANES, qt.shape[1]), 0)
    for s in range(qt.shape[0] // LANES):
        slab = qt[s * LANES:(s + 1) * LANES, :]
        qz_ref[2 * s] = jnp.where(rows < HEAD_DIM, slab, 0.0).astype(BF16)
        qz_ref[2 * s + 1] = jnp.where(rows >= HEAD_DIM, slab, 0.0).astype(BF16)


def _softmax_step(n, s, vt, m_ref, l_ref, acc_ref):
    m_old = m_ref[n]
    m_new = jnp.maximum(m_old, jnp.max(s, axis=0, keepdims=True))
    p = jnp.exp2(s - m_new)
    alpha = jnp.exp2(m_old - m_new)
    l_ref[n] = alpha * l_ref[n] + jnp.sum(p, axis=0, keepdims=True)
    acc_ref[n] = alpha * acc_ref[n] + _dot(vt, p.astype(BF16))
    m_ref[n] = m_new


def _tile_ids(t):
    key = lax.broadcasted_iota(jnp.int32, (KEY_TILE, t), 0)
    qry = lax.broadcasted_iota(jnp.int32, (KEY_TILE, t), 1)
    return key, qry


def _key_rows(u):
    return pl.ds(pl.multiple_of(u * KEY_TILE, KEY_TILE), KEY_TILE)


def _attn_a_kernel(lam_ref, q_ref, k_ref, vt_ref, g_ref, o_ref, qz_ref, m_ref, l_ref, acc_ref, *,
                   lam_init):
    t = q_ref.shape[1]
    i = pl.program_id(1)
    lp = lam_ref[0]
    lam = (jnp.exp(jnp.sum(lp[0:1] * lp[1:2], axis=-1, keepdims=True))
           - jnp.exp(jnp.sum(lp[2:3] * lp[3:4], axis=-1, keepdims=True)) + lam_init)
    key, qry = _tile_ids(t)
    sub = t // KEY_TILE
    _load_qt(q_ref, qz_ref)
    m_ref[...] = jnp.full(m_ref.shape, NEG, F32)
    l_ref[...] = jnp.zeros(l_ref.shape, F32)
    acc_ref[...] = jnp.zeros(acc_ref.shape, F32)

    def group(g, diag):
        scores = []
        for r in range(sub):
            for n in range(2 * A_HEADS):
                kb = k_ref[0, _key_rows(g * sub + r), (n // 2) * LANES:(n // 2 + 1) * LANES]
                scores.append(_dot(kb, qz_ref[n]))
        for r in range(sub):
            for n in range(2 * A_HEADS):
                s = scores[r * 2 * A_HEADS + n]
                if diag:
                    s = jnp.where(((key + r * KEY_TILE) // CHUNK) <= (qry // CHUNK), s, NEG)
                vt = vt_ref[0, g * sub + r, (n // 2) * LANES:(n // 2 + 1) * LANES, :]
                _softmax_step(n, s, vt, m_ref, l_ref, acc_ref)

    def body(g, carry):
        group(g, False)
        return carry

    lax.fori_loop(0, i, body, 0)
    group(i, True)
    gain = g_ref[0] * (1.0 - lam_init)
    for h in range(A_HEADS):
        o = acc_ref[2 * h] / l_ref[2 * h] - lam * (acc_ref[2 * h + 1] / l_ref[2 * h + 1])
        ms = jnp.mean(o * o, axis=0, keepdims=True)
        y = (o * lax.rsqrt(ms + EPS)).T * gain
        o_ref[0, :, h * LANES:(h + 1) * LANES] = y.astype(o_ref.dtype)


def _attn_a_call(qk, vt, lam_p, subln, layer, lam_init, t):
    b, s, _ = qk.shape
    nt, _, kt = vt.shape[1:]
    ns = 2 * A_HEADS
    return pl.pallas_call(
        functools.partial(_attn_a_kernel, lam_init=lam_init),
        grid=(b, s // t),
        in_specs=[pl.BlockSpec((1, 4, HEAD_DIM), lambda bi, i: (layer, 0, 0)),
                  pl.BlockSpec((1, t, GROUP), lambda bi, i: (bi, i, 0)),
                  pl.BlockSpec((1, s, GROUP), lambda bi, i: (bi, 0, 1)),
                  pl.BlockSpec((1, nt, GROUP, kt), lambda bi, i: (bi, 0, 0, 0)),
                  pl.BlockSpec((1, 1, LANES), lambda bi, i: (layer, 0, 0))],
        out_specs=pl.BlockSpec((1, t, GROUP), lambda bi, i: (bi, i, 0)),
        out_shape=jax.ShapeDtypeStruct((b, s, GROUP), BF16),
        scratch_shapes=[pltpu.VMEM((ns, LANES, t), BF16), pltpu.VMEM((ns, 1, t), F32),
                        pltpu.VMEM((ns, 1, t), F32), pltpu.VMEM((ns, LANES, t), F32)],
        compiler_params=_params("parallel", "arbitrary"),
    )(lam_p, qk, qk, vt, subln)


def _attn_b_kernel(q_ref, k_ref, vt_ref, lmat_ref, o_ref, qz_ref, r_ref, acc_ref):
    t = q_ref.shape[1]
    i = pl.program_id(1)
    key, qry = _tile_ids(t)
    sub = t // KEY_TILE
    lmat = lmat_ref[...]
    _load_qt(q_ref, qz_ref)
    r_ref[...] = jnp.zeros(r_ref.shape, F32)
    acc_ref[...] = jnp.zeros(acc_ref.shape, F32)

    def group(g, diag):
        order = [(r, n) for r in reversed(range(sub)) for n in range(B_HEADS)]
        strict = {r: (key + r * KEY_TILE) < qry for r in range(sub)} if diag else None
        logits = {}
        for r, n in order:
            kb = k_ref[0, _key_rows(g * sub + r), (n // 2) * LANES:(n // 2 + 1) * LANES]
            logits[r, n] = _dot(kb, qz_ref[n])
        log_betas, laters, sums = {}, {}, {}
        for r, n in order:
            z = logits[r, n]
            soft = jnp.log2(1.0 + jnp.exp2(jnp.abs(z) * -1.0))
            log_beta = jnp.minimum(z, 0.0) - soft
            log_om = log_beta - z
            if diag:
                log_om = jnp.where(strict[r], log_om, 0.0)
            laters[r, n] = _dot(lmat, log_om.astype(BF16))
            log_betas[r, n] = log_beta
            sums[r, n] = jnp.sum(log_om, axis=0, keepdims=True)
        for r, n in order:
            a = jnp.exp2(log_betas[r, n] + laters[r, n])
            if diag:
                a = jnp.where(strict[r], a, 0.0)
            vt = vt_ref[0, g * sub + r, n * HEAD_DIM:(n + 1) * HEAD_DIM, :]
            visited = r_ref[n]
            acc_ref[n] = acc_ref[n] + _dot(vt, a.astype(BF16)) * jnp.exp2(visited)
            r_ref[n] = visited + sums[r, n]

    def body(n, carry):
        group(i - 1 - n, False)
        return carry

    group(i, True)
    lax.fori_loop(0, i, body, 0)
    o_ref[0] = acc_ref[...].reshape(B_HEADS * HEAD_DIM, t).T.astype(o_ref.dtype)


def _attn_b_call(bqk, vt, lmat, t):
    b, s, _ = bqk.shape
    nt, _, kt = vt.shape[1:]
    return pl.pallas_call(
        _attn_b_kernel,
        grid=(b, s // t),
        in_specs=[pl.BlockSpec((1, t, GROUP), lambda bi, i: (bi, i, 0)),
                  pl.BlockSpec((1, s, GROUP), lambda bi, i: (bi, 0, 1)),
                  pl.BlockSpec((1, nt, GROUP, kt), lambda bi, i: (bi, 0, 1, 0)),
                  pl.BlockSpec((kt, kt), lambda bi, i: (0, 0))],
        out_specs=pl.BlockSpec((1, t, GROUP), lambda bi, i: (bi, i, 0)),
        out_shape=jax.ShapeDtypeStruct((b, s, GROUP), BF16),
        scratch_shapes=[pltpu.VMEM((B_HEADS, LANES, t), BF16), pltpu.VMEM((B_HEADS, 1, t), F32),
                        pltpu.VMEM((B_HEADS, HEAD_DIM, t), F32)],
        compiler_params=_params("parallel", "arbitrary"),
    )(bqk, bqk, vt, lmat)


def _fold_rows(x):
    parts = [x[r * SUBLANES:(r + 1) * SUBLANES] for r in range(x.shape[0] // SUBLANES)]
    while len(parts) > 1:
        parts = [a + b for a, b in zip(parts[0::2], parts[1::2])]
    return parts[0]


def _dsa_mask_kernel(q_ref, kw_ref, wq_ref, lt_ref, o_ref, key_ref, qz_ref, *, topk):
    t = q_ref.shape[1]
    nj = key_ref.shape[0]
    i = pl.program_id(1)
    nch = (i + 1) * (t // KEY_TILE)
    qt = q_ref[0].T
    pad = jnp.zeros((LANES - IDX_DIM, t), F32)
    for h in range(IDX_HEADS):
        qz_ref[h] = jnp.concatenate([qt[h * IDX_DIM:(h + 1) * IDX_DIM], pad], axis=0).astype(BF16)
    wt = wq_ref[0].T
    wh = [wt[IDX_DIM + h:IDX_DIM + h + 1, :] for h in range(IDX_HEADS)]
    key0, qry0 = _tile_ids(t)
    sub = t // KEY_TILE

    def chunk_ok(r):
        return ((key0 + r * KEY_TILE) // CHUNK) <= (qry0 // CHUNK)

    def fill(g, diag):
        for r in range(sub):
            c = g * sub + r
            kb = kw_ref[0, _key_rows(c), :].astype(BF16)
            score = jnp.zeros((KEY_TILE, t), F32)
            for h in range(IDX_HEADS):
                score = score + wh[h] * jnp.maximum(_dot(kb, qz_ref[h]), 0.0)
            if diag:
                score = jnp.where(chunk_ok(r), score, -jnp.inf)
            bits = pltpu.bitcast(score, jnp.int32)
            okey = jnp.where(bits < 0, bits ^ jnp.int32(0x7FFFFFFF), bits)
            key_ref[c] = jnp.where(score == 0.0, 0, okey)

    def fill_body(g, carry):
        fill(g, False)
        return carry

    lax.fori_loop(0, i, fill_body, 0)
    fill(i, True)

    def count(pred, thr):
        def body(g, acc):
            for r in range(sub):
                acc = acc + _fold_rows(jnp.where(pred(key_ref[g * sub + r], thr), 1.0, 0.0))
            return acc
        acc = lax.fori_loop(0, i + 1, body, jnp.zeros((SUBLANES, t), F32))
        return jnp.sum(acc, axis=0, keepdims=True)

    def bit_body(n, lo):
        cand = lo + jnp.left_shift(jnp.int32(1), 31 - n)
        cnt = count(lambda k, th: k >= th, cand)
        return jnp.where(cnt >= topk, cand, lo)

    thr = lax.fori_loop(0, 32, bit_body, jnp.full((1, t), INT_MIN, jnp.int32))
    need = topk - count(lambda k, th: k > th, thr)
    lt = lt_ref[...]

    def emit(g, run, diag):
        for r in range(sub):
            c = g * sub + r
            okey = key_ref[c]
            eq = okey == thr
            eqf = jnp.where(eq, 1.0, 0.0)
            before = _dot(lt, eqf.astype(BF16)) + run
            sel = (okey > thr) | (eq & (before < need))
            if diag:
                sel = sel & chunk_ok(r)
            o_ref[0, 0, c] = jnp.where(sel, 0.0, NEG).astype(o_ref.dtype)
            run = run + jnp.sum(eqf, axis=0, keepdims=True)
        return run

    run = lax.fori_loop(0, i, lambda g, run: emit(g, run, False), jnp.zeros((1, t), F32))
    emit(i, run, True)

    def blank(c, carry):
        o_ref[0, 0, c] = jnp.full((KEY_TILE, t), NEG, o_ref.dtype)
        return carry

    lax.fori_loop(nch, nj, blank, 0)


def _dsa_mask_call(idx, lt, topk, t):
    b, s, _ = idx.shape
    kt = lt.shape[0]
    assert t >= topk
    nt, nk = s // t, s // kt
    return pl.pallas_call(
        functools.partial(_dsa_mask_kernel, topk=topk),
        grid=(b, nt),
        in_specs=[pl.BlockSpec((1, t, IDX_Q), lambda bi, i: (bi, i, 0)),
                  pl.BlockSpec((1, s, LANES), lambda bi, i: (bi, 0, IDX_Q // LANES)),
                  pl.BlockSpec((1, t, LANES), lambda bi, i: (bi, i, IDX_Q // LANES)),
                  pl.BlockSpec((kt, kt), lambda bi, i: (0, 0))],
        out_specs=pl.BlockSpec((1, 1, nk, kt, t), lambda bi, i: (bi, i, 0, 0, 0)),
        out_shape=jax.ShapeDtypeStruct((b, nt, nk, kt, t), BF16),
        scratch_shapes=[pltpu.VMEM((nk, kt, t), jnp.int32), pltpu.VMEM((IDX_HEADS, LANES, t), BF16)],
        compiler_params=_params("parallel", "arbitrary"),
    )(idx, idx, idx, lt)


def _attn_c_kernel(q_ref, k_ref, vt_ref, b_ref, o_ref, qz_ref, m_ref, l_ref, acc_ref):
    t = q_ref.shape[1]
    i = pl.program_id(1)
    _load_qt(q_ref, qz_ref)
    m_ref[...] = jnp.full(m_ref.shape, NEG, F32)
    l_ref[...] = jnp.zeros(l_ref.shape, F32)
    acc_ref[...] = jnp.zeros(acc_ref.shape, F32)

    key, qry = _tile_ids(KEY_TILE)
    ident = jnp.where(key == qry, 1.0, 0.0).astype(BF16)

    sub = t // KEY_TILE

    def body(g, carry):
        scores = []
        for r in range(sub):
            u = g * sub + r
            bias = b_ref[0, 0, u]
            for n in range(C_HEADS):
                kb = k_ref[0, _key_rows(u), (n // 2) * LANES:(n // 2 + 1) * LANES]
                lhs = jnp.concatenate([kb, ident], axis=1)
                rhs = jnp.concatenate([qz_ref[n], bias], axis=0)
                scores.append(_dot(lhs, rhs))
        for r in range(sub):
            for n in range(C_HEADS):
                vt = vt_ref[0, g * sub + r, n * HEAD_DIM:(n + 1) * HEAD_DIM, :]
                _softmax_step(n, scores[r * C_HEADS + n], vt, m_ref, l_ref, acc_ref)
        return carry

    lax.fori_loop(0, i + 1, body, 0)
    o = acc_ref[...] / l_ref[...]
    o_ref[0] = o.reshape(C_HEADS * HEAD_DIM, t).T.astype(o_ref.dtype)


def _attn_c_call(qk, vt, bias):
    b, s, _ = qk.shape
    nk, _, kt = vt.shape[1:]
    nt, t = bias.shape[1], bias.shape[4]
    return pl.pallas_call(
        _attn_c_kernel,
        grid=(b, nt),
        in_specs=[pl.BlockSpec((1, t, GROUP), lambda bi, i: (bi, i, 2)),
                  pl.BlockSpec((1, s, GROUP), lambda bi, i: (bi, 0, 3)),
                  pl.BlockSpec((1, nk, GROUP, kt), lambda bi, i: (bi, 0, 2, 0)),
                  pl.BlockSpec((1, 1, nk, kt, t), lambda bi, i: (bi, i, 0, 0, 0))],
        out_specs=pl.BlockSpec((1, t, GROUP), lambda bi, i: (bi, i, 0)),
        out_shape=jax.ShapeDtypeStruct((b, s, GROUP), BF16),
        scratch_shapes=[pltpu.VMEM((C_HEADS, LANES, t), BF16), pltpu.VMEM((C_HEADS, 1, t), F32),
                        pltpu.VMEM((C_HEADS, 1, t), F32), pltpu.VMEM((C_HEADS, HEAD_DIM, t), F32)],
        compiler_params=_params("parallel", "arbitrary"),
    )(qk, qk, vt, bias)


def _merge_kernel(x_ref, h_ref, ya_ref, yb_ref, yc_ref, wg_ref, wa_ref, wb_ref, wc_ref, wo_ref, o_ref):
    d = x_ref.shape[1]
    h = h_ref[...]
    merged = jnp.zeros(x_ref.shape, F32)
    for n, (y_ref, w_ref) in enumerate(((ya_ref, wa_ref), (yb_ref, wb_ref), (yc_ref, wc_ref))):
        gate = jax.nn.sigmoid(_dot(h, wg_ref[0, :, n * d:(n + 1) * d]))
        merged = merged + gate * _dot(y_ref[...], w_ref[0])
    o_ref[...] = x_ref[...] + _dot(merged.astype(BF16), wo_ref[0])


def _merge_call(x2, h, ya, yb, yc, wg, wa, wb, wc, wo, layer):
    n, d = x2.shape
    tm = ROW_TILE
    row = lambda cols: pl.BlockSpec((tm, cols), lambda i: (i, 0))
    full = lambda w: pl.BlockSpec((1,) + w.shape[1:], lambda i: (layer, 0, 0))
    return pl.pallas_call(
        _merge_kernel,
        grid=(n // tm,),
        in_specs=[row(d), row(d), row(ya.shape[1]), row(yb.shape[1]), row(yc.shape[1]),
                  full(wg), full(wa), full(wb), full(wc), full(wo)],
        out_specs=row(d),
        out_shape=jax.ShapeDtypeStruct((n, d), F32),
        compiler_params=_params("parallel"),
    )(x2, h, ya, yb, yc, wg, wa, wb, wc, wo)


def _mlp_kernel(x_ref, p_ref, gm_ref, wu_ref, wd_ref, gp_ref, wpg_ref, wpp_ref, o_ref):
    x = x_ref[...]
    ms = jnp.mean(x * x, axis=-1, keepdims=True)
    h = (x * lax.rsqrt(ms + EPS) * gm_ref[0]).astype(BF16)
    for f in range(wu_ref.shape[2] // FF_TILE):
        u = jnp.maximum(_dot(h, wu_ref[0, :, f * FF_TILE:(f + 1) * FF_TILE]), 0.0)
        x = x + _dot((u * u).astype(BF16), wd_ref[0, f * FF_TILE:(f + 1) * FF_TILE, :])
    ms = jnp.mean(x * x, axis=-1, keepdims=True)
    hp = (x * lax.rsqrt(ms + EPS) * gp_ref[0]).astype(BF16)
    gate = jax.nn.sigmoid(_dot(hp, wpg_ref[0]))
    o_ref[...] = x + gate * _dot(p_ref[0].astype(BF16), wpp_ref[0])


def _mlp_call(x2, p3, g_mlp, w_up, w_down, g_ple, w_pg, w_pp, layer):
    n, d = x2.shape
    pd = p3.shape[2]
    tm = ROW_TILE
    full = lambda w: pl.BlockSpec((1,) + w.shape[1:], lambda i: (layer, 0, 0),
                                  pipeline_mode=pl.Buffered(1))
    return pl.pallas_call(
        _mlp_kernel,
        grid=(n // tm,),
        in_specs=[pl.BlockSpec((tm, d), lambda i: (i, 0)),
                  pl.BlockSpec((1, tm, pd), lambda i: (layer, i, 0)),
                  full(g_mlp), full(w_up), full(w_down), full(g_ple), full(w_pg), full(w_pp)],
        out_specs=pl.BlockSpec((tm, d), lambda i: (i, 0)),
        out_shape=jax.ShapeDtypeStruct((n, d), F32),
        compiler_params=_params("parallel"),
    )(x2, p3, g_mlp, w_up, w_down, g_ple, w_pg, w_pp)


def _tile_gain(g, reps, scale=1.0):
    return jnp.tile(g.astype(F32), (1, reps)) * scale


def kernel(x, p, positions, attn_norm, w_in, a_q_norm, a_k_norm, a_lambda, a_subln,
           c_q_norm, c_k_norm, idx_k_norm, w_br_a, w_br_b, w_br_c, w_out,
           mlp_norm, w_up, w_down, ple_norm, w_ple_gate, w_ple_proj):
    b, s, d = x.shape
    depth = w_in.shape[0]
    n = b * s
    topk = min(TOPK_MAX, s // 4)
    t = min(ATTN_TILE, s)
    scale = HEAD_DIM ** -0.5
    qscale = scale * LOG2E

    inv = ROPE_THETA ** (-jnp.arange(0, HEAD_DIM, 2, dtype=F32) / HEAD_DIM)
    ang = positions.astype(F32)[..., None] * inv
    cos, sin = jnp.cos(ang), jnp.sin(ang)
    cos_t = jnp.concatenate([cos, cos, cos, cos], axis=-1).reshape(n, LANES)
    sin_t = jnp.concatenate([-sin, sin, -sin, sin], axis=-1).reshape(n, LANES)

    qa, ka, va = 0, 512, 1024
    qb, kb, vb = 1536, 2048, 2560
    qc, kc, vc = 3072, 3584, 4096
    qi = 4608
    gl = 4932
    col = lambda a, width=GROUP: w_in[:, :, a:a + width]
    w_qk = jnp.concatenate([col(qa), col(ka), col(qc), col(kc)], axis=-1).astype(BF16)
    w_bqk = jnp.concatenate([col(qb) * qscale, col(kb)], axis=-1).astype(BF16)
    w_vt = jnp.concatenate([col(va), col(vb), col(vc)], axis=-1).astype(BF16)
    w_idx = jnp.pad(col(qi, gl - qi), ((0, 0), (0, 0), (0, IDX_COLS - (gl - qi)))).astype(BF16)
    w_gl = col(gl, N_BRANCHES * d).astype(BF16)
    heads = GROUP // HEAD_DIM
    gain_qk = jnp.concatenate([_tile_gain(a_q_norm, heads, qscale), _tile_gain(a_k_norm, heads),
                               _tile_gain(c_q_norm, heads, qscale), _tile_gain(c_k_norm, heads)],
                              axis=-1)[:, None, :]
    gain_idx = jnp.pad(idx_k_norm.astype(F32), ((0, 0), (0, LANES - IDX_DIM)))[:, None, :]
    scale_idx = jnp.concatenate([jnp.full((IDX_Q,), IDX_DIM ** -0.5, F32), jnp.ones((IDX_DIM,), F32),
                                 jnp.full((IDX_HEADS,), IDX_HEADS ** -0.5, F32),
                                 jnp.zeros((LANES - IDX_DIM - IDX_HEADS,), F32)])[None, :]
    gid = jnp.arange(GROUP) // HEAD_DIM
    gmat = (gid[:, None] == gid[None, :]).astype(BF16)
    kt = min(KEY_TILE, s)
    ar = jnp.arange(kt)
    later_mat = (ar[None, :] > ar[:, None]).astype(BF16)
    before_mat = (ar[None, :] < ar[:, None]).astype(BF16)

    wa, wb, wc, wo = (w.astype(BF16) for w in (w_br_a, w_br_b, w_br_c, w_out))
    wu, wd, wpg, wpp = (w.astype(BF16) for w in (w_up, w_down, w_ple_gate, w_ple_proj))
    g_attn, g_mlp, g_ple = (g.astype(F32)[:, None, :] for g in (attn_norm, mlp_norm, ple_norm))
    subln = a_subln.astype(F32)[:, None, :]
    lam_p = a_lambda.astype(F32)
    p3 = p.reshape(depth, n, p.shape[-1])

    x2 = x.reshape(n, d)
    for i in range(depth):
        lam_init = 0.8 - 0.6 * math.exp(-0.3 * i)
        h, qk, bqk, vt, idx = _front_call(x2, g_attn, w_qk, w_bqk, w_vt, w_idx, gain_qk, gain_idx,
                                          scale_idx, cos_t, sin_t, gmat, i, kt)
        qk, bqk, idx = (a.reshape(b, s, -1) for a in (qk, bqk, idx))
        vt = vt.reshape(b, s // kt, -1, kt)
        ya = _attn_a_call(qk, vt, lam_p, subln, i, lam_init, t)
        yb = _attn_b_call(bqk, vt, later_mat, t)
        bias = _dsa_mask_call(idx, before_mat, topk, t)
        yc = _attn_c_call(qk, vt, bias)
        x2 = _merge_call(x2, h, ya.reshape(n, -1), yb.reshape(n, -1), yc.reshape(n, -1),
                         w_gl, wa, wb, wc, wo, i)
        x2 = _mlp_call(x2, p3, g_mlp, wu, wd, g_ple, wpg, wpp, i)
    return x2.reshape(b, s, d)
```

```python
import functools
import math

import jax
import jax.numpy as jnp
from jax import lax
from jax.experimental import pallas as pl
from jax.experimental.pallas import tpu as pltpu

F32 = jnp.float32
BF16 = jnp.bfloat16

HEAD_DIM = 64
CHUNK = 64
A_HEADS = 4
B_HEADS = 8
C_HEADS = 8
IDX_HEADS = 4
IDX_DIM = 64
TOPK_MAX = 256
N_BRANCHES = 3
ROPE_THETA = 10000.0
EPS = 1e-6

LANES = 128
SUBLANES = 8
HALF_HEAD = HEAD_DIM // 2
NEG = -1e30
INT_MIN = -(2 ** 31)
LOG2E = 1.4426950408889634
VMEM_LIMIT = 56 * 1024 * 1024

ROW_TILE = 512
ATTN_TILE = 256
KEY_TILE = 128
FF_TILE = 1024
GROUP = 512


def _params(*sem):
    return pltpu.CompilerParams(dimension_semantics=sem, vmem_limit_bytes=VMEM_LIMIT)


def _dot(a, b):
    return jnp.dot(a, b, preferred_element_type=F32)


def _split(x):
    hi = x.astype(BF16)
    return hi, (x - hi.astype(F32)).astype(BF16)


def _swap_halves(y):
    lane = lax.broadcasted_iota(jnp.int32, y.shape, 1)
    return jnp.where((lane & HALF_HEAD) != 0, pltpu.roll(y, HALF_HEAD, 1),
                     pltpu.roll(y, LANES - HALF_HEAD, 1))


def _rope_slab(y, cos, sin):
    return y * cos + _swap_halves(y) * sin


IDX_Q = IDX_HEADS * IDX_DIM
IDX_COLS = IDX_Q + LANES


def _front_kernel(x_ref, g_ref, wqk_ref, wbqk_ref, wvt_ref, widx_ref, gqk_ref, gidx_ref, sidx_ref,
                  cos_ref, sin_ref, gmat_ref, h_ref, qk_ref, bqk_ref, vt_ref, idx_ref):
    x = x_ref[...]
    ms = jnp.mean(x * x, axis=-1, keepdims=True)
    h = (x * lax.rsqrt(ms + EPS) * g_ref[0]).astype(BF16)
    h_ref[...] = h
    cos = cos_ref[...]
    sin = sin_ref[...]

    gmat = gmat_ref[...]
    for j in range(qk_ref.shape[1] // GROUP):
        cols = slice(j * GROUP, (j + 1) * GROUP)
        t = _dot(h, wqk_ref[0, :, cols])
        hi, lo = _split(t * t)
        ms = (_dot(hi, gmat) + _dot(lo, gmat)) * (1.0 / HEAD_DIM)
        y = t * lax.rsqrt(ms + EPS) * gqk_ref[0, :, cols]
        for s in range(GROUP // LANES):
            sl = slice(s * LANES, (s + 1) * LANES)
            qk_ref[:, j * GROUP + s * LANES:j * GROUP + (s + 1) * LANES] = (
                _rope_slab(y[:, sl], cos, sin).astype(qk_ref.dtype))

    for j in range(bqk_ref.shape[1] // GROUP):
        cols = slice(j * GROUP, (j + 1) * GROUP)
        bqk_ref[:, cols] = _dot(h, wbqk_ref[0, :, cols]).astype(bqk_ref.dtype)

    tt = vt_ref.shape[2]
    for j in range(vt_ref.shape[1] // GROUP):
        cols = slice(j * GROUP, (j + 1) * GROUP)
        t = _dot(h, wvt_ref[0, :, cols])
        for r in range(vt_ref.shape[0]):
            vt_ref[r, cols, :] = t[r * tt:(r + 1) * tt, :].T.astype(vt_ref.dtype)

    t = _dot(h, widx_ref[0])
    scale = sidx_ref[...]
    for s in range(IDX_Q // LANES):
        sl = slice(s * LANES, (s + 1) * LANES)
        idx_ref[:, sl] = _rope_slab(t[:, sl], cos, sin) * scale[:, sl]
    sl = slice(IDX_Q, IDX_COLS)
    tk = t[:, sl]
    lane = lax.broadcasted_iota(jnp.int32, tk.shape, 1)
    is_k = lane < IDX_DIM
    ms = jnp.sum(jnp.where(is_k, tk * tk, 0.0), axis=-1, keepdims=True) * (1.0 / IDX_DIM)
    yk = tk * lax.rsqrt(ms + EPS) * gidx_ref[0]
    idx_ref[:, sl] = jnp.where(is_k, _rope_slab(yk, cos, sin), tk) * scale[:, sl]


def _front_call(x2, g_attn, w_qk, w_bqk, w_vt, w_idx, gain_qk, gain_idx, scale_idx, cos_t, sin_t,
                gmat, layer, tt):
    n, d = x2.shape
    tm = ROW_TILE
    row = lambda cols: pl.BlockSpec((tm, cols), lambda i: (i, 0))
    full = lambda w: pl.BlockSpec((1,) + w.shape[1:], lambda i: (layer, 0, 0))
    fixed = lambda a: pl.BlockSpec(a.shape, lambda i: (0, 0))
    cqk, cbqk, cvt = w_qk.shape[2], w_bqk.shape[2], w_vt.shape[2]
    return pl.pallas_call(
        _front_kernel,
        grid=(n // tm,),
        in_specs=[row(d), full(g_attn), full(w_qk), full(w_bqk), full(w_vt), full(w_idx),
                  full(gain_qk), full(gain_idx), fixed(scale_idx), row(LANES), row(LANES),
                  fixed(gmat)],
        out_specs=[row(d), row(cqk), row(cbqk),
                   pl.BlockSpec((tm // tt, cvt, tt), lambda i: (i, 0, 0)), row(IDX_COLS)],
        out_shape=[jax.ShapeDtypeStruct((n, d), BF16), jax.ShapeDtypeStruct((n, cqk), BF16),
                   jax.ShapeDtypeStruct((n, cbqk), BF16),
                   jax.ShapeDtypeStruct((n // tt, cvt, tt), BF16),
                   jax.ShapeDtypeStruct((n, IDX_COLS), F32)],
        compiler_params=_params("parallel"),
    )(x2, g_attn, w_qk, w_bqk, w_vt, w_idx, gain_qk, gain_idx, scale_idx, cos_t, sin_t, gmat)


def _load_qt(q_ref, qz_ref):
    qt = q_ref[0].astype(F32).T
    rows = lax.broadcasted_iota(jnp.int32, (LANES, qt.shape[1]), 0)
    for s in range(qt.shape[0] // LANES):
        slab = qt[s * LANES:(s + 1) * LANES, :]
        qz_ref[2 * s] = jnp.where(rows < HEAD_DIM, slab, 0.0).astype(BF16)
        qz_ref[2 * s + 1] = jnp.where(rows >= HEAD_DIM, slab, 0.0).astype(BF16)


ONES_ROWS = 16


def _softmax_step(n, s, vt, m_ref, l_ref, acc_ref):
    m_old = m_ref[n]
    m_new = jnp.maximum(m_old, jnp.max(s, axis=0, keepdims=True))
    p = jnp.exp2(s - m_new).astype(BF16)
    alpha = jnp.exp2(m_old - m_new)
    f = vt.shape[0]
    pv = _dot(jnp.concatenate([vt, jnp.ones((ONES_ROWS, vt.shape[1]), BF16)], axis=0), p)
    l_ref[n] = alpha * l_ref[n] + pv[f:f + 1]
    acc_ref[n] = alpha * acc_ref[n] + pv[:f]
    m_ref[n] = m_new


def _tile_ids(t):
    key = lax.broadcasted_iota(jnp.int32, (KEY_TILE, t), 0)
    qry = lax.broadcasted_iota(jnp.int32, (KEY_TILE, t), 1)
    return key, qry


def _key_rows(u):
    return pl.ds(pl.multiple_of(u * KEY_TILE, KEY_TILE), KEY_TILE)


def _interleave(*gens):
    gens = list(gens)
    while gens:
        for gen in list(gens):
            try:
                next(gen)
            except StopIteration:
                gens.remove(gen)


def _attn_ab_kernel(lam_ref, aq_ref, ak_ref, avt_ref, g_ref, bq_ref, bk_ref, bvt_ref, lmat_ref,
                    ya_ref, yb_ref, aqz_ref, m_ref, l_ref, aacc_ref, bqz_ref, r_ref, bacc_ref, *,
                    lam_init):
    t = aq_ref.shape[1]
    i = pl.program_id(1)
    lp = lam_ref[0]
    lam = (jnp.exp(jnp.sum(lp[0:1] * lp[1:2], axis=-1, keepdims=True))
           - jnp.exp(jnp.sum(lp[2:3] * lp[3:4], axis=-1, keepdims=True)) + lam_init)
    key, qry = _tile_ids(t)
    sub = t // KEY_TILE
    lmat = lmat_ref[...]
    _load_qt(aq_ref, aqz_ref)
    _load_qt(bq_ref, bqz_ref)
    m_ref[...] = jnp.full(m_ref.shape, NEG, F32)
    l_ref[...] = jnp.zeros(l_ref.shape, F32)
    aacc_ref[...] = jnp.zeros(aacc_ref.shape, F32)
    r_ref[...] = jnp.zeros(r_ref.shape, F32)
    bacc_ref[...] = jnp.zeros(bacc_ref.shape, F32)

    def a_group(g, diag):
        scores = []
        for r in range(sub):
            for n in range(2 * A_HEADS):
                kb = ak_ref[0, _key_rows(g * sub + r), (n // 2) * LANES:(n // 2 + 1) * LANES]
                scores.append(_dot(kb, aqz_ref[n]))
        yield
        for r in range(sub):
            for n in range(2 * A_HEADS):
                s = scores[r * 2 * A_HEADS + n]
                if diag:
                    s = jnp.where(((key + r * KEY_TILE) // CHUNK) <= (qry // CHUNK), s, NEG)
                vt = avt_ref[0, g * sub + r, (n // 2) * LANES:(n // 2 + 1) * LANES, :]
                _softmax_step(n, s, vt, m_ref, l_ref, aacc_ref)
                yield

    def b_group(g, diag):
        order = [(r, n) for r in reversed(range(sub)) for n in range(B_HEADS)]
        strict = {r: (key + r * KEY_TILE) < qry for r in range(sub)} if diag else None
        logits = {}
        for r, n in order:
            kb = bk_ref[0, _key_rows(g * sub + r), (n // 2) * LANES:(n // 2 + 1) * LANES]
            logits[r, n] = _dot(kb, bqz_ref[n])
        yield
        log_betas, laters, sums = {}, {}, {}
        for r, n in order:
            z = logits[r, n]
            neg_abs = pltpu.bitcast(pltpu.bitcast(z, jnp.int32) | jnp.int32(INT_MIN), F32)
            soft = jnp.log2(1.0 + jnp.exp2(neg_abs))
            log_beta = jnp.minimum(z, 0.0) - soft
            log_om = log_beta - z
            if diag:
                log_om = jnp.where(strict[r], log_om, 0.0)
            later = _dot(lmat, log_om.astype(BF16))
            laters[r, n] = later[:KEY_TILE]
            log_betas[r, n] = log_beta
            sums[r, n] = later[KEY_TILE:KEY_TILE + 1]
            yield
        for r, n in order:
            a = jnp.exp2(log_betas[r, n] + laters[r, n])
            if diag:
                a = jnp.where(strict[r], a, 0.0)
            vt = bvt_ref[0, g * sub + r, n * HEAD_DIM:(n + 1) * HEAD_DIM, :]
            visited = r_ref[n]
            bacc_ref[n] = bacc_ref[n] + _dot(vt, a.astype(BF16)) * jnp.exp2(visited)
            r_ref[n] = visited + sums[r, n]
            yield

    def body(n, carry):
        _interleave(a_group(n, False), b_group(i - 1 - n, False))
        return carry

    _interleave(b_group(i, True))
    lax.fori_loop(0, i, body, 0)
    _interleave(a_group(i, True))

    gain = g_ref[0] * (1.0 - lam_init)
    for h in range(A_HEADS):
        o = aacc_ref[2 * h] / l_ref[2 * h] - lam * (aacc_ref[2 * h + 1] / l_ref[2 * h + 1])
        ms = jnp.mean(o * o, axis=0, keepdims=True)
        y = (o * lax.rsqrt(ms + EPS)).T * gain
        ya_ref[0, :, h * LANES:(h + 1) * LANES] = y.astype(ya_ref.dtype)
    yb_ref[0] = bacc_ref[...].reshape(B_HEADS * HEAD_DIM, t).T.astype(yb_ref.dtype)


def _attn_ab_call(qk, bqk, vt, lam_p, subln, lmat, layer, lam_init, t):
    b, s, _ = qk.shape
    nt, _, kt = vt.shape[1:]
    ns = 2 * A_HEADS
    qspec = pl.BlockSpec((1, t, GROUP), lambda bi, i: (bi, i, 0))
    kspec = pl.BlockSpec((1, s, GROUP), lambda bi, i: (bi, 0, 1))
    vspec = lambda f: pl.BlockSpec((1, nt, GROUP, kt), lambda bi, i: (bi, 0, f, 0))
    return pl.pallas_call(
        functools.partial(_attn_ab_kernel, lam_init=lam_init),
        grid=(b, s // t),
        in_specs=[pl.BlockSpec((1, 4, HEAD_DIM), lambda bi, i: (layer, 0, 0)),
                  qspec, kspec, vspec(0),
                  pl.BlockSpec((1, 1, LANES), lambda bi, i: (layer, 0, 0)),
                  qspec, kspec, vspec(1),
                  pl.BlockSpec(lmat.shape, lambda bi, i: (0, 0))],
        out_specs=[qspec, qspec],
        out_shape=[jax.ShapeDtypeStruct((b, s, GROUP), BF16)] * 2,
        scratch_shapes=[pltpu.VMEM((ns, LANES, t), BF16), pltpu.VMEM((ns, 1, t), F32),
                        pltpu.VMEM((ns, 1, t), F32), pltpu.VMEM((ns, LANES, t), F32),
                        pltpu.VMEM((B_HEADS, LANES, t), BF16), pltpu.VMEM((B_HEADS, 1, t), F32),
                        pltpu.VMEM((B_HEADS, HEAD_DIM, t), F32)],
        compiler_params=_params("parallel", "arbitrary"),
    )(lam_p, qk, qk, vt, subln, bqk, bqk, vt, lmat)


def _fold_rows(x):
    parts = [x[r * SUBLANES:(r + 1) * SUBLANES] for r in range(x.shape[0] // SUBLANES)]
    while len(parts) > 1:
        parts = [a + b for a, b in zip(parts[0::2], parts[1::2])]
    return parts[0]


def _dsa_mask_kernel(q_ref, kw_ref, wq_ref, lt_ref, o_ref, key_ref, qz_ref, *, topk):
    t = q_ref.shape[1]
    nj = key_ref.shape[0]
    i = pl.program_id(1)
    nch = (i + 1) * (t // KEY_TILE)
    qt = q_ref[0].T
    pad = jnp.zeros((LANES - IDX_DIM, t), F32)
    for h in range(IDX_HEADS):
        qz_ref[h] = jnp.concatenate([qt[h * IDX_DIM:(h + 1) * IDX_DIM], pad], axis=0).astype(BF16)
    wt = wq_ref[0].T
    wh = [wt[IDX_DIM + h:IDX_DIM + h + 1, :] for h in range(IDX_HEADS)]
    key0, qry0 = _tile_ids(t)
    sub = t // KEY_TILE

    def chunk_ok(r):
        return ((key0 + r * KEY_TILE) // CHUNK) <= (qry0 // CHUNK)

    def fill(g, diag):
        for r in range(sub):
            c = g * sub + r
            kb = kw_ref[0, _key_rows(c), :].astype(BF16)
            score = jnp.zeros((KEY_TILE, t), F32)
            for h in range(IDX_HEADS):
                score = score + wh[h] * jnp.maximum(_dot(kb, qz_ref[h]), 0.0)
            if diag:
                score = jnp.where(chunk_ok(r), score, -jnp.inf)
            bits = pltpu.bitcast(score, jnp.int32)
            okey = jnp.where(bits < 0, bits ^ jnp.int32(0x7FFFFFFF), bits)
            key_ref[c] = jnp.where(score == 0.0, 0, okey)

    def fill_body(g, carry):
        fill(g, False)
        return carry

    lax.fori_loop(0, i, fill_body, 0)
    fill(i, True)

    def count(pred, thr):
        def body(g, acc):
            for r in range(sub):
                acc = acc + _fold_rows(jnp.where(pred(key_ref[g * sub + r], thr), 1.0, 0.0))
            return acc
        acc = lax.fori_loop(0, i + 1, body, jnp.zeros((SUBLANES, t), F32))
        return jnp.sum(acc, axis=0, keepdims=True)

    def bit_body(n, lo):
        cand = lo + jnp.left_shift(jnp.int32(1), 31 - n)
        cnt = count(lambda k, th: k >= th, cand)
        return jnp.where(cnt >= topk, cand, lo)

    thr = lax.fori_loop(0, 32, bit_body, jnp.full((1, t), INT_MIN, jnp.int32))
    need = topk - count(lambda k, th: k > th, thr)
    lt = lt_ref[...]

    def emit(g, run, diag):
        for r in range(sub):
            c = g * sub + r
            okey = key_ref[c]
            eq = okey == thr
            eqf = jnp.where(eq, 1.0, 0.0)
            before = _dot(lt, eqf.astype(BF16)) + run
            sel = (okey > thr) | (eq & (before < need))
            if diag:
                sel = sel & chunk_ok(r)
            o_ref[0, 0, c] = jnp.where(sel, 0.0, NEG).astype(o_ref.dtype)
            run = run + jnp.sum(eqf, axis=0, keepdims=True)
        return run

    run = lax.fori_loop(0, i, lambda g, run: emit(g, run, False), jnp.zeros((1, t), F32))
    emit(i, run, True)

    def blank(c, carry):
        o_ref[0, 0, c] = jnp.full((KEY_TILE, t), NEG, o_ref.dtype)
        return carry

    lax.fori_loop(nch, nj, blank, 0)


def _dsa_mask_call(idx, lt, topk, t):
    b, s, _ = idx.shape
    kt = lt.shape[0]
    assert t >= topk
    nt, nk = s // t, s // kt
    return pl.pallas_call(
        functools.partial(_dsa_mask_kernel, topk=topk),
        grid=(b, nt),
        in_specs=[pl.BlockSpec((1, t, IDX_Q), lambda bi, i: (bi, i, 0)),
                  pl.BlockSpec((1, s, LANES), lambda bi, i: (bi, 0, IDX_Q // LANES)),
                  pl.BlockSpec((1, t, LANES), lambda bi, i: (bi, i, IDX_Q // LANES)),
                  pl.BlockSpec((kt, kt), lambda bi, i: (0, 0))],
        out_specs=pl.BlockSpec((1, 1, nk, kt, t), lambda bi, i: (bi, i, 0, 0, 0)),
        out_shape=jax.ShapeDtypeStruct((b, nt, nk, kt, t), BF16),
        scratch_shapes=[pltpu.VMEM((nk, kt, t), jnp.int32), pltpu.VMEM((IDX_HEADS, LANES, t), BF16)],
        compiler_params=_params("parallel", "arbitrary"),
    )(idx, idx, idx, lt)


def _attn_c_kernel(q_ref, k_ref, vt_ref, b_ref, o_ref, qz_ref, m_ref, l_ref, acc_ref):
    t = q_ref.shape[1]
    i = pl.program_id(1)
    _load_qt(q_ref, qz_ref)
    m_ref[...] = jnp.full(m_ref.shape, NEG, F32)
    l_ref[...] = jnp.zeros(l_ref.shape, F32)
    acc_ref[...] = jnp.zeros(acc_ref.shape, F32)

    key, qry = _tile_ids(KEY_TILE)
    ident = jnp.where(key == qry, 1.0, 0.0).astype(BF16)

    sub = t // KEY_TILE

    def body(g, carry):
        scores = []
        for r in range(sub):
            u = g * sub + r
            bias = b_ref[0, 0, u]
            for n in range(C_HEADS):
                kb = k_ref[0, _key_rows(u), (n // 2) * LANES:(n // 2 + 1) * LANES]
                lhs = jnp.concatenate([kb, ident], axis=1)
                rhs = jnp.concatenate([qz_ref[n], bias], axis=0)
                scores.append(_dot(lhs, rhs))
        for r in range(sub):
            for n in range(C_HEADS):
                vt = vt_ref[0, g * sub + r, n * HEAD_DIM:(n + 1) * HEAD_DIM, :]
                _softmax_step(n, scores[r * C_HEADS + n], vt, m_ref, l_ref, acc_ref)
        return carry

    lax.fori_loop(0, i + 1, body, 0)
    o = acc_ref[...] / l_ref[...]
    o_ref[0] = o.reshape(C_HEADS * HEAD_DIM, t).T.astype(o_ref.dtype)


def _attn_c_call(qk, vt, bias):
    b, s, _ = qk.shape
    nk, _, kt = vt.shape[1:]
    nt, t = bias.shape[1], bias.shape[4]
    return pl.pallas_call(
        _attn_c_kernel,
        grid=(b, nt),
        in_specs=[pl.BlockSpec((1, t, GROUP), lambda bi, i: (bi, i, 2)),
                  pl.BlockSpec((1, s, GROUP), lambda bi, i: (bi, 0, 3)),
                  pl.BlockSpec((1, nk, GROUP, kt), lambda bi, i: (bi, 0, 2, 0)),
                  pl.BlockSpec((1, 1, nk, kt, t), lambda bi, i: (bi, i, 0, 0, 0))],
        out_specs=pl.BlockSpec((1, t, GROUP), lambda bi, i: (bi, i, 0)),
        out_shape=jax.ShapeDtypeStruct((b, s, GROUP), BF16),
        scratch_shapes=[pltpu.VMEM((C_HEADS, LANES, t), BF16), pltpu.VMEM((C_HEADS, 1, t), F32),
                        pltpu.VMEM((C_HEADS, 1, t), F32), pltpu.VMEM((C_HEADS, HEAD_DIM, t), F32)],
        compiler_params=_params("parallel", "arbitrary"),
    )(qk, qk, vt, bias)


def _merge_kernel(x_ref, h_ref, ya_ref, yb_ref, yc_ref, wg_ref, wa_ref, wb_ref, wc_ref, wo_ref, o_ref):
    d = x_ref.shape[1]
    h = h_ref[...]
    merged = jnp.zeros(x_ref.shape, F32)
    for n, (y_ref, w_ref) in enumerate(((ya_ref, wa_ref), (yb_ref, wb_ref), (yc_ref, wc_ref))):
        gate = jax.nn.sigmoid(_dot(h, wg_ref[0, :, n * d:(n + 1) * d]))
        merged = merged + gate * _dot(y_ref[...], w_ref[0])
    o_ref[...] = x_ref[...] + _dot(merged.astype(BF16), wo_ref[0])


def _merge_call(x2, h, ya, yb, yc, wg, wa, wb, wc, wo, layer):
    n, d = x2.shape
    tm = ROW_TILE
    row = lambda cols: pl.BlockSpec((tm, cols), lambda i: (i, 0))
    full = lambda w: pl.BlockSpec((1,) + w.shape[1:], lambda i: (layer, 0, 0))
    return pl.pallas_call(
        _merge_kernel,
        grid=(n // tm,),
        in_specs=[row(d), row(d), row(ya.shape[1]), row(yb.shape[1]), row(yc.shape[1]),
                  full(wg), full(wa), full(wb), full(wc), full(wo)],
        out_specs=row(d),
        out_shape=jax.ShapeDtypeStruct((n, d), F32),
        compiler_params=_params("parallel"),
    )(x2, h, ya, yb, yc, wg, wa, wb, wc, wo)


def _mlp_kernel(x_ref, p_ref, gm_ref, wu_ref, wd_ref, gp_ref, wpg_ref, wpp_ref, o_ref):
    x = x_ref[...]
    ms = jnp.mean(x * x, axis=-1, keepdims=True)
    h = (x * lax.rsqrt(ms + EPS) * gm_ref[0]).astype(BF16)
    for f in range(wu_ref.shape[2] // FF_TILE):
        u = jnp.maximum(_dot(h, wu_ref[0, :, f * FF_TILE:(f + 1) * FF_TILE]), 0.0)
        x = x + _dot((u * u).astype(BF16), wd_ref[0, f * FF_TILE:(f + 1) * FF_TILE, :])
    ms = jnp.mean(x * x, axis=-1, keepdims=True)
    hp = (x * lax.rsqrt(ms + EPS) * gp_ref[0]).astype(BF16)
    gate = jax.nn.sigmoid(_dot(hp, wpg_ref[0]))
    o_ref[...] = x + gate * _dot(p_ref[0].astype(BF16), wpp_ref[0])


def _mlp_call(x2, p3, g_mlp, w_up, w_down, g_ple, w_pg, w_pp, layer):
    n, d = x2.shape
    pd = p3.shape[2]
    tm = ROW_TILE
    full = lambda w: pl.BlockSpec((1,) + w.shape[1:], lambda i: (layer, 0, 0),
                                  pipeline_mode=pl.Buffered(1))
    return pl.pallas_call(
        _mlp_kernel,
        grid=(n // tm,),
        in_specs=[pl.BlockSpec((tm, d), lambda i: (i, 0)),
                  pl.BlockSpec((1, tm, pd), lambda i: (layer, i, 0)),
                  full(g_mlp), full(w_up), full(w_down), full(g_ple), full(w_pg), full(w_pp)],
        out_specs=pl.BlockSpec((tm, d), lambda i: (i, 0)),
        out_shape=jax.ShapeDtypeStruct((n, d), F32),
        compiler_params=_params("parallel"),
    )(x2, p3, g_mlp, w_up, w_down, g_ple, w_pg, w_pp)


def _tile_gain(g, reps, scale=1.0):
    return jnp.tile(g.astype(F32), (1, reps)) * scale


def kernel(x, p, positions, attn_norm, w_in, a_q_norm, a_k_norm, a_lambda, a_subln,
           c_q_norm, c_k_norm, idx_k_norm, w_br_a, w_br_b, w_br_c, w_out,
           mlp_norm, w_up, w_down, ple_norm, w_ple_gate, w_ple_proj):
    b, s, d = x.shape
    depth = w_in.shape[0]
    n = b * s
    topk = min(TOPK_MAX, s // 4)
    t = min(ATTN_TILE, s)
    scale = HEAD_DIM ** -0.5
    qscale = scale * LOG2E

    inv = ROPE_THETA ** (-jnp.arange(0, HEAD_DIM, 2, dtype=F32) / HEAD_DIM)
    ang = positions.astype(F32)[..., None] * inv
    cos, sin = jnp.cos(ang), jnp.sin(ang)
    cos_t = jnp.concatenate([cos, cos, cos, cos], axis=-1).reshape(n, LANES)
    sin_t = jnp.concatenate([-sin, sin, -sin, sin], axis=-1).reshape(n, LANES)

    qa, ka, va = 0, 512, 1024
    qb, kb, vb = 1536, 2048, 2560
    qc, kc, vc = 3072, 3584, 4096
    qi = 4608
    gl = 4932
    col = lambda a, width=GROUP: w_in[:, :, a:a + width]
    w_qk = jnp.concatenate([col(qa), col(ka), col(qc), col(kc)], axis=-1).astype(BF16)
    w_bqk = jnp.concatenate([col(qb) * qscale, col(kb)], axis=-1).astype(BF16)
    w_vt = jnp.concatenate([col(va), col(vb), col(vc)], axis=-1).astype(BF16)
    w_idx = jnp.pad(col(qi, gl - qi), ((0, 0), (0, 0), (0, IDX_COLS - (gl - qi)))).astype(BF16)
    w_gl = col(gl, N_BRANCHES * d).astype(BF16)
    heads = GROUP // HEAD_DIM
    gain_qk = jnp.concatenate([_tile_gain(a_q_norm, heads, qscale), _tile_gain(a_k_norm, heads),
                               _tile_gain(c_q_norm, heads, qscale), _tile_gain(c_k_norm, heads)],
                              axis=-1)[:, None, :]
    gain_idx = jnp.pad(idx_k_norm.astype(F32), ((0, 0), (0, LANES - IDX_DIM)))[:, None, :]
    scale_idx = jnp.concatenate([jnp.full((IDX_Q,), IDX_DIM ** -0.5, F32), jnp.ones((IDX_DIM,), F32),
                                 jnp.full((IDX_HEADS,), IDX_HEADS ** -0.5, F32),
                                 jnp.zeros((LANES - IDX_DIM - IDX_HEADS,), F32)])[None, :]
    gid = jnp.arange(GROUP) // HEAD_DIM
    gmat = (gid[:, None] == gid[None, :]).astype(BF16)
    kt = min(KEY_TILE, s)
    ar = jnp.arange(kt)
    later_mat = jnp.concatenate([(ar[None, :] > ar[:, None]).astype(BF16),
                                 jnp.ones((ONES_ROWS, kt), BF16)], axis=0)
    before_mat = (ar[None, :] < ar[:, None]).astype(BF16)

    wa, wb, wc, wo = (w.astype(BF16) for w in (w_br_a, w_br_b, w_br_c, w_out))
    wu, wd, wpg, wpp = (w.astype(BF16) for w in (w_up, w_down, w_ple_gate, w_ple_proj))
    g_attn, g_mlp, g_ple = (g.astype(F32)[:, None, :] for g in (attn_norm, mlp_norm, ple_norm))
    subln = a_subln.astype(F32)[:, None, :]
    lam_p = a_lambda.astype(F32)
    p3 = p.reshape(depth, n, p.shape[-1])

    x2 = x.reshape(n, d)
    for i in range(depth):
        lam_init = 0.8 - 0.6 * math.exp(-0.3 * i)
        h, qk, bqk, vt, idx = _front_call(x2, g_attn, w_qk, w_bqk, w_vt, w_idx, gain_qk, gain_idx,
                                          scale_idx, cos_t, sin_t, gmat, i, kt)
        qk, bqk, idx = (a.reshape(b, s, -1) for a in (qk, bqk, idx))
        vt = vt.reshape(b, s // kt, -1, kt)
        ya, yb = _attn_ab_call(qk, bqk, vt, lam_p, subln, later_mat, i, lam_init, t)
        bias = _dsa_mask_call(idx, before_mat, topk, t)
        yc = _attn_c_call(qk, vt, bias)
        x2 = _merge_call(x2, h, ya.reshape(n, -1), yb.reshape(n, -1), yc.reshape(n, -1),
                         w_gl, wa, wb, wc, wo, i)
        x2 = _mlp_call(x2, p3, g_mlp, wu, wd, g_ple, wpg, wpp, i)
    return x2.reshape(b, s, d)
```

```python
import functools
import math

import jax
import jax.numpy as jnp
from jax import lax
from jax.experimental import pallas as pl
from jax.experimental.pallas import tpu as pltpu

F32 = jnp.float32
BF16 = jnp.bfloat16

HEAD_DIM = 64
CHUNK = 64
A_HEADS = 4
B_HEADS = 8
C_HEADS = 8
IDX_HEADS = 4
IDX_DIM = 64
TOPK_MAX = 256
N_BRANCHES = 3
ROPE_THETA = 10000.0
EPS = 1e-6

LANES = 128
SUBLANES = 8
HALF_HEAD = HEAD_DIM // 2
NEG = -1e30
INT_MIN = -(2 ** 31)
NEG_INF_KEY = INT_MIN + 0x7FFFFF
TINY_PREFIXES = 128
LOG2E = 1.4426950408889634
VMEM_LIMIT = 56 * 1024 * 1024

ROW_TILE = 512
ATTN_TILE = 256
KEY_TILE = 128
FF_TILE = 1024
GROUP = 512


def _params(*sem):
    return pltpu.CompilerParams(dimension_semantics=sem, vmem_limit_bytes=VMEM_LIMIT)


def _dot(a, b):
    return jnp.dot(a, b, preferred_element_type=F32)


def _split(x):
    hi = x.astype(BF16)
    return hi, (x - hi.astype(F32)).astype(BF16)


def _swap_halves(y):
    lane = lax.broadcasted_iota(jnp.int32, y.shape, 1)
    return jnp.where((lane & HALF_HEAD) != 0, pltpu.roll(y, HALF_HEAD, 1),
                     pltpu.roll(y, LANES - HALF_HEAD, 1))


def _rope_slab(y, cos, sin):
    return y * cos + _swap_halves(y) * sin


IDX_Q = IDX_HEADS * IDX_DIM
IDX_COLS = IDX_Q + LANES


def _front_kernel(x_ref, g_ref, wqk_ref, wbqk_ref, wvt_ref, widx_ref, gqk_ref, gidx_ref, sidx_ref,
                  cos_ref, sin_ref, gmat_ref, h_ref, qk_ref, bqk_ref, vt_ref, idx_ref):
    x = x_ref[...]
    ms = jnp.mean(x * x, axis=-1, keepdims=True)
    h = (x * lax.rsqrt(ms + EPS) * g_ref[0]).astype(BF16)
    h_ref[...] = h
    cos = cos_ref[...]
    sin = sin_ref[...]

    gmat = gmat_ref[...]
    for j in range(qk_ref.shape[1] // GROUP):
        cols = slice(j * GROUP, (j + 1) * GROUP)
        t = _dot(h, wqk_ref[0, :, cols])
        hi, lo = _split(t * t)
        ms = (_dot(hi, gmat) + _dot(lo, gmat)) * (1.0 / HEAD_DIM)
        y = t * lax.rsqrt(ms + EPS) * gqk_ref[0, :, cols]
        for s in range(GROUP // LANES):
            sl = slice(s * LANES, (s + 1) * LANES)
            qk_ref[:, j * GROUP + s * LANES:j * GROUP + (s + 1) * LANES] = (
                _rope_slab(y[:, sl], cos, sin).astype(qk_ref.dtype))

    for j in range(bqk_ref.shape[1] // GROUP):
        cols = slice(j * GROUP, (j + 1) * GROUP)
        bqk_ref[:, cols] = _dot(h, wbqk_ref[0, :, cols]).astype(bqk_ref.dtype)

    tt = vt_ref.shape[2]
    for j in range(vt_ref.shape[1] // GROUP):
        cols = slice(j * GROUP, (j + 1) * GROUP)
        t = _dot(h, wvt_ref[0, :, cols])
        for r in range(vt_ref.shape[0]):
            vt_ref[r, cols, :] = t[r * tt:(r + 1) * tt, :].T.astype(vt_ref.dtype)

    t = _dot(h, widx_ref[0])
    scale = sidx_ref[...]
    for s in range(IDX_Q // LANES):
        sl = slice(s * LANES, (s + 1) * LANES)
        idx_ref[:, sl] = _rope_slab(t[:, sl], cos, sin) * scale[:, sl]
    sl = slice(IDX_Q, IDX_COLS)
    tk = t[:, sl]
    lane = lax.broadcasted_iota(jnp.int32, tk.shape, 1)
    is_k = lane < IDX_DIM
    ms = jnp.sum(jnp.where(is_k, tk * tk, 0.0), axis=-1, keepdims=True) * (1.0 / IDX_DIM)
    yk = tk * lax.rsqrt(ms + EPS) * gidx_ref[0]
    idx_ref[:, sl] = jnp.where(is_k, _rope_slab(yk, cos, sin), tk) * scale[:, sl]


def _front_call(x2, g_attn, w_qk, w_bqk, w_vt, w_idx, gain_qk, gain_idx, scale_idx, cos_t, sin_t,
                gmat, layer, tt):
    n, d = x2.shape
    tm = ROW_TILE
    row = lambda cols: pl.BlockSpec((tm, cols), lambda i: (i, 0))
    full = lambda w: pl.BlockSpec((1,) + w.shape[1:], lambda i: (layer, 0, 0))
    fixed = lambda a: pl.BlockSpec(a.shape, lambda i: (0, 0))
    cqk, cbqk, cvt = w_qk.shape[2], w_bqk.shape[2], w_vt.shape[2]
    return pl.pallas_call(
        _front_kernel,
        grid=(n // tm,),
        in_specs=[row(d), full(g_attn), full(w_qk), full(w_bqk), full(w_vt), full(w_idx),
                  full(gain_qk), full(gain_idx), fixed(scale_idx), row(LANES), row(LANES),
                  fixed(gmat)],
        out_specs=[row(d), row(cqk), row(cbqk),
                   pl.BlockSpec((tm // tt, cvt, tt), lambda i: (i, 0, 0)), row(IDX_COLS)],
        out_shape=[jax.ShapeDtypeStruct((n, d), BF16), jax.ShapeDtypeStruct((n, cqk), BF16),
                   jax.ShapeDtypeStruct((n, cbqk), BF16),
                   jax.ShapeDtypeStruct((n // tt, cvt, tt), BF16),
                   jax.ShapeDtypeStruct((n, IDX_COLS), F32)],
        compiler_params=_params("parallel"),
    )(x2, g_attn, w_qk, w_bqk, w_vt, w_idx, gain_qk, gain_idx, scale_idx, cos_t, sin_t, gmat)


def _load_qt(q_ref, qz_ref):
    qt = q_ref[0].astype(F32).T
    rows = lax.broadcasted_iota(jnp.int32, (LANES, qt.shape[1]), 0)
    for s in range(qt.shape[0] // LANES):
        slab = qt[s * LANES:(s + 1) * LANES, :]
        qz_ref[2 * s] = jnp.where(rows < HEAD_DIM, slab, 0.0).astype(BF16)
        qz_ref[2 * s + 1] = jnp.where(rows >= HEAD_DIM, slab, 0.0).astype(BF16)


ONES_ROWS = 16


def _softmax_step(n, s, vt, m_ref, l_ref, acc_ref):
    m_old = m_ref[n]
    m_new = jnp.maximum(m_old, jnp.max(s, axis=0, keepdims=True))
    p = jnp.exp2(s - m_new).astype(BF16)
    alpha = jnp.exp2(m_old - m_new)
    f = vt.shape[0]
    pv = _dot(jnp.concatenate([vt, jnp.ones((ONES_ROWS, vt.shape[1]), BF16)], axis=0), p)
    l_ref[n] = alpha * l_ref[n] + pv[f:f + 1]
    acc_ref[n] = alpha * acc_ref[n] + pv[:f]
    m_ref[n] = m_new


def _tile_ids(t):
    key = lax.broadcasted_iota(jnp.int32, (KEY_TILE, t), 0)
    qry = lax.broadcasted_iota(jnp.int32, (KEY_TILE, t), 1)
    return key, qry


def _key_rows(u):
    return pl.ds(pl.multiple_of(u * KEY_TILE, KEY_TILE), KEY_TILE)


def _interleave(*gens):
    gens = list(gens)
    while gens:
        for gen in list(gens):
            try:
                next(gen)
            except StopIteration:
                gens.remove(gen)


def _attn_ab_kernel(lam_ref, aq_ref, ak_ref, avt_ref, g_ref, bq_ref, bk_ref, bvt_ref, lmat_ref,
                    ya_ref, yb_ref, aqz_ref, m_ref, l_ref, aacc_ref, bqz_ref, r_ref, bacc_ref, *,
                    lam_init):
    t = aq_ref.shape[1]
    i = pl.program_id(1)
    lp = lam_ref[0]
    lam = (jnp.exp(jnp.sum(lp[0:1] * lp[1:2], axis=-1, keepdims=True))
           - jnp.exp(jnp.sum(lp[2:3] * lp[3:4], axis=-1, keepdims=True)) + lam_init)
    key, qry = _tile_ids(t)
    sub = t // KEY_TILE
    lmat = lmat_ref[...]
    _load_qt(aq_ref, aqz_ref)
    _load_qt(bq_ref, bqz_ref)
    m_ref[...] = jnp.full(m_ref.shape, NEG, F32)
    l_ref[...] = jnp.zeros(l_ref.shape, F32)
    aacc_ref[...] = jnp.zeros(aacc_ref.shape, F32)
    r_ref[...] = jnp.zeros(r_ref.shape, F32)
    bacc_ref[...] = jnp.zeros(bacc_ref.shape, F32)

    def a_group(g, diag):
        scores = []
        for r in range(sub):
            for n in range(2 * A_HEADS):
                kb = ak_ref[0, _key_rows(g * sub + r), (n // 2) * LANES:(n // 2 + 1) * LANES]
                scores.append(_dot(kb, aqz_ref[n]))
        yield
        for r in range(sub):
            for n in range(2 * A_HEADS):
                s = scores[r * 2 * A_HEADS + n]
                if diag:
                    s = jnp.where(((key + r * KEY_TILE) // CHUNK) <= (qry // CHUNK), s, NEG)
                vt = avt_ref[0, g * sub + r, (n // 2) * LANES:(n // 2 + 1) * LANES, :]
                _softmax_step(n, s, vt, m_ref, l_ref, aacc_ref)
                yield

    def b_group(g, diag):
        order = [(r, n) for r in reversed(range(sub)) for n in range(B_HEADS)]
        strict = {r: (key + r * KEY_TILE) < qry for r in range(sub)} if diag else None
        logits = {}
        for r, n in order:
            kb = bk_ref[0, _key_rows(g * sub + r), (n // 2) * LANES:(n // 2 + 1) * LANES]
            logits[r, n] = _dot(kb, bqz_ref[n])
        yield
        log_betas, laters, sums = {}, {}, {}
        for r, n in order:
            z = logits[r, n]
            neg_abs = pltpu.bitcast(pltpu.bitcast(z, jnp.int32) | jnp.int32(INT_MIN), F32)
            soft = jnp.log2(1.0 + jnp.exp2(neg_abs))
            log_beta = jnp.minimum(z, 0.0) - soft
            log_om = log_beta - z
            if diag:
                log_om = jnp.where(strict[r], log_om, 0.0)
            later = _dot(lmat, log_om.astype(BF16))
            laters[r, n] = later[:KEY_TILE]
            log_betas[r, n] = log_beta
            sums[r, n] = later[KEY_TILE:KEY_TILE + 1]
            yield
        for r, n in order:
            a = jnp.exp2(log_betas[r, n] + laters[r, n])
            if diag:
                a = jnp.where(strict[r], a, 0.0)
            vt = bvt_ref[0, g * sub + r, n * HEAD_DIM:(n + 1) * HEAD_DIM, :]
            visited = r_ref[n]
            bacc_ref[n] = bacc_ref[n] + _dot(vt, a.astype(BF16)) * jnp.exp2(visited)
            r_ref[n] = visited + sums[r, n]
            yield

    def body(n, carry):
        _interleave(a_group(n, False), b_group(i - 1 - n, False))
        return carry

    _interleave(b_group(i, True))
    lax.fori_loop(0, i, body, 0)
    _interleave(a_group(i, True))

    gain = g_ref[0] * (1.0 - lam_init)
    for h in range(A_HEADS):
        o = aacc_ref[2 * h] / l_ref[2 * h] - lam * (aacc_ref[2 * h + 1] / l_ref[2 * h + 1])
        ms = jnp.mean(o * o, axis=0, keepdims=True)
        y = (o * lax.rsqrt(ms + EPS)).T * gain
        ya_ref[0, :, h * LANES:(h + 1) * LANES] = y.astype(ya_ref.dtype)
    yb_ref[0] = bacc_ref[...].reshape(B_HEADS * HEAD_DIM, t).T.astype(yb_ref.dtype)


def _attn_ab_call(qk, bqk, vt, lam_p, subln, lmat, layer, lam_init, t):
    b, s, _ = qk.shape
    nt, _, kt = vt.shape[1:]
    ns = 2 * A_HEADS
    qspec = pl.BlockSpec((1, t, GROUP), lambda bi, i: (bi, i, 0))
    kspec = pl.BlockSpec((1, s, GROUP), lambda bi, i: (bi, 0, 1))
    vspec = lambda f: pl.BlockSpec((1, nt, GROUP, kt), lambda bi, i: (bi, 0, f, 0))
    return pl.pallas_call(
        functools.partial(_attn_ab_kernel, lam_init=lam_init),
        grid=(b, s // t),
        in_specs=[pl.BlockSpec((1, 4, HEAD_DIM), lambda bi, i: (layer, 0, 0)),
                  qspec, kspec, vspec(0),
                  pl.BlockSpec((1, 1, LANES), lambda bi, i: (layer, 0, 0)),
                  qspec, kspec, vspec(1),
                  pl.BlockSpec(lmat.shape, lambda bi, i: (0, 0))],
        out_specs=[qspec, qspec],
        out_shape=[jax.ShapeDtypeStruct((b, s, GROUP), BF16)] * 2,
        scratch_shapes=[pltpu.VMEM((ns, LANES, t), BF16), pltpu.VMEM((ns, 1, t), F32),
                        pltpu.VMEM((ns, 1, t), F32), pltpu.VMEM((ns, LANES, t), F32),
                        pltpu.VMEM((B_HEADS, LANES, t), BF16), pltpu.VMEM((B_HEADS, 1, t), F32),
                        pltpu.VMEM((B_HEADS, HEAD_DIM, t), F32)],
        compiler_params=_params("parallel", "arbitrary"),
    )(lam_p, qk, qk, vt, subln, bqk, bqk, vt, lmat)


def _fold_rows(x, group=SUBLANES):
    parts = [x[r * group:(r + 1) * group] for r in range(x.shape[0] // group)]
    while len(parts) > 1:
        parts = [a + b for a, b in zip(parts[0::2], parts[1::2])]
    return parts[0]


def _dsa_mask_kernel(q_ref, kw_ref, wq_ref, lt_ref, o_ref, key_ref, hi_ref, qz_ref, *, topk):
    t = q_ref.shape[1]
    nj = key_ref.shape[0]
    i = pl.program_id(1)
    nch = (i + 1) * (t // KEY_TILE)
    qt = q_ref[0].T
    pad = jnp.zeros((LANES - IDX_DIM, t), F32)
    for h in range(IDX_HEADS):
        qz_ref[h] = jnp.concatenate([qt[h * IDX_DIM:(h + 1) * IDX_DIM], pad], axis=0).astype(BF16)
    wt = wq_ref[0].T
    wh = [wt[IDX_DIM + h:IDX_DIM + h + 1, :] for h in range(IDX_HEADS)]
    key0, qry0 = _tile_ids(t)
    sub = t // KEY_TILE

    def chunk_ok(r):
        return ((key0 + r * KEY_TILE) // CHUNK) <= (qry0 // CHUNK)

    def fill(g, diag):
        for r in range(sub):
            c = g * sub + r
            kb = kw_ref[0, _key_rows(c), :].astype(BF16)
            score = jnp.zeros((KEY_TILE, t), F32)
            for h in range(IDX_HEADS):
                score = score + wh[h] * jnp.maximum(_dot(kb, qz_ref[h]), 0.0)
            if diag:
                score = jnp.where(chunk_ok(r), score, -jnp.inf)
            bits = pltpu.bitcast(jnp.where(score == 0.0, 0.0, score), jnp.int32)
            key_ref[c] = jnp.where(bits < 0, bits ^ jnp.int32(0x7FFFFFFF), bits)
            hi_ref[c] = pltpu.bitcast(bits & jnp.int32(-65536), F32).astype(BF16)

    def fill_body(g, carry):
        fill(g, False)
        return carry

    lax.fori_loop(0, i, fill_body, 0)
    fill(i, True)

    def count(pred, thr):
        def body(g, acc):
            for r in range(sub):
                acc = acc + _fold_rows(jnp.where(pred(key_ref[g * sub + r], thr), 1.0, 0.0))
            return acc
        acc = lax.fori_loop(0, i + 1, body, jnp.zeros((SUBLANES, t), F32))
        return jnp.sum(acc, axis=0, keepdims=True)

    def count_hi(cand):
        one = jnp.ones((), BF16)
        zero = jnp.zeros((), BF16)

        def body(g, acc):
            for r in range(sub):
                hit = jnp.where(hi_ref[g * sub + r] >= cand, one, zero)
                acc = acc + _fold_rows(hit, 2 * SUBLANES).astype(F32)
            return acc
        acc = lax.fori_loop(0, i + 1, body, jnp.zeros((2 * SUBLANES, t), F32))
        return jnp.sum(acc, axis=0, keepdims=True)

    def hi_body(n, lo):
        cand = lo + jnp.left_shift(jnp.int32(1), 15 - n)
        pattern = jnp.where(cand < 0, cand ^ jnp.int32(0x7FFF), cand)
        value = pltpu.bitcast(jnp.left_shift(pattern, 16), F32).astype(BF16)
        return jnp.where(count_hi(value) >= topk, cand, lo)

    def lo_body(n, lo):
        cand = lo + jnp.left_shift(jnp.int32(1), 15 - n)
        cnt = count(lambda k, th: k >= th, cand)
        return jnp.where(cnt >= topk, cand, lo)

    def tiny_body(n, lo):
        cand = lo + jnp.left_shift(jnp.int32(1), 23 - n)
        cnt = count(lambda k, th: k >= th, cand)
        return jnp.where(tiny & (cnt >= topk), cand, lo)

    hi = lax.fori_loop(0, 16, hi_body, jnp.full((1, t), -(2 ** 15), jnp.int32))
    hi = jnp.maximum(hi, jnp.int32(NEG_INF_KEY >> 16))
    tiny = (hi >= -TINY_PREFIXES) & (hi < TINY_PREFIXES)
    base = jnp.where(tiny, jnp.int32(-TINY_PREFIXES * 65536), jnp.left_shift(hi, 16))
    base = lax.cond(jnp.max(jnp.where(tiny, 1, 0)) > 0,
                    lambda b: lax.fori_loop(0, 8, tiny_body, b), lambda b: b, base)
    thr = lax.fori_loop(0, 16, lo_body, base)
    need = topk - count(lambda k, th: k > th, thr)
    lt = lt_ref[...]

    def emit(g, run, diag):
        for r in range(sub):
            c = g * sub + r
            okey = key_ref[c]
            eq = okey == thr
            eqf = jnp.where(eq, 1.0, 0.0)
            before = _dot(lt, eqf.astype(BF16)) + run
            sel = (okey > thr) | (eq & (before < need))
            if diag:
                sel = sel & chunk_ok(r)
            o_ref[0, 0, c] = jnp.where(sel, 0.0, NEG).astype(o_ref.dtype)
            run = run + jnp.sum(eqf, axis=0, keepdims=True)
        return run

    run = lax.fori_loop(0, i, lambda g, run: emit(g, run, False), jnp.zeros((1, t), F32))
    emit(i, run, True)

    def blank(c, carry):
        o_ref[0, 0, c] = jnp.full((KEY_TILE, t), NEG, o_ref.dtype)
        return carry

    lax.fori_loop(nch, nj, blank, 0)


def _dsa_mask_call(idx, lt, topk, t):
    b, s, _ = idx.shape
    kt = lt.shape[0]
    assert t >= topk
    nt, nk = s // t, s // kt
    return pl.pallas_call(
        functools.partial(_dsa_mask_kernel, topk=topk),
        grid=(b, nt),
        in_specs=[pl.BlockSpec((1, t, IDX_Q), lambda bi, i: (bi, i, 0)),
                  pl.BlockSpec((1, s, LANES), lambda bi, i: (bi, 0, IDX_Q // LANES)),
                  pl.BlockSpec((1, t, LANES), lambda bi, i: (bi, i, IDX_Q // LANES)),
                  pl.BlockSpec((kt, kt), lambda bi, i: (0, 0))],
        out_specs=pl.BlockSpec((1, 1, nk, kt, t), lambda bi, i: (bi, i, 0, 0, 0)),
        out_shape=jax.ShapeDtypeStruct((b, nt, nk, kt, t), BF16),
        scratch_shapes=[pltpu.VMEM((nk, kt, t), jnp.int32), pltpu.VMEM((nk, kt, t), BF16),
                        pltpu.VMEM((IDX_HEADS, LANES, t), BF16)],
        compiler_params=_params("parallel", "arbitrary"),
    )(idx, idx, idx, lt)


def _attn_c_kernel(q_ref, k_ref, vt_ref, b_ref, o_ref, qz_ref, m_ref, l_ref, acc_ref):
    t = q_ref.shape[1]
    i = pl.program_id(1)
    _load_qt(q_ref, qz_ref)
    m_ref[...] = jnp.full(m_ref.shape, NEG, F32)
    l_ref[...] = jnp.zeros(l_ref.shape, F32)
    acc_ref[...] = jnp.zeros(acc_ref.shape, F32)

    key, qry = _tile_ids(KEY_TILE)
    ident = jnp.where(key == qry, 1.0, 0.0).astype(BF16)

    sub = t // KEY_TILE

    def body(g, carry):
        scores = []
        for r in range(sub):
            u = g * sub + r
            bias = b_ref[0, 0, u]
            for n in range(C_HEADS):
                kb = k_ref[0, _key_rows(u), (n // 2) * LANES:(n // 2 + 1) * LANES]
                lhs = jnp.concatenate([kb, ident], axis=1)
                rhs = jnp.concatenate([qz_ref[n], bias], axis=0)
                scores.append(_dot(lhs, rhs))
        for r in range(sub):
            for n in range(C_HEADS):
                vt = vt_ref[0, g * sub + r, n * HEAD_DIM:(n + 1) * HEAD_DIM, :]
                _softmax_step(n, scores[r * C_HEADS + n], vt, m_ref, l_ref, acc_ref)
        return carry

    lax.fori_loop(0, i + 1, body, 0)
    o = acc_ref[...] / l_ref[...]
    o_ref[0] = o.reshape(C_HEADS * HEAD_DIM, t).T.astype(o_ref.dtype)


def _attn_c_call(qk, vt, bias):
    b, s, _ = qk.shape
    nk, _, kt = vt.shape[1:]
    nt, t = bias.shape[1], bias.shape[4]
    return pl.pallas_call(
        _attn_c_kernel,
        grid=(b, nt),
        in_specs=[pl.BlockSpec((1, t, GROUP), lambda bi, i: (bi, i, 2)),
                  pl.BlockSpec((1, s, GROUP), lambda bi, i: (bi, 0, 3)),
                  pl.BlockSpec((1, nk, GROUP, kt), lambda bi, i: (bi, 0, 2, 0)),
                  pl.BlockSpec((1, 1, nk, kt, t), lambda bi, i: (bi, i, 0, 0, 0))],
        out_specs=pl.BlockSpec((1, t, GROUP), lambda bi, i: (bi, i, 0)),
        out_shape=jax.ShapeDtypeStruct((b, s, GROUP), BF16),
        scratch_shapes=[pltpu.VMEM((C_HEADS, LANES, t), BF16), pltpu.VMEM((C_HEADS, 1, t), F32),
                        pltpu.VMEM((C_HEADS, 1, t), F32), pltpu.VMEM((C_HEADS, HEAD_DIM, t), F32)],
        compiler_params=_params("parallel", "arbitrary"),
    )(qk, qk, vt, bias)


def _merge_kernel(x_ref, h_ref, ya_ref, yb_ref, yc_ref, wg_ref, wa_ref, wb_ref, wc_ref, wo_ref, o_ref):
    d = x_ref.shape[1]
    h = h_ref[...]
    merged = jnp.zeros(x_ref.shape, F32)
    for n, (y_ref, w_ref) in enumerate(((ya_ref, wa_ref), (yb_ref, wb_ref), (yc_ref, wc_ref))):
        gate = jax.nn.sigmoid(_dot(h, wg_ref[0, :, n * d:(n + 1) * d]))
        merged = merged + gate * _dot(y_ref[...], w_ref[0])
    o_ref[...] = x_ref[...] + _dot(merged.astype(BF16), wo_ref[0])


def _merge_call(x2, h, ya, yb, yc, wg, wa, wb, wc, wo, layer):
    n, d = x2.shape
    tm = ROW_TILE
    row = lambda cols: pl.BlockSpec((tm, cols), lambda i: (i, 0))
    full = lambda w: pl.BlockSpec((1,) + w.shape[1:], lambda i: (layer, 0, 0))
    return pl.pallas_call(
        _merge_kernel,
        grid=(n // tm,),
        in_specs=[row(d), row(d), row(ya.shape[1]), row(yb.shape[1]), row(yc.shape[1]),
                  full(wg), full(wa), full(wb), full(wc), full(wo)],
        out_specs=row(d),
        out_shape=jax.ShapeDtypeStruct((n, d), F32),
        compiler_params=_params("parallel"),
    )(x2, h, ya, yb, yc, wg, wa, wb, wc, wo)


def _mlp_kernel(x_ref, p_ref, gm_ref, wu_ref, wd_ref, gp_ref, wpg_ref, wpp_ref, o_ref):
    x = x_ref[...]
    ms = jnp.mean(x * x, axis=-1, keepdims=True)
    h = (x * lax.rsqrt(ms + EPS) * gm_ref[0]).astype(BF16)
    for f in range(wu_ref.shape[2] // FF_TILE):
        u = jnp.maximum(_dot(h, wu_ref[0, :, f * FF_TILE:(f + 1) * FF_TILE]), 0.0)
        x = x + _dot((u * u).astype(BF16), wd_ref[0, f * FF_TILE:(f + 1) * FF_TILE, :])
    ms = jnp.mean(x * x, axis=-1, keepdims=True)
    hp = (x * lax.rsqrt(ms + EPS) * gp_ref[0]).astype(BF16)
    gate = jax.nn.sigmoid(_dot(hp, wpg_ref[0]))
    o_ref[...] = x + gate * _dot(p_ref[0].astype(BF16), wpp_ref[0])


def _mlp_call(x2, p3, g_mlp, w_up, w_down, g_ple, w_pg, w_pp, layer):
    n, d = x2.shape
    pd = p3.shape[2]
    tm = ROW_TILE
    full = lambda w: pl.BlockSpec((1,) + w.shape[1:], lambda i: (layer, 0, 0),
                                  pipeline_mode=pl.Buffered(1))
    return pl.pallas_call(
        _mlp_kernel,
        grid=(n // tm,),
        in_specs=[pl.BlockSpec((tm, d), lambda i: (i, 0)),
                  pl.BlockSpec((1, tm, pd), lambda i: (layer, i, 0)),
                  full(g_mlp), full(w_up), full(w_down), full(g_ple), full(w_pg), full(w_pp)],
        out_specs=pl.BlockSpec((tm, d), lambda i: (i, 0)),
        out_shape=jax.ShapeDtypeStruct((n, d), F32),
        compiler_params=_params("parallel"),
    )(x2, p3, g_mlp, w_up, w_down, g_ple, w_pg, w_pp)


def _tile_gain(g, reps, scale=1.0):
    return jnp.tile(g.astype(F32), (1, reps)) * scale


def kernel(x, p, positions, attn_norm, w_in, a_q_norm, a_k_norm, a_lambda, a_subln,
           c_q_norm, c_k_norm, idx_k_norm, w_br_a, w_br_b, w_br_c, w_out,
           mlp_norm, w_up, w_down, ple_norm, w_ple_gate, w_ple_proj):
    b, s, d = x.shape
    depth = w_in.shape[0]
    n = b * s
    topk = min(TOPK_MAX, s // 4)
    t = min(ATTN_TILE, s)
    scale = HEAD_DIM ** -0.5
    qscale = scale * LOG2E

    inv = ROPE_THETA ** (-jnp.arange(0, HEAD_DIM, 2, dtype=F32) / HEAD_DIM)
    ang = positions.astype(F32)[..., None] * inv
    cos, sin = jnp.cos(ang), jnp.sin(ang)
    cos_t = jnp.concatenate([cos, cos, cos, cos], axis=-1).reshape(n, LANES)
    sin_t = jnp.concatenate([-sin, sin, -sin, sin], axis=-1).reshape(n, LANES)

    qa, ka, va = 0, 512, 1024
    qb, kb, vb = 1536, 2048, 2560
    qc, kc, vc = 3072, 3584, 4096
    qi = 4608
    gl = 4932
    col = lambda a, width=GROUP: w_in[:, :, a:a + width]
    w_qk = jnp.concatenate([col(qa), col(ka), col(qc), col(kc)], axis=-1).astype(BF16)
    w_bqk = jnp.concatenate([col(qb) * qscale, col(kb)], axis=-1).astype(BF16)
    w_vt = jnp.concatenate([col(va), col(vb), col(vc)], axis=-1).astype(BF16)
    w_idx = jnp.pad(col(qi, gl - qi), ((0, 0), (0, 0), (0, IDX_COLS - (gl - qi)))).astype(BF16)
    w_gl = col(gl, N_BRANCHES * d).astype(BF16)
    heads = GROUP // HEAD_DIM
    gain_qk = jnp.concatenate([_tile_gain(a_q_norm, heads, qscale), _tile_gain(a_k_norm, heads),
                               _tile_gain(c_q_norm, heads, qscale), _tile_gain(c_k_norm, heads)],
                              axis=-1)[:, None, :]
    gain_idx = jnp.pad(idx_k_norm.astype(F32), ((0, 0), (0, LANES - IDX_DIM)))[:, None, :]
    scale_idx = jnp.concatenate([jnp.full((IDX_Q,), IDX_DIM ** -0.5, F32), jnp.ones((IDX_DIM,), F32),
                                 jnp.full((IDX_HEADS,), IDX_HEADS ** -0.5, F32),
                                 jnp.zeros((LANES - IDX_DIM - IDX_HEADS,), F32)])[None, :]
    gid = jnp.arange(GROUP) // HEAD_DIM
    gmat = (gid[:, None] == gid[None, :]).astype(BF16)
    kt = min(KEY_TILE, s)
    ar = jnp.arange(kt)
    later_mat = jnp.concatenate([(ar[None, :] > ar[:, None]).astype(BF16),
                                 jnp.ones((ONES_ROWS, kt), BF16)], axis=0)
    before_mat = (ar[None, :] < ar[:, None]).astype(BF16)

    wa, wb, wc, wo = (w.astype(BF16) for w in (w_br_a, w_br_b, w_br_c, w_out))
    wu, wd, wpg, wpp = (w.astype(BF16) for w in (w_up, w_down, w_ple_gate, w_ple_proj))
    g_attn, g_mlp, g_ple = (g.astype(F32)[:, None, :] for g in (attn_norm, mlp_norm, ple_norm))
    subln = a_subln.astype(F32)[:, None, :]
    lam_p = a_lambda.astype(F32)
    p3 = p.reshape(depth, n, p.shape[-1])

    x2 = x.reshape(n, d)
    for i in range(depth):
        lam_init = 0.8 - 0.6 * math.exp(-0.3 * i)
        h, qk, bqk, vt, idx = _front_call(x2, g_attn, w_qk, w_bqk, w_vt, w_idx, gain_qk, gain_idx,
                                          scale_idx, cos_t, sin_t, gmat, i, kt)
        qk, bqk, idx = (a.reshape(b, s, -1) for a in (qk, bqk, idx))
        vt = vt.reshape(b, s // kt, -1, kt)
        ya, yb = _attn_ab_call(qk, bqk, vt, lam_p, subln, later_mat, i, lam_init, t)
        bias = _dsa_mask_call(idx, before_mat, topk, t)
        yc = _attn_c_call(qk, vt, bias)
        x2 = _merge_call(x2, h, ya.reshape(n, -1), yb.reshape(n, -1), yc.reshape(n, -1),
                         w_gl, wa, wb, wc, wo, i)
        x2 = _mlp_call(x2, p3, g_mlp, wu, wd, g_ple, wpg, wpp, i)
    return x2.reshape(b, s, d)
```

```python
import functools
import math

import jax
import jax.numpy as jnp
from jax import lax
from jax.experimental import pallas as pl
from jax.experimental.pallas import tpu as pltpu

F32 = jnp.float32
BF16 = jnp.bfloat16

HEAD_DIM = 64
CHUNK = 64
A_HEADS = 4
B_HEADS = 8
C_HEADS = 8
IDX_HEADS = 4
IDX_DIM = 64
TOPK_MAX = 256
N_BRANCHES = 3
ROPE_THETA = 10000.0
EPS = 1e-6

LANES = 128
SUBLANES = 8
HALF_HEAD = HEAD_DIM // 2
NEG = -1e30
INT_MIN = -(2 ** 31)
LOG2E = 1.4426950408889634
VMEM_LIMIT = 56 * 1024 * 1024

ROW_TILE = 512
ATTN_TILE = 256
KEY_TILE = 128
FF_TILE = 1024
GROUP = 512


def _params(*sem):
    return pltpu.CompilerParams(dimension_semantics=sem, vmem_limit_bytes=VMEM_LIMIT)


def _dot(a, b):
    return jnp.dot(a, b, preferred_element_type=F32)


def _split(x):
    hi = x.astype(BF16)
    return hi, (x - hi.astype(F32)).astype(BF16)


def _swap_halves(y):
    lane = lax.broadcasted_iota(jnp.int32, y.shape, 1)
    return jnp.where((lane & HALF_HEAD) != 0, pltpu.roll(y, HALF_HEAD, 1),
                     pltpu.roll(y, LANES - HALF_HEAD, 1))


def _rope_slab(y, cos, sin):
    return y * cos + _swap_halves(y) * sin


IDX_Q = IDX_HEADS * IDX_DIM
IDX_COLS = IDX_Q + LANES


def _front_kernel(x_ref, g_ref, wqk_ref, wbqk_ref, wvt_ref, widx_ref, gqk_ref, gidx_ref, sidx_ref,
                  cos_ref, sin_ref, gmat_ref, h_ref, qk_ref, bqk_ref, vt_ref, idx_ref):
    x = x_ref[...]
    ms = jnp.mean(x * x, axis=-1, keepdims=True)
    h = (x * lax.rsqrt(ms + EPS) * g_ref[0]).astype(BF16)
    h_ref[...] = h
    cos = cos_ref[...]
    sin = sin_ref[...]

    gmat = gmat_ref[...]
    for j in range(qk_ref.shape[1] // GROUP):
        cols = slice(j * GROUP, (j + 1) * GROUP)
        t = _dot(h, wqk_ref[0, :, cols])
        hi, lo = _split(t * t)
        ms = (_dot(hi, gmat) + _dot(lo, gmat)) * (1.0 / HEAD_DIM)
        y = t * lax.rsqrt(ms + EPS) * gqk_ref[0, :, cols]
        for s in range(GROUP // LANES):
            sl = slice(s * LANES, (s + 1) * LANES)
            qk_ref[:, j * GROUP + s * LANES:j * GROUP + (s + 1) * LANES] = (
                _rope_slab(y[:, sl], cos, sin).astype(qk_ref.dtype))

    for j in range(bqk_ref.shape[1] // GROUP):
        cols = slice(j * GROUP, (j + 1) * GROUP)
        bqk_ref[:, cols] = _dot(h, wbqk_ref[0, :, cols]).astype(bqk_ref.dtype)

    tt = vt_ref.shape[2]
    for j in range(vt_ref.shape[1] // GROUP):
        cols = slice(j * GROUP, (j + 1) * GROUP)
        t = _dot(h, wvt_ref[0, :, cols])
        for r in range(vt_ref.shape[0]):
            vt_ref[r, cols, :] = t[r * tt:(r + 1) * tt, :].T.astype(vt_ref.dtype)

    t = _dot(h, widx_ref[0])
    scale = sidx_ref[...]
    for s in range(IDX_Q // LANES):
        sl = slice(s * LANES, (s + 1) * LANES)
        idx_ref[:, sl] = _rope_slab(t[:, sl], cos, sin) * scale[:, sl]
    sl = slice(IDX_Q, IDX_COLS)
    tk = t[:, sl]
    lane = lax.broadcasted_iota(jnp.int32, tk.shape, 1)
    is_k = lane < IDX_DIM
    ms = jnp.sum(jnp.where(is_k, tk * tk, 0.0), axis=-1, keepdims=True) * (1.0 / IDX_DIM)
    yk = tk * lax.rsqrt(ms + EPS) * gidx_ref[0]
    idx_ref[:, sl] = jnp.where(is_k, _rope_slab(yk, cos, sin), tk) * scale[:, sl]


def _front_call(x2, g_attn, w_qk, w_bqk, w_vt, w_idx, gain_qk, gain_idx, scale_idx, cos_t, sin_t,
                gmat, layer, tt):
    n, d = x2.shape
    tm = ROW_TILE
    row = lambda cols: pl.BlockSpec((tm, cols), lambda i: (i, 0))
    full = lambda w: pl.BlockSpec((1,) + w.shape[1:], lambda i: (layer, 0, 0))
    fixed = lambda a: pl.BlockSpec(a.shape, lambda i: (0, 0))
    cqk, cbqk, cvt = w_qk.shape[2], w_bqk.shape[2], w_vt.shape[2]
    return pl.pallas_call(
        _front_kernel,
        grid=(n // tm,),
        in_specs=[row(d), full(g_attn), full(w_qk), full(w_bqk), full(w_vt), full(w_idx),
                  full(gain_qk), full(gain_idx), fixed(scale_idx), row(LANES), row(LANES),
                  fixed(gmat)],
        out_specs=[row(d), row(cqk), row(cbqk),
                   pl.BlockSpec((tm // tt, cvt, tt), lambda i: (i, 0, 0)), row(IDX_COLS)],
        out_shape=[jax.ShapeDtypeStruct((n, d), BF16), jax.ShapeDtypeStruct((n, cqk), BF16),
                   jax.ShapeDtypeStruct((n, cbqk), BF16),
                   jax.ShapeDtypeStruct((n // tt, cvt, tt), BF16),
                   jax.ShapeDtypeStruct((n, IDX_COLS), F32)],
        compiler_params=_params("parallel"),
    )(x2, g_attn, w_qk, w_bqk, w_vt, w_idx, gain_qk, gain_idx, scale_idx, cos_t, sin_t, gmat)


def _load_qt(q_ref, qz_ref):
    qt = q_ref[0].astype(F32).T
    rows = lax.broadcasted_iota(jnp.int32, (LANES, qt.shape[1]), 0)
    for s in range(qt.shape[0] // LANES):
        slab = qt[s * LANES:(s + 1) * LANES, :]
        qz_ref[2 * s] = jnp.where(rows < HEAD_DIM, slab, 0.0).astype(BF16)
        qz_ref[2 * s + 1] = jnp.where(rows >= HEAD_DIM, slab, 0.0).astype(BF16)


ONES_ROWS = 16


def _softmax_step(n, s, vt, m_ref, l_ref, acc_ref):
    m_old = m_ref[n]
    m_new = jnp.maximum(m_old, jnp.max(s, axis=0, keepdims=True))
    p = jnp.exp2(s - m_new).astype(BF16)
    alpha = jnp.exp2(m_old - m_new)
    f = vt.shape[0]
    pv = _dot(jnp.concatenate([vt, jnp.ones((ONES_ROWS, vt.shape[1]), BF16)], axis=0), p)
    l_ref[n] = alpha * l_ref[n] + pv[f:f + 1]
    acc_ref[n] = alpha * acc_ref[n] + pv[:f]
    m_ref[n] = m_new


def _tile_ids(t):
    key = lax.broadcasted_iota(jnp.int32, (KEY_TILE, t), 0)
    qry = lax.broadcasted_iota(jnp.int32, (KEY_TILE, t), 1)
    return key, qry


def _key_rows(u):
    return pl.ds(pl.multiple_of(u * KEY_TILE, KEY_TILE), KEY_TILE)


def _interleave(*gens):
    gens = list(gens)
    while gens:
        for gen in list(gens):
            try:
                next(gen)
            except StopIteration:
                gens.remove(gen)


def _attn_ab_kernel(lam_ref, aq_ref, ak_ref, avt_ref, g_ref, bq_ref, bk_ref, bvt_ref, lmat_ref,
                    ya_ref, yb_ref, aqz_ref, m_ref, l_ref, aacc_ref, bqz_ref, r_ref, bacc_ref, *,
                    lam_init):
    t = aq_ref.shape[1]
    i = pl.program_id(1)
    lp = lam_ref[0]
    lam = (jnp.exp(jnp.sum(lp[0:1] * lp[1:2], axis=-1, keepdims=True))
           - jnp.exp(jnp.sum(lp[2:3] * lp[3:4], axis=-1, keepdims=True)) + lam_init)
    key, qry = _tile_ids(t)
    sub = t // KEY_TILE
    lmat = lmat_ref[...]
    _load_qt(aq_ref, aqz_ref)
    _load_qt(bq_ref, bqz_ref)
    m_ref[...] = jnp.full(m_ref.shape, NEG, F32)
    l_ref[...] = jnp.zeros(l_ref.shape, F32)
    aacc_ref[...] = jnp.zeros(aacc_ref.shape, F32)
    r_ref[...] = jnp.zeros(r_ref.shape, F32)
    bacc_ref[...] = jnp.zeros(bacc_ref.shape, F32)

    def a_group(g, diag):
        scores = []
        for r in range(sub):
            for n in range(2 * A_HEADS):
                kb = ak_ref[0, _key_rows(g * sub + r), (n // 2) * LANES:(n // 2 + 1) * LANES]
                scores.append(_dot(kb, aqz_ref[n]))
        yield
        for r in range(sub):
            for n in range(2 * A_HEADS):
                s = scores[r * 2 * A_HEADS + n]
                if diag:
                    s = jnp.where(((key + r * KEY_TILE) // CHUNK) <= (qry // CHUNK), s, NEG)
                vt = avt_ref[0, g * sub + r, (n // 2) * LANES:(n // 2 + 1) * LANES, :]
                _softmax_step(n, s, vt, m_ref, l_ref, aacc_ref)
                yield

    def b_group(g, diag):
        order = [(r, n) for r in reversed(range(sub)) for n in range(B_HEADS)]
        strict = {r: (key + r * KEY_TILE) < qry for r in range(sub)} if diag else None
        logits = {}
        for r, n in order:
            kb = bk_ref[0, _key_rows(g * sub + r), (n // 2) * LANES:(n // 2 + 1) * LANES]
            logits[r, n] = _dot(kb, bqz_ref[n])
        yield
        log_betas, laters, sums = {}, {}, {}
        for r, n in order:
            z = logits[r, n]
            neg_abs = pltpu.bitcast(pltpu.bitcast(z, jnp.int32) | jnp.int32(INT_MIN), F32)
            soft = jnp.log2(1.0 + jnp.exp2(neg_abs))
            log_beta = jnp.minimum(z, 0.0) - soft
            log_om = log_beta - z
            if diag:
                log_om = jnp.where(strict[r], log_om, 0.0)
            later = _dot(lmat, log_om.astype(BF16))
            laters[r, n] = later[:KEY_TILE]
            log_betas[r, n] = log_beta
            sums[r, n] = later[KEY_TILE:KEY_TILE + 1]
            yield
        for r, n in order:
            a = jnp.exp2(log_betas[r, n] + laters[r, n])
            if diag:
                a = jnp.where(strict[r], a, 0.0)
            vt = bvt_ref[0, g * sub + r, n * HEAD_DIM:(n + 1) * HEAD_DIM, :]
            visited = r_ref[n]
            bacc_ref[n] = bacc_ref[n] + _dot(vt, a.astype(BF16)) * jnp.exp2(visited)
            r_ref[n] = visited + sums[r, n]
            yield

    def body(n, carry):
        _interleave(a_group(n, False), b_group(i - 1 - n, False))
        return carry

    _interleave(b_group(i, True))
    lax.fori_loop(0, i, body, 0)
    _interleave(a_group(i, True))

    gain = g_ref[0] * (1.0 - lam_init)
    for h in range(A_HEADS):
        o = aacc_ref[2 * h] / l_ref[2 * h] - lam * (aacc_ref[2 * h + 1] / l_ref[2 * h + 1])
        ms = jnp.mean(o * o, axis=0, keepdims=True)
        y = (o * lax.rsqrt(ms + EPS)).T * gain
        ya_ref[0, :, h * LANES:(h + 1) * LANES] = y.astype(ya_ref.dtype)
    yb_ref[0] = bacc_ref[...].reshape(B_HEADS * HEAD_DIM, t).T.astype(yb_ref.dtype)


def _attn_ab_call(qk, bqk, vt, lam_p, subln, lmat, layer, lam_init, t):
    b, s, _ = qk.shape
    nt, _, kt = vt.shape[1:]
    ns = 2 * A_HEADS
    qspec = pl.BlockSpec((1, t, GROUP), lambda bi, i: (bi, i, 0))
    kspec = pl.BlockSpec((1, s, GROUP), lambda bi, i: (bi, 0, 1))
    vspec = lambda f: pl.BlockSpec((1, nt, GROUP, kt), lambda bi, i: (bi, 0, f, 0))
    return pl.pallas_call(
        functools.partial(_attn_ab_kernel, lam_init=lam_init),
        grid=(b, s // t),
        in_specs=[pl.BlockSpec((1, 4, HEAD_DIM), lambda bi, i: (layer, 0, 0)),
                  qspec, kspec, vspec(0),
                  pl.BlockSpec((1, 1, LANES), lambda bi, i: (layer, 0, 0)),
                  qspec, kspec, vspec(1),
                  pl.BlockSpec(lmat.shape, lambda bi, i: (0, 0))],
        out_specs=[qspec, qspec],
        out_shape=[jax.ShapeDtypeStruct((b, s, GROUP), BF16)] * 2,
        scratch_shapes=[pltpu.VMEM((ns, LANES, t), BF16), pltpu.VMEM((ns, 1, t), F32),
                        pltpu.VMEM((ns, 1, t), F32), pltpu.VMEM((ns, LANES, t), F32),
                        pltpu.VMEM((B_HEADS, LANES, t), BF16), pltpu.VMEM((B_HEADS, 1, t), F32),
                        pltpu.VMEM((B_HEADS, HEAD_DIM, t), F32)],
        compiler_params=_params("parallel", "arbitrary"),
    )(lam_p, qk, qk, vt, subln, bqk, bqk, vt, lmat)


def _fold_rows(x):
    parts = [x[r * SUBLANES:(r + 1) * SUBLANES] for r in range(x.shape[0] // SUBLANES)]
    while len(parts) > 1:
        parts = [a + b for a, b in zip(parts[0::2], parts[1::2])]
    return parts[0]


def _dsa_mask_kernel(q_ref, kw_ref, wq_ref, lt_ref, o_ref, key_ref, qz_ref, *, topk):
    t = q_ref.shape[1]
    nj = key_ref.shape[0]
    i = pl.program_id(1)
    nch = (i + 1) * (t // KEY_TILE)
    qt = q_ref[0].T
    pad = jnp.zeros((LANES - IDX_DIM, t), F32)
    for h in range(IDX_HEADS):
        qz_ref[h] = jnp.concatenate([qt[h * IDX_DIM:(h + 1) * IDX_DIM], pad], axis=0).astype(BF16)
    wt = wq_ref[0].T
    wh = [wt[IDX_DIM + h:IDX_DIM + h + 1, :] for h in range(IDX_HEADS)]
    key0, qry0 = _tile_ids(t)
    sub = t // KEY_TILE

    def chunk_ok(r):
        return ((key0 + r * KEY_TILE) // CHUNK) <= (qry0 // CHUNK)

    def fill(g, diag):
        for r in range(sub):
            c = g * sub + r
            kb = kw_ref[0, _key_rows(c), :].astype(BF16)
            score = jnp.zeros((KEY_TILE, t), F32)
            for h in range(IDX_HEADS):
                score = score + wh[h] * jnp.maximum(_dot(kb, qz_ref[h]), 0.0)
            if diag:
                score = jnp.where(chunk_ok(r), score, -jnp.inf)
            bits = pltpu.bitcast(score, jnp.int32)
            okey = jnp.where(bits < 0, bits ^ jnp.int32(0x7FFFFFFF), bits)
            key_ref[c] = jnp.where(score == 0.0, 0, okey)

    def fill_body(g, carry):
        fill(g, False)
        return carry

    lax.fori_loop(0, i, fill_body, 0)
    fill(i, True)

    def count(pred, thr):
        def body(g, acc):
            for r in range(sub):
                acc = acc + _fold_rows(jnp.where(pred(key_ref[g * sub + r], thr), 1.0, 0.0))
            return acc
        acc = lax.fori_loop(0, i + 1, body, jnp.zeros((SUBLANES, t), F32))
        return jnp.sum(acc, axis=0, keepdims=True)

    def bit_body(n, carry):
        lo, at_lo = carry
        cand = lo + jnp.left_shift(jnp.int32(1), 31 - n)
        cnt = count(lambda k, th: k >= th, cand)
        take = cnt >= topk
        return jnp.where(take, cand, lo), jnp.where(take, cnt, at_lo)

    thr, at_thr = lax.fori_loop(
        0, 32, bit_body,
        (jnp.full((1, t), INT_MIN, jnp.int32),
         jnp.broadcast_to((nch * KEY_TILE).astype(F32), (1, t))))

    def emit_plain(g, diag):
        for r in range(sub):
            c = g * sub + r
            sel = key_ref[c] >= thr
            if diag:
                sel = sel & chunk_ok(r)
            o_ref[0, 0, c] = jnp.where(sel, 0.0, NEG).astype(o_ref.dtype)

    def emit_ties(g, run, need, diag):
        lt = lt_ref[...]
        for r in range(sub):
            c = g * sub + r
            okey = key_ref[c]
            eq = okey == thr
            eqf = jnp.where(eq, 1.0, 0.0)
            before = _dot(lt, eqf.astype(BF16)) + run
            sel = (okey > thr) | (eq & (before < need))
            if diag:
                sel = sel & chunk_ok(r)
            o_ref[0, 0, c] = jnp.where(sel, 0.0, NEG).astype(o_ref.dtype)
            run = run + jnp.sum(eqf, axis=0, keepdims=True)
        return run

    def with_ties():
        need = topk - count(lambda k, th: k > th, thr)
        run = lax.fori_loop(0, i, lambda g, run: emit_ties(g, run, need, False),
                            jnp.zeros((1, t), F32))
        emit_ties(i, run, need, True)

    def without_ties():
        def body(g, carry):
            emit_plain(g, False)
            return carry
        lax.fori_loop(0, i, body, 0)
        emit_plain(i, True)

    lax.cond(jnp.max(at_thr) > topk, with_ties, without_ties)

    def blank(c, carry):
        o_ref[0, 0, c] = jnp.full((KEY_TILE, t), NEG, o_ref.dtype)
        return carry

    lax.fori_loop(nch, nj, blank, 0)


def _dsa_mask_call(idx, lt, topk, t):
    b, s, _ = idx.shape
    kt = lt.shape[0]
    assert t >= topk
    nt, nk = s // t, s // kt
    return pl.pallas_call(
        functools.partial(_dsa_mask_kernel, topk=topk),
        grid=(b, nt),
        in_specs=[pl.BlockSpec((1, t, IDX_Q), lambda bi, i: (bi, i, 0)),
                  pl.BlockSpec((1, s, LANES), lambda bi, i: (bi, 0, IDX_Q // LANES)),
                  pl.BlockSpec((1, t, LANES), lambda bi, i: (bi, i, IDX_Q // LANES)),
                  pl.BlockSpec((kt, kt), lambda bi, i: (0, 0))],
        out_specs=pl.BlockSpec((1, 1, nk, kt, t), lambda bi, i: (bi, i, 0, 0, 0)),
        out_shape=jax.ShapeDtypeStruct((b, nt, nk, kt, t), BF16),
        scratch_shapes=[pltpu.VMEM((nk, kt, t), jnp.int32), pltpu.VMEM((IDX_HEADS, LANES, t), BF16)],
        compiler_params=_params("parallel", "arbitrary"),
    )(idx, idx, idx, lt)


def _attn_c_kernel(q_ref, k_ref, vt_ref, b_ref, o_ref, qz_ref, m_ref, l_ref, acc_ref):
    t = q_ref.shape[1]
    i = pl.program_id(1)
    _load_qt(q_ref, qz_ref)
    m_ref[...] = jnp.full(m_ref.shape, NEG, F32)
    l_ref[...] = jnp.zeros(l_ref.shape, F32)
    acc_ref[...] = jnp.zeros(acc_ref.shape, F32)

    key, qry = _tile_ids(KEY_TILE)
    ident = jnp.where(key == qry, 1.0, 0.0).astype(BF16)

    sub = t // KEY_TILE

    def body(g, carry):
        scores = []
        for r in range(sub):
            u = g * sub + r
            bias = b_ref[0, 0, u]
            for n in range(C_HEADS):
                kb = k_ref[0, _key_rows(u), (n // 2) * LANES:(n // 2 + 1) * LANES]
                lhs = jnp.concatenate([kb, ident], axis=1)
                rhs = jnp.concatenate([qz_ref[n], bias], axis=0)
                scores.append(_dot(lhs, rhs))
        for r in range(sub):
            for n in range(C_HEADS):
                vt = vt_ref[0, g * sub + r, n * HEAD_DIM:(n + 1) * HEAD_DIM, :]
                _softmax_step(n, scores[r * C_HEADS + n], vt, m_ref, l_ref, acc_ref)
        return carry

    lax.fori_loop(0, i + 1, body, 0)
    o = acc_ref[...] / l_ref[...]
    o_ref[0] = o.reshape(C_HEADS * HEAD_DIM, t).T.astype(o_ref.dtype)


def _attn_c_call(qk, vt, bias):
    b, s, _ = qk.shape
    nk, _, kt = vt.shape[1:]
    nt, t = bias.shape[1], bias.shape[4]
    return pl.pallas_call(
        _attn_c_kernel,
        grid=(b, nt),
        in_specs=[pl.BlockSpec((1, t, GROUP), lambda bi, i: (bi, i, 2)),
                  pl.BlockSpec((1, s, GROUP), lambda bi, i: (bi, 0, 3)),
                  pl.BlockSpec((1, nk, GROUP, kt), lambda bi, i: (bi, 0, 2, 0)),
                  pl.BlockSpec((1, 1, nk, kt, t), lambda bi, i: (bi, i, 0, 0, 0))],
        out_specs=pl.BlockSpec((1, t, GROUP), lambda bi, i: (bi, i, 0)),
        out_shape=jax.ShapeDtypeStruct((b, s, GROUP), BF16),
        scratch_shapes=[pltpu.VMEM((C_HEADS, LANES, t), BF16), pltpu.VMEM((C_HEADS, 1, t), F32),
                        pltpu.VMEM((C_HEADS, 1, t), F32), pltpu.VMEM((C_HEADS, HEAD_DIM, t), F32)],
        compiler_params=_params("parallel", "arbitrary"),
    )(qk, qk, vt, bias)


def _merge_kernel(x_ref, h_ref, ya_ref, yb_ref, yc_ref, wg_ref, wa_ref, wb_ref, wc_ref, wo_ref, o_ref):
    d = x_ref.shape[1]
    h = h_ref[...]
    merged = jnp.zeros(x_ref.shape, F32)
    for n, (y_ref, w_ref) in enumerate(((ya_ref, wa_ref), (yb_ref, wb_ref), (yc_ref, wc_ref))):
        gate = jax.nn.sigmoid(_dot(h, wg_ref[0, :, n * d:(n + 1) * d]))
        merged = merged + gate * _dot(y_ref[...], w_ref[0])
    o_ref[...] = x_ref[...] + _dot(merged.astype(BF16), wo_ref[0])


def _merge_call(x2, h, ya, yb, yc, wg, wa, wb, wc, wo, layer):
    n, d = x2.shape
    tm = ROW_TILE
    row = lambda cols: pl.BlockSpec((tm, cols), lambda i: (i, 0))
    full = lambda w: pl.BlockSpec((1,) + w.shape[1:], lambda i: (layer, 0, 0))
    return pl.pallas_call(
        _merge_kernel,
        grid=(n // tm,),
        in_specs=[row(d), row(d), row(ya.shape[1]), row(yb.shape[1]), row(yc.shape[1]),
                  full(wg), full(wa), full(wb), full(wc), full(wo)],
        out_specs=row(d),
        out_shape=jax.ShapeDtypeStruct((n, d), F32),
        compiler_params=_params("parallel"),
    )(x2, h, ya, yb, yc, wg, wa, wb, wc, wo)


def _mlp_kernel(x_ref, p_ref, gm_ref, wu_ref, wd_ref, gp_ref, wpg_ref, wpp_ref, o_ref):
    x = x_ref[...]
    ms = jnp.mean(x * x, axis=-1, keepdims=True)
    h = (x * lax.rsqrt(ms + EPS) * gm_ref[0]).astype(BF16)
    for f in range(wu_ref.shape[2] // FF_TILE):
        u = jnp.maximum(_dot(h, wu_ref[0, :, f * FF_TILE:(f + 1) * FF_TILE]), 0.0)
        x = x + _dot((u * u).astype(BF16), wd_ref[0, f * FF_TILE:(f + 1) * FF_TILE, :])
    ms = jnp.mean(x * x, axis=-1, keepdims=True)
    hp = (x * lax.rsqrt(ms + EPS) * gp_ref[0]).astype(BF16)
    gate = jax.nn.sigmoid(_dot(hp, wpg_ref[0]))
    o_ref[...] = x + gate * _dot(p_ref[0].astype(BF16), wpp_ref[0])


def _mlp_call(x2, p3, g_mlp, w_up, w_down, g_ple, w_pg, w_pp, layer):
    n, d = x2.shape
    pd = p3.shape[2]
    tm = ROW_TILE
    full = lambda w: pl.BlockSpec((1,) + w.shape[1:], lambda i: (layer, 0, 0),
                                  pipeline_mode=pl.Buffered(1))
    return pl.pallas_call(
        _mlp_kernel,
        grid=(n // tm,),
        in_specs=[pl.BlockSpec((tm, d), lambda i: (i, 0)),
                  pl.BlockSpec((1, tm, pd), lambda i: (layer, i, 0)),
                  full(g_mlp), full(w_up), full(w_down), full(g_ple), full(w_pg), full(w_pp)],
        out_specs=pl.BlockSpec((tm, d), lambda i: (i, 0)),
        out_shape=jax.ShapeDtypeStruct((n, d), F32),
        compiler_params=_params("parallel"),
    )(x2, p3, g_mlp, w_up, w_down, g_ple, w_pg, w_pp)


def _tile_gain(g, reps, scale=1.0):
    return jnp.tile(g.astype(F32), (1, reps)) * scale


def kernel(x, p, positions, attn_norm, w_in, a_q_norm, a_k_norm, a_lambda, a_subln,
           c_q_norm, c_k_norm, idx_k_norm, w_br_a, w_br_b, w_br_c, w_out,
           mlp_norm, w_up, w_down, ple_norm, w_ple_gate, w_ple_proj):
    b, s, d = x.shape
    depth = w_in.shape[0]
    n = b * s
    topk = min(TOPK_MAX, s // 4)
    t = min(ATTN_TILE, s)
    scale = HEAD_DIM ** -0.5
    qscale = scale * LOG2E

    inv = ROPE_THETA ** (-jnp.arange(0, HEAD_DIM, 2, dtype=F32) / HEAD_DIM)
    ang = positions.astype(F32)[..., None] * inv
    cos, sin = jnp.cos(ang), jnp.sin(ang)
    cos_t = jnp.concatenate([cos, cos, cos, cos], axis=-1).reshape(n, LANES)
    sin_t = jnp.concatenate([-sin, sin, -sin, sin], axis=-1).reshape(n, LANES)

    qa, ka, va = 0, 512, 1024
    qb, kb, vb = 1536, 2048, 2560
    qc, kc, vc = 3072, 3584, 4096
    qi = 4608
    gl = 4932
    col = lambda a, width=GROUP: w_in[:, :, a:a + width]
    w_qk = jnp.concatenate([col(qa), col(ka), col(qc), col(kc)], axis=-1).astype(BF16)
    w_bqk = jnp.concatenate([col(qb) * qscale, col(kb)], axis=-1).astype(BF16)
    w_vt = jnp.concatenate([col(va), col(vb), col(vc)], axis=-1).astype(BF16)
    w_idx = jnp.pad(col(qi, gl - qi), ((0, 0), (0, 0), (0, IDX_COLS - (gl - qi)))).astype(BF16)
    w_gl = col(gl, N_BRANCHES * d).astype(BF16)
    heads = GROUP // HEAD_DIM
    gain_qk = jnp.concatenate([_tile_gain(a_q_norm, heads, qscale), _tile_gain(a_k_norm, heads),
                               _tile_gain(c_q_norm, heads, qscale), _tile_gain(c_k_norm, heads)],
                              axis=-1)[:, None, :]
    gain_idx = jnp.pad(idx_k_norm.astype(F32), ((0, 0), (0, LANES - IDX_DIM)))[:, None, :]
    scale_idx = jnp.concatenate([jnp.full((IDX_Q,), IDX_DIM ** -0.5, F32), jnp.ones((IDX_DIM,), F32),
                                 jnp.full((IDX_HEADS,), IDX_HEADS ** -0.5, F32),
                                 jnp.zeros((LANES - IDX_DIM - IDX_HEADS,), F32)])[None, :]
    gid = jnp.arange(GROUP) // HEAD_DIM
    gmat = (gid[:, None] == gid[None, :]).astype(BF16)
    kt = min(KEY_TILE, s)
    ar = jnp.arange(kt)
    later_mat = jnp.concatenate([(ar[None, :] > ar[:, None]).astype(BF16),
                                 jnp.ones((ONES_ROWS, kt), BF16)], axis=0)
    before_mat = (ar[None, :] < ar[:, None]).astype(BF16)

    wa, wb, wc, wo = (w.astype(BF16) for w in (w_br_a, w_br_b, w_br_c, w_out))
    wu, wd, wpg, wpp = (w.astype(BF16) for w in (w_up, w_down, w_ple_gate, w_ple_proj))
    g_attn, g_mlp, g_ple = (g.astype(F32)[:, None, :] for g in (attn_norm, mlp_norm, ple_norm))
    subln = a_subln.astype(F32)[:, None, :]
    lam_p = a_lambda.astype(F32)
    p3 = p.reshape(depth, n, p.shape[-1])

    x2 = x.reshape(n, d)
    for i in range(depth):
        lam_init = 0.8 - 0.6 * math.exp(-0.3 * i)
        h, qk, bqk, vt, idx = _front_call(x2, g_attn, w_qk, w_bqk, w_vt, w_idx, gain_qk, gain_idx,
                                          scale_idx, cos_t, sin_t, gmat, i, kt)
        qk, bqk, idx = (a.reshape(b, s, -1) for a in (qk, bqk, idx))
        vt = vt.reshape(b, s // kt, -1, kt)
        ya, yb = _attn_ab_call(qk, bqk, vt, lam_p, subln, later_mat, i, lam_init, t)
        bias = _dsa_mask_call(idx, before_mat, topk, t)
        yc = _attn_c_call(qk, vt, bias)
        x2 = _merge_call(x2, h, ya.reshape(n, -1), yb.reshape(n, -1), yc.reshape(n, -1),
                         w_gl, wa, wb, wc, wo, i)
        x2 = _mlp_call(x2, p3, g_mlp, wu, wd, g_ple, wpg, wpp, i)
    return x2.reshape(b, s, d)
```

```python
import functools
import math

import jax
import jax.numpy as jnp
from jax import lax
from jax.experimental import pallas as pl
from jax.experimental.pallas import tpu as pltpu

F32 = jnp.float32
BF16 = jnp.bfloat16

HEAD_DIM = 64
CHUNK = 64
A_HEADS = 4
B_HEADS = 8
C_HEADS = 8
IDX_HEADS = 4
IDX_DIM = 64
TOPK_MAX = 256
N_BRANCHES = 3
ROPE_THETA = 10000.0
EPS = 1e-6

LANES = 128
SUBLANES = 8
HALF_HEAD = HEAD_DIM // 2
NEG = -1e30
INT_MIN = -(2 ** 31)
LOG2E = 1.4426950408889634
VMEM_LIMIT = 56 * 1024 * 1024

ROW_TILE = 512
ATTN_TILE = 256
KEY_TILE = 128
FF_TILE = 1024
GROUP = 512


def _params(*sem):
    return pltpu.CompilerParams(dimension_semantics=sem, vmem_limit_bytes=VMEM_LIMIT)


def _dot(a, b):
    return jnp.dot(a, b, preferred_element_type=F32)


def _split(x):
    hi = x.astype(BF16)
    return hi, (x - hi.astype(F32)).astype(BF16)


def _swap_halves(y):
    lane = lax.broadcasted_iota(jnp.int32, y.shape, 1)
    return jnp.where((lane & HALF_HEAD) != 0, pltpu.roll(y, HALF_HEAD, 1),
                     pltpu.roll(y, LANES - HALF_HEAD, 1))


def _rope_slab(y, cos, sin):
    return y * cos + _swap_halves(y) * sin


IDX_Q = IDX_HEADS * IDX_DIM
IDX_COLS = IDX_Q + LANES


def _front_kernel(x_ref, g_ref, wqk_ref, wbqk_ref, wvt_ref, widx_ref, gqk_ref, gidx_ref, sidx_ref,
                  cos_ref, sin_ref, gmat_ref, h_ref, qk_ref, bqk_ref, vt_ref, idx_ref):
    x = x_ref[...]
    ms = jnp.mean(x * x, axis=-1, keepdims=True)
    h = (x * lax.rsqrt(ms + EPS) * g_ref[0]).astype(BF16)
    h_ref[...] = h
    cos = cos_ref[...]
    sin = sin_ref[...]

    gmat = gmat_ref[...]
    for j in range(qk_ref.shape[1] // GROUP):
        cols = slice(j * GROUP, (j + 1) * GROUP)
        t = _dot(h, wqk_ref[0, :, cols])
        hi, lo = _split(t * t)
        ms = (_dot(hi, gmat) + _dot(lo, gmat)) * (1.0 / HEAD_DIM)
        y = t * lax.rsqrt(ms + EPS) * gqk_ref[0, :, cols]
        for s in range(GROUP // LANES):
            sl = slice(s * LANES, (s + 1) * LANES)
            qk_ref[:, j * GROUP + s * LANES:j * GROUP + (s + 1) * LANES] = (
                _rope_slab(y[:, sl], cos, sin).astype(qk_ref.dtype))

    for j in range(bqk_ref.shape[1] // GROUP):
        cols = slice(j * GROUP, (j + 1) * GROUP)
        bqk_ref[:, cols] = _dot(h, wbqk_ref[0, :, cols]).astype(bqk_ref.dtype)

    tt = vt_ref.shape[2]
    for j in range(vt_ref.shape[1] // GROUP):
        cols = slice(j * GROUP, (j + 1) * GROUP)
        t = _dot(h, wvt_ref[0, :, cols])
        for r in range(vt_ref.shape[0]):
            vt_ref[r, cols, :] = t[r * tt:(r + 1) * tt, :].T.astype(vt_ref.dtype)

    t = _dot(h, widx_ref[0])
    scale = sidx_ref[...]
    for s in range(IDX_Q // LANES):
        sl = slice(s * LANES, (s + 1) * LANES)
        idx_ref[:, sl] = _rope_slab(t[:, sl], cos, sin) * scale[:, sl]
    sl = slice(IDX_Q, IDX_COLS)
    tk = t[:, sl]
    lane = lax.broadcasted_iota(jnp.int32, tk.shape, 1)
    is_k = lane < IDX_DIM
    ms = jnp.sum(jnp.where(is_k, tk * tk, 0.0), axis=-1, keepdims=True) * (1.0 / IDX_DIM)
    yk = tk * lax.rsqrt(ms + EPS) * gidx_ref[0]
    idx_ref[:, sl] = jnp.where(is_k, _rope_slab(yk, cos, sin), tk) * scale[:, sl]


def _front_call(x2, g_attn, w_qk, w_bqk, w_vt, w_idx, gain_qk, gain_idx, scale_idx, cos_t, sin_t,
                gmat, layer, tt):
    n, d = x2.shape
    tm = ROW_TILE
    row = lambda cols: pl.BlockSpec((tm, cols), lambda i: (i, 0))
    full = lambda w: pl.BlockSpec((1,) + w.shape[1:], lambda i: (layer, 0, 0))
    fixed = lambda a: pl.BlockSpec(a.shape, lambda i: (0, 0))
    cqk, cbqk, cvt = w_qk.shape[2], w_bqk.shape[2], w_vt.shape[2]
    return pl.pallas_call(
        _front_kernel,
        grid=(n // tm,),
        in_specs=[row(d), full(g_attn), full(w_qk), full(w_bqk), full(w_vt), full(w_idx),
                  full(gain_qk), full(gain_idx), fixed(scale_idx), row(LANES), row(LANES),
                  fixed(gmat)],
        out_specs=[row(d), row(cqk), row(cbqk),
                   pl.BlockSpec((tm // tt, cvt, tt), lambda i: (i, 0, 0)), row(IDX_COLS)],
        out_shape=[jax.ShapeDtypeStruct((n, d), BF16), jax.ShapeDtypeStruct((n, cqk), BF16),
                   jax.ShapeDtypeStruct((n, cbqk), BF16),
                   jax.ShapeDtypeStruct((n // tt, cvt, tt), BF16),
                   jax.ShapeDtypeStruct((n, IDX_COLS), F32)],
        compiler_params=_params("parallel"),
    )(x2, g_attn, w_qk, w_bqk, w_vt, w_idx, gain_qk, gain_idx, scale_idx, cos_t, sin_t, gmat)


def _load_qt(q_ref, qz_ref):
    qt = q_ref[0].astype(F32).T
    rows = lax.broadcasted_iota(jnp.int32, (LANES, qt.shape[1]), 0)
    for s in range(qt.shape[0] // LANES):
        slab = qt[s * LANES:(s + 1) * LANES, :]
        qz_ref[2 * s] = jnp.where(rows < HEAD_DIM, slab, 0.0).astype(BF16)
        qz_ref[2 * s + 1] = jnp.where(rows >= HEAD_DIM, slab, 0.0).astype(BF16)


ONES_ROWS = 16


def _softmax_step(n, s, vt, m_ref, l_ref, acc_ref):
    m_old = m_ref[n]
    m_new = jnp.maximum(m_old, jnp.max(s, axis=0, keepdims=True))
    p = jnp.exp2(s - m_new).astype(BF16)
    alpha = jnp.exp2(m_old - m_new)
    f = vt.shape[0]
    pv = _dot(jnp.concatenate([vt, jnp.ones((ONES_ROWS, vt.shape[1]), BF16)], axis=0), p)
    l_ref[n] = alpha * l_ref[n] + pv[f:f + 1]
    acc_ref[n] = alpha * acc_ref[n] + pv[:f]
    m_ref[n] = m_new


def _tile_ids(t):
    key = lax.broadcasted_iota(jnp.int32, (KEY_TILE, t), 0)
    qry = lax.broadcasted_iota(jnp.int32, (KEY_TILE, t), 1)
    return key, qry


def _key_rows(u):
    return pl.ds(pl.multiple_of(u * KEY_TILE, KEY_TILE), KEY_TILE)


def _interleave(*gens):
    gens = list(gens)
    while gens:
        for gen in list(gens):
            try:
                next(gen)
            except StopIteration:
                gens.remove(gen)


def _attn_ab_kernel(lam_ref, aq_ref, ak_ref, avt_ref, g_ref, bq_ref, bk_ref, bvt_ref, lmat_ref,
                    ya_ref, yb_ref, aqz_ref, m_ref, l_ref, aacc_ref, bqz_ref, r_ref, bacc_ref, *,
                    lam_init):
    t = aq_ref.shape[1]
    i = pl.program_id(1)
    lp = lam_ref[0]
    lam = (jnp.exp(jnp.sum(lp[0:1] * lp[1:2], axis=-1, keepdims=True))
           - jnp.exp(jnp.sum(lp[2:3] * lp[3:4], axis=-1, keepdims=True)) + lam_init)
    key, qry = _tile_ids(t)
    sub = t // KEY_TILE
    lmat = lmat_ref[...]
    _load_qt(aq_ref, aqz_ref)
    _load_qt(bq_ref, bqz_ref)
    m_ref[...] = jnp.full(m_ref.shape, NEG, F32)
    l_ref[...] = jnp.zeros(l_ref.shape, F32)
    aacc_ref[...] = jnp.zeros(aacc_ref.shape, F32)
    r_ref[...] = jnp.zeros(r_ref.shape, F32)
    bacc_ref[...] = jnp.zeros(bacc_ref.shape, F32)

    def a_group(g, diag):
        scores = []
        for r in range(sub):
            for n in range(2 * A_HEADS):
                kb = ak_ref[0, _key_rows(g * sub + r), (n // 2) * LANES:(n // 2 + 1) * LANES]
                scores.append(_dot(kb, aqz_ref[n]))
        yield
        for r in range(sub):
            for n in range(2 * A_HEADS):
                s = scores[r * 2 * A_HEADS + n]
                if diag:
                    s = jnp.where(((key + r * KEY_TILE) // CHUNK) <= (qry // CHUNK), s, NEG)
                vt = avt_ref[0, g * sub + r, (n // 2) * LANES:(n // 2 + 1) * LANES, :]
                _softmax_step(n, s, vt, m_ref, l_ref, aacc_ref)
                yield

    def b_group(g, diag):
        order = [(r, n) for r in reversed(range(sub)) for n in range(B_HEADS)]
        strict = {r: (key + r * KEY_TILE) < qry for r in range(sub)} if diag else None
        logits = {}
        for r, n in order:
            kb = bk_ref[0, _key_rows(g * sub + r), (n // 2) * LANES:(n // 2 + 1) * LANES]
            logits[r, n] = _dot(kb, bqz_ref[n])
        yield
        laters, sums = {}, {}
        for r, n in order:
            z = logits[r, n]
            neg_abs = pltpu.bitcast(pltpu.bitcast(z, jnp.int32) | jnp.int32(INT_MIN), F32)
            cost = jnp.maximum(z, 0.0) + jnp.log2(1.0 + jnp.exp2(neg_abs))
            if diag:
                cost = jnp.where(strict[r], cost, 0.0)
            later = _dot(lmat, cost.astype(BF16))
            laters[r, n] = later[:KEY_TILE]
            sums[r, n] = later[KEY_TILE:KEY_TILE + 1]
            yield
        for r, n in order:
            a = jnp.exp2(logits[r, n] + laters[r, n])
            if diag:
                a = jnp.where(strict[r], a, 0.0)
            vt = bvt_ref[0, g * sub + r, n * HEAD_DIM:(n + 1) * HEAD_DIM, :]
            visited = r_ref[n]
            bacc_ref[n] = bacc_ref[n] + _dot(vt, a.astype(BF16)) * jnp.exp2(visited)
            r_ref[n] = visited + sums[r, n]
            yield

    def body(n, carry):
        _interleave(a_group(n, False), b_group(i - 1 - n, False))
        return carry

    _interleave(b_group(i, True))
    lax.fori_loop(0, i, body, 0)
    _interleave(a_group(i, True))

    gain = g_ref[0] * (1.0 - lam_init)
    for h in range(A_HEADS):
        o = aacc_ref[2 * h] / l_ref[2 * h] - lam * (aacc_ref[2 * h + 1] / l_ref[2 * h + 1])
        ms = jnp.mean(o * o, axis=0, keepdims=True)
        y = (o * lax.rsqrt(ms + EPS)).T * gain
        ya_ref[0, :, h * LANES:(h + 1) * LANES] = y.astype(ya_ref.dtype)
    yb_ref[0] = bacc_ref[...].reshape(B_HEADS * HEAD_DIM, t).T.astype(yb_ref.dtype)


def _attn_ab_call(qk, bqk, vt, lam_p, subln, lmat, layer, lam_init, t):
    b, s, _ = qk.shape
    nt, _, kt = vt.shape[1:]
    ns = 2 * A_HEADS
    qspec = pl.BlockSpec((1, t, GROUP), lambda bi, i: (bi, i, 0))
    kspec = pl.BlockSpec((1, s, GROUP), lambda bi, i: (bi, 0, 1))
    vspec = lambda f: pl.BlockSpec((1, nt, GROUP, kt), lambda bi, i: (bi, 0, f, 0))
    return pl.pallas_call(
        functools.partial(_attn_ab_kernel, lam_init=lam_init),
        grid=(b, s // t),
        in_specs=[pl.BlockSpec((1, 4, HEAD_DIM), lambda bi, i: (layer, 0, 0)),
                  qspec, kspec, vspec(0),
                  pl.BlockSpec((1, 1, LANES), lambda bi, i: (layer, 0, 0)),
                  qspec, kspec, vspec(1),
                  pl.BlockSpec(lmat.shape, lambda bi, i: (0, 0))],
        out_specs=[qspec, qspec],
        out_shape=[jax.ShapeDtypeStruct((b, s, GROUP), BF16)] * 2,
        scratch_shapes=[pltpu.VMEM((ns, LANES, t), BF16), pltpu.VMEM((ns, 1, t), F32),
                        pltpu.VMEM((ns, 1, t), F32), pltpu.VMEM((ns, LANES, t), F32),
                        pltpu.VMEM((B_HEADS, LANES, t), BF16), pltpu.VMEM((B_HEADS, 1, t), F32),
                        pltpu.VMEM((B_HEADS, HEAD_DIM, t), F32)],
        compiler_params=_params("parallel", "arbitrary"),
    )(lam_p, qk, qk, vt, subln, bqk, bqk, vt, lmat)


def _fold_rows(x):
    parts = [x[r * SUBLANES:(r + 1) * SUBLANES] for r in range(x.shape[0] // SUBLANES)]
    while len(parts) > 1:
        parts = [a + b for a, b in zip(parts[0::2], parts[1::2])]
    return parts[0]


def _dsa_mask_kernel(q_ref, kw_ref, wq_ref, lt_ref, o_ref, key_ref, qz_ref, *, topk):
    t = q_ref.shape[1]
    nj = key_ref.shape[0]
    i = pl.program_id(1)
    nch = (i + 1) * (t // KEY_TILE)
    qt = q_ref[0].T
    pad = jnp.zeros((LANES - IDX_DIM, t), F32)
    for h in range(IDX_HEADS):
        qz_ref[h] = jnp.concatenate([qt[h * IDX_DIM:(h + 1) * IDX_DIM], pad], axis=0).astype(BF16)
    wt = wq_ref[0].T
    wh = [wt[IDX_DIM + h:IDX_DIM + h + 1, :] for h in range(IDX_HEADS)]
    key0, qry0 = _tile_ids(t)
    sub = t // KEY_TILE

    def chunk_ok(r):
        return ((key0 + r * KEY_TILE) // CHUNK) <= (qry0 // CHUNK)

    def fill(g, diag):
        for r in range(sub):
            c = g * sub + r
            kb = kw_ref[0, _key_rows(c), :].astype(BF16)
            score = jnp.zeros((KEY_TILE, t), F32)
            for h in range(IDX_HEADS):
                score = score + wh[h] * jnp.maximum(_dot(kb, qz_ref[h]), 0.0)
            if diag:
                score = jnp.where(chunk_ok(r), score, -jnp.inf)
            bits = pltpu.bitcast(score, jnp.int32)
            okey = jnp.where(bits < 0, bits ^ jnp.int32(0x7FFFFFFF), bits)
            key_ref[c] = jnp.where(score == 0.0, 0, okey)

    def in_pairs(one, carry):
        carry = lax.fori_loop(0, i // 2, lambda p, c: one(2 * p + 1, one(2 * p, c)), carry)
        return lax.cond(i % 2 == 1, lambda c: one(i - 1, c), lambda c: c, carry)

    def fill_one(g, carry):
        fill(g, False)
        return carry

    in_pairs(fill_one, 0)
    fill(i, True)

    def count(pred, thr):
        def body(g, acc):
            for r in range(sub):
                acc = acc + _fold_rows(jnp.where(pred(key_ref[g * sub + r], thr), 1.0, 0.0))
            return acc
        acc = lax.fori_loop(0, i + 1, body, jnp.zeros((SUBLANES, t), F32))
        return jnp.sum(acc, axis=0, keepdims=True)

    def bit_body(n, lo):
        cand = lo + jnp.left_shift(jnp.int32(1), 31 - n)
        cnt = count(lambda k, th: k >= th, cand)
        return jnp.where(cnt >= topk, cand, lo)

    thr = lax.fori_loop(0, 32, bit_body, jnp.full((1, t), INT_MIN, jnp.int32))
    need = topk - count(lambda k, th: k > th, thr)
    lt = lt_ref[...]

    def emit(g, run, diag):
        for r in range(sub):
            c = g * sub + r
            okey = key_ref[c]
            eq = okey == thr
            eqf = jnp.where(eq, 1.0, 0.0)
            before = _dot(lt, eqf.astype(BF16)) + run
            sel = (okey > thr) | (eq & (before < need))
            if diag:
                sel = sel & chunk_ok(r)
            o_ref[0, 0, c] = jnp.where(sel, 0.0, NEG).astype(o_ref.dtype)
            run = run + jnp.sum(eqf, axis=0, keepdims=True)
        return run

    run = in_pairs(lambda g, run: emit(g, run, False), jnp.zeros((1, t), F32))
    emit(i, run, True)

    def blank(c, carry):
        o_ref[0, 0, c] = jnp.full((KEY_TILE, t), NEG, o_ref.dtype)
        return carry

    lax.fori_loop(nch, nj, blank, 0)


def _dsa_mask_call(idx, lt, topk, t):
    b, s, _ = idx.shape
    kt = lt.shape[0]
    assert t >= topk
    nt, nk = s // t, s // kt
    return pl.pallas_call(
        functools.partial(_dsa_mask_kernel, topk=topk),
        grid=(b, nt),
        in_specs=[pl.BlockSpec((1, t, IDX_Q), lambda bi, i: (bi, i, 0)),
                  pl.BlockSpec((1, s, LANES), lambda bi, i: (bi, 0, IDX_Q // LANES)),
                  pl.BlockSpec((1, t, LANES), lambda bi, i: (bi, i, IDX_Q // LANES)),
                  pl.BlockSpec((kt, kt), lambda bi, i: (0, 0))],
        out_specs=pl.BlockSpec((1, 1, nk, kt, t), lambda bi, i: (bi, i, 0, 0, 0)),
        out_shape=jax.ShapeDtypeStruct((b, nt, nk, kt, t), BF16),
        scratch_shapes=[pltpu.VMEM((nk, kt, t), jnp.int32), pltpu.VMEM((IDX_HEADS, LANES, t), BF16)],
        compiler_params=_params("parallel", "arbitrary"),
    )(idx, idx, idx, lt)


def _attn_c_kernel(q_ref, k_ref, vt_ref, b_ref, o_ref, qz_ref, m_ref, l_ref, acc_ref):
    t = q_ref.shape[1]
    i = pl.program_id(1)
    _load_qt(q_ref, qz_ref)
    m_ref[...] = jnp.full(m_ref.shape, NEG, F32)
    l_ref[...] = jnp.zeros(l_ref.shape, F32)
    acc_ref[...] = jnp.zeros(acc_ref.shape, F32)

    key, qry = _tile_ids(KEY_TILE)
    ident = jnp.where(key == qry, 1.0, 0.0).astype(BF16)

    sub = t // KEY_TILE

    def body(g, carry):
        scores = []
        for r in range(sub):
            u = g * sub + r
            bias = b_ref[0, 0, u]
            for n in range(C_HEADS):
                kb = k_ref[0, _key_rows(u), (n // 2) * LANES:(n // 2 + 1) * LANES]
                lhs = jnp.concatenate([kb, ident], axis=1)
                rhs = jnp.concatenate([qz_ref[n], bias], axis=0)
                scores.append(_dot(lhs, rhs))
        for r in range(sub):
            for n in range(C_HEADS):
                vt = vt_ref[0, g * sub + r, n * HEAD_DIM:(n + 1) * HEAD_DIM, :]
                _softmax_step(n, scores[r * C_HEADS + n], vt, m_ref, l_ref, acc_ref)
        return carry

    lax.fori_loop(0, i + 1, body, 0)
    o = acc_ref[...] / l_ref[...]
    o_ref[0] = o.reshape(C_HEADS * HEAD_DIM, t).T.astype(o_ref.dtype)


def _attn_c_call(qk, vt, bias):
    b, s, _ = qk.shape
    nk, _, kt = vt.shape[1:]
    nt, t = bias.shape[1], bias.shape[4]
    return pl.pallas_call(
        _attn_c_kernel,
        grid=(b, nt),
        in_specs=[pl.BlockSpec((1, t, GROUP), lambda bi, i: (bi, i, 2)),
                  pl.BlockSpec((1, s, GROUP), lambda bi, i: (bi, 0, 3)),
                  pl.BlockSpec((1, nk, GROUP, kt), lambda bi, i: (bi, 0, 2, 0)),
                  pl.BlockSpec((1, 1, nk, kt, t), lambda bi, i: (bi, i, 0, 0, 0))],
        out_specs=pl.BlockSpec((1, t, GROUP), lambda bi, i: (bi, i, 0)),
        out_shape=jax.ShapeDtypeStruct((b, s, GROUP), BF16),
        scratch_shapes=[pltpu.VMEM((C_HEADS, LANES, t), BF16), pltpu.VMEM((C_HEADS, 1, t), F32),
                        pltpu.VMEM((C_HEADS, 1, t), F32), pltpu.VMEM((C_HEADS, HEAD_DIM, t), F32)],
        compiler_params=_params("parallel", "arbitrary"),
    )(qk, qk, vt, bias)


def _merge_kernel(x_ref, h_ref, ya_ref, yb_ref, yc_ref, wg_ref, wa_ref, wb_ref, wc_ref, wo_ref, o_ref):
    d = x_ref.shape[1]
    h = h_ref[...]
    merged = jnp.zeros(x_ref.shape, F32)
    for n, (y_ref, w_ref) in enumerate(((ya_ref, wa_ref), (yb_ref, wb_ref), (yc_ref, wc_ref))):
        gate = jax.nn.sigmoid(_dot(h, wg_ref[0, :, n * d:(n + 1) * d]))
        merged = merged + gate * _dot(y_ref[...], w_ref[0])
    o_ref[...] = x_ref[...] + _dot(merged.astype(BF16), wo_ref[0])


def _merge_call(x2, h, ya, yb, yc, wg, wa, wb, wc, wo, layer):
    n, d = x2.shape
    tm = ROW_TILE
    row = lambda cols: pl.BlockSpec((tm, cols), lambda i: (i, 0))
    full = lambda w: pl.BlockSpec((1,) + w.shape[1:], lambda i: (layer, 0, 0))
    return pl.pallas_call(
        _merge_kernel,
        grid=(n // tm,),
        in_specs=[row(d), row(d), row(ya.shape[1]), row(yb.shape[1]), row(yc.shape[1]),
                  full(wg), full(wa), full(wb), full(wc), full(wo)],
        out_specs=row(d),
        out_shape=jax.ShapeDtypeStruct((n, d), F32),
        compiler_params=_params("parallel"),
    )(x2, h, ya, yb, yc, wg, wa, wb, wc, wo)


def _mlp_kernel(x_ref, p_ref, gm_ref, wu_ref, wd_ref, gp_ref, wpg_ref, wpp_ref, o_ref):
    x = x_ref[...]
    ms = jnp.mean(x * x, axis=-1, keepdims=True)
    h = (x * lax.rsqrt(ms + EPS) * gm_ref[0]).astype(BF16)
    for f in range(wu_ref.shape[2] // FF_TILE):
        u = jnp.maximum(_dot(h, wu_ref[0, :, f * FF_TILE:(f + 1) * FF_TILE]), 0.0)
        x = x + _dot((u * u).astype(BF16), wd_ref[0, f * FF_TILE:(f + 1) * FF_TILE, :])
    ms = jnp.mean(x * x, axis=-1, keepdims=True)
    hp = (x * lax.rsqrt(ms + EPS) * gp_ref[0]).astype(BF16)
    gate = jax.nn.sigmoid(_dot(hp, wpg_ref[0]))
    o_ref[...] = x + gate * _dot(p_ref[0].astype(BF16), wpp_ref[0])


def _mlp_call(x2, p3, g_mlp, w_up, w_down, g_ple, w_pg, w_pp, layer):
    n, d = x2.shape
    pd = p3.shape[2]
    tm = ROW_TILE
    full = lambda w: pl.BlockSpec((1,) + w.shape[1:], lambda i: (layer, 0, 0),
                                  pipeline_mode=pl.Buffered(1))
    return pl.pallas_call(
        _mlp_kernel,
        grid=(n // tm,),
        in_specs=[pl.BlockSpec((tm, d), lambda i: (i, 0)),
                  pl.BlockSpec((1, tm, pd), lambda i: (layer, i, 0)),
                  full(g_mlp), full(w_up), full(w_down), full(g_ple), full(w_pg), full(w_pp)],
        out_specs=pl.BlockSpec((tm, d), lambda i: (i, 0)),
        out_shape=jax.ShapeDtypeStruct((n, d), F32),
        compiler_params=_params("parallel"),
    )(x2, p3, g_mlp, w_up, w_down, g_ple, w_pg, w_pp)


def _tile_gain(g, reps, scale=1.0):
    return jnp.tile(g.astype(F32), (1, reps)) * scale


def kernel(x, p, positions, attn_norm, w_in, a_q_norm, a_k_norm, a_lambda, a_subln,
           c_q_norm, c_k_norm, idx_k_norm, w_br_a, w_br_b, w_br_c, w_out,
           mlp_norm, w_up, w_down, ple_norm, w_ple_gate, w_ple_proj):
    b, s, d = x.shape
    depth = w_in.shape[0]
    n = b * s
    topk = min(TOPK_MAX, s // 4)
    t = min(ATTN_TILE, s)
    scale = HEAD_DIM ** -0.5
    qscale = scale * LOG2E

    inv = ROPE_THETA ** (-jnp.arange(0, HEAD_DIM, 2, dtype=F32) / HEAD_DIM)
    ang = positions.astype(F32)[..., None] * inv
    cos, sin = jnp.cos(ang), jnp.sin(ang)
    cos_t = jnp.concatenate([cos, cos, cos, cos], axis=-1).reshape(n, LANES)
    sin_t = jnp.concatenate([-sin, sin, -sin, sin], axis=-1).reshape(n, LANES)

    qa, ka, va = 0, 512, 1024
    qb, kb, vb = 1536, 2048, 2560
    qc, kc, vc = 3072, 3584, 4096
    qi = 4608
    gl = 4932
    col = lambda a, width=GROUP: w_in[:, :, a:a + width]
    w_qk = jnp.concatenate([col(qa), col(ka), col(qc), col(kc)], axis=-1).astype(BF16)
    w_bqk = jnp.concatenate([col(qb) * qscale, col(kb)], axis=-1).astype(BF16)
    w_vt = jnp.concatenate([col(va), col(vb), col(vc)], axis=-1).astype(BF16)
    w_idx = jnp.pad(col(qi, gl - qi), ((0, 0), (0, 0), (0, IDX_COLS - (gl - qi)))).astype(BF16)
    w_gl = col(gl, N_BRANCHES * d).astype(BF16)
    heads = GROUP // HEAD_DIM
    gain_qk = jnp.concatenate([_tile_gain(a_q_norm, heads, qscale), _tile_gain(a_k_norm, heads),
                               _tile_gain(c_q_norm, heads, qscale), _tile_gain(c_k_norm, heads)],
                              axis=-1)[:, None, :]
    gain_idx = jnp.pad(idx_k_norm.astype(F32), ((0, 0), (0, LANES - IDX_DIM)))[:, None, :]
    scale_idx = jnp.concatenate([jnp.full((IDX_Q,), IDX_DIM ** -0.5, F32), jnp.ones((IDX_DIM,), F32),
                                 jnp.full((IDX_HEADS,), IDX_HEADS ** -0.5, F32),
                                 jnp.zeros((LANES - IDX_DIM - IDX_HEADS,), F32)])[None, :]
    gid = jnp.arange(GROUP) // HEAD_DIM
    gmat = (gid[:, None] == gid[None, :]).astype(BF16)
    kt = min(KEY_TILE, s)
    ar = jnp.arange(kt)
    later_mat = -jnp.concatenate([(ar[None, :] >= ar[:, None]).astype(BF16),
                                  jnp.ones((ONES_ROWS, kt), BF16)], axis=0)
    before_mat = (ar[None, :] < ar[:, None]).astype(BF16)

    wa, wb, wc, wo = (w.astype(BF16) for w in (w_br_a, w_br_b, w_br_c, w_out))
    wu, wd, wpg, wpp = (w.astype(BF16) for w in (w_up, w_down, w_ple_gate, w_ple_proj))
    g_attn, g_mlp, g_ple = (g.astype(F32)[:, None, :] for g in (attn_norm, mlp_norm, ple_norm))
    subln = a_subln.astype(F32)[:, None, :]
    lam_p = a_lambda.astype(F32)
    p3 = p.reshape(depth, n, p.shape[-1])

    x2 = x.reshape(n, d)
    for i in range(depth):
        lam_init = 0.8 - 0.6 * math.exp(-0.3 * i)
        h, qk, bqk, vt, idx = _front_call(x2, g_attn, w_qk, w_bqk, w_vt, w_idx, gain_qk, gain_idx,
                                          scale_idx, cos_t, sin_t, gmat, i, kt)
        qk, bqk, idx = (a.reshape(b, s, -1) for a in (qk, bqk, idx))
        vt = vt.reshape(b, s // kt, -1, kt)
        ya, yb = _attn_ab_call(qk, bqk, vt, lam_p, subln, later_mat, i, lam_init, t)
        bias = _dsa_mask_call(idx, before_mat, topk, t)
        yc = _attn_c_call(qk, vt, bias)
        x2 = _merge_call(x2, h, ya.reshape(n, -1), yb.reshape(n, -1), yc.reshape(n, -1),
                         w_gl, wa, wb, wc, wo, i)
        x2 = _mlp_call(x2, p3, g_mlp, wu, wd, g_ple, wpg, wpp, i)
    return x2.reshape(b, s, d)
```

```python
import functools
import math

import jax
import jax.numpy as jnp
from jax import lax
from jax.experimental import pallas as pl
from jax.experimental.pallas import tpu as pltpu

F32 = jnp.float32
BF16 = jnp.bfloat16

HEAD_DIM = 64
CHUNK = 64
A_HEADS = 4
B_HEADS = 8
C_HEADS = 8
IDX_HEADS = 4
IDX_DIM = 64
TOPK_MAX = 256
N_BRANCHES = 3
ROPE_THETA = 10000.0
EPS = 1e-6

LANES = 128
SUBLANES = 8
HALF_HEAD = HEAD_DIM // 2
NEG = -1e30
INT_MIN = -(2 ** 31)
LOG2E = 1.4426950408889634
VMEM_LIMIT = 56 * 1024 * 1024

ROW_TILE = 512
ATTN_TILE = 256
KEY_TILE = 128
FF_TILE = 1024
GROUP = 512


def _params(*sem):
    return pltpu.CompilerParams(dimension_semantics=sem, vmem_limit_bytes=VMEM_LIMIT)


def _dot(a, b):
    return jnp.dot(a, b, preferred_element_type=F32)


def _split(x):
    hi = x.astype(BF16)
    return hi, (x - hi.astype(F32)).astype(BF16)


def _swap_halves(y):
    lane = lax.broadcasted_iota(jnp.int32, y.shape, 1)
    return jnp.where((lane & HALF_HEAD) != 0, pltpu.roll(y, HALF_HEAD, 1),
                     pltpu.roll(y, LANES - HALF_HEAD, 1))


def _rope_slab(y, cos, sin):
    return y * cos + _swap_halves(y) * sin


IDX_Q = IDX_HEADS * IDX_DIM
IDX_COLS = IDX_Q + LANES


def _front_kernel(x_ref, g_ref, wqk_ref, wbqk_ref, wvt_ref, widx_ref, gqk_ref, gidx_ref, sidx_ref,
                  cos_ref, sin_ref, gmat_ref, h_ref, qk_ref, bqk_ref, vt_ref, idx_ref):
    x = x_ref[...]
    ms = jnp.mean(x * x, axis=-1, keepdims=True)
    h = (x * lax.rsqrt(ms + EPS) * g_ref[0]).astype(BF16)
    h_ref[...] = h
    cos = cos_ref[...]
    sin = sin_ref[...]

    gmat = gmat_ref[...]
    for j in range(qk_ref.shape[1] // GROUP):
        cols = slice(j * GROUP, (j + 1) * GROUP)
        t = _dot(h, wqk_ref[0, :, cols])
        hi, lo = _split(t * t)
        ms = (_dot(hi, gmat) + _dot(lo, gmat)) * (1.0 / HEAD_DIM)
        y = t * lax.rsqrt(ms + EPS) * gqk_ref[0, :, cols]
        for s in range(GROUP // LANES):
            sl = slice(s * LANES, (s + 1) * LANES)
            qk_ref[:, j * GROUP + s * LANES:j * GROUP + (s + 1) * LANES] = (
                _rope_slab(y[:, sl], cos, sin).astype(qk_ref.dtype))

    for j in range(bqk_ref.shape[1] // GROUP):
        cols = slice(j * GROUP, (j + 1) * GROUP)
        bqk_ref[:, cols] = _dot(h, wbqk_ref[0, :, cols]).astype(bqk_ref.dtype)

    tt = vt_ref.shape[2]
    for j in range(vt_ref.shape[1] // GROUP):
        cols = slice(j * GROUP, (j + 1) * GROUP)
        t = _dot(h, wvt_ref[0, :, cols])
        for r in range(vt_ref.shape[0]):
            vt_ref[r, cols, :] = t[r * tt:(r + 1) * tt, :].T.astype(vt_ref.dtype)

    t = _dot(h, widx_ref[0])
    scale = sidx_ref[...]
    for s in range(IDX_Q // LANES):
        sl = slice(s * LANES, (s + 1) * LANES)
        idx_ref[:, sl] = _rope_slab(t[:, sl], cos, sin) * scale[:, sl]
    sl = slice(IDX_Q, IDX_COLS)
    tk = t[:, sl]
    lane = lax.broadcasted_iota(jnp.int32, tk.shape, 1)
    is_k = lane < IDX_DIM
    ms = jnp.sum(jnp.where(is_k, tk * tk, 0.0), axis=-1, keepdims=True) * (1.0 / IDX_DIM)
    yk = tk * lax.rsqrt(ms + EPS) * gidx_ref[0]
    idx_ref[:, sl] = jnp.where(is_k, _rope_slab(yk, cos, sin), tk) * scale[:, sl]


def _front_call(x2, g_attn, w_qk, w_bqk, w_vt, w_idx, gain_qk, gain_idx, scale_idx, cos_t, sin_t,
                gmat, layer, tt):
    n, d = x2.shape
    tm = ROW_TILE
    row = lambda cols: pl.BlockSpec((tm, cols), lambda i: (i, 0))
    full = lambda w: pl.BlockSpec((1,) + w.shape[1:], lambda i: (layer, 0, 0))
    fixed = lambda a: pl.BlockSpec(a.shape, lambda i: (0, 0))
    cqk, cbqk, cvt = w_qk.shape[2], w_bqk.shape[2], w_vt.shape[2]
    return pl.pallas_call(
        _front_kernel,
        grid=(n // tm,),
        in_specs=[row(d), full(g_attn), full(w_qk), full(w_bqk), full(w_vt), full(w_idx),
                  full(gain_qk), full(gain_idx), fixed(scale_idx), row(LANES), row(LANES),
                  fixed(gmat)],
        out_specs=[row(d), row(cqk), row(cbqk),
                   pl.BlockSpec((tm // tt, cvt, tt), lambda i: (i, 0, 0)), row(IDX_COLS)],
        out_shape=[jax.ShapeDtypeStruct((n, d), BF16), jax.ShapeDtypeStruct((n, cqk), BF16),
                   jax.ShapeDtypeStruct((n, cbqk), BF16),
                   jax.ShapeDtypeStruct((n // tt, cvt, tt), BF16),
                   jax.ShapeDtypeStruct((n, IDX_COLS), F32)],
        compiler_params=_params("parallel"),
    )(x2, g_attn, w_qk, w_bqk, w_vt, w_idx, gain_qk, gain_idx, scale_idx, cos_t, sin_t, gmat)


def _load_qt(q_ref, qz_ref):
    qt = q_ref[0].astype(F32).T
    rows = lax.broadcasted_iota(jnp.int32, (LANES, qt.shape[1]), 0)
    for s in range(qt.shape[0] // LANES):
        slab = qt[s * LANES:(s + 1) * LANES, :]
        qz_ref[2 * s] = jnp.where(rows < HEAD_DIM, slab, 0.0).astype(BF16)
        qz_ref[2 * s + 1] = jnp.where(rows >= HEAD_DIM, slab, 0.0).astype(BF16)


ONES_ROWS = 16


def _softmax_step(n, s, vt, m_ref, l_ref, acc_ref):
    m_old = m_ref[n]
    m_new = jnp.maximum(m_old, jnp.max(s, axis=0, keepdims=True))
    p = jnp.exp2(s - m_new).astype(BF16)
    alpha = jnp.exp2(m_old - m_new)
    f = vt.shape[0]
    pv = _dot(jnp.concatenate([vt, jnp.ones((ONES_ROWS, vt.shape[1]), BF16)], axis=0), p)
    l_ref[n] = alpha * l_ref[n] + pv[f:f + 1]
    acc_ref[n] = alpha * acc_ref[n] + pv[:f]
    m_ref[n] = m_new


def _tile_ids(t):
    key = lax.broadcasted_iota(jnp.int32, (KEY_TILE, t), 0)
    qry = lax.broadcasted_iota(jnp.int32, (KEY_TILE, t), 1)
    return key, qry


def _key_rows(u):
    return pl.ds(pl.multiple_of(u * KEY_TILE, KEY_TILE), KEY_TILE)


def _in_pairs(n, one, carry):
    carry = lax.fori_loop(0, n // 2, lambda p, c: one(2 * p + 1, one(2 * p, c)), carry)
    return lax.cond(n % 2 == 1, lambda c: one(n - 1, c), lambda c: c, carry)


def _interleave(*gens):
    gens = list(gens)
    while gens:
        for gen in list(gens):
            try:
                next(gen)
            except StopIteration:
                gens.remove(gen)


def _attn_ab_kernel(lam_ref, aq_ref, ak_ref, avt_ref, g_ref, bq_ref, bk_ref, bvt_ref, lmat_ref,
                    ya_ref, yb_ref, aqz_ref, m_ref, l_ref, aacc_ref, bqz_ref, r_ref, bacc_ref, *,
                    lam_init):
    t = aq_ref.shape[1]
    i = pl.program_id(1)
    lp = lam_ref[0]
    lam = (jnp.exp(jnp.sum(lp[0:1] * lp[1:2], axis=-1, keepdims=True))
           - jnp.exp(jnp.sum(lp[2:3] * lp[3:4], axis=-1, keepdims=True)) + lam_init)
    key, qry = _tile_ids(t)
    sub = t // KEY_TILE
    lmat = lmat_ref[...]
    _load_qt(aq_ref, aqz_ref)
    _load_qt(bq_ref, bqz_ref)
    m_ref[...] = jnp.full(m_ref.shape, NEG, F32)
    l_ref[...] = jnp.zeros(l_ref.shape, F32)
    aacc_ref[...] = jnp.zeros(aacc_ref.shape, F32)
    r_ref[...] = jnp.zeros(r_ref.shape, F32)
    bacc_ref[...] = jnp.zeros(bacc_ref.shape, F32)

    def a_group(g, diag):
        scores = []
        for r in range(sub):
            for n in range(2 * A_HEADS):
                kb = ak_ref[0, _key_rows(g * sub + r), (n // 2) * LANES:(n // 2 + 1) * LANES]
                scores.append(_dot(kb, aqz_ref[n]))
        yield
        for r in range(sub):
            for n in range(2 * A_HEADS):
                s = scores[r * 2 * A_HEADS + n]
                if diag:
                    s = jnp.where(((key + r * KEY_TILE) // CHUNK) <= (qry // CHUNK), s, NEG)
                vt = avt_ref[0, g * sub + r, (n // 2) * LANES:(n // 2 + 1) * LANES, :]
                _softmax_step(n, s, vt, m_ref, l_ref, aacc_ref)
                yield

    def b_group(g, diag):
        order = [(r, n) for r in reversed(range(sub)) for n in range(B_HEADS)]
        strict = {r: (key + r * KEY_TILE) < qry for r in range(sub)} if diag else None
        logits = {}
        for r, n in order:
            kb = bk_ref[0, _key_rows(g * sub + r), (n // 2) * LANES:(n // 2 + 1) * LANES]
            logits[r, n] = _dot(kb, bqz_ref[n])
        yield
        log_betas, laters, sums = {}, {}, {}
        for r, n in order:
            z = logits[r, n]
            neg_abs = pltpu.bitcast(pltpu.bitcast(z, jnp.int32) | jnp.int32(INT_MIN), F32)
            soft = jnp.log2(1.0 + jnp.exp2(neg_abs))
            log_beta = jnp.minimum(z, 0.0) - soft
            log_om = log_beta - z
            if diag:
                log_om = jnp.where(strict[r], log_om, 0.0)
            later = _dot(lmat, log_om.astype(BF16))
            laters[r, n] = later[:KEY_TILE]
            log_betas[r, n] = log_beta
            sums[r, n] = later[KEY_TILE:KEY_TILE + 1]
            yield
        for r, n in order:
            a = jnp.exp2(log_betas[r, n] + laters[r, n])
            if diag:
                a = jnp.where(strict[r], a, 0.0)
            vt = bvt_ref[0, g * sub + r, n * HEAD_DIM:(n + 1) * HEAD_DIM, :]
            visited = r_ref[n]
            bacc_ref[n] = bacc_ref[n] + _dot(vt, a.astype(BF16)) * jnp.exp2(visited)
            r_ref[n] = visited + sums[r, n]
            yield

    def body(n, carry):
        _interleave(a_group(n, False), b_group(i - 1 - n, False))
        return carry

    _interleave(b_group(i, True))
    lax.fori_loop(0, i, body, 0)
    _interleave(a_group(i, True))

    gain = g_ref[0] * (1.0 - lam_init)
    for h in range(A_HEADS):
        o = aacc_ref[2 * h] / l_ref[2 * h] - lam * (aacc_ref[2 * h + 1] / l_ref[2 * h + 1])
        ms = jnp.mean(o * o, axis=0, keepdims=True)
        y = (o * lax.rsqrt(ms + EPS)).T * gain
        ya_ref[0, :, h * LANES:(h + 1) * LANES] = y.astype(ya_ref.dtype)
    yb_ref[0] = bacc_ref[...].reshape(B_HEADS * HEAD_DIM, t).T.astype(yb_ref.dtype)


def _attn_ab_call(qk, bqk, vt, lam_p, subln, lmat, layer, lam_init, t):
    b, s, _ = qk.shape
    nt, _, kt = vt.shape[1:]
    ns = 2 * A_HEADS
    qspec = pl.BlockSpec((1, t, GROUP), lambda bi, i: (bi, i, 0))
    kspec = pl.BlockSpec((1, s, GROUP), lambda bi, i: (bi, 0, 1))
    vspec = lambda f: pl.BlockSpec((1, nt, GROUP, kt), lambda bi, i: (bi, 0, f, 0))
    return pl.pallas_call(
        functools.partial(_attn_ab_kernel, lam_init=lam_init),
        grid=(b, s // t),
        in_specs=[pl.BlockSpec((1, 4, HEAD_DIM), lambda bi, i: (layer, 0, 0)),
                  qspec, kspec, vspec(0),
                  pl.BlockSpec((1, 1, LANES), lambda bi, i: (layer, 0, 0)),
                  qspec, kspec, vspec(1),
                  pl.BlockSpec(lmat.shape, lambda bi, i: (0, 0))],
        out_specs=[qspec, qspec],
        out_shape=[jax.ShapeDtypeStruct((b, s, GROUP), BF16)] * 2,
        scratch_shapes=[pltpu.VMEM((ns, LANES, t), BF16), pltpu.VMEM((ns, 1, t), F32),
                        pltpu.VMEM((ns, 1, t), F32), pltpu.VMEM((ns, LANES, t), F32),
                        pltpu.VMEM((B_HEADS, LANES, t), BF16), pltpu.VMEM((B_HEADS, 1, t), F32),
                        pltpu.VMEM((B_HEADS, HEAD_DIM, t), F32)],
        compiler_params=_params("parallel", "arbitrary"),
    )(lam_p, qk, qk, vt, subln, bqk, bqk, vt, lmat)


def _fold_rows(x):
    parts = [x[r * SUBLANES:(r + 1) * SUBLANES] for r in range(x.shape[0] // SUBLANES)]
    while len(parts) > 1:
        parts = [a + b for a, b in zip(parts[0::2], parts[1::2])]
    return parts[0]


def _dsa_mask_kernel(q_ref, kw_ref, wq_ref, lt_ref, o_ref, key_ref, qz_ref, *, topk):
    t = q_ref.shape[1]
    nj = key_ref.shape[0]
    i = pl.program_id(1)
    nch = (i + 1) * (t // KEY_TILE)
    qt = q_ref[0].T
    pad = jnp.zeros((LANES - IDX_DIM, t), F32)
    for h in range(IDX_HEADS):
        qz_ref[h] = jnp.concatenate([qt[h * IDX_DIM:(h + 1) * IDX_DIM], pad], axis=0).astype(BF16)
    wt = wq_ref[0].T
    wh = [wt[IDX_DIM + h:IDX_DIM + h + 1, :] for h in range(IDX_HEADS)]
    key0, qry0 = _tile_ids(t)
    sub = t // KEY_TILE

    def chunk_ok(r):
        return ((key0 + r * KEY_TILE) // CHUNK) <= (qry0 // CHUNK)

    def fill(g, diag):
        for r in range(sub):
            c = g * sub + r
            kb = kw_ref[0, _key_rows(c), :].astype(BF16)
            score = jnp.zeros((KEY_TILE, t), F32)
            for h in range(IDX_HEADS):
                score = score + wh[h] * jnp.maximum(_dot(kb, qz_ref[h]), 0.0)
            if diag:
                score = jnp.where(chunk_ok(r), score, -jnp.inf)
            bits = pltpu.bitcast(score, jnp.int32)
            okey = jnp.where(bits < 0, bits ^ jnp.int32(0x7FFFFFFF), bits)
            key_ref[c] = jnp.where(score == 0.0, 0, okey)

    def fill_one(g, carry):
        fill(g, False)
        return carry

    _in_pairs(i, fill_one, 0)
    fill(i, True)

    def count(pred, thr):
        def one(g, acc):
            for r in range(sub):
                acc = acc + _fold_rows(jnp.where(pred(key_ref[g * sub + r], thr), 1.0, 0.0))
            return acc
        acc = _in_pairs(i + 1, one, jnp.zeros((SUBLANES, t), F32))
        return jnp.sum(acc, axis=0, keepdims=True)

    def bit_body(n, lo):
        cand = lo + jnp.left_shift(jnp.int32(1), 31 - n)
        cnt = count(lambda k, th: k >= th, cand)
        return jnp.where(cnt >= topk, cand, lo)

    thr = lax.fori_loop(0, 32, bit_body, jnp.full((1, t), INT_MIN, jnp.int32))
    need = topk - count(lambda k, th: k > th, thr)
    lt = lt_ref[...]

    def emit(g, run, diag):
        for r in range(sub):
            c = g * sub + r
            okey = key_ref[c]
            eq = okey == thr
            eqf = jnp.where(eq, 1.0, 0.0)
            before = _dot(lt, eqf.astype(BF16)) + run
            sel = (okey > thr) | (eq & (before < need))
            if diag:
                sel = sel & chunk_ok(r)
            o_ref[0, 0, c] = jnp.where(sel, 0.0, NEG).astype(o_ref.dtype)
            run = run + jnp.sum(eqf, axis=0, keepdims=True)
        return run

    run = _in_pairs(i, lambda g, run: emit(g, run, False), jnp.zeros((1, t), F32))
    emit(i, run, True)

    def blank(c, carry):
        o_ref[0, 0, c] = jnp.full((KEY_TILE, t), NEG, o_ref.dtype)
        return carry

    lax.fori_loop(nch, nj, blank, 0)


def _dsa_mask_call(idx, lt, topk, t):
    b, s, _ = idx.shape
    kt = lt.shape[0]
    assert t >= topk
    nt, nk = s // t, s // kt
    return pl.pallas_call(
        functools.partial(_dsa_mask_kernel, topk=topk),
        grid=(b, nt),
        in_specs=[pl.BlockSpec((1, t, IDX_Q), lambda bi, i: (bi, i, 0)),
                  pl.BlockSpec((1, s, LANES), lambda bi, i: (bi, 0, IDX_Q // LANES)),
                  pl.BlockSpec((1, t, LANES), lambda bi, i: (bi, i, IDX_Q // LANES)),
                  pl.BlockSpec((kt, kt), lambda bi, i: (0, 0))],
        out_specs=pl.BlockSpec((1, 1, nk, kt, t), lambda bi, i: (bi, i, 0, 0, 0)),
        out_shape=jax.ShapeDtypeStruct((b, nt, nk, kt, t), BF16),
        scratch_shapes=[pltpu.VMEM((nk, kt, t), jnp.int32), pltpu.VMEM((IDX_HEADS, LANES, t), BF16)],
        compiler_params=_params("parallel", "arbitrary"),
    )(idx, idx, idx, lt)


def _attn_c_kernel(q_ref, k_ref, vt_ref, b_ref, o_ref, qz_ref, m_ref, l_ref, acc_ref):
    t = q_ref.shape[1]
    i = pl.program_id(1)
    _load_qt(q_ref, qz_ref)
    m_ref[...] = jnp.full(m_ref.shape, NEG, F32)
    l_ref[...] = jnp.zeros(l_ref.shape, F32)
    acc_ref[...] = jnp.zeros(acc_ref.shape, F32)

    key, qry = _tile_ids(KEY_TILE)
    ident = jnp.where(key == qry, 1.0, 0.0).astype(BF16)

    sub = t // KEY_TILE

    def body(g, carry):
        scores = []
        for r in range(sub):
            u = g * sub + r
            bias = b_ref[0, 0, u]
            for n in range(C_HEADS):
                kb = k_ref[0, _key_rows(u), (n // 2) * LANES:(n // 2 + 1) * LANES]
                lhs = jnp.concatenate([kb, ident], axis=1)
                rhs = jnp.concatenate([qz_ref[n], bias], axis=0)
                scores.append(_dot(lhs, rhs))
        for r in range(sub):
            for n in range(C_HEADS):
                vt = vt_ref[0, g * sub + r, n * HEAD_DIM:(n + 1) * HEAD_DIM, :]
                _softmax_step(n, scores[r * C_HEADS + n], vt, m_ref, l_ref, acc_ref)
        return carry

    _in_pairs(i + 1, body, 0)
    o = acc_ref[...] / l_ref[...]
    o_ref[0] = o.reshape(C_HEADS * HEAD_DIM, t).T.astype(o_ref.dtype)


def _attn_c_call(qk, vt, bias):
    b, s, _ = qk.shape
    nk, _, kt = vt.shape[1:]
    nt, t = bias.shape[1], bias.shape[4]
    return pl.pallas_call(
        _attn_c_kernel,
        grid=(b, nt),
        in_specs=[pl.BlockSpec((1, t, GROUP), lambda bi, i: (bi, i, 2)),
                  pl.BlockSpec((1, s, GROUP), lambda bi, i: (bi, 0, 3)),
                  pl.BlockSpec((1, nk, GROUP, kt), lambda bi, i: (bi, 0, 2, 0)),
                  pl.BlockSpec((1, 1, nk, kt, t), lambda bi, i: (bi, i, 0, 0, 0))],
        out_specs=pl.BlockSpec((1, t, GROUP), lambda bi, i: (bi, i, 0)),
        out_shape=jax.ShapeDtypeStruct((b, s, GROUP), BF16),
        scratch_shapes=[pltpu.VMEM((C_HEADS, LANES, t), BF16), pltpu.VMEM((C_HEADS, 1, t), F32),
                        pltpu.VMEM((C_HEADS, 1, t), F32), pltpu.VMEM((C_HEADS, HEAD_DIM, t), F32)],
        compiler_params=_params("parallel", "arbitrary"),
    )(qk, qk, vt, bias)


def _merge_kernel(x_ref, h_ref, ya_ref, yb_ref, yc_ref, wg_ref, wa_ref, wb_ref, wc_ref, wo_ref, o_ref):
    d = x_ref.shape[1]
    h = h_ref[...]
    merged = jnp.zeros(x_ref.shape, F32)
    for n, (y_ref, w_ref) in enumerate(((ya_ref, wa_ref), (yb_ref, wb_ref), (yc_ref, wc_ref))):
        gate = jax.nn.sigmoid(_dot(h, wg_ref[0, :, n * d:(n + 1) * d]))
        merged = merged + gate * _dot(y_ref[...], w_ref[0])
    o_ref[...] = x_ref[...] + _dot(merged.astype(BF16), wo_ref[0])


def _merge_call(x2, h, ya, yb, yc, wg, wa, wb, wc, wo, layer):
    n, d = x2.shape
    tm = ROW_TILE
    row = lambda cols: pl.BlockSpec((tm, cols), lambda i: (i, 0))
    full = lambda w: pl.BlockSpec((1,) + w.shape[1:], lambda i: (layer, 0, 0))
    return pl.pallas_call(
        _merge_kernel,
        grid=(n // tm,),
        in_specs=[row(d), row(d), row(ya.shape[1]), row(yb.shape[1]), row(yc.shape[1]),
                  full(wg), full(wa), full(wb), full(wc), full(wo)],
        out_specs=row(d),
        out_shape=jax.ShapeDtypeStruct((n, d), F32),
        compiler_params=_params("parallel"),
    )(x2, h, ya, yb, yc, wg, wa, wb, wc, wo)


def _mlp_kernel(x_ref, p_ref, gm_ref, wu_ref, wd_ref, gp_ref, wpg_ref, wpp_ref, o_ref):
    x = x_ref[...]
    ms = jnp.mean(x * x, axis=-1, keepdims=True)
    h = (x * lax.rsqrt(ms + EPS) * gm_ref[0]).astype(BF16)
    for f in range(wu_ref.shape[2] // FF_TILE):
        u = jnp.maximum(_dot(h, wu_ref[0, :, f * FF_TILE:(f + 1) * FF_TILE]), 0.0)
        x = x + _dot((u * u).astype(BF16), wd_ref[0, f * FF_TILE:(f + 1) * FF_TILE, :])
    ms = jnp.mean(x * x, axis=-1, keepdims=True)
    hp = (x * lax.rsqrt(ms + EPS) * gp_ref[0]).astype(BF16)
    gate = jax.nn.sigmoid(_dot(hp, wpg_ref[0]))
    o_ref[...] = x + gate * _dot(p_ref[0].astype(BF16), wpp_ref[0])


def _mlp_call(x2, p3, g_mlp, w_up, w_down, g_ple, w_pg, w_pp, layer):
    n, d = x2.shape
    pd = p3.shape[2]
    tm = ROW_TILE
    full = lambda w: pl.BlockSpec((1,) + w.shape[1:], lambda i: (layer, 0, 0),
                                  pipeline_mode=pl.Buffered(1))
    return pl.pallas_call(
        _mlp_kernel,
        grid=(n // tm,),
        in_specs=[pl.BlockSpec((tm, d), lambda i: (i, 0)),
                  pl.BlockSpec((1, tm, pd), lambda i: (layer, i, 0)),
                  full(g_mlp), full(w_up), full(w_down), full(g_ple), full(w_pg), full(w_pp)],
        out_specs=pl.BlockSpec((tm, d), lambda i: (i, 0)),
        out_shape=jax.ShapeDtypeStruct((n, d), F32),
        compiler_params=_params("parallel"),
    )(x2, p3, g_mlp, w_up, w_down, g_ple, w_pg, w_pp)


def _tile_gain(g, reps, scale=1.0):
    return jnp.tile(g.astype(F32), (1, reps)) * scale


def kernel(x, p, positions, attn_norm, w_in, a_q_norm, a_k_norm, a_lambda, a_subln,
           c_q_norm, c_k_norm, idx_k_norm, w_br_a, w_br_b, w_br_c, w_out,
           mlp_norm, w_up, w_down, ple_norm, w_ple_gate, w_ple_proj):
    b, s, d = x.shape
    depth = w_in.shape[0]
    n = b * s
    topk = min(TOPK_MAX, s // 4)
    t = min(ATTN_TILE, s)
    scale = HEAD_DIM ** -0.5
    qscale = scale * LOG2E

    inv = ROPE_THETA ** (-jnp.arange(0, HEAD_DIM, 2, dtype=F32) / HEAD_DIM)
    ang = positions.astype(F32)[..., None] * inv
    cos, sin = jnp.cos(ang), jnp.sin(ang)
    cos_t = jnp.concatenate([cos, cos, cos, cos], axis=-1).reshape(n, LANES)
    sin_t = jnp.concatenate([-sin, sin, -sin, sin], axis=-1).reshape(n, LANES)

    qa, ka, va = 0, 512, 1024
    qb, kb, vb = 1536, 2048, 2560
    qc, kc, vc = 3072, 3584, 4096
    qi = 4608
    gl = 4932
    col = lambda a, width=GROUP: w_in[:, :, a:a + width]
    w_qk = jnp.concatenate([col(qa), col(ka), col(qc), col(kc)], axis=-1).astype(BF16)
    w_bqk = jnp.concatenate([col(qb) * qscale, col(kb)], axis=-1).astype(BF16)
    w_vt = jnp.concatenate([col(va), col(vb), col(vc)], axis=-1).astype(BF16)
    w_idx = jnp.pad(col(qi, gl - qi), ((0, 0), (0, 0), (0, IDX_COLS - (gl - qi)))).astype(BF16)
    w_gl = col(gl, N_BRANCHES * d).astype(BF16)
    heads = GROUP // HEAD_DIM
    gain_qk = jnp.concatenate([_tile_gain(a_q_norm, heads, qscale), _tile_gain(a_k_norm, heads),
                               _tile_gain(c_q_norm, heads, qscale), _tile_gain(c_k_norm, heads)],
                              axis=-1)[:, None, :]
    gain_idx = jnp.pad(idx_k_norm.astype(F32), ((0, 0), (0, LANES - IDX_DIM)))[:, None, :]
    scale_idx = jnp.concatenate([jnp.full((IDX_Q,), IDX_DIM ** -0.5, F32), jnp.ones((IDX_DIM,), F32),
                                 jnp.full((IDX_HEADS,), IDX_HEADS ** -0.5, F32),
                                 jnp.zeros((LANES - IDX_DIM - IDX_HEADS,), F32)])[None, :]
    gid = jnp.arange(GROUP) // HEAD_DIM
    gmat = (gid[:, None] == gid[None, :]).astype(BF16)
    kt = min(KEY_TILE, s)
    ar = jnp.arange(kt)
    later_mat = jnp.concatenate([(ar[None, :] > ar[:, None]).astype(BF16),
                                 jnp.ones((ONES_ROWS, kt), BF16)], axis=0)
    before_mat = (ar[None, :] < ar[:, None]).astype(BF16)

    wa, wb, wc, wo = (w.astype(BF16) for w in (w_br_a, w_br_b, w_br_c, w_out))
    wu, wd, wpg, wpp = (w.astype(BF16) for w in (w_up, w_down, w_ple_gate, w_ple_proj))
    g_attn, g_mlp, g_ple = (g.astype(F32)[:, None, :] for g in (attn_norm, mlp_norm, ple_norm))
    subln = a_subln.astype(F32)[:, None, :]
    lam_p = a_lambda.astype(F32)
    p3 = p.reshape(depth, n, p.shape[-1])

    x2 = x.reshape(n, d)
    for i in range(depth):
        lam_init = 0.8 - 0.6 * math.exp(-0.3 * i)
        h, qk, bqk, vt, idx = _front_call(x2, g_attn, w_qk, w_bqk, w_vt, w_idx, gain_qk, gain_idx,
                                          scale_idx, cos_t, sin_t, gmat, i, kt)
        qk, bqk, idx = (a.reshape(b, s, -1) for a in (qk, bqk, idx))
        vt = vt.reshape(b, s // kt, -1, kt)
        ya, yb = _attn_ab_call(qk, bqk, vt, lam_p, subln, later_mat, i, lam_init, t)
        bias = _dsa_mask_call(idx, before_mat, topk, t)
        yc = _attn_c_call(qk, vt, bias)
        x2 = _merge_call(x2, h, ya.reshape(n, -1), yb.reshape(n, -1), yc.reshape(n, -1),
                         w_gl, wa, wb, wc, wo, i)
        x2 = _mlp_call(x2, p3, g_mlp, wu, wd, g_ple, wpg, wpp, i)
    return x2.reshape(b, s, d)
```

```python
import functools
import math

import jax
import jax.numpy as jnp
from jax import lax
from jax.experimental import pallas as pl
from jax.experimental.pallas import tpu as pltpu

F32 = jnp.float32
BF16 = jnp.bfloat16

HEAD_DIM = 64
CHUNK = 64
A_HEADS = 4
B_HEADS = 8
C_HEADS = 8
IDX_HEADS = 4
IDX_DIM = 64
TOPK_MAX = 256
N_BRANCHES = 3
ROPE_THETA = 10000.0
EPS = 1e-6

LANES = 128
SUBLANES = 8
HALF_HEAD = HEAD_DIM // 2
NEG = -1e30
INT_MIN = -(2 ** 31)
LOG2E = 1.4426950408889634
VMEM_LIMIT = 56 * 1024 * 1024

ROW_TILE = 512
ATTN_TILE = 256
KEY_TILE = 128
FF_TILE = 1024
GROUP = 512


def _params(*sem):
    return pltpu.CompilerParams(dimension_semantics=sem, vmem_limit_bytes=VMEM_LIMIT)


def _dot(a, b):
    return jnp.dot(a, b, preferred_element_type=F32)


def _split(x):
    hi = x.astype(BF16)
    return hi, (x - hi.astype(F32)).astype(BF16)


def _swap_halves(y):
    lane = lax.broadcasted_iota(jnp.int32, y.shape, 1)
    return jnp.where((lane & HALF_HEAD) != 0, pltpu.roll(y, HALF_HEAD, 1),
                     pltpu.roll(y, LANES - HALF_HEAD, 1))


def _rope_slab(y, cos, sin):
    return y * cos + _swap_halves(y) * sin


IDX_Q = IDX_HEADS * IDX_DIM
IDX_COLS = IDX_Q + LANES


def _front_kernel(x_ref, g_ref, wqk_ref, wbqk_ref, wvt_ref, widx_ref, gqk_ref, gidx_ref, sidx_ref,
                  cos_ref, sin_ref, gmat_ref, h_ref, qk_ref, bqk_ref, vt_ref, idx_ref):
    x = x_ref[...]
    ms = jnp.mean(x * x, axis=-1, keepdims=True)
    h = (x * lax.rsqrt(ms + EPS) * g_ref[0]).astype(BF16)
    h_ref[...] = h
    cos = cos_ref[...]
    sin = sin_ref[...]

    gmat = gmat_ref[...]
    for j in range(qk_ref.shape[1] // GROUP):
        cols = slice(j * GROUP, (j + 1) * GROUP)
        t = _dot(h, wqk_ref[0, :, cols])
        hi, lo = _split(t * t)
        ms = (_dot(hi, gmat) + _dot(lo, gmat)) * (1.0 / HEAD_DIM)
        y = t * lax.rsqrt(ms + EPS) * gqk_ref[0, :, cols]
        for s in range(GROUP // LANES):
            sl = slice(s * LANES, (s + 1) * LANES)
            qk_ref[:, j * GROUP + s * LANES:j * GROUP + (s + 1) * LANES] = (
                _rope_slab(y[:, sl], cos, sin).astype(qk_ref.dtype))

    for j in range(bqk_ref.shape[1] // GROUP):
        cols = slice(j * GROUP, (j + 1) * GROUP)
        bqk_ref[:, cols] = _dot(h, wbqk_ref[0, :, cols]).astype(bqk_ref.dtype)

    tt = vt_ref.shape[2]
    for j in range(vt_ref.shape[1] // GROUP):
        cols = slice(j * GROUP, (j + 1) * GROUP)
        t = _dot(h, wvt_ref[0, :, cols])
        for r in range(vt_ref.shape[0]):
            vt_ref[r, cols, :] = t[r * tt:(r + 1) * tt, :].T.astype(vt_ref.dtype)

    t = _dot(h, widx_ref[0])
    scale = sidx_ref[...]
    for s in range(IDX_Q // LANES):
        sl = slice(s * LANES, (s + 1) * LANES)
        idx_ref[:, sl] = _rope_slab(t[:, sl], cos, sin) * scale[:, sl]
    sl = slice(IDX_Q, IDX_COLS)
    tk = t[:, sl]
    lane = lax.broadcasted_iota(jnp.int32, tk.shape, 1)
    is_k = lane < IDX_DIM
    ms = jnp.sum(jnp.where(is_k, tk * tk, 0.0), axis=-1, keepdims=True) * (1.0 / IDX_DIM)
    yk = tk * lax.rsqrt(ms + EPS) * gidx_ref[0]
    idx_ref[:, sl] = jnp.where(is_k, _rope_slab(yk, cos, sin), tk) * scale[:, sl]


def _front_call(x2, g_attn, w_qk, w_bqk, w_vt, w_idx, gain_qk, gain_idx, scale_idx, cos_t, sin_t,
                gmat, layer, tt):
    n, d = x2.shape
    tm = ROW_TILE
    row = lambda cols: pl.BlockSpec((tm, cols), lambda i: (i, 0))
    full = lambda w: pl.BlockSpec((1,) + w.shape[1:], lambda i: (layer, 0, 0))
    fixed = lambda a: pl.BlockSpec(a.shape, lambda i: (0, 0))
    cqk, cbqk, cvt = w_qk.shape[2], w_bqk.shape[2], w_vt.shape[2]
    return pl.pallas_call(
        _front_kernel,
        grid=(n // tm,),
        in_specs=[row(d), full(g_attn), full(w_qk), full(w_bqk), full(w_vt), full(w_idx),
                  full(gain_qk), full(gain_idx), fixed(scale_idx), row(LANES), row(LANES),
                  fixed(gmat)],
        out_specs=[row(d), row(cqk), row(cbqk),
                   pl.BlockSpec((tm // tt, cvt, tt), lambda i: (i, 0, 0)), row(IDX_COLS)],
        out_shape=[jax.ShapeDtypeStruct((n, d), BF16), jax.ShapeDtypeStruct((n, cqk), BF16),
                   jax.ShapeDtypeStruct((n, cbqk), BF16),
                   jax.ShapeDtypeStruct((n // tt, cvt, tt), BF16),
                   jax.ShapeDtypeStruct((n, IDX_COLS), F32)],
        compiler_params=_params("parallel"),
    )(x2, g_attn, w_qk, w_bqk, w_vt, w_idx, gain_qk, gain_idx, scale_idx, cos_t, sin_t, gmat)


def _load_qt(q_ref, qz_ref):
    qt = q_ref[0].astype(F32).T
    rows = lax.broadcasted_iota(jnp.int32, (LANES, qt.shape[1]), 0)
    for s in range(qt.shape[0] // LANES):
        slab = qt[s * LANES:(s + 1) * LANES, :]
        qz_ref[2 * s] = jnp.where(rows < HEAD_DIM, slab, 0.0).astype(BF16)
        qz_ref[2 * s + 1] = jnp.where(rows >= HEAD_DIM, slab, 0.0).astype(BF16)


ONES_ROWS = 16


def _softmax_step(n, s, vt, m_ref, l_ref, acc_ref):
    m_old = m_ref[n]
    m_new = jnp.maximum(m_old, jnp.max(s, axis=0, keepdims=True))
    p = jnp.exp2(s - m_new).astype(BF16)
    alpha = jnp.exp2(m_old - m_new)
    f = vt.shape[0]
    pv = _dot(jnp.concatenate([vt, jnp.ones((ONES_ROWS, vt.shape[1]), BF16)], axis=0), p)
    l_ref[n] = alpha * l_ref[n] + pv[f:f + 1]
    acc_ref[n] = alpha * acc_ref[n] + pv[:f]
    m_ref[n] = m_new


def _tile_ids(t):
    key = lax.broadcasted_iota(jnp.int32, (KEY_TILE, t), 0)
    qry = lax.broadcasted_iota(jnp.int32, (KEY_TILE, t), 1)
    return key, qry


def _key_rows(u):
    return pl.ds(pl.multiple_of(u * KEY_TILE, KEY_TILE), KEY_TILE)


def _in_pairs(n, one, carry):
    carry = lax.fori_loop(0, n // 2, lambda p, c: one(2 * p + 1, one(2 * p, c)), carry)
    return lax.cond(n % 2 == 1, lambda c: one(n - 1, c), lambda c: c, carry)


def _interleave(*gens):
    gens = list(gens)
    while gens:
        for gen in list(gens):
            try:
                next(gen)
            except StopIteration:
                gens.remove(gen)


def _attn_ab_kernel(lam_ref, aq_ref, ak_ref, avt_ref, g_ref, bq_ref, bk_ref, bvt_ref, lmat_ref,
                    ya_ref, yb_ref, aqz_ref, m_ref, l_ref, aacc_ref, bqz_ref, r_ref, bacc_ref, *,
                    lam_init):
    t = aq_ref.shape[1]
    i = pl.program_id(1)
    lp = lam_ref[0]
    lam = (jnp.exp(jnp.sum(lp[0:1] * lp[1:2], axis=-1, keepdims=True))
           - jnp.exp(jnp.sum(lp[2:3] * lp[3:4], axis=-1, keepdims=True)) + lam_init)
    key, qry = _tile_ids(t)
    sub = t // KEY_TILE
    lmat = lmat_ref[...]
    _load_qt(aq_ref, aqz_ref)
    _load_qt(bq_ref, bqz_ref)
    m_ref[...] = jnp.full(m_ref.shape, NEG, F32)
    l_ref[...] = jnp.zeros(l_ref.shape, F32)
    aacc_ref[...] = jnp.zeros(aacc_ref.shape, F32)
    r_ref[...] = jnp.zeros(r_ref.shape, F32)
    bacc_ref[...] = jnp.zeros(bacc_ref.shape, F32)

    def a_group(g, diag):
        scores = []
        for r in range(sub):
            for n in range(2 * A_HEADS):
                kb = ak_ref[0, _key_rows(g * sub + r), (n // 2) * LANES:(n // 2 + 1) * LANES]
                scores.append(_dot(kb, aqz_ref[n]))
        yield
        for r in range(sub):
            for n in range(2 * A_HEADS):
                s = scores[r * 2 * A_HEADS + n]
                if diag:
                    s = jnp.where(((key + r * KEY_TILE) // CHUNK) <= (qry // CHUNK), s, NEG)
                vt = avt_ref[0, g * sub + r, (n // 2) * LANES:(n // 2 + 1) * LANES, :]
                _softmax_step(n, s, vt, m_ref, l_ref, aacc_ref)
                yield

    def b_group(g, diag):
        order = [(r, n) for r in reversed(range(sub)) for n in range(B_HEADS)]
        strict = {r: (key + r * KEY_TILE) < qry for r in range(sub)} if diag else None
        logits = {}
        for r, n in order:
            kb = bk_ref[0, _key_rows(g * sub + r), (n // 2) * LANES:(n // 2 + 1) * LANES]
            logits[r, n] = _dot(kb, bqz_ref[n])
        yield
        log_betas, laters, sums = {}, {}, {}
        for r, n in order:
            z = logits[r, n]
            neg_abs = pltpu.bitcast(pltpu.bitcast(z, jnp.int32) | jnp.int32(INT_MIN), F32)
            soft = jnp.log2(1.0 + jnp.exp2(neg_abs))
            log_beta = jnp.minimum(z, 0.0) - soft
            log_om = log_beta - z
            if diag:
                log_om = jnp.where(strict[r], log_om, 0.0)
            later = _dot(lmat, log_om.astype(BF16))
            laters[r, n] = later[:KEY_TILE]
            log_betas[r, n] = log_beta
            sums[r, n] = later[KEY_TILE:KEY_TILE + 1]
            yield
        for r, n in order:
            a = jnp.exp2(log_betas[r, n] + laters[r, n])
            if diag:
                a = jnp.where(strict[r], a, 0.0)
            vt = bvt_ref[0, g * sub + r, n * HEAD_DIM:(n + 1) * HEAD_DIM, :]
            visited = r_ref[n]
            bacc_ref[n] = bacc_ref[n] + _dot(vt, a.astype(BF16)) * jnp.exp2(visited)
            r_ref[n] = visited + sums[r, n]
            yield

    def body(n, carry):
        _interleave(a_group(n, False), b_group(i - 1 - n, False))
        return carry

    _interleave(b_group(i, True))
    _in_pairs(i, body, 0)
    _interleave(a_group(i, True))

    gain = g_ref[0] * (1.0 - lam_init)
    for h in range(A_HEADS):
        o = aacc_ref[2 * h] / l_ref[2 * h] - lam * (aacc_ref[2 * h + 1] / l_ref[2 * h + 1])
        ms = jnp.mean(o * o, axis=0, keepdims=True)
        y = (o * lax.rsqrt(ms + EPS)).T * gain
        ya_ref[0, :, h * LANES:(h + 1) * LANES] = y.astype(ya_ref.dtype)
    yb_ref[0] = bacc_ref[...].reshape(B_HEADS * HEAD_DIM, t).T.astype(yb_ref.dtype)


def _attn_ab_call(qk, bqk, vt, lam_p, subln, lmat, layer, lam_init, t):
    b, s, _ = qk.shape
    nt, _, kt = vt.shape[1:]
    ns = 2 * A_HEADS
    qspec = pl.BlockSpec((1, t, GROUP), lambda bi, i: (bi, i, 0))
    kspec = pl.BlockSpec((1, s, GROUP), lambda bi, i: (bi, 0, 1))
    vspec = lambda f: pl.BlockSpec((1, nt, GROUP, kt), lambda bi, i: (bi, 0, f, 0))
    return pl.pallas_call(
        functools.partial(_attn_ab_kernel, lam_init=lam_init),
        grid=(b, s // t),
        in_specs=[pl.BlockSpec((1, 4, HEAD_DIM), lambda bi, i: (layer, 0, 0)),
                  qspec, kspec, vspec(0),
                  pl.BlockSpec((1, 1, LANES), lambda bi, i: (layer, 0, 0)),
                  qspec, kspec, vspec(1),
                  pl.BlockSpec(lmat.shape, lambda bi, i: (0, 0))],
        out_specs=[qspec, qspec],
        out_shape=[jax.ShapeDtypeStruct((b, s, GROUP), BF16)] * 2,
        scratch_shapes=[pltpu.VMEM((ns, LANES, t), BF16), pltpu.VMEM((ns, 1, t), F32),
                        pltpu.VMEM((ns, 1, t), F32), pltpu.VMEM((ns, LANES, t), F32),
                        pltpu.VMEM((B_HEADS, LANES, t), BF16), pltpu.VMEM((B_HEADS, 1, t), F32),
                        pltpu.VMEM((B_HEADS, HEAD_DIM, t), F32)],
        compiler_params=_params("parallel", "arbitrary"),
    )(lam_p, qk, qk, vt, subln, bqk, bqk, vt, lmat)


def _fold_rows(x):
    parts = [x[r * SUBLANES:(r + 1) * SUBLANES] for r in range(x.shape[0] // SUBLANES)]
    while len(parts) > 1:
        parts = [a + b for a, b in zip(parts[0::2], parts[1::2])]
    return parts[0]


def _dsa_mask_kernel(q_ref, kw_ref, wq_ref, lt_ref, o_ref, key_ref, qz_ref, *, topk):
    t = q_ref.shape[1]
    nj = key_ref.shape[0]
    i = pl.program_id(1)
    nch = (i + 1) * (t // KEY_TILE)
    qt = q_ref[0].T
    pad = jnp.zeros((LANES - IDX_DIM, t), F32)
    for h in range(IDX_HEADS):
        qz_ref[h] = jnp.concatenate([qt[h * IDX_DIM:(h + 1) * IDX_DIM], pad], axis=0).astype(BF16)
    wt = wq_ref[0].T
    wh = [wt[IDX_DIM + h:IDX_DIM + h + 1, :] for h in range(IDX_HEADS)]
    key0, qry0 = _tile_ids(t)
    sub = t // KEY_TILE

    def chunk_ok(r):
        return ((key0 + r * KEY_TILE) // CHUNK) <= (qry0 // CHUNK)

    def fill(g, diag):
        for r in range(sub):
            c = g * sub + r
            kb = kw_ref[0, _key_rows(c), :].astype(BF16)
            score = jnp.zeros((KEY_TILE, t), F32)
            for h in range(IDX_HEADS):
                score = score + wh[h] * jnp.maximum(_dot(kb, qz_ref[h]), 0.0)
            if diag:
                score = jnp.where(chunk_ok(r), score, -jnp.inf)
            bits = pltpu.bitcast(score, jnp.int32)
            okey = jnp.where(bits < 0, bits ^ jnp.int32(0x7FFFFFFF), bits)
            key_ref[c] = jnp.where(score == 0.0, 0, okey)

    def fill_one(g, carry):
        fill(g, False)
        return carry

    _in_pairs(i, fill_one, 0)
    fill(i, True)

    def count(pred, thr):
        def one(g, acc):
            for r in range(sub):
                acc = acc + _fold_rows(jnp.where(pred(key_ref[g * sub + r], thr), 1.0, 0.0))
            return acc
        acc = _in_pairs(i + 1, one, jnp.zeros((SUBLANES, t), F32))
        return jnp.sum(acc, axis=0, keepdims=True)

    def bit_body(n, lo):
        cand = lo + jnp.left_shift(jnp.int32(1), 31 - n)
        cnt = count(lambda k, th: k >= th, cand)
        return jnp.where(cnt >= topk, cand, lo)

    thr = lax.fori_loop(0, 32, bit_body, jnp.full((1, t), INT_MIN, jnp.int32))
    need = topk - count(lambda k, th: k > th, thr)
    lt = lt_ref[...]

    def emit(g, run, diag):
        for r in range(sub):
            c = g * sub + r
            okey = key_ref[c]
            eq = okey == thr
            eqf = jnp.where(eq, 1.0, 0.0)
            before = _dot(lt, eqf.astype(BF16)) + run
            sel = (okey > thr) | (eq & (before < need))
            if diag:
                sel = sel & chunk_ok(r)
            o_ref[0, 0, c] = jnp.where(sel, 0.0, NEG).astype(o_ref.dtype)
            run = run + jnp.sum(eqf, axis=0, keepdims=True)
        return run

    run = _in_pairs(i, lambda g, run: emit(g, run, False), jnp.zeros((1, t), F32))
    emit(i, run, True)

    def blank(c, carry):
        o_ref[0, 0, c] = jnp.full((KEY_TILE, t), NEG, o_ref.dtype)
        return carry

    lax.fori_loop(nch, nj, blank, 0)


def _dsa_mask_call(idx, lt, topk, t):
    b, s, _ = idx.shape
    kt = lt.shape[0]
    assert t >= topk
    nt, nk = s // t, s // kt
    return pl.pallas_call(
        functools.partial(_dsa_mask_kernel, topk=topk),
        grid=(b, nt),
        in_specs=[pl.BlockSpec((1, t, IDX_Q), lambda bi, i: (bi, i, 0)),
                  pl.BlockSpec((1, s, LANES), lambda bi, i: (bi, 0, IDX_Q // LANES)),
                  pl.BlockSpec((1, t, LANES), lambda bi, i: (bi, i, IDX_Q // LANES)),
                  pl.BlockSpec((kt, kt), lambda bi, i: (0, 0))],
        out_specs=pl.BlockSpec((1, 1, nk, kt, t), lambda bi, i: (bi, i, 0, 0, 0)),
        out_shape=jax.ShapeDtypeStruct((b, nt, nk, kt, t), BF16),
        scratch_shapes=[pltpu.VMEM((nk, kt, t), jnp.int32), pltpu.VMEM((IDX_HEADS, LANES, t), BF16)],
        compiler_params=_params("parallel", "arbitrary"),
    )(idx, idx, idx, lt)


def _attn_c_kernel(q_ref, k_ref, vt_ref, b_ref, o_ref, qz_ref, m_ref, l_ref, acc_ref):
    t = q_ref.shape[1]
    i = pl.program_id(1)
    _load_qt(q_ref, qz_ref)
    m_ref[...] = jnp.full(m_ref.shape, NEG, F32)
    l_ref[...] = jnp.zeros(l_ref.shape, F32)
    acc_ref[...] = jnp.zeros(acc_ref.shape, F32)

    key, qry = _tile_ids(KEY_TILE)
    ident = jnp.where(key == qry, 1.0, 0.0).astype(BF16)

    sub = t // KEY_TILE

    def body(g, carry):
        scores = []
        for r in range(sub):
            u = g * sub + r
            bias = b_ref[0, 0, u]
            for n in range(C_HEADS):
                kb = k_ref[0, _key_rows(u), (n // 2) * LANES:(n // 2 + 1) * LANES]
                lhs = jnp.concatenate([kb, ident], axis=1)
                rhs = jnp.concatenate([qz_ref[n], bias], axis=0)
                scores.append(_dot(lhs, rhs))
        for r in range(sub):
            for n in range(C_HEADS):
                vt = vt_ref[0, g * sub + r, n * HEAD_DIM:(n + 1) * HEAD_DIM, :]
                _softmax_step(n, scores[r * C_HEADS + n], vt, m_ref, l_ref, acc_ref)
        return carry

    _in_pairs(i + 1, body, 0)
    o = acc_ref[...] / l_ref[...]
    o_ref[0] = o.reshape(C_HEADS * HEAD_DIM, t).T.astype(o_ref.dtype)


def _attn_c_call(qk, vt, bias):
    b, s, _ = qk.shape
    nk, _, kt = vt.shape[1:]
    nt, t = bias.shape[1], bias.shape[4]
    return pl.pallas_call(
        _attn_c_kernel,
        grid=(b, nt),
        in_specs=[pl.BlockSpec((1, t, GROUP), lambda bi, i: (bi, i, 2)),
                  pl.BlockSpec((1, s, GROUP), lambda bi, i: (bi, 0, 3)),
                  pl.BlockSpec((1, nk, GROUP, kt), lambda bi, i: (bi, 0, 2, 0)),
                  pl.BlockSpec((1, 1, nk, kt, t), lambda bi, i: (bi, i, 0, 0, 0))],
        out_specs=pl.BlockSpec((1, t, GROUP), lambda bi, i: (bi, i, 0)),
        out_shape=jax.ShapeDtypeStruct((b, s, GROUP), BF16),
        scratch_shapes=[pltpu.VMEM((C_HEADS, LANES, t), BF16), pltpu.VMEM((C_HEADS, 1, t), F32),
                        pltpu.VMEM((C_HEADS, 1, t), F32), pltpu.VMEM((C_HEADS, HEAD_DIM, t), F32)],
        compiler_params=_params("parallel", "arbitrary"),
    )(qk, qk, vt, bias)


def _merge_kernel(x_ref, h_ref, ya_ref, yb_ref, yc_ref, wg_ref, wa_ref, wb_ref, wc_ref, wo_ref, o_ref):
    d = x_ref.shape[1]
    h = h_ref[...]
    merged = jnp.zeros(x_ref.shape, F32)
    for n, (y_ref, w_ref) in enumerate(((ya_ref, wa_ref), (yb_ref, wb_ref), (yc_ref, wc_ref))):
        gate = jax.nn.sigmoid(_dot(h, wg_ref[0, :, n * d:(n + 1) * d]))
        merged = merged + gate * _dot(y_ref[...], w_ref[0])
    o_ref[...] = x_ref[...] + _dot(merged.astype(BF16), wo_ref[0])


def _merge_call(x2, h, ya, yb, yc, wg, wa, wb, wc, wo, layer):
    n, d = x2.shape
    tm = ROW_TILE
    row = lambda cols: pl.BlockSpec((tm, cols), lambda i: (i, 0))
    full = lambda w: pl.BlockSpec((1,) + w.shape[1:], lambda i: (layer, 0, 0))
    return pl.pallas_call(
        _merge_kernel,
        grid=(n // tm,),
        in_specs=[row(d), row(d), row(ya.shape[1]), row(yb.shape[1]), row(yc.shape[1]),
                  full(wg), full(wa), full(wb), full(wc), full(wo)],
        out_specs=row(d),
        out_shape=jax.ShapeDtypeStruct((n, d), F32),
        compiler_params=_params("parallel"),
    )(x2, h, ya, yb, yc, wg, wa, wb, wc, wo)


def _mlp_kernel(x_ref, p_ref, gm_ref, wu_ref, wd_ref, gp_ref, wpg_ref, wpp_ref, o_ref):
    x = x_ref[...]
    ms = jnp.mean(x * x, axis=-1, keepdims=True)
    h = (x * lax.rsqrt(ms + EPS) * gm_ref[0]).astype(BF16)
    for f in range(wu_ref.shape[2] // FF_TILE):
        u = jnp.maximum(_dot(h, wu_ref[0, :, f * FF_TILE:(f + 1) * FF_TILE]), 0.0)
        x = x + _dot((u * u).astype(BF16), wd_ref[0, f * FF_TILE:(f + 1) * FF_TILE, :])
    ms = jnp.mean(x * x, axis=-1, keepdims=True)
    hp = (x * lax.rsqrt(ms + EPS) * gp_ref[0]).astype(BF16)
    gate = jax.nn.sigmoid(_dot(hp, wpg_ref[0]))
    o_ref[...] = x + gate * _dot(p_ref[0].astype(BF16), wpp_ref[0])


def _mlp_call(x2, p3, g_mlp, w_up, w_down, g_ple, w_pg, w_pp, layer):
    n, d = x2.shape
    pd = p3.shape[2]
    tm = ROW_TILE
    full = lambda w: pl.BlockSpec((1,) + w.shape[1:], lambda i: (layer, 0, 0),
                                  pipeline_mode=pl.Buffered(1))
    return pl.pallas_call(
        _mlp_kernel,
        grid=(n // tm,),
        in_specs=[pl.BlockSpec((tm, d), lambda i: (i, 0)),
                  pl.BlockSpec((1, tm, pd), lambda i: (layer, i, 0)),
                  full(g_mlp), full(w_up), full(w_down), full(g_ple), full(w_pg), full(w_pp)],
        out_specs=pl.BlockSpec((tm, d), lambda i: (i, 0)),
        out_shape=jax.ShapeDtypeStruct((n, d), F32),
        compiler_params=_params("parallel"),
    )(x2, p3, g_mlp, w_up, w_down, g_ple, w_pg, w_pp)


def _tile_gain(g, reps, scale=1.0):
    return jnp.tile(g.astype(F32), (1, reps)) * scale


def kernel(x, p, positions, attn_norm, w_in, a_q_norm, a_k_norm, a_lambda, a_subln,
           c_q_norm, c_k_norm, idx_k_norm, w_br_a, w_br_b, w_br_c, w_out,
           mlp_norm, w_up, w_down, ple_norm, w_ple_gate, w_ple_proj):
    b, s, d = x.shape
    depth = w_in.shape[0]
    n = b * s
    topk = min(TOPK_MAX, s // 4)
    t = min(ATTN_TILE, s)
    scale = HEAD_DIM ** -0.5
    qscale = scale * LOG2E

    inv = ROPE_THETA ** (-jnp.arange(0, HEAD_DIM, 2, dtype=F32) / HEAD_DIM)
    ang = positions.astype(F32)[..., None] * inv
    cos, sin = jnp.cos(ang), jnp.sin(ang)
    cos_t = jnp.concatenate([cos, cos, cos, cos], axis=-1).reshape(n, LANES)
    sin_t = jnp.concatenate([-sin, sin, -sin, sin], axis=-1).reshape(n, LANES)

    qa, ka, va = 0, 512, 1024
    qb, kb, vb = 1536, 2048, 2560
    qc, kc, vc = 3072, 3584, 4096
    qi = 4608
    gl = 4932
    col = lambda a, width=GROUP: w_in[:, :, a:a + width]
    w_qk = jnp.concatenate([col(qa), col(ka), col(qc), col(kc)], axis=-1).astype(BF16)
    w_bqk = jnp.concatenate([col(qb) * qscale, col(kb)], axis=-1).astype(BF16)
    w_vt = jnp.concatenate([col(va), col(vb), col(vc)], axis=-1).astype(BF16)
    w_idx = jnp.pad(col(qi, gl - qi), ((0, 0), (0, 0), (0, IDX_COLS - (gl - qi)))).astype(BF16)
    w_gl = col(gl, N_BRANCHES * d).astype(BF16)
    heads = GROUP // HEAD_DIM
    gain_qk = jnp.concatenate([_tile_gain(a_q_norm, heads, qscale), _tile_gain(a_k_norm, heads),
                               _tile_gain(c_q_norm, heads, qscale), _tile_gain(c_k_norm, heads)],
                              axis=-1)[:, None, :]
    gain_idx = jnp.pad(idx_k_norm.astype(F32), ((0, 0), (0, LANES - IDX_DIM)))[:, None, :]
    scale_idx = jnp.concatenate([jnp.full((IDX_Q,), IDX_DIM ** -0.5, F32), jnp.ones((IDX_DIM,), F32),
                                 jnp.full((IDX_HEADS,), IDX_HEADS ** -0.5, F32),
                                 jnp.zeros((LANES - IDX_DIM - IDX_HEADS,), F32)])[None, :]
    gid = jnp.arange(GROUP) // HEAD_DIM
    gmat = (gid[:, None] == gid[None, :]).astype(BF16)
    kt = min(KEY_TILE, s)
    ar = jnp.arange(kt)
    later_mat = jnp.concatenate([(ar[None, :] > ar[:, None]).astype(BF16),
                                 jnp.ones((ONES_ROWS, kt), BF16)], axis=0)
    before_mat = (ar[None, :] < ar[:, None]).astype(BF16)

    wa, wb, wc, wo = (w.astype(BF16) for w in (w_br_a, w_br_b, w_br_c, w_out))
    wu, wd, wpg, wpp = (w.astype(BF16) for w in (w_up, w_down, w_ple_gate, w_ple_proj))
    g_attn, g_mlp, g_ple = (g.astype(F32)[:, None, :] for g in (attn_norm, mlp_norm, ple_norm))
    subln = a_subln.astype(F32)[:, None, :]
    lam_p = a_lambda.astype(F32)
    p3 = p.reshape(depth, n, p.shape[-1])

    x2 = x.reshape(n, d)
    for i in range(depth):
        lam_init = 0.8 - 0.6 * math.exp(-0.3 * i)
        h, qk, bqk, vt, idx = _front_call(x2, g_attn, w_qk, w_bqk, w_vt, w_idx, gain_qk, gain_idx,
                                          scale_idx, cos_t, sin_t, gmat, i, kt)
        qk, bqk, idx = (a.reshape(b, s, -1) for a in (qk, bqk, idx))
        vt = vt.reshape(b, s // kt, -1, kt)
        ya, yb = _attn_ab_call(qk, bqk, vt, lam_p, subln, later_mat, i, lam_init, t)
        bias = _dsa_mask_call(idx, before_mat, topk, t)
        yc = _attn_c_call(qk, vt, bias)
        x2 = _merge_call(x2, h, ya.reshape(n, -1), yb.reshape(n, -1), yc.reshape(n, -1),
                         w_gl, wa, wb, wc, wo, i)
        x2 = _mlp_call(x2, p3, g_mlp, wu, wd, g_ple, wpg, wpp, i)
    return x2.reshape(b, s, d)
```

```python
import functools
import math

import jax
import jax.numpy as jnp
from jax import lax
from jax.experimental import pallas as pl
from jax.experimental.pallas import tpu as pltpu

F32 = jnp.float32
BF16 = jnp.bfloat16

HEAD_DIM = 64
CHUNK = 64
A_HEADS = 4
B_HEADS = 8
C_HEADS = 8
IDX_HEADS = 4
IDX_DIM = 64
TOPK_MAX = 256
N_BRANCHES = 3
ROPE_THETA = 10000.0
EPS = 1e-6

LANES = 128
SUBLANES = 8
HALF_HEAD = HEAD_DIM // 2
NEG = -1e30
INT_MIN = -(2 ** 31)
LOG2E = 1.4426950408889634
VMEM_LIMIT = 56 * 1024 * 1024

ROW_TILE = 512
ATTN_TILE = 256
KEY_TILE = 128
FF_TILE = 1024
GROUP = 512


def _params(*sem):
    return pltpu.CompilerParams(dimension_semantics=sem, vmem_limit_bytes=VMEM_LIMIT)


def _dot(a, b):
    return jnp.dot(a, b, preferred_element_type=F32)


def _swap_halves(y):
    lane = lax.broadcasted_iota(jnp.int32, y.shape, 1)
    return jnp.where((lane & HALF_HEAD) != 0, pltpu.roll(y, HALF_HEAD, 1),
                     pltpu.roll(y, LANES - HALF_HEAD, 1))


def _rope_slab(y, cos, sin):
    return y * cos + _swap_halves(y) * sin


IDX_Q = IDX_HEADS * IDX_DIM
IDX_COLS = IDX_Q + LANES


def _front_kernel(x_ref, g_ref, wqk_ref, wbqk_ref, wvt_ref, widx_ref, gqk_ref, gidx_ref, sidx_ref,
                  cos_ref, sin_ref, gmat_ref, h_ref, qk_ref, bqk_ref, vt_ref, idx_ref):
    x = x_ref[...]
    ms = jnp.mean(x * x, axis=-1, keepdims=True)
    h = (x * lax.rsqrt(ms + EPS) * g_ref[0]).astype(BF16)
    h_ref[...] = h
    cos = cos_ref[...]
    sin = sin_ref[...]

    gmat = gmat_ref[...]
    for j in range(qk_ref.shape[1] // GROUP):
        cols = slice(j * GROUP, (j + 1) * GROUP)
        t = _dot(h, wqk_ref[0, :, cols])
        ms = _dot((t * t).astype(BF16), gmat) * (1.0 / HEAD_DIM)
        y = t * lax.rsqrt(ms + EPS) * gqk_ref[0, :, cols]
        for s in range(GROUP // LANES):
            sl = slice(s * LANES, (s + 1) * LANES)
            qk_ref[:, j * GROUP + s * LANES:j * GROUP + (s + 1) * LANES] = (
                _rope_slab(y[:, sl], cos, sin).astype(qk_ref.dtype))

    for j in range(bqk_ref.shape[1] // GROUP):
        cols = slice(j * GROUP, (j + 1) * GROUP)
        bqk_ref[:, cols] = _dot(h, wbqk_ref[0, :, cols]).astype(bqk_ref.dtype)

    tt = vt_ref.shape[2]
    for j in range(vt_ref.shape[1] // GROUP):
        cols = slice(j * GROUP, (j + 1) * GROUP)
        t = _dot(h, wvt_ref[0, :, cols])
        for r in range(vt_ref.shape[0]):
            vt_ref[r, cols, :] = t[r * tt:(r + 1) * tt, :].T.astype(vt_ref.dtype)

    t = _dot(h, widx_ref[0])
    scale = sidx_ref[...]
    for s in range(IDX_Q // LANES):
        sl = slice(s * LANES, (s + 1) * LANES)
        idx_ref[:, sl] = _rope_slab(t[:, sl], cos, sin) * scale[:, sl]
    sl = slice(IDX_Q, IDX_COLS)
    tk = t[:, sl]
    lane = lax.broadcasted_iota(jnp.int32, tk.shape, 1)
    is_k = lane < IDX_DIM
    ms = jnp.sum(jnp.where(is_k, tk * tk, 0.0), axis=-1, keepdims=True) * (1.0 / IDX_DIM)
    yk = tk * lax.rsqrt(ms + EPS) * gidx_ref[0]
    idx_ref[:, sl] = jnp.where(is_k, _rope_slab(yk, cos, sin), tk) * scale[:, sl]


def _front_call(x2, g_attn, w_qk, w_bqk, w_vt, w_idx, gain_qk, gain_idx, scale_idx, cos_t, sin_t,
                gmat, layer, tt):
    n, d = x2.shape
    tm = ROW_TILE
    row = lambda cols: pl.BlockSpec((tm, cols), lambda i: (i, 0))
    full = lambda w: pl.BlockSpec((1,) + w.shape[1:], lambda i: (layer, 0, 0))
    fixed = lambda a: pl.BlockSpec(a.shape, lambda i: (0, 0))
    cqk, cbqk, cvt = w_qk.shape[2], w_bqk.shape[2], w_vt.shape[2]
    return pl.pallas_call(
        _front_kernel,
        grid=(n // tm,),
        in_specs=[row(d), full(g_attn), full(w_qk), full(w_bqk), full(w_vt), full(w_idx),
                  full(gain_qk), full(gain_idx), fixed(scale_idx), row(LANES), row(LANES),
                  fixed(gmat)],
        out_specs=[row(d), row(cqk), row(cbqk),
                   pl.BlockSpec((tm // tt, cvt, tt), lambda i: (i, 0, 0)), row(IDX_COLS)],
        out_shape=[jax.ShapeDtypeStruct((n, d), BF16), jax.ShapeDtypeStruct((n, cqk), BF16),
                   jax.ShapeDtypeStruct((n, cbqk), BF16),
                   jax.ShapeDtypeStruct((n // tt, cvt, tt), BF16),
                   jax.ShapeDtypeStruct((n, IDX_COLS), F32)],
        compiler_params=_params("parallel"),
    )(x2, g_attn, w_qk, w_bqk, w_vt, w_idx, gain_qk, gain_idx, scale_idx, cos_t, sin_t, gmat)


def _load_qt(q_ref, qz_ref):
    qt = q_ref[0].astype(F32).T
    rows = lax.broadcasted_iota(jnp.int32, (LANES, qt.shape[1]), 0)
    for s in range(qt.shape[0] // LANES):
        slab = qt[s * LANES:(s + 1) * LANES, :]
        qz_ref[2 * s] = jnp.where(rows < HEAD_DIM, slab, 0.0).astype(BF16)
        qz_ref[2 * s + 1] = jnp.where(rows >= HEAD_DIM, slab, 0.0).astype(BF16)


ONES_ROWS = 16


def _softmax_step(n, s, vt, m_ref, l_ref, acc_ref):
    m_old = m_ref[n]
    m_new = jnp.maximum(m_old, jnp.max(s, axis=0, keepdims=True))
    p = jnp.exp2(s - m_new).astype(BF16)
    alpha = jnp.exp2(m_old - m_new)
    f = vt.shape[0]
    pv = _dot(jnp.concatenate([vt, jnp.ones((ONES_ROWS, vt.shape[1]), BF16)], axis=0), p)
    l_ref[n] = alpha * l_ref[n] + pv[f:f + 1]
    acc_ref[n] = alpha * acc_ref[n] + pv[:f]
    m_ref[n] = m_new


def _tile_ids(t):
    key = lax.broadcasted_iota(jnp.int32, (KEY_TILE, t), 0)
    qry = lax.broadcasted_iota(jnp.int32, (KEY_TILE, t), 1)
    return key, qry


def _key_rows(u):
    return pl.ds(pl.multiple_of(u * KEY_TILE, KEY_TILE), KEY_TILE)


def _in_pairs(n, one, carry):
    carry = lax.fori_loop(0, n // 2, lambda p, c: one(2 * p + 1, one(2 * p, c)), carry)
    return lax.cond(n % 2 == 1, lambda c: one(n - 1, c), lambda c: c, carry)


def _interleave(*gens):
    gens = list(gens)
    while gens:
        for gen in list(gens):
            try:
                next(gen)
            except StopIteration:
                gens.remove(gen)


def _attn_ab_kernel(lam_ref, aq_ref, ak_ref, avt_ref, g_ref, bq_ref, bk_ref, bvt_ref, lmat_ref,
                    ya_ref, yb_ref, aqz_ref, m_ref, l_ref, aacc_ref, bqz_ref, r_ref, bacc_ref, *,
                    lam_init):
    t = aq_ref.shape[1]
    i = pl.program_id(1)
    lp = lam_ref[0]
    lam = (jnp.exp(jnp.sum(lp[0:1] * lp[1:2], axis=-1, keepdims=True))
           - jnp.exp(jnp.sum(lp[2:3] * lp[3:4], axis=-1, keepdims=True)) + lam_init)
    key, qry = _tile_ids(t)
    sub = t // KEY_TILE
    lmat = lmat_ref[...]
    _load_qt(aq_ref, aqz_ref)
    _load_qt(bq_ref, bqz_ref)
    m_ref[...] = jnp.full(m_ref.shape, NEG, F32)
    l_ref[...] = jnp.zeros(l_ref.shape, F32)
    aacc_ref[...] = jnp.zeros(aacc_ref.shape, F32)
    r_ref[...] = jnp.zeros(r_ref.shape, F32)
    bacc_ref[...] = jnp.zeros(bacc_ref.shape, F32)

    def a_group(g, diag):
        scores = []
        for r in range(sub):
            for n in range(2 * A_HEADS):
                kb = ak_ref[0, _key_rows(g * sub + r), (n // 2) * LANES:(n // 2 + 1) * LANES]
                scores.append(_dot(kb, aqz_ref[n]))
        yield
        for r in range(sub):
            for n in range(2 * A_HEADS):
                s = scores[r * 2 * A_HEADS + n]
                if diag:
                    s = jnp.where(((key + r * KEY_TILE) // CHUNK) <= (qry // CHUNK), s, NEG)
                vt = avt_ref[0, g * sub + r, (n // 2) * LANES:(n // 2 + 1) * LANES, :]
                _softmax_step(n, s, vt, m_ref, l_ref, aacc_ref)
                yield

    def b_group(g, diag):
        order = [(r, n) for r in reversed(range(sub)) for n in range(B_HEADS)]
        strict = {r: (key + r * KEY_TILE) < qry for r in range(sub)} if diag else None
        logits = {}
        for r, n in order:
            kb = bk_ref[0, _key_rows(g * sub + r), (n // 2) * LANES:(n // 2 + 1) * LANES]
            logits[r, n] = _dot(kb, bqz_ref[n])
        yield
        log_betas, laters, sums = {}, {}, {}
        for r, n in order:
            z = logits[r, n]
            neg_abs = pltpu.bitcast(pltpu.bitcast(z, jnp.int32) | jnp.int32(INT_MIN), F32)
            soft = jnp.log2(1.0 + jnp.exp2(neg_abs))
            log_beta = jnp.minimum(z, 0.0) - soft
            log_om = log_beta - z
            if diag:
                log_om = jnp.where(strict[r], log_om, 0.0)
            later = _dot(lmat, log_om.astype(BF16))
            laters[r, n] = later[:KEY_TILE]
            log_betas[r, n] = log_beta
            sums[r, n] = later[KEY_TILE:KEY_TILE + 1]
            yield
        for r, n in order:
            a = jnp.exp2(log_betas[r, n] + laters[r, n])
            if diag:
                a = jnp.where(strict[r], a, 0.0)
            vt = bvt_ref[0, g * sub + r, n * HEAD_DIM:(n + 1) * HEAD_DIM, :]
            visited = r_ref[n]
            bacc_ref[n] = bacc_ref[n] + _dot(vt, a.astype(BF16)) * jnp.exp2(visited)
            r_ref[n] = visited + sums[r, n]
            yield

    def body(n, carry):
        _interleave(a_group(n, False), b_group(i - 1 - n, False))
        return carry

    _interleave(b_group(i, True))
    _in_pairs(i, body, 0)
    _interleave(a_group(i, True))

    gain = g_ref[0] * (1.0 - lam_init)
    for h in range(A_HEADS):
        o = aacc_ref[2 * h] / l_ref[2 * h] - lam * (aacc_ref[2 * h + 1] / l_ref[2 * h + 1])
        ms = jnp.mean(o * o, axis=0, keepdims=True)
        y = (o * lax.rsqrt(ms + EPS)).T * gain
        ya_ref[0, :, h * LANES:(h + 1) * LANES] = y.astype(ya_ref.dtype)
    yb_ref[0] = bacc_ref[...].reshape(B_HEADS * HEAD_DIM, t).T.astype(yb_ref.dtype)


def _attn_ab_call(qk, bqk, vt, lam_p, subln, lmat, layer, lam_init, t):
    b, s, _ = qk.shape
    nt, _, kt = vt.shape[1:]
    ns = 2 * A_HEADS
    qspec = pl.BlockSpec((1, t, GROUP), lambda bi, i: (bi, i, 0))
    kspec = pl.BlockSpec((1, s, GROUP), lambda bi, i: (bi, 0, 1))
    vspec = lambda f: pl.BlockSpec((1, nt, GROUP, kt), lambda bi, i: (bi, 0, f, 0))
    return pl.pallas_call(
        functools.partial(_attn_ab_kernel, lam_init=lam_init),
        grid=(b, s // t),
        in_specs=[pl.BlockSpec((1, 4, HEAD_DIM), lambda bi, i: (layer, 0, 0)),
                  qspec, kspec, vspec(0),
                  pl.BlockSpec((1, 1, LANES), lambda bi, i: (layer, 0, 0)),
                  qspec, kspec, vspec(1),
                  pl.BlockSpec(lmat.shape, lambda bi, i: (0, 0))],
        out_specs=[qspec, qspec],
        out_shape=[jax.ShapeDtypeStruct((b, s, GROUP), BF16)] * 2,
        scratch_shapes=[pltpu.VMEM((ns, LANES, t), BF16), pltpu.VMEM((ns, 1, t), F32),
                        pltpu.VMEM((ns, 1, t), F32), pltpu.VMEM((ns, LANES, t), F32),
                        pltpu.VMEM((B_HEADS, LANES, t), BF16), pltpu.VMEM((B_HEADS, 1, t), F32),
                        pltpu.VMEM((B_HEADS, HEAD_DIM, t), F32)],
        compiler_params=_params("parallel", "arbitrary"),
    )(lam_p, qk, qk, vt, subln, bqk, bqk, vt, lmat)


def _fold_rows(x):
    parts = [x[r * SUBLANES:(r + 1) * SUBLANES] for r in range(x.shape[0] // SUBLANES)]
    while len(parts) > 1:
        parts = [a + b for a, b in zip(parts[0::2], parts[1::2])]
    return parts[0]


def _dsa_mask_kernel(q_ref, kw_ref, wq_ref, lt_ref, o_ref, key_ref, qz_ref, *, topk):
    t = q_ref.shape[1]
    nj = key_ref.shape[0]
    i = pl.program_id(1)
    nch = (i + 1) * (t // KEY_TILE)
    qt = q_ref[0].T
    pad = jnp.zeros((LANES - IDX_DIM, t), F32)
    for h in range(IDX_HEADS):
        qz_ref[h] = jnp.concatenate([qt[h * IDX_DIM:(h + 1) * IDX_DIM], pad], axis=0).astype(BF16)
    wt = wq_ref[0].T
    wh = [wt[IDX_DIM + h:IDX_DIM + h + 1, :] for h in range(IDX_HEADS)]
    key0, qry0 = _tile_ids(t)
    sub = t // KEY_TILE

    def chunk_ok(r):
        return ((key0 + r * KEY_TILE) // CHUNK) <= (qry0 // CHUNK)

    def fill(g, diag):
        for r in range(sub):
            c = g * sub + r
            kb = kw_ref[0, _key_rows(c), :].astype(BF16)
            score = jnp.zeros((KEY_TILE, t), F32)
            for h in range(IDX_HEADS):
                score = score + wh[h] * jnp.maximum(_dot(kb, qz_ref[h]), 0.0)
            if diag:
                score = jnp.where(chunk_ok(r), score, -jnp.inf)
            bits = pltpu.bitcast(score, jnp.int32)
            okey = jnp.where(bits < 0, bits ^ jnp.int32(0x7FFFFFFF), bits)
            key_ref[c] = jnp.where(score == 0.0, 0, okey)

    def fill_one(g, carry):
        fill(g, False)
        return carry

    _in_pairs(i, fill_one, 0)
    fill(i, True)

    def count_at_or_above(cand):
        def one(g, acc):
            for r in range(sub):
                acc = acc + _fold_rows(jnp.where(key_ref[g * sub + r] >= cand, 1.0, 0.0))
            return acc
        acc = _in_pairs(i + 1, one, jnp.zeros((SUBLANES, t), F32))
        return jnp.sum(acc, axis=0, keepdims=True)

    def bit_body(n, carry):
        lo, above = carry
        cand = lo + jnp.left_shift(jnp.int32(1), 31 - n)
        cnt = count_at_or_above(cand)
        take = cnt >= topk
        return jnp.where(take, cand, lo), jnp.where(take, above, cnt)

    thr, above = lax.fori_loop(
        0, 32, bit_body, (jnp.full((1, t), INT_MIN, jnp.int32), jnp.zeros((1, t), F32)))
    need = topk - above
    lt = lt_ref[...]

    def emit(g, run, diag):
        for r in range(sub):
            c = g * sub + r
            okey = key_ref[c]
            eq = okey == thr
            eqf = jnp.where(eq, 1.0, 0.0)
            before = _dot(lt, eqf.astype(BF16)) + run
            sel = (okey > thr) | (eq & (before < need))
            if diag:
                sel = sel & chunk_ok(r)
            o_ref[0, 0, c] = jnp.where(sel, 0.0, NEG).astype(o_ref.dtype)
            run = run + jnp.sum(eqf, axis=0, keepdims=True)
        return run

    run = _in_pairs(i, lambda g, run: emit(g, run, False), jnp.zeros((1, t), F32))
    emit(i, run, True)

    def blank(c, carry):
        o_ref[0, 0, c] = jnp.full((KEY_TILE, t), NEG, o_ref.dtype)
        return carry

    lax.fori_loop(nch, nj, blank, 0)


def _dsa_mask_call(idx, lt, topk, t):
    b, s, _ = idx.shape
    kt = lt.shape[0]
    assert t >= topk
    nt, nk = s // t, s // kt
    return pl.pallas_call(
        functools.partial(_dsa_mask_kernel, topk=topk),
        grid=(b, nt),
        in_specs=[pl.BlockSpec((1, t, IDX_Q), lambda bi, i: (bi, i, 0)),
                  pl.BlockSpec((1, s, LANES), lambda bi, i: (bi, 0, IDX_Q // LANES)),
                  pl.BlockSpec((1, t, LANES), lambda bi, i: (bi, i, IDX_Q // LANES)),
                  pl.BlockSpec((kt, kt), lambda bi, i: (0, 0))],
        out_specs=pl.BlockSpec((1, 1, nk, kt, t), lambda bi, i: (bi, i, 0, 0, 0)),
        out_shape=jax.ShapeDtypeStruct((b, nt, nk, kt, t), BF16),
        scratch_shapes=[pltpu.VMEM((nk, kt, t), jnp.int32), pltpu.VMEM((IDX_HEADS, LANES, t), BF16)],
        compiler_params=_params("parallel", "arbitrary"),
    )(idx, idx, idx, lt)


def _attn_c_kernel(q_ref, k_ref, vt_ref, b_ref, o_ref, qz_ref, m_ref, l_ref, acc_ref):
    t = q_ref.shape[1]
    i = pl.program_id(1)
    _load_qt(q_ref, qz_ref)
    m_ref[...] = jnp.full(m_ref.shape, NEG, F32)
    l_ref[...] = jnp.zeros(l_ref.shape, F32)
    acc_ref[...] = jnp.zeros(acc_ref.shape, F32)

    key, qry = _tile_ids(KEY_TILE)
    ident = jnp.where(key == qry, 1.0, 0.0).astype(BF16)

    sub = t // KEY_TILE

    def body(g, carry):
        scores = []
        for r in range(sub):
            u = g * sub + r
            bias = b_ref[0, 0, u]
            for n in range(C_HEADS):
                kb = k_ref[0, _key_rows(u), (n // 2) * LANES:(n // 2 + 1) * LANES]
                lhs = jnp.concatenate([kb, ident], axis=1)
                rhs = jnp.concatenate([qz_ref[n], bias], axis=0)
                scores.append(_dot(lhs, rhs))
        for r in range(sub):
            for n in range(C_HEADS):
                vt = vt_ref[0, g * sub + r, n * HEAD_DIM:(n + 1) * HEAD_DIM, :]
                _softmax_step(n, scores[r * C_HEADS + n], vt, m_ref, l_ref, acc_ref)
        return carry

    _in_pairs(i + 1, body, 0)
    o = acc_ref[...] / l_ref[...]
    o_ref[0] = o.reshape(C_HEADS * HEAD_DIM, t).T.astype(o_ref.dtype)


def _attn_c_call(qk, vt, bias):
    b, s, _ = qk.shape
    nk, _, kt = vt.shape[1:]
    nt, t = bias.shape[1], bias.shape[4]
    return pl.pallas_call(
        _attn_c_kernel,
        grid=(b, nt),
        in_specs=[pl.BlockSpec((1, t, GROUP), lambda bi, i: (bi, i, 2)),
                  pl.BlockSpec((1, s, GROUP), lambda bi, i: (bi, 0, 3)),
                  pl.BlockSpec((1, nk, GROUP, kt), lambda bi, i: (bi, 0, 2, 0)),
                  pl.BlockSpec((1, 1, nk, kt, t), lambda bi, i: (bi, i, 0, 0, 0))],
        out_specs=pl.BlockSpec((1, t, GROUP), lambda bi, i: (bi, i, 0)),
        out_shape=jax.ShapeDtypeStruct((b, s, GROUP), BF16),
        scratch_shapes=[pltpu.VMEM((C_HEADS, LANES, t), BF16), pltpu.VMEM((C_HEADS, 1, t), F32),
                        pltpu.VMEM((C_HEADS, 1, t), F32), pltpu.VMEM((C_HEADS, HEAD_DIM, t), F32)],
        compiler_params=_params("parallel", "arbitrary"),
    )(qk, qk, vt, bias)


def _merge_kernel(x_ref, h_ref, ya_ref, yb_ref, yc_ref, wg_ref, wa_ref, wb_ref, wc_ref, wo_ref, o_ref):
    d = x_ref.shape[1]
    h = h_ref[...]
    merged = jnp.zeros(x_ref.shape, F32)
    for n, (y_ref, w_ref) in enumerate(((ya_ref, wa_ref), (yb_ref, wb_ref), (yc_ref, wc_ref))):
        gate = jax.nn.sigmoid(_dot(h, wg_ref[0, :, n * d:(n + 1) * d]))
        merged = merged + gate * _dot(y_ref[...], w_ref[0])
    o_ref[...] = x_ref[...] + _dot(merged.astype(BF16), wo_ref[0])


def _merge_call(x2, h, ya, yb, yc, wg, wa, wb, wc, wo, layer):
    n, d = x2.shape
    tm = ROW_TILE
    row = lambda cols: pl.BlockSpec((tm, cols), lambda i: (i, 0))
    full = lambda w: pl.BlockSpec((1,) + w.shape[1:], lambda i: (layer, 0, 0))
    return pl.pallas_call(
        _merge_kernel,
        grid=(n // tm,),
        in_specs=[row(d), row(d), row(ya.shape[1]), row(yb.shape[1]), row(yc.shape[1]),
                  full(wg), full(wa), full(wb), full(wc), full(wo)],
        out_specs=row(d),
        out_shape=jax.ShapeDtypeStruct((n, d), F32),
        compiler_params=_params("parallel"),
    )(x2, h, ya, yb, yc, wg, wa, wb, wc, wo)


def _mlp_kernel(x_ref, p_ref, gm_ref, wu_ref, wd_ref, gp_ref, wpg_ref, wpp_ref, o_ref):
    x = x_ref[...]
    ms = jnp.mean(x * x, axis=-1, keepdims=True)
    h = (x * lax.rsqrt(ms + EPS) * gm_ref[0]).astype(BF16)
    for f in range(wu_ref.shape[2] // FF_TILE):
        u = jnp.maximum(_dot(h, wu_ref[0, :, f * FF_TILE:(f + 1) * FF_TILE]), 0.0)
        x = x + _dot((u * u).astype(BF16), wd_ref[0, f * FF_TILE:(f + 1) * FF_TILE, :])
    ms = jnp.mean(x * x, axis=-1, keepdims=True)
    hp = (x * lax.rsqrt(ms + EPS) * gp_ref[0]).astype(BF16)
    gate = jax.nn.sigmoid(_dot(hp, wpg_ref[0]))
    o_ref[...] = x + gate * _dot(p_ref[0].astype(BF16), wpp_ref[0])


def _mlp_call(x2, p3, g_mlp, w_up, w_down, g_ple, w_pg, w_pp, layer):
    n, d = x2.shape
    pd = p3.shape[2]
    tm = ROW_TILE
    full = lambda w: pl.BlockSpec((1,) + w.shape[1:], lambda i: (layer, 0, 0),
                                  pipeline_mode=pl.Buffered(1))
    return pl.pallas_call(
        _mlp_kernel,
        grid=(n // tm,),
        in_specs=[pl.BlockSpec((tm, d), lambda i: (i, 0)),
                  pl.BlockSpec((1, tm, pd), lambda i: (layer, i, 0)),
                  full(g_mlp), full(w_up), full(w_down), full(g_ple), full(w_pg), full(w_pp)],
        out_specs=pl.BlockSpec((tm, d), lambda i: (i, 0)),
        out_shape=jax.ShapeDtypeStruct((n, d), F32),
        compiler_params=_params("parallel"),
    )(x2, p3, g_mlp, w_up, w_down, g_ple, w_pg, w_pp)


def _tile_gain(g, reps, scale=1.0):
    return jnp.tile(g.astype(F32), (1, reps)) * scale


def kernel(x, p, positions, attn_norm, w_in, a_q_norm, a_k_norm, a_lambda, a_subln,
           c_q_norm, c_k_norm, idx_k_norm, w_br_a, w_br_b, w_br_c, w_out,
           mlp_norm, w_up, w_down, ple_norm, w_ple_gate, w_ple_proj):
    b, s, d = x.shape
    depth = w_in.shape[0]
    n = b * s
    topk = min(TOPK_MAX, s // 4)
    t = min(ATTN_TILE, s)
    scale = HEAD_DIM ** -0.5
    qscale = scale * LOG2E

    inv = ROPE_THETA ** (-jnp.arange(0, HEAD_DIM, 2, dtype=F32) / HEAD_DIM)
    ang = positions.astype(F32)[..., None] * inv
    cos, sin = jnp.cos(ang), jnp.sin(ang)
    cos_t = jnp.concatenate([cos, cos, cos, cos], axis=-1).reshape(n, LANES)
    sin_t = jnp.concatenate([-sin, sin, -sin, sin], axis=-1).reshape(n, LANES)

    qa, ka, va = 0, 512, 1024
    qb, kb, vb = 1536, 2048, 2560
    qc, kc, vc = 3072, 3584, 4096
    qi = 4608
    gl = 4932
    col = lambda a, width=GROUP: w_in[:, :, a:a + width]
    w_qk = jnp.concatenate([col(qa), col(ka), col(qc), col(kc)], axis=-1).astype(BF16)
    w_bqk = jnp.concatenate([col(qb) * qscale, col(kb)], axis=-1).astype(BF16)
    w_vt = jnp.concatenate([col(va), col(vb), col(vc)], axis=-1).astype(BF16)
    w_idx = jnp.pad(col(qi, gl - qi), ((0, 0), (0, 0), (0, IDX_COLS - (gl - qi)))).astype(BF16)
    w_gl = col(gl, N_BRANCHES * d).astype(BF16)
    heads = GROUP // HEAD_DIM
    gain_qk = jnp.concatenate([_tile_gain(a_q_norm, heads, qscale), _tile_gain(a_k_norm, heads),
                               _tile_gain(c_q_norm, heads, qscale), _tile_gain(c_k_norm, heads)],
                              axis=-1)[:, None, :]
    gain_idx = jnp.pad(idx_k_norm.astype(F32), ((0, 0), (0, LANES - IDX_DIM)))[:, None, :]
    scale_idx = jnp.concatenate([jnp.full((IDX_Q,), IDX_DIM ** -0.5, F32), jnp.ones((IDX_DIM,), F32),
                                 jnp.full((IDX_HEADS,), IDX_HEADS ** -0.5, F32),
                                 jnp.zeros((LANES - IDX_DIM - IDX_HEADS,), F32)])[None, :]
    gid = jnp.arange(GROUP) // HEAD_DIM
    gmat = (gid[:, None] == gid[None, :]).astype(BF16)
    kt = min(KEY_TILE, s)
    ar = jnp.arange(kt)
    later_mat = jnp.concatenate([(ar[None, :] > ar[:, None]).astype(BF16),
                                 jnp.ones((ONES_ROWS, kt), BF16)], axis=0)
    before_mat = (ar[None, :] < ar[:, None]).astype(BF16)

    wa, wb, wc, wo = (w.astype(BF16) for w in (w_br_a, w_br_b, w_br_c, w_out))
    wu, wd, wpg, wpp = (w.astype(BF16) for w in (w_up, w_down, w_ple_gate, w_ple_proj))
    g_attn, g_mlp, g_ple = (g.astype(F32)[:, None, :] for g in (attn_norm, mlp_norm, ple_norm))
    subln = a_subln.astype(F32)[:, None, :]
    lam_p = a_lambda.astype(F32)
    p3 = p.reshape(depth, n, p.shape[-1])

    x2 = x.reshape(n, d)
    for i in range(depth):
        lam_init = 0.8 - 0.6 * math.exp(-0.3 * i)
        h, qk, bqk, vt, idx = _front_call(x2, g_attn, w_qk, w_bqk, w_vt, w_idx, gain_qk, gain_idx,
                                          scale_idx, cos_t, sin_t, gmat, i, kt)
        qk, bqk, idx = (a.reshape(b, s, -1) for a in (qk, bqk, idx))
        vt = vt.reshape(b, s // kt, -1, kt)
        ya, yb = _attn_ab_call(qk, bqk, vt, lam_p, subln, later_mat, i, lam_init, t)
        bias = _dsa_mask_call(idx, before_mat, topk, t)
        yc = _attn_c_call(qk, vt, bias)
        x2 = _merge_call(x2, h, ya.reshape(n, -1), yb.reshape(n, -1), yc.reshape(n, -1),
                         w_gl, wa, wb, wc, wo, i)
        x2 = _mlp_call(x2, p3, g_mlp, wu, wd, g_ple, wpg, wpp, i)
    return x2.reshape(b, s, d)
```

```python
import functools
import math

import jax
import jax.numpy as jnp
from jax import lax
from jax.experimental import pallas as pl
from jax.experimental.pallas import tpu as pltpu

F32 = jnp.float32
BF16 = jnp.bfloat16

HEAD_DIM = 64
CHUNK = 64
A_HEADS = 4
B_HEADS = 8
C_HEADS = 8
IDX_HEADS = 4
IDX_DIM = 64
TOPK_MAX = 256
N_BRANCHES = 3
ROPE_THETA = 10000.0
EPS = 1e-6

LANES = 128
SUBLANES = 8
HALF_HEAD = HEAD_DIM // 2
NEG = -1e30
INT_MIN = -(2 ** 31)
LOG2E = 1.4426950408889634
V7X_VMEM_BYTES = 64 * 1024 * 1024
VMEM_LIMIT = V7X_VMEM_BYTES * 7 // 8

ROW_TILE = 512
ATTN_TILE = 256
KEY_TILE = 128
FF_TILE = 1024
GROUP = 512


def _params(*sem):
    return pltpu.CompilerParams(dimension_semantics=sem, vmem_limit_bytes=VMEM_LIMIT)


def _dot(a, b):
    return jnp.dot(a, b, preferred_element_type=F32)


def _swap_halves(y):
    lane = lax.broadcasted_iota(jnp.int32, y.shape, 1)
    return jnp.where((lane & HALF_HEAD) != 0, pltpu.roll(y, HALF_HEAD, 1),
                     pltpu.roll(y, LANES - HALF_HEAD, 1))


def _rope_slab(y, cos, sin):
    return y * cos + _swap_halves(y) * sin


IDX_Q = IDX_HEADS * IDX_DIM
IDX_COLS = IDX_Q + LANES


def _front_kernel(x_ref, g_ref, wqk_ref, wbqk_ref, wvt_ref, widx_ref, gqk_ref, gidx_ref, sidx_ref,
                  cos_ref, sin_ref, gmat_ref, h_ref, qk_ref, bqk_ref, vt_ref, idx_ref):
    x = x_ref[...]
    ms = jnp.mean(x * x, axis=-1, keepdims=True)
    h = (x * lax.rsqrt(ms + EPS) * g_ref[0]).astype(BF16)
    h_ref[...] = h
    cos = cos_ref[...]
    sin = sin_ref[...]

    gmat = gmat_ref[...]
    for j in range(qk_ref.shape[1] // GROUP):
        cols = slice(j * GROUP, (j + 1) * GROUP)
        t = _dot(h, wqk_ref[0, :, cols])
        ms = _dot((t * t).astype(BF16), gmat) * (1.0 / HEAD_DIM)
        y = t * lax.rsqrt(ms + EPS) * gqk_ref[0, :, cols]
        for s in range(GROUP // LANES):
            sl = slice(s * LANES, (s + 1) * LANES)
            qk_ref[:, j * GROUP + s * LANES:j * GROUP + (s + 1) * LANES] = (
                _rope_slab(y[:, sl], cos, sin).astype(qk_ref.dtype))

    for j in range(bqk_ref.shape[1] // GROUP):
        cols = slice(j * GROUP, (j + 1) * GROUP)
        bqk_ref[:, cols] = _dot(h, wbqk_ref[0, :, cols]).astype(bqk_ref.dtype)

    tt = vt_ref.shape[2]
    for j in range(vt_ref.shape[1] // GROUP):
        cols = slice(j * GROUP, (j + 1) * GROUP)
        t = _dot(h, wvt_ref[0, :, cols])
        for r in range(vt_ref.shape[0]):
            vt_ref[r, cols, :] = t[r * tt:(r + 1) * tt, :].T.astype(vt_ref.dtype)

    t = _dot(h, widx_ref[0])
    scale = sidx_ref[...]
    for s in range(IDX_Q // LANES):
        sl = slice(s * LANES, (s + 1) * LANES)
        idx_ref[:, sl] = _rope_slab(t[:, sl], cos, sin) * scale[:, sl]
    sl = slice(IDX_Q, IDX_COLS)
    tk = t[:, sl]
    lane = lax.broadcasted_iota(jnp.int32, tk.shape, 1)
    is_k = lane < IDX_DIM
    ms = jnp.sum(jnp.where(is_k, tk * tk, 0.0), axis=-1, keepdims=True) * (1.0 / IDX_DIM)
    yk = tk * lax.rsqrt(ms + EPS) * gidx_ref[0]
    idx_ref[:, sl] = jnp.where(is_k, _rope_slab(yk, cos, sin), tk) * scale[:, sl]


def _front_call(x2, g_attn, w_qk, w_bqk, w_vt, w_idx, gain_qk, gain_idx, scale_idx, cos_t, sin_t,
                gmat, layer, tt):
    n, d = x2.shape
    tm = ROW_TILE
    row = lambda cols: pl.BlockSpec((tm, cols), lambda i: (i, 0))
    full = lambda w: pl.BlockSpec((1,) + w.shape[1:], lambda i: (layer, 0, 0))
    fixed = lambda a: pl.BlockSpec(a.shape, lambda i: (0, 0))
    cqk, cbqk, cvt = w_qk.shape[2], w_bqk.shape[2], w_vt.shape[2]
    return pl.pallas_call(
        _front_kernel,
        grid=(n // tm,),
        in_specs=[row(d), full(g_attn), full(w_qk), full(w_bqk), full(w_vt), full(w_idx),
                  full(gain_qk), full(gain_idx), fixed(scale_idx), row(LANES), row(LANES),
                  fixed(gmat)],
        out_specs=[row(d), row(cqk), row(cbqk),
                   pl.BlockSpec((tm // tt, cvt, tt), lambda i: (i, 0, 0)), row(IDX_COLS)],
        out_shape=[jax.ShapeDtypeStruct((n, d), BF16), jax.ShapeDtypeStruct((n, cqk), BF16),
                   jax.ShapeDtypeStruct((n, cbqk), BF16),
                   jax.ShapeDtypeStruct((n // tt, cvt, tt), BF16),
                   jax.ShapeDtypeStruct((n, IDX_COLS), F32)],
        compiler_params=_params("parallel"),
    )(x2, g_attn, w_qk, w_bqk, w_vt, w_idx, gain_qk, gain_idx, scale_idx, cos_t, sin_t, gmat)


def _load_qt(q_ref, qz_ref):
    qt = q_ref[0].astype(F32).T
    rows = lax.broadcasted_iota(jnp.int32, (LANES, qt.shape[1]), 0)
    for s in range(qt.shape[0] // LANES):
        slab = qt[s * LANES:(s + 1) * LANES, :]
        qz_ref[2 * s] = jnp.where(rows < HEAD_DIM, slab, 0.0).astype(BF16)
        qz_ref[2 * s + 1] = jnp.where(rows >= HEAD_DIM, slab, 0.0).astype(BF16)


ONES_ROWS = 16


def _softmax_step(n, s, vt, m_ref, l_ref, acc_ref):
    m_old = m_ref[n]
    m_new = jnp.maximum(m_old, jnp.max(s, axis=0, keepdims=True))
    p = jnp.exp2(s - m_new).astype(BF16)
    alpha = jnp.exp2(m_old - m_new)
    f = vt.shape[0]
    pv = _dot(jnp.concatenate([vt, jnp.ones((ONES_ROWS, vt.shape[1]), BF16)], axis=0), p)
    l_ref[n] = alpha * l_ref[n] + pv[f:f + 1]
    acc_ref[n] = alpha * acc_ref[n] + pv[:f]
    m_ref[n] = m_new


def _tile_ids(t):
    key = lax.broadcasted_iota(jnp.int32, (KEY_TILE, t), 0)
    qry = lax.broadcasted_iota(jnp.int32, (KEY_TILE, t), 1)
    return key, qry


def _key_rows(u):
    return pl.ds(pl.multiple_of(u * KEY_TILE, KEY_TILE), KEY_TILE)


def _in_pairs(n, one, carry):
    carry = lax.fori_loop(0, n // 2, lambda p, c: one(2 * p + 1, one(2 * p, c)), carry)
    return lax.cond(n % 2 == 1, lambda c: one(n - 1, c), lambda c: c, carry)


def _interleave(*gens):
    gens = list(gens)
    while gens:
        for gen in list(gens):
            try:
                next(gen)
            except StopIteration:
                gens.remove(gen)


def _attn_ab_kernel(lam_ref, aq_ref, ak_ref, avt_ref, g_ref, bq_ref, bk_ref, bvt_ref, lmat_ref,
                    ya_ref, yb_ref, aqz_ref, m_ref, l_ref, aacc_ref, bqz_ref, r_ref, bacc_ref, *,
                    lam_init):
    t = aq_ref.shape[1]
    i = pl.program_id(1)
    lp = lam_ref[0]
    lam = (jnp.exp(jnp.sum(lp[0:1] * lp[1:2], axis=-1, keepdims=True))
           - jnp.exp(jnp.sum(lp[2:3] * lp[3:4], axis=-1, keepdims=True)) + lam_init)
    key, qry = _tile_ids(t)
    sub = t // KEY_TILE
    lmat = lmat_ref[...]
    _load_qt(aq_ref, aqz_ref)
    _load_qt(bq_ref, bqz_ref)
    m_ref[...] = jnp.full(m_ref.shape, NEG, F32)
    l_ref[...] = jnp.zeros(l_ref.shape, F32)
    aacc_ref[...] = jnp.zeros(aacc_ref.shape, F32)
    r_ref[...] = jnp.zeros(r_ref.shape, F32)
    bacc_ref[...] = jnp.zeros(bacc_ref.shape, F32)

    def a_group(g, diag):
        scores = []
        for r in range(sub):
            for n in range(2 * A_HEADS):
                kb = ak_ref[0, _key_rows(g * sub + r), (n // 2) * LANES:(n // 2 + 1) * LANES]
                scores.append(_dot(kb, aqz_ref[n]))
        yield
        for r in range(sub):
            for n in range(2 * A_HEADS):
                s = scores[r * 2 * A_HEADS + n]
                if diag:
                    s = jnp.where(((key + r * KEY_TILE) // CHUNK) <= (qry // CHUNK), s, NEG)
                vt = avt_ref[0, g * sub + r, (n // 2) * LANES:(n // 2 + 1) * LANES, :]
                _softmax_step(n, s, vt, m_ref, l_ref, aacc_ref)
                yield

    def b_group(g, diag):
        order = [(r, n) for r in reversed(range(sub)) for n in range(B_HEADS)]
        strict = {r: (key + r * KEY_TILE) < qry for r in range(sub)} if diag else None
        logits = {}
        for r, n in order:
            kb = bk_ref[0, _key_rows(g * sub + r), (n // 2) * LANES:(n // 2 + 1) * LANES]
            logits[r, n] = _dot(kb, bqz_ref[n])
        yield
        log_betas, laters, sums = {}, {}, {}
        for r, n in order:
            z = logits[r, n]
            neg_abs = pltpu.bitcast(pltpu.bitcast(z, jnp.int32) | jnp.int32(INT_MIN), F32)
            soft = jnp.log2(1.0 + jnp.exp2(neg_abs))
            log_beta = jnp.minimum(z, 0.0) - soft
            log_om = log_beta - z
            if diag:
                log_om = jnp.where(strict[r], log_om, 0.0)
            later = _dot(lmat, log_om.astype(BF16))
            laters[r, n] = later[:KEY_TILE]
            log_betas[r, n] = log_beta
            sums[r, n] = later[KEY_TILE:KEY_TILE + 1]
            yield
        for r, n in order:
            a = jnp.exp2(log_betas[r, n] + laters[r, n])
            if diag:
                a = jnp.where(strict[r], a, 0.0)
            vt = bvt_ref[0, g * sub + r, n * HEAD_DIM:(n + 1) * HEAD_DIM, :]
            visited = r_ref[n]
            bacc_ref[n] = bacc_ref[n] + _dot(vt, a.astype(BF16)) * jnp.exp2(visited)
            r_ref[n] = visited + sums[r, n]
            yield

    def body(n, carry):
        _interleave(a_group(n, False), b_group(i - 1 - n, False))
        return carry

    _interleave(a_group(i, True), b_group(i, True))
    _in_pairs(i, body, 0)

    gain = g_ref[0] * (1.0 - lam_init)
    for h in range(A_HEADS):
        o = aacc_ref[2 * h] / l_ref[2 * h] - lam * (aacc_ref[2 * h + 1] / l_ref[2 * h + 1])
        ms = jnp.mean(o * o, axis=0, keepdims=True)
        y = (o * lax.rsqrt(ms + EPS)).T * gain
        ya_ref[0, :, h * LANES:(h + 1) * LANES] = y.astype(ya_ref.dtype)
    yb_ref[0] = bacc_ref[...].reshape(B_HEADS * HEAD_DIM, t).T.astype(yb_ref.dtype)


def _attn_ab_call(qk, bqk, vt, lam_p, subln, lmat, layer, lam_init, t):
    b, s, _ = qk.shape
    nt, _, kt = vt.shape[1:]
    ns = 2 * A_HEADS
    qspec = pl.BlockSpec((1, t, GROUP), lambda bi, i: (bi, i, 0))
    kspec = pl.BlockSpec((1, s, GROUP), lambda bi, i: (bi, 0, 1))
    vspec = lambda f: pl.BlockSpec((1, nt, GROUP, kt), lambda bi, i: (bi, 0, f, 0))
    return pl.pallas_call(
        functools.partial(_attn_ab_kernel, lam_init=lam_init),
        grid=(b, s // t),
        in_specs=[pl.BlockSpec((1, 4, HEAD_DIM), lambda bi, i: (layer, 0, 0)),
                  qspec, kspec, vspec(0),
                  pl.BlockSpec((1, 1, LANES), lambda bi, i: (layer, 0, 0)),
                  qspec, kspec, vspec(1),
                  pl.BlockSpec(lmat.shape, lambda bi, i: (0, 0))],
        out_specs=[qspec, qspec],
        out_shape=[jax.ShapeDtypeStruct((b, s, GROUP), BF16)] * 2,
        scratch_shapes=[pltpu.VMEM((ns, LANES, t), BF16), pltpu.VMEM((ns, 1, t), F32),
                        pltpu.VMEM((ns, 1, t), F32), pltpu.VMEM((ns, LANES, t), F32),
                        pltpu.VMEM((B_HEADS, LANES, t), BF16), pltpu.VMEM((B_HEADS, 1, t), F32),
                        pltpu.VMEM((B_HEADS, HEAD_DIM, t), F32)],
        compiler_params=_params("parallel", "arbitrary"),
    )(lam_p, qk, qk, vt, subln, bqk, bqk, vt, lmat)


def _fold_rows(x):
    parts = [x[r * SUBLANES:(r + 1) * SUBLANES] for r in range(x.shape[0] // SUBLANES)]
    while len(parts) > 1:
        parts = [a + b for a, b in zip(parts[0::2], parts[1::2])]
    return parts[0]


def _dsa_mask_kernel(q_ref, kw_ref, wq_ref, lt_ref, o_ref, key_ref, qz_ref, *, topk):
    t = q_ref.shape[1]
    nj = key_ref.shape[0]
    i = pl.program_id(1)
    nch = (i + 1) * (t // KEY_TILE)
    qt = q_ref[0].T
    pad = jnp.zeros((LANES - IDX_DIM, t), F32)
    for h in range(IDX_HEADS):
        qz_ref[h] = jnp.concatenate([qt[h * IDX_DIM:(h + 1) * IDX_DIM], pad], axis=0).astype(BF16)
    wt = wq_ref[0].T
    wh = [wt[IDX_DIM + h:IDX_DIM + h + 1, :] for h in range(IDX_HEADS)]
    key0, qry0 = _tile_ids(t)
    sub = t // KEY_TILE

    def chunk_ok(r):
        return ((key0 + r * KEY_TILE) // CHUNK) <= (qry0 // CHUNK)

    def fill(g, diag):
        for r in range(sub):
            c = g * sub + r
            kb = kw_ref[0, _key_rows(c), :].astype(BF16)
            score = jnp.zeros((KEY_TILE, t), F32)
            for h in range(IDX_HEADS):
                score = score + wh[h] * jnp.maximum(_dot(kb, qz_ref[h]), 0.0)
            if diag:
                score = jnp.where(chunk_ok(r), score, -jnp.inf)
            bits = pltpu.bitcast(score, jnp.int32)
            okey = jnp.where(bits < 0, bits ^ jnp.int32(0x7FFFFFFF), bits)
            key_ref[c] = jnp.where(score == 0.0, 0, okey)

    def fill_one(g, carry):
        fill(g, False)
        return carry

    _in_pairs(i, fill_one, 0)
    fill(i, True)

    def count_at_or_above(cand):
        def one(g, acc):
            for r in range(sub):
                acc = acc + _fold_rows(jnp.where(key_ref[g * sub + r] >= cand, 1.0, 0.0))
            return acc
        acc = _in_pairs(i + 1, one, jnp.zeros((SUBLANES, t), F32))
        return jnp.sum(acc, axis=0, keepdims=True)

    def bit_body(n, carry):
        lo, above = carry
        cand = lo + jnp.left_shift(jnp.int32(1), 31 - n)
        cnt = count_at_or_above(cand)
        take = cnt >= topk
        return jnp.where(take, cand, lo), jnp.where(take, above, cnt)

    thr, above = lax.fori_loop(
        0, 32, bit_body, (jnp.full((1, t), INT_MIN, jnp.int32), jnp.zeros((1, t), F32)))
    need = topk - above
    lt = lt_ref[...]

    def emit(g, run, diag):
        for r in range(sub):
            c = g * sub + r
            okey = key_ref[c]
            eq = okey == thr
            eqf = jnp.where(eq, 1.0, 0.0)
            before = _dot(lt, eqf.astype(BF16)) + run
            sel = (okey > thr) | (eq & (before < need))
            if diag:
                sel = sel & chunk_ok(r)
            o_ref[0, 0, c] = jnp.where(sel, 0.0, NEG).astype(o_ref.dtype)
            run = run + jnp.sum(eqf, axis=0, keepdims=True)
        return run

    run = _in_pairs(i, lambda g, run: emit(g, run, False), jnp.zeros((1, t), F32))
    emit(i, run, True)

    def blank(c, carry):
        o_ref[0, 0, c] = jnp.full((KEY_TILE, t), NEG, o_ref.dtype)
        return carry

    lax.fori_loop(nch, nj, blank, 0)


def _dsa_mask_call(idx, lt, topk, t):
    b, s, _ = idx.shape
    kt = lt.shape[0]
    assert t >= topk
    nt, nk = s // t, s // kt
    return pl.pallas_call(
        functools.partial(_dsa_mask_kernel, topk=topk),
        grid=(b, nt),
        in_specs=[pl.BlockSpec((1, t, IDX_Q), lambda bi, i: (bi, i, 0)),
                  pl.BlockSpec((1, s, LANES), lambda bi, i: (bi, 0, IDX_Q // LANES)),
                  pl.BlockSpec((1, t, LANES), lambda bi, i: (bi, i, IDX_Q // LANES)),
                  pl.BlockSpec((kt, kt), lambda bi, i: (0, 0))],
        out_specs=pl.BlockSpec((1, 1, nk, kt, t), lambda bi, i: (bi, i, 0, 0, 0)),
        out_shape=jax.ShapeDtypeStruct((b, nt, nk, kt, t), BF16),
        scratch_shapes=[pltpu.VMEM((nk, kt, t), jnp.int32), pltpu.VMEM((IDX_HEADS, LANES, t), BF16)],
        compiler_params=_params("parallel", "arbitrary"),
    )(idx, idx, idx, lt)


def _attn_c_kernel(q_ref, k_ref, vt_ref, b_ref, o_ref, qz_ref, m_ref, l_ref, acc_ref):
    t = q_ref.shape[1]
    i = pl.program_id(1)
    _load_qt(q_ref, qz_ref)
    m_ref[...] = jnp.full(m_ref.shape, NEG, F32)
    l_ref[...] = jnp.zeros(l_ref.shape, F32)
    acc_ref[...] = jnp.zeros(acc_ref.shape, F32)

    key, qry = _tile_ids(KEY_TILE)
    ident = jnp.where(key == qry, 1.0, 0.0).astype(BF16)

    sub = t // KEY_TILE

    def body(g, carry):
        scores = []
        for r in range(sub):
            u = g * sub + r
            bias = b_ref[0, 0, u]
            for n in range(C_HEADS):
                kb = k_ref[0, _key_rows(u), (n // 2) * LANES:(n // 2 + 1) * LANES]
                lhs = jnp.concatenate([kb, ident], axis=1)
                rhs = jnp.concatenate([qz_ref[n], bias], axis=0)
                scores.append(_dot(lhs, rhs))
        for r in range(sub):
            for n in range(C_HEADS):
                vt = vt_ref[0, g * sub + r, n * HEAD_DIM:(n + 1) * HEAD_DIM, :]
                _softmax_step(n, scores[r * C_HEADS + n], vt, m_ref, l_ref, acc_ref)
        return carry

    _in_pairs(i + 1, body, 0)
    o = acc_ref[...] / l_ref[...]
    o_ref[0] = o.reshape(C_HEADS * HEAD_DIM, t).T.astype(o_ref.dtype)


def _attn_c_call(qk, vt, bias):
    b, s, _ = qk.shape
    nk, _, kt = vt.shape[1:]
    nt, t = bias.shape[1], bias.shape[4]
    return pl.pallas_call(
        _attn_c_kernel,
        grid=(b, nt),
        in_specs=[pl.BlockSpec((1, t, GROUP), lambda bi, i: (bi, i, 2)),
                  pl.BlockSpec((1, s, GROUP), lambda bi, i: (bi, 0, 3)),
                  pl.BlockSpec((1, nk, GROUP, kt), lambda bi, i: (bi, 0, 2, 0)),
                  pl.BlockSpec((1, 1, nk, kt, t), lambda bi, i: (bi, i, 0, 0, 0))],
        out_specs=pl.BlockSpec((1, t, GROUP), lambda bi, i: (bi, i, 0)),
        out_shape=jax.ShapeDtypeStruct((b, s, GROUP), BF16),
        scratch_shapes=[pltpu.VMEM((C_HEADS, LANES, t), BF16), pltpu.VMEM((C_HEADS, 1, t), F32),
                        pltpu.VMEM((C_HEADS, 1, t), F32), pltpu.VMEM((C_HEADS, HEAD_DIM, t), F32)],
        compiler_params=_params("parallel", "arbitrary"),
    )(qk, qk, vt, bias)


def _merge_kernel(x_ref, h_ref, ya_ref, yb_ref, yc_ref, wg_ref, wa_ref, wb_ref, wc_ref, wo_ref, o_ref):
    d = x_ref.shape[1]
    h = h_ref[...]
    merged = jnp.zeros(x_ref.shape, F32)
    for n, (y_ref, w_ref) in enumerate(((ya_ref, wa_ref), (yb_ref, wb_ref), (yc_ref, wc_ref))):
        gate = jax.nn.sigmoid(_dot(h, wg_ref[0, :, n * d:(n + 1) * d]))
        merged = merged + gate * _dot(y_ref[...], w_ref[0])
    o_ref[...] = x_ref[...] + _dot(merged.astype(BF16), wo_ref[0])


def _merge_call(x2, h, ya, yb, yc, wg, wa, wb, wc, wo, layer):
    n, d = x2.shape
    tm = ROW_TILE
    row = lambda cols: pl.BlockSpec((tm, cols), lambda i: (i, 0))
    full = lambda w: pl.BlockSpec((1,) + w.shape[1:], lambda i: (layer, 0, 0))
    return pl.pallas_call(
        _merge_kernel,
        grid=(n // tm,),
        in_specs=[row(d), row(d), row(ya.shape[1]), row(yb.shape[1]), row(yc.shape[1]),
                  full(wg), full(wa), full(wb), full(wc), full(wo)],
        out_specs=row(d),
        out_shape=jax.ShapeDtypeStruct((n, d), F32),
        compiler_params=_params("parallel"),
    )(x2, h, ya, yb, yc, wg, wa, wb, wc, wo)


def _mlp_kernel(x_ref, p_ref, gm_ref, wu_ref, wd_ref, gp_ref, wpg_ref, wpp_ref, o_ref):
    x = x_ref[...]
    ms = jnp.mean(x * x, axis=-1, keepdims=True)
    h = (x * lax.rsqrt(ms + EPS) * gm_ref[0]).astype(BF16)
    for f in range(wu_ref.shape[2] // FF_TILE):
        u = jnp.maximum(_dot(h, wu_ref[0, :, f * FF_TILE:(f + 1) * FF_TILE]), 0.0)
        x = x + _dot((u * u).astype(BF16), wd_ref[0, f * FF_TILE:(f + 1) * FF_TILE, :])
    ms = jnp.mean(x * x, axis=-1, keepdims=True)
    hp = (x * lax.rsqrt(ms + EPS) * gp_ref[0]).astype(BF16)
    gate = jax.nn.sigmoid(_dot(hp, wpg_ref[0]))
    o_ref[...] = x + gate * _dot(p_ref[0].astype(BF16), wpp_ref[0])


def _mlp_call(x2, p3, g_mlp, w_up, w_down, g_ple, w_pg, w_pp, layer):
    n, d = x2.shape
    pd = p3.shape[2]
    tm = ROW_TILE
    full = lambda w: pl.BlockSpec((1,) + w.shape[1:], lambda i: (layer, 0, 0),
                                  pipeline_mode=pl.Buffered(1))
    return pl.pallas_call(
        _mlp_kernel,
        grid=(n // tm,),
        in_specs=[pl.BlockSpec((tm, d), lambda i: (i, 0)),
                  pl.BlockSpec((1, tm, pd), lambda i: (layer, i, 0)),
                  full(g_mlp), full(w_up), full(w_down), full(g_ple), full(w_pg), full(w_pp)],
        out_specs=pl.BlockSpec((tm, d), lambda i: (i, 0)),
        out_shape=jax.ShapeDtypeStruct((n, d), F32),
        compiler_params=_params("parallel"),
    )(x2, p3, g_mlp, w_up, w_down, g_ple, w_pg, w_pp)


def _tile_gain(g, reps, scale=1.0):
    return jnp.tile(g.astype(F32), (1, reps)) * scale


def kernel(x, p, positions, attn_norm, w_in, a_q_norm, a_k_norm, a_lambda, a_subln,
           c_q_norm, c_k_norm, idx_k_norm, w_br_a, w_br_b, w_br_c, w_out,
           mlp_norm, w_up, w_down, ple_norm, w_ple_gate, w_ple_proj):
    b, s, d = x.shape
    depth = w_in.shape[0]
    n = b * s
    topk = min(TOPK_MAX, s // 4)
    t = min(ATTN_TILE, s)
    scale = HEAD_DIM ** -0.5
    qscale = scale * LOG2E

    inv = ROPE_THETA ** (-jnp.arange(0, HEAD_DIM, 2, dtype=F32) / HEAD_DIM)
    ang = positions.astype(F32)[..., None] * inv
    cos, sin = jnp.cos(ang), jnp.sin(ang)
    cos_t = jnp.concatenate([cos, cos, cos, cos], axis=-1).reshape(n, LANES)
    sin_t = jnp.concatenate([-sin, sin, -sin, sin], axis=-1).reshape(n, LANES)

    qa, ka, va = 0, 512, 1024
    qb, kb, vb = 1536, 2048, 2560
    qc, kc, vc = 3072, 3584, 4096
    qi = 4608
    gl = 4932
    col = lambda a, width=GROUP: w_in[:, :, a:a + width]
    w_qk = jnp.concatenate([col(qa), col(ka), col(qc), col(kc)], axis=-1).astype(BF16)
    w_bqk = jnp.concatenate([col(qb) * qscale, col(kb)], axis=-1).astype(BF16)
    w_vt = jnp.concatenate([col(va), col(vb), col(vc)], axis=-1).astype(BF16)
    w_idx = jnp.pad(col(qi, gl - qi), ((0, 0), (0, 0), (0, IDX_COLS - (gl - qi)))).astype(BF16)
    w_gl = col(gl, N_BRANCHES * d).astype(BF16)
    heads = GROUP // HEAD_DIM
    gain_qk = jnp.concatenate([_tile_gain(a_q_norm, heads, qscale), _tile_gain(a_k_norm, heads),
                               _tile_gain(c_q_norm, heads, qscale), _tile_gain(c_k_norm, heads)],
                              axis=-1)[:, None, :]
    gain_idx = jnp.pad(idx_k_norm.astype(F32), ((0, 0), (0, LANES - IDX_DIM)))[:, None, :]
    scale_idx = jnp.concatenate([jnp.full((IDX_Q,), IDX_DIM ** -0.5, F32), jnp.ones((IDX_DIM,), F32),
                                 jnp.full((IDX_HEADS,), IDX_HEADS ** -0.5, F32),
                                 jnp.zeros((LANES - IDX_DIM - IDX_HEADS,), F32)])[None, :]
    gid = jnp.arange(GROUP) // HEAD_DIM
    gmat = (gid[:, None] == gid[None, :]).astype(BF16)
    kt = min(KEY_TILE, s)
    ar = jnp.arange(kt)
    later_mat = jnp.concatenate([(ar[None, :] > ar[:, None]).astype(BF16),
                                 jnp.ones((ONES_ROWS, kt), BF16)], axis=0)
    before_mat = (ar[None, :] < ar[:, None]).astype(BF16)

    wa, wb, wc, wo = (w.astype(BF16) for w in (w_br_a, w_br_b, w_br_c, w_out))
    wu, wd, wpg, wpp = (w.astype(BF16) for w in (w_up, w_down, w_ple_gate, w_ple_proj))
    g_attn, g_mlp, g_ple = (g.astype(F32)[:, None, :] for g in (attn_norm, mlp_norm, ple_norm))
    subln = a_subln.astype(F32)[:, None, :]
    lam_p = a_lambda.astype(F32)
    p3 = p.reshape(depth, n, p.shape[-1])

    x2 = x.reshape(n, d)
    for i in range(depth):
        lam_init = 0.8 - 0.6 * math.exp(-0.3 * i)
        h, qk, bqk, vt, idx = _front_call(x2, g_attn, w_qk, w_bqk, w_vt, w_idx, gain_qk, gain_idx,
                                          scale_idx, cos_t, sin_t, gmat, i, kt)
        qk, bqk, idx = (a.reshape(b, s, -1) for a in (qk, bqk, idx))
        vt = vt.reshape(b, s // kt, -1, kt)
        ya, yb = _attn_ab_call(qk, bqk, vt, lam_p, subln, later_mat, i, lam_init, t)
        bias = _dsa_mask_call(idx, before_mat, topk, t)
        yc = _attn_c_call(qk, vt, bias)
        x2 = _merge_call(x2, h, ya.reshape(n, -1), yb.reshape(n, -1), yc.reshape(n, -1),
                         w_gl, wa, wb, wc, wo, i)
        x2 = _mlp_call(x2, p3, g_mlp, wu, wd, g_ple, wpg, wpp, i)
    return x2.reshape(b, s, d)
```

```python
import functools
import math

import jax
import jax.numpy as jnp
from jax import lax
from jax.experimental import pallas as pl
from jax.experimental.pallas import tpu as pltpu

F32 = jnp.float32
BF16 = jnp.bfloat16

HEAD_DIM = 64
CHUNK = 64
A_HEADS = 4
B_HEADS = 8
C_HEADS = 8
IDX_HEADS = 4
IDX_DIM = 64
TOPK_MAX = 256
N_BRANCHES = 3
ROPE_THETA = 10000.0
EPS = 1e-6

LANES = 128
SUBLANES = 8
HALF_HEAD = HEAD_DIM // 2
NEG = -1e30
INT_MIN = -(2 ** 31)
LOG2E = 1.4426950408889634
V7X_VMEM_BYTES = 64 * 1024 * 1024
VMEM_LIMIT = V7X_VMEM_BYTES * 7 // 8

ROW_TILE = 512
ATTN_TILE = 256
KEY_TILE = 128
FF_TILE = 1024
TAIL_TILE = 512
GROUP = 512


def _params(*sem):
    return pltpu.CompilerParams(dimension_semantics=sem, vmem_limit_bytes=VMEM_LIMIT)


def _dot(a, b):
    return jnp.dot(a, b, preferred_element_type=F32)


def _swap_halves(y):
    lane = lax.broadcasted_iota(jnp.int32, y.shape, 1)
    return jnp.where((lane & HALF_HEAD) != 0, pltpu.roll(y, HALF_HEAD, 1),
                     pltpu.roll(y, LANES - HALF_HEAD, 1))


def _rope_slab(y, cos, sin):
    return y * cos + _swap_halves(y) * sin


IDX_Q = IDX_HEADS * IDX_DIM
IDX_COLS = IDX_Q + LANES


def _front_kernel(x_ref, g_ref, wqk_ref, wbqk_ref, wvt_ref, widx_ref, gqk_ref, gidx_ref, sidx_ref,
                  cos_ref, sin_ref, gmat_ref, h_ref, qk_ref, bqk_ref, vt_ref, idx_ref):
    x = x_ref[...]
    ms = jnp.mean(x * x, axis=-1, keepdims=True)
    h = (x * lax.rsqrt(ms + EPS) * g_ref[0]).astype(BF16)
    h_ref[...] = h
    cos = cos_ref[...]
    sin = sin_ref[...]

    gmat = gmat_ref[...]
    for j in range(qk_ref.shape[1] // GROUP):
        cols = slice(j * GROUP, (j + 1) * GROUP)
        t = _dot(h, wqk_ref[0, :, cols])
        ms = _dot((t * t).astype(BF16), gmat) * (1.0 / HEAD_DIM)
        y = t * lax.rsqrt(ms + EPS) * gqk_ref[0, :, cols]
        for s in range(GROUP // LANES):
            sl = slice(s * LANES, (s + 1) * LANES)
            qk_ref[:, j * GROUP + s * LANES:j * GROUP + (s + 1) * LANES] = (
                _rope_slab(y[:, sl], cos, sin).astype(qk_ref.dtype))

    for j in range(bqk_ref.shape[1] // GROUP):
        cols = slice(j * GROUP, (j + 1) * GROUP)
        bqk_ref[:, cols] = _dot(h, wbqk_ref[0, :, cols]).astype(bqk_ref.dtype)

    tt = vt_ref.shape[2]
    for j in range(vt_ref.shape[1] // GROUP):
        cols = slice(j * GROUP, (j + 1) * GROUP)
        t = _dot(h, wvt_ref[0, :, cols])
        for r in range(vt_ref.shape[0]):
            vt_ref[r, cols, :] = t[r * tt:(r + 1) * tt, :].T.astype(vt_ref.dtype)

    t = _dot(h, widx_ref[0])
    scale = sidx_ref[...]
    for s in range(IDX_Q // LANES):
        sl = slice(s * LANES, (s + 1) * LANES)
        idx_ref[:, sl] = _rope_slab(t[:, sl], cos, sin) * scale[:, sl]
    sl = slice(IDX_Q, IDX_COLS)
    tk = t[:, sl]
    lane = lax.broadcasted_iota(jnp.int32, tk.shape, 1)
    is_k = lane < IDX_DIM
    ms = jnp.sum(jnp.where(is_k, tk * tk, 0.0), axis=-1, keepdims=True) * (1.0 / IDX_DIM)
    yk = tk * lax.rsqrt(ms + EPS) * gidx_ref[0]
    idx_ref[:, sl] = jnp.where(is_k, _rope_slab(yk, cos, sin), tk) * scale[:, sl]


def _front_call(x2, g_attn, w_qk, w_bqk, w_vt, w_idx, gain_qk, gain_idx, scale_idx, cos_t, sin_t,
                gmat, layer, tt):
    n, d = x2.shape
    tm = ROW_TILE
    row = lambda cols: pl.BlockSpec((tm, cols), lambda i: (i, 0))
    full = lambda w: pl.BlockSpec((1,) + w.shape[1:], lambda i: (layer, 0, 0))
    fixed = lambda a: pl.BlockSpec(a.shape, lambda i: (0, 0))
    cqk, cbqk, cvt = w_qk.shape[2], w_bqk.shape[2], w_vt.shape[2]
    return pl.pallas_call(
        _front_kernel,
        grid=(n // tm,),
        in_specs=[row(d), full(g_attn), full(w_qk), full(w_bqk), full(w_vt), full(w_idx),
                  full(gain_qk), full(gain_idx), fixed(scale_idx), row(LANES), row(LANES),
                  fixed(gmat)],
        out_specs=[row(d), row(cqk), row(cbqk),
                   pl.BlockSpec((tm // tt, cvt, tt), lambda i: (i, 0, 0)), row(IDX_COLS)],
        out_shape=[jax.ShapeDtypeStruct((n, d), BF16), jax.ShapeDtypeStruct((n, cqk), BF16),
                   jax.ShapeDtypeStruct((n, cbqk), BF16),
                   jax.ShapeDtypeStruct((n // tt, cvt, tt), BF16),
                   jax.ShapeDtypeStruct((n, IDX_COLS), F32)],
        compiler_params=_params("parallel"),
    )(x2, g_attn, w_qk, w_bqk, w_vt, w_idx, gain_qk, gain_idx, scale_idx, cos_t, sin_t, gmat)


def _load_qt(q_ref, qz_ref):
    qt = q_ref[0].astype(F32).T
    rows = lax.broadcasted_iota(jnp.int32, (LANES, qt.shape[1]), 0)
    for s in range(qt.shape[0] // LANES):
        slab = qt[s * LANES:(s + 1) * LANES, :]
        qz_ref[2 * s] = jnp.where(rows < HEAD_DIM, slab, 0.0).astype(BF16)
        qz_ref[2 * s + 1] = jnp.where(rows >= HEAD_DIM, slab, 0.0).astype(BF16)


ONES_ROWS = 16


def _softmax_step(n, s, vt, m_ref, l_ref, acc_ref):
    m_old = m_ref[n]
    m_new = jnp.maximum(m_old, jnp.max(s, axis=0, keepdims=True))
    p = jnp.exp2(s - m_new).astype(BF16)
    alpha = jnp.exp2(m_old - m_new)
    f = vt.shape[0]
    pv = _dot(jnp.concatenate([vt, jnp.ones((ONES_ROWS, vt.shape[1]), BF16)], axis=0), p)
    l_ref[n] = alpha * l_ref[n] + pv[f:f + 1]
    acc_ref[n] = alpha * acc_ref[n] + pv[:f]
    m_ref[n] = m_new


def _tile_ids(t):
    key = lax.broadcasted_iota(jnp.int32, (KEY_TILE, t), 0)
    qry = lax.broadcasted_iota(jnp.int32, (KEY_TILE, t), 1)
    return key, qry


def _key_rows(u):
    return pl.ds(pl.multiple_of(u * KEY_TILE, KEY_TILE), KEY_TILE)


def _in_pairs(n, one, carry):
    carry = lax.fori_loop(0, n // 2, lambda p, c: one(2 * p + 1, one(2 * p, c)), carry)
    return lax.cond(n % 2 == 1, lambda c: one(n - 1, c), lambda c: c, carry)


def _interleave(*gens):
    gens = list(gens)
    while gens:
        for gen in list(gens):
            try:
                next(gen)
            except StopIteration:
                gens.remove(gen)


def _attn_ab_kernel(lam_ref, aq_ref, ak_ref, avt_ref, g_ref, bq_ref, bk_ref, bvt_ref, lmat_ref,
                    ya_ref, yb_ref, aqz_ref, m_ref, l_ref, aacc_ref, bqz_ref, r_ref, bacc_ref, *,
                    lam_init):
    t = aq_ref.shape[1]
    i = pl.program_id(1)
    lp = lam_ref[0]
    lam = (jnp.exp(jnp.sum(lp[0:1] * lp[1:2], axis=-1, keepdims=True))
           - jnp.exp(jnp.sum(lp[2:3] * lp[3:4], axis=-1, keepdims=True)) + lam_init)
    key, qry = _tile_ids(t)
    sub = t // KEY_TILE
    lmat = lmat_ref[...]
    _load_qt(aq_ref, aqz_ref)
    _load_qt(bq_ref, bqz_ref)
    m_ref[...] = jnp.full(m_ref.shape, NEG, F32)
    l_ref[...] = jnp.zeros(l_ref.shape, F32)
    aacc_ref[...] = jnp.zeros(aacc_ref.shape, F32)
    r_ref[...] = jnp.zeros(r_ref.shape, F32)
    bacc_ref[...] = jnp.zeros(bacc_ref.shape, F32)

    def a_group(g, diag):
        scores = []
        for r in range(sub):
            for n in range(2 * A_HEADS):
                kb = ak_ref[0, _key_rows(g * sub + r), (n // 2) * LANES:(n // 2 + 1) * LANES]
                scores.append(_dot(kb, aqz_ref[n]))
        yield
        for r in range(sub):
            for n in range(2 * A_HEADS):
                s = scores[r * 2 * A_HEADS + n]
                if diag:
                    s = jnp.where(((key + r * KEY_TILE) // CHUNK) <= (qry // CHUNK), s, NEG)
                vt = avt_ref[0, g * sub + r, (n // 2) * LANES:(n // 2 + 1) * LANES, :]
                _softmax_step(n, s, vt, m_ref, l_ref, aacc_ref)
                yield

    def b_group(g, diag):
        order = [(r, n) for r in reversed(range(sub)) for n in range(B_HEADS)]
        strict = {r: (key + r * KEY_TILE) < qry for r in range(sub)} if diag else None
        logits = {}
        for r, n in order:
            kb = bk_ref[0, _key_rows(g * sub + r), (n // 2) * LANES:(n // 2 + 1) * LANES]
            logits[r, n] = _dot(kb, bqz_ref[n])
        yield
        log_betas, laters, sums = {}, {}, {}
        for r, n in order:
            z = logits[r, n]
            neg_abs = pltpu.bitcast(pltpu.bitcast(z, jnp.int32) | jnp.int32(INT_MIN), F32)
            soft = jnp.log2(1.0 + jnp.exp2(neg_abs))
            log_beta = jnp.minimum(z, 0.0) - soft
            log_om = log_beta - z
            if diag:
                log_om = jnp.where(strict[r], log_om, 0.0)
            later = _dot(lmat, log_om.astype(BF16))
            laters[r, n] = later[:KEY_TILE]
            log_betas[r, n] = log_beta
            sums[r, n] = later[KEY_TILE:KEY_TILE + 1]
            yield
        for r, n in order:
            a = jnp.exp2(log_betas[r, n] + laters[r, n])
            if diag:
                a = jnp.where(strict[r], a, 0.0)
            vt = bvt_ref[0, g * sub + r, n * HEAD_DIM:(n + 1) * HEAD_DIM, :]
            visited = r_ref[n]
            bacc_ref[n] = bacc_ref[n] + _dot(vt, a.astype(BF16)) * jnp.exp2(visited)
            r_ref[n] = visited + sums[r, n]
            yield

    def body(n, carry):
        _interleave(a_group(n, False), b_group(i - 1 - n, False))
        return carry

    _interleave(a_group(i, True), b_group(i, True))
    _in_pairs(i, body, 0)

    gain = g_ref[0] * (1.0 - lam_init)
    for h in range(A_HEADS):
        o = aacc_ref[2 * h] / l_ref[2 * h] - lam * (aacc_ref[2 * h + 1] / l_ref[2 * h + 1])
        ms = jnp.mean(o * o, axis=0, keepdims=True)
        y = (o * lax.rsqrt(ms + EPS)).T * gain
        ya_ref[0, :, h * LANES:(h + 1) * LANES] = y.astype(ya_ref.dtype)
    yb_ref[0] = bacc_ref[...].reshape(B_HEADS * HEAD_DIM, t).T.astype(yb_ref.dtype)


def _attn_ab_call(qk, bqk, vt, lam_p, subln, lmat, layer, lam_init, t):
    b, s, _ = qk.shape
    nt, _, kt = vt.shape[1:]
    ns = 2 * A_HEADS
    qspec = pl.BlockSpec((1, t, GROUP), lambda bi, i: (bi, i, 0))
    kspec = pl.BlockSpec((1, s, GROUP), lambda bi, i: (bi, 0, 1))
    vspec = lambda f: pl.BlockSpec((1, nt, GROUP, kt), lambda bi, i: (bi, 0, f, 0))
    return pl.pallas_call(
        functools.partial(_attn_ab_kernel, lam_init=lam_init),
        grid=(b, s // t),
        in_specs=[pl.BlockSpec((1, 4, HEAD_DIM), lambda bi, i: (layer, 0, 0)),
                  qspec, kspec, vspec(0),
                  pl.BlockSpec((1, 1, LANES), lambda bi, i: (layer, 0, 0)),
                  qspec, kspec, vspec(1),
                  pl.BlockSpec(lmat.shape, lambda bi, i: (0, 0))],
        out_specs=[qspec, qspec],
        out_shape=[jax.ShapeDtypeStruct((b, s, GROUP), BF16)] * 2,
        scratch_shapes=[pltpu.VMEM((ns, LANES, t), BF16), pltpu.VMEM((ns, 1, t), F32),
                        pltpu.VMEM((ns, 1, t), F32), pltpu.VMEM((ns, LANES, t), F32),
                        pltpu.VMEM((B_HEADS, LANES, t), BF16), pltpu.VMEM((B_HEADS, 1, t), F32),
                        pltpu.VMEM((B_HEADS, HEAD_DIM, t), F32)],
        compiler_params=_params("parallel", "arbitrary"),
    )(lam_p, qk, qk, vt, subln, bqk, bqk, vt, lmat)


def _fold_rows(x):
    parts = [x[r * SUBLANES:(r + 1) * SUBLANES] for r in range(x.shape[0] // SUBLANES)]
    while len(parts) > 1:
        parts = [a + b for a, b in zip(parts[0::2], parts[1::2])]
    return parts[0]


def _dsa_mask_kernel(q_ref, kw_ref, wq_ref, lt_ref, o_ref, key_ref, qz_ref, *, topk):
    t = q_ref.shape[1]
    nj = key_ref.shape[0]
    i = pl.program_id(1)
    nch = (i + 1) * (t // KEY_TILE)
    qt = q_ref[0].T
    pad = jnp.zeros((LANES - IDX_DIM, t), F32)
    for h in range(IDX_HEADS):
        qz_ref[h] = jnp.concatenate([qt[h * IDX_DIM:(h + 1) * IDX_DIM], pad], axis=0).astype(BF16)
    wt = wq_ref[0].T
    wh = [wt[IDX_DIM + h:IDX_DIM + h + 1, :] for h in range(IDX_HEADS)]
    key0, qry0 = _tile_ids(t)
    sub = t // KEY_TILE

    def chunk_ok(r):
        return ((key0 + r * KEY_TILE) // CHUNK) <= (qry0 // CHUNK)

    def fill(g, diag):
        for r in range(sub):
            c = g * sub + r
            kb = kw_ref[0, _key_rows(c), :].astype(BF16)
            score = jnp.zeros((KEY_TILE, t), F32)
            for h in range(IDX_HEADS):
                score = score + wh[h] * jnp.maximum(_dot(kb, qz_ref[h]), 0.0)
            if diag:
                score = jnp.where(chunk_ok(r), score, -jnp.inf)
            bits = pltpu.bitcast(score, jnp.int32)
            okey = jnp.where(bits < 0, bits ^ jnp.int32(0x7FFFFFFF), bits)
            key_ref[c] = jnp.where(score == 0.0, 0, okey)

    def fill_one(g, carry):
        fill(g, False)
        return carry

    _in_pairs(i, fill_one, 0)
    fill(i, True)

    def count_at_or_above(cand):
        def one(g, acc):
            for r in range(sub):
                acc = acc + _fold_rows(jnp.where(key_ref[g * sub + r] >= cand, 1.0, 0.0))
            return acc
        acc = _in_pairs(i + 1, one, jnp.zeros((SUBLANES, t), F32))
        return jnp.sum(acc, axis=0, keepdims=True)

    def bit_body(n, carry):
        lo, above = carry
        cand = lo + jnp.left_shift(jnp.int32(1), 31 - n)
        cnt = count_at_or_above(cand)
        take = cnt >= topk
        return jnp.where(take, cand, lo), jnp.where(take, above, cnt)

    thr, above = lax.fori_loop(
        0, 32, bit_body, (jnp.full((1, t), INT_MIN, jnp.int32), jnp.zeros((1, t), F32)))
    need = topk - above
    lt = lt_ref[...]

    def emit(g, run, diag):
        for r in range(sub):
            c = g * sub + r
            okey = key_ref[c]
            eq = okey == thr
            eqf = jnp.where(eq, 1.0, 0.0)
            before = _dot(lt, eqf.astype(BF16)) + run
            sel = (okey > thr) | (eq & (before < need))
            if diag:
                sel = sel & chunk_ok(r)
            o_ref[0, 0, c] = jnp.where(sel, 0.0, NEG).astype(o_ref.dtype)
            run = run + jnp.sum(eqf, axis=0, keepdims=True)
        return run

    run = _in_pairs(i, lambda g, run: emit(g, run, False), jnp.zeros((1, t), F32))
    emit(i, run, True)

    def blank(c, carry):
        o_ref[0, 0, c] = jnp.full((KEY_TILE, t), NEG, o_ref.dtype)
        return carry

    lax.fori_loop(nch, nj, blank, 0)


def _dsa_mask_call(idx, lt, topk, t):
    b, s, _ = idx.shape
    kt = lt.shape[0]
    assert t >= topk
    nt, nk = s // t, s // kt
    return pl.pallas_call(
        functools.partial(_dsa_mask_kernel, topk=topk),
        grid=(b, nt),
        in_specs=[pl.BlockSpec((1, t, IDX_Q), lambda bi, i: (bi, i, 0)),
                  pl.BlockSpec((1, s, LANES), lambda bi, i: (bi, 0, IDX_Q // LANES)),
                  pl.BlockSpec((1, t, LANES), lambda bi, i: (bi, i, IDX_Q // LANES)),
                  pl.BlockSpec((kt, kt), lambda bi, i: (0, 0))],
        out_specs=pl.BlockSpec((1, 1, nk, kt, t), lambda bi, i: (bi, i, 0, 0, 0)),
        out_shape=jax.ShapeDtypeStruct((b, nt, nk, kt, t), BF16),
        scratch_shapes=[pltpu.VMEM((nk, kt, t), jnp.int32), pltpu.VMEM((IDX_HEADS, LANES, t), BF16)],
        compiler_params=_params("parallel", "arbitrary"),
    )(idx, idx, idx, lt)


def _attn_c_kernel(q_ref, k_ref, vt_ref, b_ref, o_ref, qz_ref, m_ref, l_ref, acc_ref):
    t = q_ref.shape[1]
    i = pl.program_id(1)
    _load_qt(q_ref, qz_ref)
    m_ref[...] = jnp.full(m_ref.shape, NEG, F32)
    l_ref[...] = jnp.zeros(l_ref.shape, F32)
    acc_ref[...] = jnp.zeros(acc_ref.shape, F32)

    key, qry = _tile_ids(KEY_TILE)
    ident = jnp.where(key == qry, 1.0, 0.0).astype(BF16)

    sub = t // KEY_TILE

    def body(g, carry):
        scores = []
        for r in range(sub):
            u = g * sub + r
            bias = b_ref[0, 0, u]
            for n in range(C_HEADS):
                kb = k_ref[0, _key_rows(u), (n // 2) * LANES:(n // 2 + 1) * LANES]
                lhs = jnp.concatenate([kb, ident], axis=1)
                rhs = jnp.concatenate([qz_ref[n], bias], axis=0)
                scores.append(_dot(lhs, rhs))
        for r in range(sub):
            for n in range(C_HEADS):
                vt = vt_ref[0, g * sub + r, n * HEAD_DIM:(n + 1) * HEAD_DIM, :]
                _softmax_step(n, scores[r * C_HEADS + n], vt, m_ref, l_ref, acc_ref)
        return carry

    _in_pairs(i + 1, body, 0)
    o = acc_ref[...] / l_ref[...]
    o_ref[0] = o.reshape(C_HEADS * HEAD_DIM, t).T.astype(o_ref.dtype)


def _attn_c_call(qk, vt, bias):
    b, s, _ = qk.shape
    nk, _, kt = vt.shape[1:]
    nt, t = bias.shape[1], bias.shape[4]
    return pl.pallas_call(
        _attn_c_kernel,
        grid=(b, nt),
        in_specs=[pl.BlockSpec((1, t, GROUP), lambda bi, i: (bi, i, 2)),
                  pl.BlockSpec((1, s, GROUP), lambda bi, i: (bi, 0, 3)),
                  pl.BlockSpec((1, nk, GROUP, kt), lambda bi, i: (bi, 0, 2, 0)),
                  pl.BlockSpec((1, 1, nk, kt, t), lambda bi, i: (bi, i, 0, 0, 0))],
        out_specs=pl.BlockSpec((1, t, GROUP), lambda bi, i: (bi, i, 0)),
        out_shape=jax.ShapeDtypeStruct((b, s, GROUP), BF16),
        scratch_shapes=[pltpu.VMEM((C_HEADS, LANES, t), BF16), pltpu.VMEM((C_HEADS, 1, t), F32),
                        pltpu.VMEM((C_HEADS, 1, t), F32), pltpu.VMEM((C_HEADS, HEAD_DIM, t), F32)],
        compiler_params=_params("parallel", "arbitrary"),
    )(qk, qk, vt, bias)


def _tail_kernel(x_ref, h_ref, ya_ref, yb_ref, yc_ref, p_ref, wg_ref, wa_ref, wb_ref, wc_ref, wo_ref,
                 gm_ref, wu_ref, wd_ref, gp_ref, wpg_ref, wpp_ref, o_ref):
    d = x_ref.shape[1]
    h = h_ref[...]
    merged = jnp.zeros(x_ref.shape, F32)
    for n, (y_ref, w_ref) in enumerate(((ya_ref, wa_ref), (yb_ref, wb_ref), (yc_ref, wc_ref))):
        gate = jax.nn.sigmoid(_dot(h, wg_ref[0, :, n * d:(n + 1) * d]))
        merged = merged + gate * _dot(y_ref[...], w_ref[0])
    x = x_ref[...] + _dot(merged.astype(BF16), wo_ref[0])
    ms = jnp.mean(x * x, axis=-1, keepdims=True)
    hm = (x * lax.rsqrt(ms + EPS) * gm_ref[0]).astype(BF16)
    for f in range(wu_ref.shape[2] // FF_TILE):
        u = jnp.maximum(_dot(hm, wu_ref[0, :, f * FF_TILE:(f + 1) * FF_TILE]), 0.0)
        x = x + _dot((u * u).astype(BF16), wd_ref[0, f * FF_TILE:(f + 1) * FF_TILE, :])
    ms = jnp.mean(x * x, axis=-1, keepdims=True)
    hp = (x * lax.rsqrt(ms + EPS) * gp_ref[0]).astype(BF16)
    gate = jax.nn.sigmoid(_dot(hp, wpg_ref[0]))
    o_ref[...] = x + gate * _dot(p_ref[0].astype(BF16), wpp_ref[0])


def _tail_call(x2, h, ya, yb, yc, p3, wg, wa, wb, wc, wo, g_mlp, w_up, w_down, g_ple, w_pg, w_pp,
               layer):
    n, d = x2.shape
    pd = p3.shape[2]
    tm = TAIL_TILE
    row = lambda cols: pl.BlockSpec((tm, cols), lambda i: (i, 0))
    full = lambda w: pl.BlockSpec((1,) + w.shape[1:], lambda i: (layer, 0, 0),
                                  pipeline_mode=pl.Buffered(1))
    return pl.pallas_call(
        _tail_kernel,
        grid=(n // tm,),
        in_specs=[row(d), row(d), row(ya.shape[1]), row(yb.shape[1]), row(yc.shape[1]),
                  pl.BlockSpec((1, tm, pd), lambda i: (layer, i, 0)),
                  full(wg), full(wa), full(wb), full(wc), full(wo),
                  full(g_mlp), full(w_up), full(w_down), full(g_ple), full(w_pg), full(w_pp)],
        out_specs=row(d),
        out_shape=jax.ShapeDtypeStruct((n, d), F32),
        compiler_params=_params("parallel"),
    )(x2, h, ya, yb, yc, p3, wg, wa, wb, wc, wo, g_mlp, w_up, w_down, g_ple, w_pg, w_pp)


def _tile_gain(g, reps, scale=1.0):
    return jnp.tile(g.astype(F32), (1, reps)) * scale


def kernel(x, p, positions, attn_norm, w_in, a_q_norm, a_k_norm, a_lambda, a_subln,
           c_q_norm, c_k_norm, idx_k_norm, w_br_a, w_br_b, w_br_c, w_out,
           mlp_norm, w_up, w_down, ple_norm, w_ple_gate, w_ple_proj):
    b, s, d = x.shape
    depth = w_in.shape[0]
    n = b * s
    topk = min(TOPK_MAX, s // 4)
    t = min(ATTN_TILE, s)
    scale = HEAD_DIM ** -0.5
    qscale = scale * LOG2E

    inv = ROPE_THETA ** (-jnp.arange(0, HEAD_DIM, 2, dtype=F32) / HEAD_DIM)
    ang = positions.astype(F32)[..., None] * inv
    cos, sin = jnp.cos(ang), jnp.sin(ang)
    cos_t = jnp.concatenate([cos, cos, cos, cos], axis=-1).reshape(n, LANES)
    sin_t = jnp.concatenate([-sin, sin, -sin, sin], axis=-1).reshape(n, LANES)

    qa, ka, va = 0, 512, 1024
    qb, kb, vb = 1536, 2048, 2560
    qc, kc, vc = 3072, 3584, 4096
    qi = 4608
    gl = 4932
    col = lambda a, width=GROUP: w_in[:, :, a:a + width]
    w_qk = jnp.concatenate([col(qa), col(ka), col(qc), col(kc)], axis=-1).astype(BF16)
    w_bqk = jnp.concatenate([col(qb) * qscale, col(kb)], axis=-1).astype(BF16)
    w_vt = jnp.concatenate([col(va), col(vb), col(vc)], axis=-1).astype(BF16)
    w_idx = jnp.pad(col(qi, gl - qi), ((0, 0), (0, 0), (0, IDX_COLS - (gl - qi)))).astype(BF16)
    w_gl = col(gl, N_BRANCHES * d).astype(BF16)
    heads = GROUP // HEAD_DIM
    gain_qk = jnp.concatenate([_tile_gain(a_q_norm, heads, qscale), _tile_gain(a_k_norm, heads),
                               _tile_gain(c_q_norm, heads, qscale), _tile_gain(c_k_norm, heads)],
                              axis=-1)[:, None, :]
    gain_idx = jnp.pad(idx_k_norm.astype(F32), ((0, 0), (0, LANES - IDX_DIM)))[:, None, :]
    scale_idx = jnp.concatenate([jnp.full((IDX_Q,), IDX_DIM ** -0.5, F32), jnp.ones((IDX_DIM,), F32),
                                 jnp.full((IDX_HEADS,), IDX_HEADS ** -0.5, F32),
                                 jnp.zeros((LANES - IDX_DIM - IDX_HEADS,), F32)])[None, :]
    gid = jnp.arange(GROUP) // HEAD_DIM
    gmat = (gid[:, None] == gid[None, :]).astype(BF16)
    kt = min(KEY_TILE, s)
    ar = jnp.arange(kt)
    later_mat = jnp.concatenate([(ar[None, :] > ar[:, None]).astype(BF16),
                                 jnp.ones((ONES_ROWS, kt), BF16)], axis=0)
    before_mat = (ar[None, :] < ar[:, None]).astype(BF16)

    wa, wb, wc, wo = (w.astype(BF16) for w in (w_br_a, w_br_b, w_br_c, w_out))
    wu, wd, wpg, wpp = (w.astype(BF16) for w in (w_up, w_down, w_ple_gate, w_ple_proj))
    g_attn, g_mlp, g_ple = (g.astype(F32)[:, None, :] for g in (attn_norm, mlp_norm, ple_norm))
    subln = a_subln.astype(F32)[:, None, :]
    lam_p = a_lambda.astype(F32)
    p3 = p.reshape(depth, n, p.shape[-1])

    x2 = x.reshape(n, d)
    for i in range(depth):
        lam_init = 0.8 - 0.6 * math.exp(-0.3 * i)
        h, qk, bqk, vt, idx = _front_call(x2, g_attn, w_qk, w_bqk, w_vt, w_idx, gain_qk, gain_idx,
                                          scale_idx, cos_t, sin_t, gmat, i, kt)
        qk, bqk, idx = (a.reshape(b, s, -1) for a in (qk, bqk, idx))
        vt = vt.reshape(b, s // kt, -1, kt)
        ya, yb = _attn_ab_call(qk, bqk, vt, lam_p, subln, later_mat, i, lam_init, t)
        bias = _dsa_mask_call(idx, before_mat, topk, t)
        yc = _attn_c_call(qk, vt, bias)
        x2 = _tail_call(x2, h, ya.reshape(n, -1), yb.reshape(n, -1), yc.reshape(n, -1), p3,
                        w_gl, wa, wb, wc, wo, g_mlp, wu, wd, g_ple, wpg, wpp, i)
    return x2.reshape(b, s, d)
```

```python
import functools
import math

import jax
import jax.numpy as jnp
from jax import lax
from jax.experimental import pallas as pl
from jax.experimental.pallas import tpu as pltpu

F32 = jnp.float32
BF16 = jnp.bfloat16

HEAD_DIM = 64
CHUNK = 64
A_HEADS = 4
B_HEADS = 8
C_HEADS = 8
IDX_HEADS = 4
IDX_DIM = 64
TOPK_MAX = 256
N_BRANCHES = 3
ROPE_THETA = 10000.0
EPS = 1e-6

LANES = 128
SUBLANES = 8
HALF_HEAD = HEAD_DIM // 2
NEG = -1e30
INT_MIN = -(2 ** 31)
LOG2E = 1.4426950408889634
V7X_VMEM_BYTES = 64 * 1024 * 1024
VMEM_LIMIT = V7X_VMEM_BYTES * 7 // 8

ROW_TILE = 512
ATTN_TILE = 256
KEY_TILE = 128
FF_TILE = 1024
TAIL_TILE = 512
GROUP = 512


def _params(*sem):
    return pltpu.CompilerParams(dimension_semantics=sem, vmem_limit_bytes=VMEM_LIMIT)


def _dot(a, b):
    return jnp.dot(a, b, preferred_element_type=F32)


def _swap_halves(y):
    lane = lax.broadcasted_iota(jnp.int32, y.shape, 1)
    return jnp.where((lane & HALF_HEAD) != 0, pltpu.roll(y, HALF_HEAD, 1),
                     pltpu.roll(y, LANES - HALF_HEAD, 1))


def _rope_slab(y, cos, sin):
    return y * cos + _swap_halves(y) * sin


IDX_Q = IDX_HEADS * IDX_DIM
IDX_COLS = IDX_Q + LANES


def _front_kernel(x_ref, g_ref, wqk_ref, wbqk_ref, wvt_ref, widx_ref, gqk_ref, gidx_ref, sidx_ref,
                  cos_ref, sin_ref, gmat_ref, h_ref, qk_ref, bqk_ref, vt_ref, idx_ref):
    x = x_ref[...]
    ms = jnp.mean(x * x, axis=-1, keepdims=True)
    h = (x * lax.rsqrt(ms + EPS) * g_ref[0]).astype(BF16)
    h_ref[...] = h
    cos = cos_ref[...]
    sin = sin_ref[...]

    gmat = gmat_ref[...]
    for j in range(qk_ref.shape[1] // GROUP):
        cols = slice(j * GROUP, (j + 1) * GROUP)
        t = _dot(h, wqk_ref[0, :, cols])
        ms = _dot((t * t).astype(BF16), gmat) * (1.0 / HEAD_DIM)
        y = t * lax.rsqrt(ms + EPS) * gqk_ref[0, :, cols]
        for s in range(GROUP // LANES):
            sl = slice(s * LANES, (s + 1) * LANES)
            qk_ref[:, j * GROUP + s * LANES:j * GROUP + (s + 1) * LANES] = (
                _rope_slab(y[:, sl], cos, sin).astype(qk_ref.dtype))

    for j in range(bqk_ref.shape[1] // GROUP):
        cols = slice(j * GROUP, (j + 1) * GROUP)
        bqk_ref[:, cols] = _dot(h, wbqk_ref[0, :, cols]).astype(bqk_ref.dtype)

    tt = vt_ref.shape[2]
    for j in range(vt_ref.shape[1] // GROUP):
        cols = slice(j * GROUP, (j + 1) * GROUP)
        t = _dot(h, wvt_ref[0, :, cols])
        for r in range(vt_ref.shape[0]):
            vt_ref[r, cols, :] = t[r * tt:(r + 1) * tt, :].T.astype(vt_ref.dtype)

    t = _dot(h, widx_ref[0])
    scale = sidx_ref[...]
    for s in range(IDX_Q // LANES):
        sl = slice(s * LANES, (s + 1) * LANES)
        idx_ref[:, sl] = _rope_slab(t[:, sl], cos, sin) * scale[:, sl]
    sl = slice(IDX_Q, IDX_COLS)
    tk = t[:, sl]
    lane = lax.broadcasted_iota(jnp.int32, tk.shape, 1)
    is_k = lane < IDX_DIM
    ms = jnp.sum(jnp.where(is_k, tk * tk, 0.0), axis=-1, keepdims=True) * (1.0 / IDX_DIM)
    yk = tk * lax.rsqrt(ms + EPS) * gidx_ref[0]
    idx_ref[:, sl] = jnp.where(is_k, _rope_slab(yk, cos, sin), tk) * scale[:, sl]


def _front_call(x2, g_attn, w_qk, w_bqk, w_vt, w_idx, gain_qk, gain_idx, scale_idx, cos_t, sin_t,
                gmat, layer, tt):
    n, d = x2.shape
    tm = ROW_TILE
    row = lambda cols: pl.BlockSpec((tm, cols), lambda i: (i, 0))
    full = lambda w: pl.BlockSpec((1,) + w.shape[1:], lambda i: (layer, 0, 0))
    fixed = lambda a: pl.BlockSpec(a.shape, lambda i: (0, 0))
    cqk, cbqk, cvt = w_qk.shape[2], w_bqk.shape[2], w_vt.shape[2]
    return pl.pallas_call(
        _front_kernel,
        grid=(n // tm,),
        in_specs=[row(d), full(g_attn), full(w_qk), full(w_bqk), full(w_vt), full(w_idx),
                  full(gain_qk), full(gain_idx), fixed(scale_idx), row(LANES), row(LANES),
                  fixed(gmat)],
        out_specs=[row(d), row(cqk), row(cbqk),
                   pl.BlockSpec((tm // tt, cvt, tt), lambda i: (i, 0, 0)), row(IDX_COLS)],
        out_shape=[jax.ShapeDtypeStruct((n, d), BF16), jax.ShapeDtypeStruct((n, cqk), BF16),
                   jax.ShapeDtypeStruct((n, cbqk), BF16),
                   jax.ShapeDtypeStruct((n // tt, cvt, tt), BF16),
                   jax.ShapeDtypeStruct((n, IDX_COLS), F32)],
        compiler_params=_params("parallel"),
    )(x2, g_attn, w_qk, w_bqk, w_vt, w_idx, gain_qk, gain_idx, scale_idx, cos_t, sin_t, gmat)


def _load_qt(q_ref, qz_ref):
    qt = q_ref[0].astype(F32).T
    rows = lax.broadcasted_iota(jnp.int32, (LANES, qt.shape[1]), 0)
    for s in range(qt.shape[0] // LANES):
        slab = qt[s * LANES:(s + 1) * LANES, :]
        qz_ref[2 * s] = jnp.where(rows < HEAD_DIM, slab, 0.0).astype(BF16)
        qz_ref[2 * s + 1] = jnp.where(rows >= HEAD_DIM, slab, 0.0).astype(BF16)


ONES_ROWS = 16


def _softmax_step(n, s, vt, m_ref, l_ref, acc_ref):
    m_old = m_ref[n]
    m_new = jnp.maximum(m_old, jnp.max(s, axis=0, keepdims=True))
    p = jnp.exp2(s - m_new).astype(BF16)
    alpha = jnp.exp2(m_old - m_new)
    f = vt.shape[0]
    pv = _dot(jnp.concatenate([vt, jnp.ones((ONES_ROWS, vt.shape[1]), BF16)], axis=0), p)
    l_ref[n] = alpha * l_ref[n] + pv[f:f + 1]
    acc_ref[n] = alpha * acc_ref[n] + pv[:f]
    m_ref[n] = m_new


def _tile_ids(t):
    key = lax.broadcasted_iota(jnp.int32, (KEY_TILE, t), 0)
    qry = lax.broadcasted_iota(jnp.int32, (KEY_TILE, t), 1)
    return key, qry


def _key_rows(u):
    return pl.ds(pl.multiple_of(u * KEY_TILE, KEY_TILE), KEY_TILE)


def _in_pairs(n, one, carry):
    carry = lax.fori_loop(0, n // 2, lambda p, c: one(2 * p + 1, one(2 * p, c)), carry)
    return lax.cond(n % 2 == 1, lambda c: one(n - 1, c), lambda c: c, carry)


def _in_fours(n, one, carry):
    def four(p, c):
        for j in range(4):
            c = one(4 * p + j, c)
        return c
    carry = lax.fori_loop(0, n // 4, four, carry)
    return lax.fori_loop((n // 4) * 4, n, one, carry)


def _interleave(*gens):
    gens = list(gens)
    while gens:
        for gen in list(gens):
            try:
                next(gen)
            except StopIteration:
                gens.remove(gen)


def _attn_ab_kernel(lam_ref, aq_ref, ak_ref, avt_ref, g_ref, bq_ref, bk_ref, bvt_ref, lmat_ref,
                    ya_ref, yb_ref, aqz_ref, m_ref, l_ref, aacc_ref, bqz_ref, r_ref, bacc_ref, *,
                    lam_init):
    t = aq_ref.shape[1]
    i = pl.program_id(1)
    lp = lam_ref[0]
    lam = (jnp.exp(jnp.sum(lp[0:1] * lp[1:2], axis=-1, keepdims=True))
           - jnp.exp(jnp.sum(lp[2:3] * lp[3:4], axis=-1, keepdims=True)) + lam_init)
    key, qry = _tile_ids(t)
    sub = t // KEY_TILE
    lmat = lmat_ref[...]
    _load_qt(aq_ref, aqz_ref)
    _load_qt(bq_ref, bqz_ref)
    m_ref[...] = jnp.full(m_ref.shape, NEG, F32)
    l_ref[...] = jnp.zeros(l_ref.shape, F32)
    aacc_ref[...] = jnp.zeros(aacc_ref.shape, F32)
    r_ref[...] = jnp.zeros(r_ref.shape, F32)
    bacc_ref[...] = jnp.zeros(bacc_ref.shape, F32)

    def a_group(g, diag):
        scores = []
        for r in range(sub):
            for n in range(2 * A_HEADS):
                kb = ak_ref[0, _key_rows(g * sub + r), (n // 2) * LANES:(n // 2 + 1) * LANES]
                scores.append(_dot(kb, aqz_ref[n]))
        yield
        for r in range(sub):
            for n in range(2 * A_HEADS):
                s = scores[r * 2 * A_HEADS + n]
                if diag:
                    s = jnp.where(((key + r * KEY_TILE) // CHUNK) <= (qry // CHUNK), s, NEG)
                vt = avt_ref[0, g * sub + r, (n // 2) * LANES:(n // 2 + 1) * LANES, :]
                _softmax_step(n, s, vt, m_ref, l_ref, aacc_ref)
                yield

    def b_group(g, diag):
        order = [(r, n) for r in reversed(range(sub)) for n in range(B_HEADS)]
        strict = {r: (key + r * KEY_TILE) < qry for r in range(sub)} if diag else None
        logits = {}
        for r, n in order:
            kb = bk_ref[0, _key_rows(g * sub + r), (n // 2) * LANES:(n // 2 + 1) * LANES]
            logits[r, n] = _dot(kb, bqz_ref[n])
        yield
        log_betas, laters, sums = {}, {}, {}
        for r, n in order:
            z = logits[r, n]
            neg_abs = pltpu.bitcast(pltpu.bitcast(z, jnp.int32) | jnp.int32(INT_MIN), F32)
            soft = jnp.log2(1.0 + jnp.exp2(neg_abs))
            log_beta = jnp.minimum(z, 0.0) - soft
            log_om = log_beta - z
            if diag:
                log_om = jnp.where(strict[r], log_om, 0.0)
            later = _dot(lmat, log_om.astype(BF16))
            laters[r, n] = later[:KEY_TILE]
            log_betas[r, n] = log_beta
            sums[r, n] = later[KEY_TILE:KEY_TILE + 1]
            yield
        for r, n in order:
            a = jnp.exp2(log_betas[r, n] + laters[r, n])
            if diag:
                a = jnp.where(strict[r], a, 0.0)
            vt = bvt_ref[0, g * sub + r, n * HEAD_DIM:(n + 1) * HEAD_DIM, :]
            visited = r_ref[n]
            bacc_ref[n] = bacc_ref[n] + _dot(vt, a.astype(BF16)) * jnp.exp2(visited)
            r_ref[n] = visited + sums[r, n]
            yield

    def body(n, carry):
        _interleave(a_group(n, False), b_group(i - 1 - n, False))
        return carry

    _interleave(a_group(i, True), b_group(i, True))
    _in_pairs(i, body, 0)

    gain = g_ref[0] * (1.0 - lam_init)
    for h in range(A_HEADS):
        o = aacc_ref[2 * h] / l_ref[2 * h] - lam * (aacc_ref[2 * h + 1] / l_ref[2 * h + 1])
        ms = jnp.mean(o * o, axis=0, keepdims=True)
        y = (o * lax.rsqrt(ms + EPS)).T * gain
        ya_ref[0, :, h * LANES:(h + 1) * LANES] = y.astype(ya_ref.dtype)
    yb_ref[0] = bacc_ref[...].reshape(B_HEADS * HEAD_DIM, t).T.astype(yb_ref.dtype)


def _attn_ab_call(qk, bqk, vt, lam_p, subln, lmat, layer, lam_init, t):
    b, s, _ = qk.shape
    nt, _, kt = vt.shape[1:]
    ns = 2 * A_HEADS
    qspec = pl.BlockSpec((1, t, GROUP), lambda bi, i: (bi, i, 0))
    kspec = pl.BlockSpec((1, s, GROUP), lambda bi, i: (bi, 0, 1))
    vspec = lambda f: pl.BlockSpec((1, nt, GROUP, kt), lambda bi, i: (bi, 0, f, 0))
    return pl.pallas_call(
        functools.partial(_attn_ab_kernel, lam_init=lam_init),
        grid=(b, s // t),
        in_specs=[pl.BlockSpec((1, 4, HEAD_DIM), lambda bi, i: (layer, 0, 0)),
                  qspec, kspec, vspec(0),
                  pl.BlockSpec((1, 1, LANES), lambda bi, i: (layer, 0, 0)),
                  qspec, kspec, vspec(1),
                  pl.BlockSpec(lmat.shape, lambda bi, i: (0, 0))],
        out_specs=[qspec, qspec],
        out_shape=[jax.ShapeDtypeStruct((b, s, GROUP), BF16)] * 2,
        scratch_shapes=[pltpu.VMEM((ns, LANES, t), BF16), pltpu.VMEM((ns, 1, t), F32),
                        pltpu.VMEM((ns, 1, t), F32), pltpu.VMEM((ns, LANES, t), F32),
                        pltpu.VMEM((B_HEADS, LANES, t), BF16), pltpu.VMEM((B_HEADS, 1, t), F32),
                        pltpu.VMEM((B_HEADS, HEAD_DIM, t), F32)],
        compiler_params=_params("parallel", "arbitrary"),
    )(lam_p, qk, qk, vt, subln, bqk, bqk, vt, lmat)


def _fold_rows(x):
    parts = [x[r * SUBLANES:(r + 1) * SUBLANES] for r in range(x.shape[0] // SUBLANES)]
    while len(parts) > 1:
        parts = [a + b for a, b in zip(parts[0::2], parts[1::2])]
    return parts[0]


def _dsa_mask_kernel(q_ref, kw_ref, wq_ref, lt_ref, o_ref, key_ref, qz_ref, *, topk):
    t = q_ref.shape[1]
    nj = key_ref.shape[0]
    i = pl.program_id(1)
    nch = (i + 1) * (t // KEY_TILE)
    qt = q_ref[0].T
    pad = jnp.zeros((LANES - IDX_DIM, t), F32)
    for h in range(IDX_HEADS):
        qz_ref[h] = jnp.concatenate([qt[h * IDX_DIM:(h + 1) * IDX_DIM], pad], axis=0).astype(BF16)
    wt = wq_ref[0].T
    wh = [wt[IDX_DIM + h:IDX_DIM + h + 1, :] for h in range(IDX_HEADS)]
    key0, qry0 = _tile_ids(t)
    sub = t // KEY_TILE

    def chunk_ok(r):
        return ((key0 + r * KEY_TILE) // CHUNK) <= (qry0 // CHUNK)

    def fill(g, diag):
        for r in range(sub):
            c = g * sub + r
            kb = kw_ref[0, _key_rows(c), :].astype(BF16)
            score = jnp.zeros((KEY_TILE, t), F32)
            for h in range(IDX_HEADS):
                score = score + wh[h] * jnp.maximum(_dot(kb, qz_ref[h]), 0.0)
            if diag:
                score = jnp.where(chunk_ok(r), score, -jnp.inf)
            bits = pltpu.bitcast(score, jnp.int32)
            okey = jnp.where(bits < 0, bits ^ jnp.int32(0x7FFFFFFF), bits)
            key_ref[c] = jnp.where(score == 0.0, 0, okey)

    def fill_one(g, carry):
        fill(g, False)
        return carry

    _in_fours(i, fill_one, 0)
    fill(i, True)

    def count_at_or_above(cand):
        def one(g, acc):
            for r in range(sub):
                acc = acc + _fold_rows(jnp.where(key_ref[g * sub + r] >= cand, 1.0, 0.0))
            return acc
        acc = _in_fours(i + 1, one, jnp.zeros((SUBLANES, t), F32))
        return jnp.sum(acc, axis=0, keepdims=True)

    def bit_body(n, carry):
        lo, above = carry
        cand = lo + jnp.left_shift(jnp.int32(1), 31 - n)
        cnt = count_at_or_above(cand)
        take = cnt >= topk
        return jnp.where(take, cand, lo), jnp.where(take, above, cnt)

    thr, above = lax.fori_loop(
        0, 32, bit_body, (jnp.full((1, t), INT_MIN, jnp.int32), jnp.zeros((1, t), F32)))
    need = topk - above
    lt = lt_ref[...]

    def emit(g, run, diag):
        for r in range(sub):
            c = g * sub + r
            okey = key_ref[c]
            eq = okey == thr
            eqf = jnp.where(eq, 1.0, 0.0)
            before = _dot(lt, eqf.astype(BF16)) + run
            sel = (okey > thr) | (eq & (before < need))
            if diag:
                sel = sel & chunk_ok(r)
            o_ref[0, 0, c] = jnp.where(sel, 0.0, NEG).astype(o_ref.dtype)
            run = run + jnp.sum(eqf, axis=0, keepdims=True)
        return run

    run = _in_fours(i, lambda g, run: emit(g, run, False), jnp.zeros((1, t), F32))
    emit(i, run, True)

    def blank(c, carry):
        o_ref[0, 0, c] = jnp.full((KEY_TILE, t), NEG, o_ref.dtype)
        return carry

    lax.fori_loop(nch, nj, blank, 0)


def _dsa_mask_call(idx, lt, topk, t):
    b, s, _ = idx.shape
    kt = lt.shape[0]
    assert t >= topk
    nt, nk = s // t, s // kt
    return pl.pallas_call(
        functools.partial(_dsa_mask_kernel, topk=topk),
        grid=(b, nt),
        in_specs=[pl.BlockSpec((1, t, IDX_Q), lambda bi, i: (bi, i, 0)),
                  pl.BlockSpec((1, s, LANES), lambda bi, i: (bi, 0, IDX_Q // LANES)),
                  pl.BlockSpec((1, t, LANES), lambda bi, i: (bi, i, IDX_Q // LANES)),
                  pl.BlockSpec((kt, kt), lambda bi, i: (0, 0))],
        out_specs=pl.BlockSpec((1, 1, nk, kt, t), lambda bi, i: (bi, i, 0, 0, 0)),
        out_shape=jax.ShapeDtypeStruct((b, nt, nk, kt, t), BF16),
        scratch_shapes=[pltpu.VMEM((nk, kt, t), jnp.int32), pltpu.VMEM((IDX_HEADS, LANES, t), BF16)],
        compiler_params=_params("parallel", "arbitrary"),
    )(idx, idx, idx, lt)


def _attn_c_kernel(q_ref, k_ref, vt_ref, b_ref, o_ref, qz_ref, m_ref, l_ref, acc_ref):
    t = q_ref.shape[1]
    i = pl.program_id(1)
    _load_qt(q_ref, qz_ref)
    m_ref[...] = jnp.full(m_ref.shape, NEG, F32)
    l_ref[...] = jnp.zeros(l_ref.shape, F32)
    acc_ref[...] = jnp.zeros(acc_ref.shape, F32)

    key, qry = _tile_ids(KEY_TILE)
    ident = jnp.where(key == qry, 1.0, 0.0).astype(BF16)

    sub = t // KEY_TILE

    def body(g, carry):
        scores = []
        for r in range(sub):
            u = g * sub + r
            bias = b_ref[0, 0, u]
            for n in range(C_HEADS):
                kb = k_ref[0, _key_rows(u), (n // 2) * LANES:(n // 2 + 1) * LANES]
                lhs = jnp.concatenate([kb, ident], axis=1)
                rhs = jnp.concatenate([qz_ref[n], bias], axis=0)
                scores.append(_dot(lhs, rhs))
        for r in range(sub):
            for n in range(C_HEADS):
                vt = vt_ref[0, g * sub + r, n * HEAD_DIM:(n + 1) * HEAD_DIM, :]
                _softmax_step(n, scores[r * C_HEADS + n], vt, m_ref, l_ref, acc_ref)
        return carry

    _in_pairs(i + 1, body, 0)
    o = acc_ref[...] / l_ref[...]
    o_ref[0] = o.reshape(C_HEADS * HEAD_DIM, t).T.astype(o_ref.dtype)


def _attn_c_call(qk, vt, bias):
    b, s, _ = qk.shape
    nk, _, kt = vt.shape[1:]
    nt, t = bias.shape[1], bias.shape[4]
    return pl.pallas_call(
        _attn_c_kernel,
        grid=(b, nt),
        in_specs=[pl.BlockSpec((1, t, GROUP), lambda bi, i: (bi, i, 2)),
                  pl.BlockSpec((1, s, GROUP), lambda bi, i: (bi, 0, 3)),
                  pl.BlockSpec((1, nk, GROUP, kt), lambda bi, i: (bi, 0, 2, 0)),
                  pl.BlockSpec((1, 1, nk, kt, t), lambda bi, i: (bi, i, 0, 0, 0))],
        out_specs=pl.BlockSpec((1, t, GROUP), lambda bi, i: (bi, i, 0)),
        out_shape=jax.ShapeDtypeStruct((b, s, GROUP), BF16),
        scratch_shapes=[pltpu.VMEM((C_HEADS, LANES, t), BF16), pltpu.VMEM((C_HEADS, 1, t), F32),
                        pltpu.VMEM((C_HEADS, 1, t), F32), pltpu.VMEM((C_HEADS, HEAD_DIM, t), F32)],
        compiler_params=_params("parallel", "arbitrary"),
    )(qk, qk, vt, bias)


def _tail_kernel(x_ref, h_ref, ya_ref, yb_ref, yc_ref, p_ref, wg_ref, wa_ref, wb_ref, wc_ref, wo_ref,
                 gm_ref, wu_ref, wd_ref, gp_ref, wpg_ref, wpp_ref, o_ref):
    d = x_ref.shape[1]
    h = h_ref[...]
    merged = jnp.zeros(x_ref.shape, F32)
    for n, (y_ref, w_ref) in enumerate(((ya_ref, wa_ref), (yb_ref, wb_ref), (yc_ref, wc_ref))):
        gate = jax.nn.sigmoid(_dot(h, wg_ref[0, :, n * d:(n + 1) * d]))
        merged = merged + gate * _dot(y_ref[...], w_ref[0])
    x = x_ref[...] + _dot(merged.astype(BF16), wo_ref[0])
    ms = jnp.mean(x * x, axis=-1, keepdims=True)
    hm = (x * lax.rsqrt(ms + EPS) * gm_ref[0]).astype(BF16)
    for f in range(wu_ref.shape[2] // FF_TILE):
        u = jnp.maximum(_dot(hm, wu_ref[0, :, f * FF_TILE:(f + 1) * FF_TILE]), 0.0)
        x = x + _dot((u * u).astype(BF16), wd_ref[0, f * FF_TILE:(f + 1) * FF_TILE, :])
    ms = jnp.mean(x * x, axis=-1, keepdims=True)
    hp = (x * lax.rsqrt(ms + EPS) * gp_ref[0]).astype(BF16)
    gate = jax.nn.sigmoid(_dot(hp, wpg_ref[0]))
    o_ref[...] = x + gate * _dot(p_ref[0].astype(BF16), wpp_ref[0])


def _tail_call(x2, h, ya, yb, yc, p3, wg, wa, wb, wc, wo, g_mlp, w_up, w_down, g_ple, w_pg, w_pp,
               layer):
    n, d = x2.shape
    pd = p3.shape[2]
    tm = TAIL_TILE
    row = lambda cols: pl.BlockSpec((tm, cols), lambda i: (i, 0))
    full = lambda w: pl.BlockSpec((1,) + w.shape[1:], lambda i: (layer, 0, 0),
                                  pipeline_mode=pl.Buffered(1))
    return pl.pallas_call(
        _tail_kernel,
        grid=(n // tm,),
        in_specs=[row(d), row(d), row(ya.shape[1]), row(yb.shape[1]), row(yc.shape[1]),
                  pl.BlockSpec((1, tm, pd), lambda i: (layer, i, 0)),
                  full(wg), full(wa), full(wb), full(wc), full(wo),
                  full(g_mlp), full(w_up), full(w_down), full(g_ple), full(w_pg), full(w_pp)],
        out_specs=row(d),
        out_shape=jax.ShapeDtypeStruct((n, d), F32),
        compiler_params=_params("parallel"),
    )(x2, h, ya, yb, yc, p3, wg, wa, wb, wc, wo, g_mlp, w_up, w_down, g_ple, w_pg, w_pp)


def _tile_gain(g, reps, scale=1.0):
    return jnp.tile(g.astype(F32), (1, reps)) * scale


def kernel(x, p, positions, attn_norm, w_in, a_q_norm, a_k_norm, a_lambda, a_subln,
           c_q_norm, c_k_norm, idx_k_norm, w_br_a, w_br_b, w_br_c, w_out,
           mlp_norm, w_up, w_down, ple_norm, w_ple_gate, w_ple_proj):
    b, s, d = x.shape
    depth = w_in.shape[0]
    n = b * s
    topk = min(TOPK_MAX, s // 4)
    t = min(ATTN_TILE, s)
    scale = HEAD_DIM ** -0.5
    qscale = scale * LOG2E

    inv = ROPE_THETA ** (-jnp.arange(0, HEAD_DIM, 2, dtype=F32) / HEAD_DIM)
    ang = positions.astype(F32)[..., None] * inv
    cos, sin = jnp.cos(ang), jnp.sin(ang)
    cos_t = jnp.concatenate([cos, cos, cos, cos], axis=-1).reshape(n, LANES)
    sin_t = jnp.concatenate([-sin, sin, -sin, sin], axis=-1).reshape(n, LANES)

    qa, ka, va = 0, 512, 1024
    qb, kb, vb = 1536, 2048, 2560
    qc, kc, vc = 3072, 3584, 4096
    qi = 4608
    gl = 4932
    col = lambda a, width=GROUP: w_in[:, :, a:a + width]
    w_qk = jnp.concatenate([col(qa), col(ka), col(qc), col(kc)], axis=-1).astype(BF16)
    w_bqk = jnp.concatenate([col(qb) * qscale, col(kb)], axis=-1).astype(BF16)
    w_vt = jnp.concatenate([col(va), col(vb), col(vc)], axis=-1).astype(BF16)
    w_idx = jnp.pad(col(qi, gl - qi), ((0, 0), (0, 0), (0, IDX_COLS - (gl - qi)))).astype(BF16)
    w_gl = col(gl, N_BRANCHES * d).astype(BF16)
    heads = GROUP // HEAD_DIM
    gain_qk = jnp.concatenate([_tile_gain(a_q_norm, heads, qscale), _tile_gain(a_k_norm, heads),
                               _tile_gain(c_q_norm, heads, qscale), _tile_gain(c_k_norm, heads)],
                              axis=-1)[:, None, :]
    gain_idx = jnp.pad(idx_k_norm.astype(F32), ((0, 0), (0, LANES - IDX_DIM)))[:, None, :]
    scale_idx = jnp.concatenate([jnp.full((IDX_Q,), IDX_DIM ** -0.5, F32), jnp.ones((IDX_DIM,), F32),
                                 jnp.full((IDX_HEADS,), IDX_HEADS ** -0.5, F32),
                                 jnp.zeros((LANES - IDX_DIM - IDX_HEADS,), F32)])[None, :]
    gid = jnp.arange(GROUP) // HEAD_DIM
    gmat = (gid[:, None] == gid[None, :]).astype(BF16)
    kt = min(KEY_TILE, s)
    ar = jnp.arange(kt)
    later_mat = jnp.concatenate([(ar[None, :] > ar[:, None]).astype(BF16),
                                 jnp.ones((ONES_ROWS, kt), BF16)], axis=0)
    before_mat = (ar[None, :] < ar[:, None]).astype(BF16)

    wa, wb, wc, wo = (w.astype(BF16) for w in (w_br_a, w_br_b, w_br_c, w_out))
    wu, wd, wpg, wpp = (w.astype(BF16) for w in (w_up, w_down, w_ple_gate, w_ple_proj))
    g_attn, g_mlp, g_ple = (g.astype(F32)[:, None, :] for g in (attn_norm, mlp_norm, ple_norm))
    subln = a_subln.astype(F32)[:, None, :]
    lam_p = a_lambda.astype(F32)
    p3 = p.reshape(depth, n, p.shape[-1])

    x2 = x.reshape(n, d)
    for i in range(depth):
        lam_init = 0.8 - 0.6 * math.exp(-0.3 * i)
        h, qk, bqk, vt, idx = _front_call(x2, g_attn, w_qk, w_bqk, w_vt, w_idx, gain_qk, gain_idx,
                                          scale_idx, cos_t, sin_t, gmat, i, kt)
        qk, bqk, idx = (a.reshape(b, s, -1) for a in (qk, bqk, idx))
        vt = vt.reshape(b, s // kt, -1, kt)
        ya, yb = _attn_ab_call(qk, bqk, vt, lam_p, subln, later_mat, i, lam_init, t)
        bias = _dsa_mask_call(idx, before_mat, topk, t)
        yc = _attn_c_call(qk, vt, bias)
        x2 = _tail_call(x2, h, ya.reshape(n, -1), yb.reshape(n, -1), yc.reshape(n, -1), p3,
                        w_gl, wa, wb, wc, wo, g_mlp, wu, wd, g_ple, wpg, wpp, i)
    return x2.reshape(b, s, d)
```

```python
import functools
import math

import jax
import jax.numpy as jnp
from jax import lax
from jax.experimental import pallas as pl
from jax.experimental.pallas import tpu as pltpu

F32 = jnp.float32
BF16 = jnp.bfloat16

HEAD_DIM = 64
CHUNK = 64
A_HEADS = 4
B_HEADS = 8
C_HEADS = 8
IDX_HEADS = 4
IDX_DIM = 64
TOPK_MAX = 256
N_BRANCHES = 3
ROPE_THETA = 10000.0
EPS = 1e-6

LANES = 128
SUBLANES = 8
HALF_HEAD = HEAD_DIM // 2
NEG = -1e30
INT_MIN = -(2 ** 31)
LOG2E = 1.4426950408889634
V7X_VMEM_BYTES = 64 * 1024 * 1024
VMEM_LIMIT = V7X_VMEM_BYTES * 7 // 8

ROW_TILE = 512
ATTN_TILE = 256
KEY_TILE = 128
FF_TILE = 1024
TAIL_TILE = 512
GROUP = 512


def _params(*sem):
    return pltpu.CompilerParams(dimension_semantics=sem, vmem_limit_bytes=VMEM_LIMIT)


def _dot(a, b):
    return jnp.dot(a, b, preferred_element_type=F32)


def _swap_halves(y):
    lane = lax.broadcasted_iota(jnp.int32, y.shape, 1)
    return jnp.where((lane & HALF_HEAD) != 0, pltpu.roll(y, HALF_HEAD, 1),
                     pltpu.roll(y, LANES - HALF_HEAD, 1))


def _rope_slab(y, cos, sin):
    return y * cos + _swap_halves(y) * sin


IDX_Q = IDX_HEADS * IDX_DIM
IDX_COLS = IDX_Q + LANES


def _front_kernel(x_ref, g_ref, wqk_ref, wbqk_ref, wvt_ref, widx_ref, gqk_ref, gidx_ref, sidx_ref,
                  cos_ref, sin_ref, gmat_ref, h_ref, qk_ref, bqk_ref, vt_ref, idx_ref):
    x = x_ref[...]
    ms = jnp.mean(x * x, axis=-1, keepdims=True)
    h = (x * lax.rsqrt(ms + EPS) * g_ref[0]).astype(BF16)
    h_ref[...] = h
    cos = cos_ref[...]
    sin = sin_ref[...]

    gmat = gmat_ref[...]
    for j in range(qk_ref.shape[1] // GROUP):
        cols = slice(j * GROUP, (j + 1) * GROUP)
        t = _dot(h, wqk_ref[0, :, cols])
        ms = _dot((t * t).astype(BF16), gmat) * (1.0 / HEAD_DIM)
        y = t * lax.rsqrt(ms + EPS) * gqk_ref[0, :, cols]
        for s in range(GROUP // LANES):
            sl = slice(s * LANES, (s + 1) * LANES)
            qk_ref[:, j * GROUP + s * LANES:j * GROUP + (s + 1) * LANES] = (
                _rope_slab(y[:, sl], cos, sin).astype(qk_ref.dtype))

    for j in range(bqk_ref.shape[1] // GROUP):
        cols = slice(j * GROUP, (j + 1) * GROUP)
        bqk_ref[:, cols] = _dot(h, wbqk_ref[0, :, cols]).astype(bqk_ref.dtype)

    tt = vt_ref.shape[2]
    for j in range(vt_ref.shape[1] // GROUP):
        cols = slice(j * GROUP, (j + 1) * GROUP)
        t = _dot(h, wvt_ref[0, :, cols])
        for r in range(vt_ref.shape[0]):
            vt_ref[r, cols, :] = t[r * tt:(r + 1) * tt, :].T.astype(vt_ref.dtype)

    t = _dot(h, widx_ref[0])
    scale = sidx_ref[...]
    for s in range(IDX_Q // LANES):
        sl = slice(s * LANES, (s + 1) * LANES)
        idx_ref[:, sl] = _rope_slab(t[:, sl], cos, sin) * scale[:, sl]
    sl = slice(IDX_Q, IDX_COLS)
    tk = t[:, sl]
    lane = lax.broadcasted_iota(jnp.int32, tk.shape, 1)
    is_k = lane < IDX_DIM
    ms = jnp.sum(jnp.where(is_k, tk * tk, 0.0), axis=-1, keepdims=True) * (1.0 / IDX_DIM)
    yk = tk * lax.rsqrt(ms + EPS) * gidx_ref[0]
    idx_ref[:, sl] = jnp.where(is_k, _rope_slab(yk, cos, sin), tk) * scale[:, sl]


def _front_call(x2, g_attn, w_qk, w_bqk, w_vt, w_idx, gain_qk, gain_idx, scale_idx, cos_t, sin_t,
                gmat, layer, tt):
    n, d = x2.shape
    tm = ROW_TILE
    row = lambda cols: pl.BlockSpec((tm, cols), lambda i: (i, 0))
    full = lambda w: pl.BlockSpec((1,) + w.shape[1:], lambda i: (layer, 0, 0))
    fixed = lambda a: pl.BlockSpec(a.shape, lambda i: (0, 0))
    cqk, cbqk, cvt = w_qk.shape[2], w_bqk.shape[2], w_vt.shape[2]
    return pl.pallas_call(
        _front_kernel,
        grid=(n // tm,),
        in_specs=[row(d), full(g_attn), full(w_qk), full(w_bqk), full(w_vt), full(w_idx),
                  full(gain_qk), full(gain_idx), fixed(scale_idx), row(LANES), row(LANES),
                  fixed(gmat)],
        out_specs=[row(d), row(cqk), row(cbqk),
                   pl.BlockSpec((tm // tt, cvt, tt), lambda i: (i, 0, 0)), row(IDX_COLS)],
        out_shape=[jax.ShapeDtypeStruct((n, d), BF16), jax.ShapeDtypeStruct((n, cqk), BF16),
                   jax.ShapeDtypeStruct((n, cbqk), BF16),
                   jax.ShapeDtypeStruct((n // tt, cvt, tt), BF16),
                   jax.ShapeDtypeStruct((n, IDX_COLS), F32)],
        compiler_params=_params("parallel"),
    )(x2, g_attn, w_qk, w_bqk, w_vt, w_idx, gain_qk, gain_idx, scale_idx, cos_t, sin_t, gmat)


def _load_qt(q_ref, qz_ref):
    qt = q_ref[0].astype(F32).T
    rows = lax.broadcasted_iota(jnp.int32, (LANES, qt.shape[1]), 0)
    for s in range(qt.shape[0] // LANES):
        slab = qt[s * LANES:(s + 1) * LANES, :]
        qz_ref[2 * s] = jnp.where(rows < HEAD_DIM, slab, 0.0).astype(BF16)
        qz_ref[2 * s + 1] = jnp.where(rows >= HEAD_DIM, slab, 0.0).astype(BF16)


ONES_ROWS = 16


def _softmax_step(n, s, vt, m_ref, l_ref, acc_ref):
    m_old = m_ref[n]
    m_new = jnp.maximum(m_old, jnp.max(s, axis=0, keepdims=True))
    p = jnp.exp2(s - m_new).astype(BF16)
    alpha = jnp.exp2(m_old - m_new)
    f = vt.shape[0]
    pv = _dot(jnp.concatenate([vt, jnp.ones((ONES_ROWS, vt.shape[1]), BF16)], axis=0), p)
    l_ref[n] = alpha * l_ref[n] + pv[f:f + 1]
    acc_ref[n] = alpha * acc_ref[n] + pv[:f]
    m_ref[n] = m_new


def _tile_ids(t):
    key = lax.broadcasted_iota(jnp.int32, (KEY_TILE, t), 0)
    qry = lax.broadcasted_iota(jnp.int32, (KEY_TILE, t), 1)
    return key, qry


def _key_rows(u):
    return pl.ds(pl.multiple_of(u * KEY_TILE, KEY_TILE), KEY_TILE)


def _in_pairs(n, one, carry):
    carry = lax.fori_loop(0, n // 2, lambda p, c: one(2 * p + 1, one(2 * p, c)), carry)
    return lax.cond(n % 2 == 1, lambda c: one(n - 1, c), lambda c: c, carry)


def _in_fours(n, one, carry):
    def four(p, c):
        for j in range(4):
            c = one(4 * p + j, c)
        return c
    carry = lax.fori_loop(0, n // 4, four, carry)
    return lax.fori_loop((n // 4) * 4, n, one, carry)


def _interleave(*gens):
    gens = list(gens)
    while gens:
        for gen in list(gens):
            try:
                next(gen)
            except StopIteration:
                gens.remove(gen)


def _attn_ab_kernel(lam_ref, aq_ref, ak_ref, avt_ref, g_ref, bq_ref, bk_ref, bvt_ref, lmat_ref,
                    ya_ref, yb_ref, aqz_ref, m_ref, l_ref, aacc_ref, bqz_ref, r_ref, bacc_ref, *,
                    lam_init):
    t = aq_ref.shape[1]
    i = pl.program_id(1)
    lp = lam_ref[0]
    lam = (jnp.exp(jnp.sum(lp[0:1] * lp[1:2], axis=-1, keepdims=True))
           - jnp.exp(jnp.sum(lp[2:3] * lp[3:4], axis=-1, keepdims=True)) + lam_init)
    key, qry = _tile_ids(t)
    sub = t // KEY_TILE
    lmat = lmat_ref[...]
    _load_qt(aq_ref, aqz_ref)
    _load_qt(bq_ref, bqz_ref)
    m_ref[...] = jnp.full(m_ref.shape, NEG, F32)
    l_ref[...] = jnp.zeros(l_ref.shape, F32)
    aacc_ref[...] = jnp.zeros(aacc_ref.shape, F32)
    r_ref[...] = jnp.zeros(r_ref.shape, F32)
    bacc_ref[...] = jnp.zeros(bacc_ref.shape, F32)

    def a_group(g, diag):
        scores = []
        for r in range(sub):
            for n in range(2 * A_HEADS):
                kb = ak_ref[0, _key_rows(g * sub + r), (n // 2) * LANES:(n // 2 + 1) * LANES]
                scores.append(_dot(kb, aqz_ref[n]))
        yield
        for r in range(sub):
            for n in range(2 * A_HEADS):
                s = scores[r * 2 * A_HEADS + n]
                if diag:
                    s = jnp.where(((key + r * KEY_TILE) // CHUNK) <= (qry // CHUNK), s, NEG)
                vt = avt_ref[0, g * sub + r, (n // 2) * LANES:(n // 2 + 1) * LANES, :]
                _softmax_step(n, s, vt, m_ref, l_ref, aacc_ref)
                yield

    def b_group(g, diag):
        order = [(r, n) for r in reversed(range(sub)) for n in range(B_HEADS)]
        strict = {r: (key + r * KEY_TILE) < qry for r in range(sub)} if diag else None
        logits = {}
        for r, n in order:
            kb = bk_ref[0, _key_rows(g * sub + r), (n // 2) * LANES:(n // 2 + 1) * LANES]
            logits[r, n] = _dot(kb, bqz_ref[n])
        yield
        log_betas, laters, sums = {}, {}, {}
        for r, n in order:
            z = logits[r, n]
            neg_abs = pltpu.bitcast(pltpu.bitcast(z, jnp.int32) | jnp.int32(INT_MIN), F32)
            soft = jnp.log2(1.0 + jnp.exp2(neg_abs))
            log_beta = jnp.minimum(z, 0.0) - soft
            log_om = log_beta - z
            if diag:
                log_om = jnp.where(strict[r], log_om, 0.0)
            later = _dot(lmat, log_om.astype(BF16))
            laters[r, n] = later[:KEY_TILE]
            log_betas[r, n] = log_beta
            sums[r, n] = later[KEY_TILE:KEY_TILE + 1]
            yield
        for r, n in order:
            a = jnp.exp2(log_betas[r, n] + laters[r, n])
            if diag:
                a = jnp.where(strict[r], a, 0.0)
            vt = bvt_ref[0, g * sub + r, n * HEAD_DIM:(n + 1) * HEAD_DIM, :]
            visited = r_ref[n]
            bacc_ref[n] = bacc_ref[n] + _dot(vt, a.astype(BF16)) * jnp.exp2(visited)
            r_ref[n] = visited + sums[r, n]
            yield

    def body(n, carry):
        _interleave(a_group(n, False), b_group(i - 1 - n, False))
        return carry

    _interleave(a_group(i, True), b_group(i, True))
    _in_pairs(i, body, 0)

    gain = g_ref[0] * (1.0 - lam_init)
    for h in range(A_HEADS):
        o = aacc_ref[2 * h] / l_ref[2 * h] - lam * (aacc_ref[2 * h + 1] / l_ref[2 * h + 1])
        ms = jnp.mean(o * o, axis=0, keepdims=True)
        y = (o * lax.rsqrt(ms + EPS)).T * gain
        ya_ref[0, :, h * LANES:(h + 1) * LANES] = y.astype(ya_ref.dtype)
    yb_ref[0] = bacc_ref[...].reshape(B_HEADS * HEAD_DIM, t).T.astype(yb_ref.dtype)


def _attn_ab_call(qk, bqk, vt, lam_p, subln, lmat, layer, lam_init, t):
    b, s, _ = qk.shape
    nt, _, kt = vt.shape[1:]
    ns = 2 * A_HEADS
    qspec = pl.BlockSpec((1, t, GROUP), lambda bi, i: (bi, i, 0))
    kspec = pl.BlockSpec((1, s, GROUP), lambda bi, i: (bi, 0, 1))
    vspec = lambda f: pl.BlockSpec((1, nt, GROUP, kt), lambda bi, i: (bi, 0, f, 0))
    return pl.pallas_call(
        functools.partial(_attn_ab_kernel, lam_init=lam_init),
        grid=(b, s // t),
        in_specs=[pl.BlockSpec((1, 4, HEAD_DIM), lambda bi, i: (layer, 0, 0)),
                  qspec, kspec, vspec(0),
                  pl.BlockSpec((1, 1, LANES), lambda bi, i: (layer, 0, 0)),
                  qspec, kspec, vspec(1),
                  pl.BlockSpec(lmat.shape, lambda bi, i: (0, 0))],
        out_specs=[qspec, qspec],
        out_shape=[jax.ShapeDtypeStruct((b, s, GROUP), BF16)] * 2,
        scratch_shapes=[pltpu.VMEM((ns, LANES, t), BF16), pltpu.VMEM((ns, 1, t), F32),
                        pltpu.VMEM((ns, 1, t), F32), pltpu.VMEM((ns, LANES, t), F32),
                        pltpu.VMEM((B_HEADS, LANES, t), BF16), pltpu.VMEM((B_HEADS, 1, t), F32),
                        pltpu.VMEM((B_HEADS, HEAD_DIM, t), F32)],
        compiler_params=_params("parallel", "arbitrary"),
    )(lam_p, qk, qk, vt, subln, bqk, bqk, vt, lmat)


def _fold_rows(x):
    parts = [x[r * SUBLANES:(r + 1) * SUBLANES] for r in range(x.shape[0] // SUBLANES)]
    while len(parts) > 1:
        parts = [a + b for a, b in zip(parts[0::2], parts[1::2])]
    return parts[0]


def _dsa_mask_kernel(q_ref, kw_ref, wq_ref, lt_ref, o_ref, key_ref, qz_ref, *, topk):
    t = q_ref.shape[1]
    nj = key_ref.shape[0]
    i = pl.program_id(1)
    nch = (i + 1) * (t // KEY_TILE)
    qt = q_ref[0].T
    pad = jnp.zeros((LANES - IDX_DIM, t), F32)
    for h in range(IDX_HEADS):
        qz_ref[h] = jnp.concatenate([qt[h * IDX_DIM:(h + 1) * IDX_DIM], pad], axis=0).astype(BF16)
    wt = wq_ref[0].T
    wh = [wt[IDX_DIM + h:IDX_DIM + h + 1, :] for h in range(IDX_HEADS)]
    key0, qry0 = _tile_ids(t)
    sub = t // KEY_TILE

    def chunk_ok(r):
        return ((key0 + r * KEY_TILE) // CHUNK) <= (qry0 // CHUNK)

    def fill(g, diag):
        for r in range(sub):
            c = g * sub + r
            kb = kw_ref[0, _key_rows(c), :].astype(BF16)
            score = jnp.zeros((KEY_TILE, t), F32)
            for h in range(IDX_HEADS):
                score = score + wh[h] * jnp.maximum(_dot(kb, qz_ref[h]), 0.0)
            if diag:
                score = jnp.where(chunk_ok(r), score, -jnp.inf)
            bits = pltpu.bitcast(score, jnp.int32)
            okey = jnp.where(bits < 0, bits ^ jnp.int32(0x7FFFFFFF), bits)
            key_ref[c] = jnp.where(score == 0.0, 0, okey)

    def fill_one(g, carry):
        fill(g, False)
        return carry

    _in_fours(i, fill_one, 0)
    fill(i, True)

    def count_at_or_above(cand):
        def one(g, acc):
            for r in range(sub):
                acc = acc + _fold_rows(jnp.where(key_ref[g * sub + r] >= cand, 1.0, 0.0))
            return acc
        acc = _in_fours(i + 1, one, jnp.zeros((SUBLANES, t), F32))
        return jnp.sum(acc, axis=0, keepdims=True)

    def bit_body(n, carry):
        lo, above = carry
        cand = lo + jnp.left_shift(jnp.int32(1), 31 - n)
        cnt = count_at_or_above(cand)
        take = cnt >= topk
        return jnp.where(take, cand, lo), jnp.where(take, above, cnt)

    thr, above = lax.fori_loop(
        0, 32, bit_body, (jnp.full((1, t), INT_MIN, jnp.int32), jnp.zeros((1, t), F32)))
    need = topk - above
    lt = lt_ref[...]

    def emit(g, run, diag):
        for r in range(sub):
            c = g * sub + r
            okey = key_ref[c]
            eq = okey == thr
            eqf = jnp.where(eq, 1.0, 0.0)
            before = _dot(lt, eqf.astype(BF16)) + run
            sel = (okey > thr) | (eq & (before < need))
            if diag:
                sel = sel & chunk_ok(r)
            o_ref[0, 0, c] = jnp.where(sel, 0.0, NEG).astype(o_ref.dtype)
            run = run + jnp.sum(eqf, axis=0, keepdims=True)
        return run

    run = _in_fours(i, lambda g, run: emit(g, run, False), jnp.zeros((1, t), F32))
    emit(i, run, True)

    def blank(c, carry):
        o_ref[0, 0, c] = jnp.full((KEY_TILE, t), NEG, o_ref.dtype)
        return carry

    lax.fori_loop(nch, nj, blank, 0)


def _dsa_mask_call(idx, lt, topk, t):
    b, s, _ = idx.shape
    kt = lt.shape[0]
    assert t >= topk
    nt, nk = s // t, s // kt
    return pl.pallas_call(
        functools.partial(_dsa_mask_kernel, topk=topk),
        grid=(b, nt),
        in_specs=[pl.BlockSpec((1, t, IDX_Q), lambda bi, i: (bi, i, 0)),
                  pl.BlockSpec((1, s, LANES), lambda bi, i: (bi, 0, IDX_Q // LANES)),
                  pl.BlockSpec((1, t, LANES), lambda bi, i: (bi, i, IDX_Q // LANES)),
                  pl.BlockSpec((kt, kt), lambda bi, i: (0, 0))],
        out_specs=pl.BlockSpec((1, 1, nk, kt, t), lambda bi, i: (bi, i, 0, 0, 0)),
        out_shape=jax.ShapeDtypeStruct((b, nt, nk, kt, t), BF16),
        scratch_shapes=[pltpu.VMEM((nk, kt, t), jnp.int32), pltpu.VMEM((IDX_HEADS, LANES, t), BF16)],
        compiler_params=_params("parallel", "arbitrary"),
    )(idx, idx, idx, lt)


def _attn_c_kernel(q_ref, k_ref, vt_ref, b_ref, o_ref, qz_ref, m_ref, l_ref, acc_ref):
    t = q_ref.shape[1]
    i = pl.program_id(1)
    _load_qt(q_ref, qz_ref)
    m_ref[...] = jnp.full(m_ref.shape, NEG, F32)
    l_ref[...] = jnp.zeros(l_ref.shape, F32)
    acc_ref[...] = jnp.zeros(acc_ref.shape, F32)

    key, qry = _tile_ids(KEY_TILE)
    ident = jnp.where(key == qry, 1.0, 0.0).astype(BF16)

    sub = t // KEY_TILE

    def body(g, carry):
        scores = []
        for r in range(sub):
            u = g * sub + r
            bias = b_ref[0, 0, u]
            for n in range(C_HEADS):
                kb = k_ref[0, _key_rows(u), (n // 2) * LANES:(n // 2 + 1) * LANES]
                lhs = jnp.concatenate([kb, ident], axis=1)
                rhs = jnp.concatenate([qz_ref[n], bias], axis=0)
                scores.append(_dot(lhs, rhs))
        for r in range(sub):
            for n in range(C_HEADS):
                vt = vt_ref[0, g * sub + r, n * HEAD_DIM:(n + 1) * HEAD_DIM, :]
                _softmax_step(n, scores[r * C_HEADS + n], vt, m_ref, l_ref, acc_ref)
        return carry

    _in_fours(i + 1, body, 0)
    o = acc_ref[...] / l_ref[...]
    o_ref[0] = o.reshape(C_HEADS * HEAD_DIM, t).T.astype(o_ref.dtype)


def _attn_c_call(qk, vt, bias):
    b, s, _ = qk.shape
    nk, _, kt = vt.shape[1:]
    nt, t = bias.shape[1], bias.shape[4]
    return pl.pallas_call(
        _attn_c_kernel,
        grid=(b, nt),
        in_specs=[pl.BlockSpec((1, t, GROUP), lambda bi, i: (bi, i, 2)),
                  pl.BlockSpec((1, s, GROUP), lambda bi, i: (bi, 0, 3)),
                  pl.BlockSpec((1, nk, GROUP, kt), lambda bi, i: (bi, 0, 2, 0)),
                  pl.BlockSpec((1, 1, nk, kt, t), lambda bi, i: (bi, i, 0, 0, 0))],
        out_specs=pl.BlockSpec((1, t, GROUP), lambda bi, i: (bi, i, 0)),
        out_shape=jax.ShapeDtypeStruct((b, s, GROUP), BF16),
        scratch_shapes=[pltpu.VMEM((C_HEADS, LANES, t), BF16), pltpu.VMEM((C_HEADS, 1, t), F32),
                        pltpu.VMEM((C_HEADS, 1, t), F32), pltpu.VMEM((C_HEADS, HEAD_DIM, t), F32)],
        compiler_params=_params("parallel", "arbitrary"),
    )(qk, qk, vt, bias)


def _tail_kernel(x_ref, h_ref, ya_ref, yb_ref, yc_ref, p_ref, wg_ref, wa_ref, wb_ref, wc_ref, wo_ref,
                 gm_ref, wu_ref, wd_ref, gp_ref, wpg_ref, wpp_ref, o_ref):
    d = x_ref.shape[1]
    h = h_ref[...]
    merged = jnp.zeros(x_ref.shape, F32)
    for n, (y_ref, w_ref) in enumerate(((ya_ref, wa_ref), (yb_ref, wb_ref), (yc_ref, wc_ref))):
        gate = jax.nn.sigmoid(_dot(h, wg_ref[0, :, n * d:(n + 1) * d]))
        merged = merged + gate * _dot(y_ref[...], w_ref[0])
    x = x_ref[...] + _dot(merged.astype(BF16), wo_ref[0])
    ms = jnp.mean(x * x, axis=-1, keepdims=True)
    hm = (x * lax.rsqrt(ms + EPS) * gm_ref[0]).astype(BF16)
    for f in range(wu_ref.shape[2] // FF_TILE):
        u = jnp.maximum(_dot(hm, wu_ref[0, :, f * FF_TILE:(f + 1) * FF_TILE]), 0.0)
        x = x + _dot((u * u).astype(BF16), wd_ref[0, f * FF_TILE:(f + 1) * FF_TILE, :])
    ms = jnp.mean(x * x, axis=-1, keepdims=True)
    hp = (x * lax.rsqrt(ms + EPS) * gp_ref[0]).astype(BF16)
    gate = jax.nn.sigmoid(_dot(hp, wpg_ref[0]))
    o_ref[...] = x + gate * _dot(p_ref[0].astype(BF16), wpp_ref[0])


def _tail_call(x2, h, ya, yb, yc, p3, wg, wa, wb, wc, wo, g_mlp, w_up, w_down, g_ple, w_pg, w_pp,
               layer):
    n, d = x2.shape
    pd = p3.shape[2]
    tm = TAIL_TILE
    row = lambda cols: pl.BlockSpec((tm, cols), lambda i: (i, 0))
    full = lambda w: pl.BlockSpec((1,) + w.shape[1:], lambda i: (layer, 0, 0),
                                  pipeline_mode=pl.Buffered(1))
    return pl.pallas_call(
        _tail_kernel,
        grid=(n // tm,),
        in_specs=[row(d), row(d), row(ya.shape[1]), row(yb.shape[1]), row(yc.shape[1]),
                  pl.BlockSpec((1, tm, pd), lambda i: (layer, i, 0)),
                  full(wg), full(wa), full(wb), full(wc), full(wo),
                  full(g_mlp), full(w_up), full(w_down), full(g_ple), full(w_pg), full(w_pp)],
        out_specs=row(d),
        out_shape=jax.ShapeDtypeStruct((n, d), F32),
        compiler_params=_params("parallel"),
    )(x2, h, ya, yb, yc, p3, wg, wa, wb, wc, wo, g_mlp, w_up, w_down, g_ple, w_pg, w_pp)


def _tile_gain(g, reps, scale=1.0):
    return jnp.tile(g.astype(F32), (1, reps)) * scale


def kernel(x, p, positions, attn_norm, w_in, a_q_norm, a_k_norm, a_lambda, a_subln,
           c_q_norm, c_k_norm, idx_k_norm, w_br_a, w_br_b, w_br_c, w_out,
           mlp_norm, w_up, w_down, ple_norm, w_ple_gate, w_ple_proj):
    b, s, d = x.shape
    depth = w_in.shape[0]
    n = b * s
    topk = min(TOPK_MAX, s // 4)
    t = min(ATTN_TILE, s)
    scale = HEAD_DIM ** -0.5
    qscale = scale * LOG2E

    inv = ROPE_THETA ** (-jnp.arange(0, HEAD_DIM, 2, dtype=F32) / HEAD_DIM)
    ang = positions.astype(F32)[..., None] * inv
    cos, sin = jnp.cos(ang), jnp.sin(ang)
    cos_t = jnp.concatenate([cos, cos, cos, cos], axis=-1).reshape(n, LANES)
    sin_t = jnp.concatenate([-sin, sin, -sin, sin], axis=-1).reshape(n, LANES)

    qa, ka, va = 0, 512, 1024
    qb, kb, vb = 1536, 2048, 2560
    qc, kc, vc = 3072, 3584, 4096
    qi = 4608
    gl = 4932
    col = lambda a, width=GROUP: w_in[:, :, a:a + width]
    w_qk = jnp.concatenate([col(qa), col(ka), col(qc), col(kc)], axis=-1).astype(BF16)
    w_bqk = jnp.concatenate([col(qb) * qscale, col(kb)], axis=-1).astype(BF16)
    w_vt = jnp.concatenate([col(va), col(vb), col(vc)], axis=-1).astype(BF16)
    w_idx = jnp.pad(col(qi, gl - qi), ((0, 0), (0, 0), (0, IDX_COLS - (gl - qi)))).astype(BF16)
    w_gl = col(gl, N_BRANCHES * d).astype(BF16)
    heads = GROUP // HEAD_DIM
    gain_qk = jnp.concatenate([_tile_gain(a_q_norm, heads, qscale), _tile_gain(a_k_norm, heads),
                               _tile_gain(c_q_norm, heads, qscale), _tile_gain(c_k_norm, heads)],
                              axis=-1)[:, None, :]
    gain_idx = jnp.pad(idx_k_norm.astype(F32), ((0, 0), (0, LANES - IDX_DIM)))[:, None, :]
    scale_idx = jnp.concatenate([jnp.full((IDX_Q,), IDX_DIM ** -0.5, F32), jnp.ones((IDX_DIM,), F32),
                                 jnp.full((IDX_HEADS,), IDX_HEADS ** -0.5, F32),
                                 jnp.zeros((LANES - IDX_DIM - IDX_HEADS,), F32)])[None, :]
    gid = jnp.arange(GROUP) // HEAD_DIM
    gmat = (gid[:, None] == gid[None, :]).astype(BF16)
    kt = min(KEY_TILE, s)
    ar = jnp.arange(kt)
    later_mat = jnp.concatenate([(ar[None, :] > ar[:, None]).astype(BF16),
                                 jnp.ones((ONES_ROWS, kt), BF16)], axis=0)
    before_mat = (ar[None, :] < ar[:, None]).astype(BF16)

    wa, wb, wc, wo = (w.astype(BF16) for w in (w_br_a, w_br_b, w_br_c, w_out))
    wu, wd, wpg, wpp = (w.astype(BF16) for w in (w_up, w_down, w_ple_gate, w_ple_proj))
    g_attn, g_mlp, g_ple = (g.astype(F32)[:, None, :] for g in (attn_norm, mlp_norm, ple_norm))
    subln = a_subln.astype(F32)[:, None, :]
    lam_p = a_lambda.astype(F32)
    p3 = p.reshape(depth, n, p.shape[-1])

    x2 = x.reshape(n, d)
    for i in range(depth):
        lam_init = 0.8 - 0.6 * math.exp(-0.3 * i)
        h, qk, bqk, vt, idx = _front_call(x2, g_attn, w_qk, w_bqk, w_vt, w_idx, gain_qk, gain_idx,
                                          scale_idx, cos_t, sin_t, gmat, i, kt)
        qk, bqk, idx = (a.reshape(b, s, -1) for a in (qk, bqk, idx))
        vt = vt.reshape(b, s // kt, -1, kt)
        ya, yb = _attn_ab_call(qk, bqk, vt, lam_p, subln, later_mat, i, lam_init, t)
        bias = _dsa_mask_call(idx, before_mat, topk, t)
        yc = _attn_c_call(qk, vt, bias)
        x2 = _tail_call(x2, h, ya.reshape(n, -1), yb.reshape(n, -1), yc.reshape(n, -1), p3,
                        w_gl, wa, wb, wc, wo, g_mlp, wu, wd, g_ple, wpg, wpp, i)
    return x2.reshape(b, s, d)
```

```python
import functools
import math

import jax
import jax.numpy as jnp
from jax import lax
from jax.experimental import pallas as pl
from jax.experimental.pallas import tpu as pltpu

F32 = jnp.float32
BF16 = jnp.bfloat16

HEAD_DIM = 64
CHUNK = 64
A_HEADS = 4
B_HEADS = 8
C_HEADS = 8
IDX_HEADS = 4
IDX_DIM = 64
TOPK_MAX = 256
N_BRANCHES = 3
ROPE_THETA = 10000.0
EPS = 1e-6

LANES = 128
SUBLANES = 8
HALF_HEAD = HEAD_DIM // 2
NEG = -1e30
INT_MIN = -(2 ** 31)
LOG2E = 1.4426950408889634
V7X_VMEM_BYTES = 64 * 1024 * 1024
VMEM_LIMIT = V7X_VMEM_BYTES * 7 // 8

ROW_TILE = 512
ATTN_TILE = 256
KEY_TILE = 128
FF_TILE = 1024
TAIL_TILE = 512
GROUP = 512


def _params(*sem):
    return pltpu.CompilerParams(dimension_semantics=sem, vmem_limit_bytes=VMEM_LIMIT)


def _dot(a, b):
    return jnp.dot(a, b, preferred_element_type=F32)


def _swap_halves(y):
    lane = lax.broadcasted_iota(jnp.int32, y.shape, 1)
    return jnp.where((lane & HALF_HEAD) != 0, pltpu.roll(y, HALF_HEAD, 1),
                     pltpu.roll(y, LANES - HALF_HEAD, 1))


def _rope_slab(y, cos, sin):
    return y * cos + _swap_halves(y) * sin


IDX_Q = IDX_HEADS * IDX_DIM
IDX_COLS = IDX_Q + LANES


def _front_kernel(x_ref, g_ref, wqk_ref, wbqk_ref, wvt_ref, widx_ref, gqk_ref, gidx_ref, sidx_ref,
                  cos_ref, sin_ref, gmat_ref, h_ref, qk_ref, bqk_ref, vt_ref, idx_ref):
    x = x_ref[...]
    ms = jnp.mean(x * x, axis=-1, keepdims=True)
    h = (x * lax.rsqrt(ms + EPS) * g_ref[0]).astype(BF16)
    h_ref[...] = h
    cos = cos_ref[...]
    sin = sin_ref[...]

    gmat = gmat_ref[...]
    for j in range(qk_ref.shape[1] // GROUP):
        cols = slice(j * GROUP, (j + 1) * GROUP)
        t = _dot(h, wqk_ref[0, :, cols])
        ms = _dot((t * t).astype(BF16), gmat) * (1.0 / HEAD_DIM)
        y = t * lax.rsqrt(ms + EPS) * gqk_ref[0, :, cols]
        for s in range(GROUP // LANES):
            sl = slice(s * LANES, (s + 1) * LANES)
            qk_ref[:, j * GROUP + s * LANES:j * GROUP + (s + 1) * LANES] = (
                _rope_slab(y[:, sl], cos, sin).astype(qk_ref.dtype))

    for j in range(bqk_ref.shape[1] // GROUP):
        cols = slice(j * GROUP, (j + 1) * GROUP)
        bqk_ref[:, cols] = _dot(h, wbqk_ref[0, :, cols]).astype(bqk_ref.dtype)

    tt = vt_ref.shape[2]
    for j in range(vt_ref.shape[1] // GROUP):
        cols = slice(j * GROUP, (j + 1) * GROUP)
        t = _dot(h, wvt_ref[0, :, cols])
        for r in range(vt_ref.shape[0]):
            vt_ref[r, cols, :] = t[r * tt:(r + 1) * tt, :].T.astype(vt_ref.dtype)

    t = _dot(h, widx_ref[0])
    scale = sidx_ref[...]
    for s in range(IDX_Q // LANES):
        sl = slice(s * LANES, (s + 1) * LANES)
        idx_ref[:, sl] = _rope_slab(t[:, sl], cos, sin) * scale[:, sl]
    sl = slice(IDX_Q, IDX_COLS)
    tk = t[:, sl]
    lane = lax.broadcasted_iota(jnp.int32, tk.shape, 1)
    is_k = lane < IDX_DIM
    ms = jnp.sum(jnp.where(is_k, tk * tk, 0.0), axis=-1, keepdims=True) * (1.0 / IDX_DIM)
    yk = tk * lax.rsqrt(ms + EPS) * gidx_ref[0]
    idx_ref[:, sl] = jnp.where(is_k, _rope_slab(yk, cos, sin), tk) * scale[:, sl]


def _front_call(x2, g_attn, w_qk, w_bqk, w_vt, w_idx, gain_qk, gain_idx, scale_idx, cos_t, sin_t,
                gmat, layer, tt):
    n, d = x2.shape
    tm = ROW_TILE
    row = lambda cols: pl.BlockSpec((tm, cols), lambda i: (i, 0))
    full = lambda w: pl.BlockSpec((1,) + w.shape[1:], lambda i: (layer, 0, 0))
    fixed = lambda a: pl.BlockSpec(a.shape, lambda i: (0, 0))
    cqk, cbqk, cvt = w_qk.shape[2], w_bqk.shape[2], w_vt.shape[2]
    return pl.pallas_call(
        _front_kernel,
        grid=(n // tm,),
        in_specs=[row(d), full(g_attn), full(w_qk), full(w_bqk), full(w_vt), full(w_idx),
                  full(gain_qk), full(gain_idx), fixed(scale_idx), row(LANES), row(LANES),
                  fixed(gmat)],
        out_specs=[row(d), row(cqk), row(cbqk),
                   pl.BlockSpec((tm // tt, cvt, tt), lambda i: (i, 0, 0)), row(IDX_COLS)],
        out_shape=[jax.ShapeDtypeStruct((n, d), BF16), jax.ShapeDtypeStruct((n, cqk), BF16),
                   jax.ShapeDtypeStruct((n, cbqk), BF16),
                   jax.ShapeDtypeStruct((n // tt, cvt, tt), BF16),
                   jax.ShapeDtypeStruct((n, IDX_COLS), F32)],
        compiler_params=_params("parallel"),
    )(x2, g_attn, w_qk, w_bqk, w_vt, w_idx, gain_qk, gain_idx, scale_idx, cos_t, sin_t, gmat)


def _load_qt(q_ref, qz_ref):
    qt = q_ref[0].astype(F32).T
    rows = lax.broadcasted_iota(jnp.int32, (LANES, qt.shape[1]), 0)
    for s in range(qt.shape[0] // LANES):
        slab = qt[s * LANES:(s + 1) * LANES, :]
        qz_ref[2 * s] = jnp.where(rows < HEAD_DIM, slab, 0.0).astype(BF16)
        qz_ref[2 * s + 1] = jnp.where(rows >= HEAD_DIM, slab, 0.0).astype(BF16)


ONES_ROWS = 16


def _softmax_step(n, s, vt, m_ref, l_ref, acc_ref):
    m_old = m_ref[n]
    m_new = jnp.maximum(m_old, jnp.max(s, axis=0, keepdims=True))
    p = jnp.exp2(s - m_new).astype(BF16)
    alpha = jnp.exp2(m_old - m_new)
    f = vt.shape[0]
    pv = _dot(jnp.concatenate([vt, jnp.ones((ONES_ROWS, vt.shape[1]), BF16)], axis=0), p)
    l_ref[n] = alpha * l_ref[n] + pv[f:f + 1]
    acc_ref[n] = alpha * acc_ref[n] + pv[:f]
    m_ref[n] = m_new


def _tile_ids(t):
    key = lax.broadcasted_iota(jnp.int32, (KEY_TILE, t), 0)
    qry = lax.broadcasted_iota(jnp.int32, (KEY_TILE, t), 1)
    return key, qry


def _key_rows(u):
    return pl.ds(pl.multiple_of(u * KEY_TILE, KEY_TILE), KEY_TILE)


def _in_pairs(n, one, carry):
    carry = lax.fori_loop(0, n // 2, lambda p, c: one(2 * p + 1, one(2 * p, c)), carry)
    return lax.cond(n % 2 == 1, lambda c: one(n - 1, c), lambda c: c, carry)


def _in_fours(n, one, carry):
    def four(p, c):
        for j in range(4):
            c = one(4 * p + j, c)
        return c
    carry = lax.fori_loop(0, n // 4, four, carry)
    base = (n // 4) * 4
    pair = (n - base) // 2
    carry = lax.fori_loop(0, pair, lambda p, c: one(base + 1, one(base, c)), carry)
    return lax.fori_loop(base + 2 * pair, n, one, carry)


def _interleave(*gens):
    gens = list(gens)
    while gens:
        for gen in list(gens):
            try:
                next(gen)
            except StopIteration:
                gens.remove(gen)


def _attn_ab_kernel(lam_ref, aq_ref, ak_ref, avt_ref, g_ref, bq_ref, bk_ref, bvt_ref, lmat_ref,
                    ya_ref, yb_ref, aqz_ref, m_ref, l_ref, aacc_ref, bqz_ref, r_ref, bacc_ref, *,
                    lam_init):
    t = aq_ref.shape[1]
    i = pl.program_id(1)
    lp = lam_ref[0]
    lam = (jnp.exp(jnp.sum(lp[0:1] * lp[1:2], axis=-1, keepdims=True))
           - jnp.exp(jnp.sum(lp[2:3] * lp[3:4], axis=-1, keepdims=True)) + lam_init)
    key, qry = _tile_ids(t)
    sub = t // KEY_TILE
    lmat = lmat_ref[...]
    _load_qt(aq_ref, aqz_ref)
    _load_qt(bq_ref, bqz_ref)
    m_ref[...] = jnp.full(m_ref.shape, NEG, F32)
    l_ref[...] = jnp.zeros(l_ref.shape, F32)
    aacc_ref[...] = jnp.zeros(aacc_ref.shape, F32)
    r_ref[...] = jnp.zeros(r_ref.shape, F32)
    bacc_ref[...] = jnp.zeros(bacc_ref.shape, F32)

    def a_group(g, diag):
        scores = []
        for r in range(sub):
            for n in range(2 * A_HEADS):
                kb = ak_ref[0, _key_rows(g * sub + r), (n // 2) * LANES:(n // 2 + 1) * LANES]
                scores.append(_dot(kb, aqz_ref[n]))
        yield
        for r in range(sub):
            for n in range(2 * A_HEADS):
                s = scores[r * 2 * A_HEADS + n]
                if diag:
                    s = jnp.where(((key + r * KEY_TILE) // CHUNK) <= (qry // CHUNK), s, NEG)
                vt = avt_ref[0, g * sub + r, (n // 2) * LANES:(n // 2 + 1) * LANES, :]
                _softmax_step(n, s, vt, m_ref, l_ref, aacc_ref)
                yield

    def b_group(g, diag):
        order = [(r, n) for r in reversed(range(sub)) for n in range(B_HEADS)]
        strict = {r: (key + r * KEY_TILE) < qry for r in range(sub)} if diag else None
        logits = {}
        for r, n in order:
            kb = bk_ref[0, _key_rows(g * sub + r), (n // 2) * LANES:(n // 2 + 1) * LANES]
            logits[r, n] = _dot(kb, bqz_ref[n])
        yield
        log_betas, laters, sums = {}, {}, {}
        for r, n in order:
            z = logits[r, n]
            neg_abs = pltpu.bitcast(pltpu.bitcast(z, jnp.int32) | jnp.int32(INT_MIN), F32)
            soft = jnp.log2(1.0 + jnp.exp2(neg_abs))
            log_beta = jnp.minimum(z, 0.0) - soft
            log_om = log_beta - z
            if diag:
                log_om = jnp.where(strict[r], log_om, 0.0)
            later = _dot(lmat, log_om.astype(BF16))
            laters[r, n] = later[:KEY_TILE]
            log_betas[r, n] = log_beta
            sums[r, n] = later[KEY_TILE:KEY_TILE + 1]
            yield
        for r, n in order:
            a = jnp.exp2(log_betas[r, n] + laters[r, n])
            if diag:
                a = jnp.where(strict[r], a, 0.0)
            vt = bvt_ref[0, g * sub + r, n * HEAD_DIM:(n + 1) * HEAD_DIM, :]
            visited = r_ref[n]
            bacc_ref[n] = bacc_ref[n] + _dot(vt, a.astype(BF16)) * jnp.exp2(visited)
            r_ref[n] = visited + sums[r, n]
            yield

    def body(n, carry):
        _interleave(a_group(n, False), b_group(i - 1 - n, False))
        return carry

    _interleave(a_group(i, True), b_group(i, True))
    _in_pairs(i, body, 0)

    gain = g_ref[0] * (1.0 - lam_init)
    for h in range(A_HEADS):
        o = aacc_ref[2 * h] / l_ref[2 * h] - lam * (aacc_ref[2 * h + 1] / l_ref[2 * h + 1])
        ms = jnp.mean(o * o, axis=0, keepdims=True)
        y = (o * lax.rsqrt(ms + EPS)).T * gain
        ya_ref[0, :, h * LANES:(h + 1) * LANES] = y.astype(ya_ref.dtype)
    yb_ref[0] = bacc_ref[...].reshape(B_HEADS * HEAD_DIM, t).T.astype(yb_ref.dtype)


def _attn_ab_call(qk, bqk, vt, lam_p, subln, lmat, layer, lam_init, t):
    b, s, _ = qk.shape
    nt, _, kt = vt.shape[1:]
    ns = 2 * A_HEADS
    qspec = pl.BlockSpec((1, t, GROUP), lambda bi, i: (bi, i, 0))
    kspec = pl.BlockSpec((1, s, GROUP), lambda bi, i: (bi, 0, 1))
    vspec = lambda f: pl.BlockSpec((1, nt, GROUP, kt), lambda bi, i: (bi, 0, f, 0))
    return pl.pallas_call(
        functools.partial(_attn_ab_kernel, lam_init=lam_init),
        grid=(b, s // t),
        in_specs=[pl.BlockSpec((1, 4, HEAD_DIM), lambda bi, i: (layer, 0, 0)),
                  qspec, kspec, vspec(0),
                  pl.BlockSpec((1, 1, LANES), lambda bi, i: (layer, 0, 0)),
                  qspec, kspec, vspec(1),
                  pl.BlockSpec(lmat.shape, lambda bi, i: (0, 0))],
        out_specs=[qspec, qspec],
        out_shape=[jax.ShapeDtypeStruct((b, s, GROUP), BF16)] * 2,
        scratch_shapes=[pltpu.VMEM((ns, LANES, t), BF16), pltpu.VMEM((ns, 1, t), F32),
                        pltpu.VMEM((ns, 1, t), F32), pltpu.VMEM((ns, LANES, t), F32),
                        pltpu.VMEM((B_HEADS, LANES, t), BF16), pltpu.VMEM((B_HEADS, 1, t), F32),
                        pltpu.VMEM((B_HEADS, HEAD_DIM, t), F32)],
        compiler_params=_params("parallel", "arbitrary"),
    )(lam_p, qk, qk, vt, subln, bqk, bqk, vt, lmat)


def _fold_rows(x):
    parts = [x[r * SUBLANES:(r + 1) * SUBLANES] for r in range(x.shape[0] // SUBLANES)]
    while len(parts) > 1:
        parts = [a + b for a, b in zip(parts[0::2], parts[1::2])]
    return parts[0]


def _dsa_mask_kernel(q_ref, kw_ref, wq_ref, lt_ref, o_ref, key_ref, qz_ref, *, topk):
    t = q_ref.shape[1]
    nj = key_ref.shape[0]
    i = pl.program_id(1)
    nch = (i + 1) * (t // KEY_TILE)
    qt = q_ref[0].T
    pad = jnp.zeros((LANES - IDX_DIM, t), F32)
    for h in range(IDX_HEADS):
        qz_ref[h] = jnp.concatenate([qt[h * IDX_DIM:(h + 1) * IDX_DIM], pad], axis=0).astype(BF16)
    wt = wq_ref[0].T
    wh = [wt[IDX_DIM + h:IDX_DIM + h + 1, :] for h in range(IDX_HEADS)]
    key0, qry0 = _tile_ids(t)
    sub = t // KEY_TILE

    def chunk_ok(r):
        return ((key0 + r * KEY_TILE) // CHUNK) <= (qry0 // CHUNK)

    def fill(g, diag):
        for r in range(sub):
            c = g * sub + r
            kb = kw_ref[0, _key_rows(c), :].astype(BF16)
            score = jnp.zeros((KEY_TILE, t), F32)
            for h in range(IDX_HEADS):
                score = score + wh[h] * jnp.maximum(_dot(kb, qz_ref[h]), 0.0)
            if diag:
                score = jnp.where(chunk_ok(r), score, -jnp.inf)
            bits = pltpu.bitcast(score, jnp.int32)
            okey = jnp.where(bits < 0, bits ^ jnp.int32(0x7FFFFFFF), bits)
            key_ref[c] = jnp.where(score == 0.0, 0, okey)

    def fill_one(g, carry):
        fill(g, False)
        return carry

    _in_fours(i, fill_one, 0)
    fill(i, True)

    def count_at_or_above(cand):
        def one(g, acc):
            for r in range(sub):
                acc = acc + _fold_rows(jnp.where(key_ref[g * sub + r] >= cand, 1.0, 0.0))
            return acc
        acc = _in_fours(i + 1, one, jnp.zeros((SUBLANES, t), F32))
        return jnp.sum(acc, axis=0, keepdims=True)

    def bit_body(n, carry):
        lo, above = carry
        cand = lo + jnp.left_shift(jnp.int32(1), 31 - n)
        cnt = count_at_or_above(cand)
        take = cnt >= topk
        return jnp.where(take, cand, lo), jnp.where(take, above, cnt)

    thr, above = lax.fori_loop(
        0, 32, bit_body, (jnp.full((1, t), INT_MIN, jnp.int32), jnp.zeros((1, t), F32)))
    need = topk - above
    lt = lt_ref[...]

    def emit(g, run, diag):
        for r in range(sub):
            c = g * sub + r
            okey = key_ref[c]
            eq = okey == thr
            eqf = jnp.where(eq, 1.0, 0.0)
            before = _dot(lt, eqf.astype(BF16)) + run
            sel = (okey > thr) | (eq & (before < need))
            if diag:
                sel = sel & chunk_ok(r)
            o_ref[0, 0, c] = jnp.where(sel, 0.0, NEG).astype(o_ref.dtype)
            run = run + jnp.sum(eqf, axis=0, keepdims=True)
        return run

    run = _in_fours(i, lambda g, run: emit(g, run, False), jnp.zeros((1, t), F32))
    emit(i, run, True)

    def blank(c, carry):
        o_ref[0, 0, c] = jnp.full((KEY_TILE, t), NEG, o_ref.dtype)
        return carry

    lax.fori_loop(nch, nj, blank, 0)


def _dsa_mask_call(idx, lt, topk, t):
    b, s, _ = idx.shape
    kt = lt.shape[0]
    assert t >= topk
    nt, nk = s // t, s // kt
    return pl.pallas_call(
        functools.partial(_dsa_mask_kernel, topk=topk),
        grid=(b, nt),
        in_specs=[pl.BlockSpec((1, t, IDX_Q), lambda bi, i: (bi, i, 0)),
                  pl.BlockSpec((1, s, LANES), lambda bi, i: (bi, 0, IDX_Q // LANES)),
                  pl.BlockSpec((1, t, LANES), lambda bi, i: (bi, i, IDX_Q // LANES)),
                  pl.BlockSpec((kt, kt), lambda bi, i: (0, 0))],
        out_specs=pl.BlockSpec((1, 1, nk, kt, t), lambda bi, i: (bi, i, 0, 0, 0)),
        out_shape=jax.ShapeDtypeStruct((b, nt, nk, kt, t), BF16),
        scratch_shapes=[pltpu.VMEM((nk, kt, t), jnp.int32), pltpu.VMEM((IDX_HEADS, LANES, t), BF16)],
        compiler_params=_params("parallel", "arbitrary"),
    )(idx, idx, idx, lt)


def _attn_c_kernel(q_ref, k_ref, vt_ref, b_ref, o_ref, qz_ref, m_ref, l_ref, acc_ref):
    t = q_ref.shape[1]
    i = pl.program_id(1)
    _load_qt(q_ref, qz_ref)
    m_ref[...] = jnp.full(m_ref.shape, NEG, F32)
    l_ref[...] = jnp.zeros(l_ref.shape, F32)
    acc_ref[...] = jnp.zeros(acc_ref.shape, F32)

    key, qry = _tile_ids(KEY_TILE)
    ident = jnp.where(key == qry, 1.0, 0.0).astype(BF16)

    sub = t // KEY_TILE

    def body(g, carry):
        scores = []
        for r in range(sub):
            u = g * sub + r
            bias = b_ref[0, 0, u]
            for n in range(C_HEADS):
                kb = k_ref[0, _key_rows(u), (n // 2) * LANES:(n // 2 + 1) * LANES]
                lhs = jnp.concatenate([kb, ident], axis=1)
                rhs = jnp.concatenate([qz_ref[n], bias], axis=0)
                scores.append(_dot(lhs, rhs))
        for r in range(sub):
            for n in range(C_HEADS):
                vt = vt_ref[0, g * sub + r, n * HEAD_DIM:(n + 1) * HEAD_DIM, :]
                _softmax_step(n, scores[r * C_HEADS + n], vt, m_ref, l_ref, acc_ref)
        return carry

    _in_fours(i + 1, body, 0)
    o = acc_ref[...] / l_ref[...]
    o_ref[0] = o.reshape(C_HEADS * HEAD_DIM, t).T.astype(o_ref.dtype)


def _attn_c_call(qk, vt, bias):
    b, s, _ = qk.shape
    nk, _, kt = vt.shape[1:]
    nt, t = bias.shape[1], bias.shape[4]
    return pl.pallas_call(
        _attn_c_kernel,
        grid=(b, nt),
        in_specs=[pl.BlockSpec((1, t, GROUP), lambda bi, i: (bi, i, 2)),
                  pl.BlockSpec((1, s, GROUP), lambda bi, i: (bi, 0, 3)),
                  pl.BlockSpec((1, nk, GROUP, kt), lambda bi, i: (bi, 0, 2, 0)),
                  pl.BlockSpec((1, 1, nk, kt, t), lambda bi, i: (bi, i, 0, 0, 0))],
        out_specs=pl.BlockSpec((1, t, GROUP), lambda bi, i: (bi, i, 0)),
        out_shape=jax.ShapeDtypeStruct((b, s, GROUP), BF16),
        scratch_shapes=[pltpu.VMEM((C_HEADS, LANES, t), BF16), pltpu.VMEM((C_HEADS, 1, t), F32),
                        pltpu.VMEM((C_HEADS, 1, t), F32), pltpu.VMEM((C_HEADS, HEAD_DIM, t), F32)],
        compiler_params=_params("parallel", "arbitrary"),
    )(qk, qk, vt, bias)


def _tail_kernel(x_ref, h_ref, ya_ref, yb_ref, yc_ref, p_ref, wg_ref, wa_ref, wb_ref, wc_ref, wo_ref,
                 gm_ref, wu_ref, wd_ref, gp_ref, wpg_ref, wpp_ref, o_ref):
    d = x_ref.shape[1]
    h = h_ref[...]
    merged = jnp.zeros(x_ref.shape, F32)
    for n, (y_ref, w_ref) in enumerate(((ya_ref, wa_ref), (yb_ref, wb_ref), (yc_ref, wc_ref))):
        gate = jax.nn.sigmoid(_dot(h, wg_ref[0, :, n * d:(n + 1) * d]))
        merged = merged + gate * _dot(y_ref[...], w_ref[0])
    x = x_ref[...] + _dot(merged.astype(BF16), wo_ref[0])
    ms = jnp.mean(x * x, axis=-1, keepdims=True)
    hm = (x * lax.rsqrt(ms + EPS) * gm_ref[0]).astype(BF16)
    for f in range(wu_ref.shape[2] // FF_TILE):
        u = jnp.maximum(_dot(hm, wu_ref[0, :, f * FF_TILE:(f + 1) * FF_TILE]), 0.0)
        x = x + _dot((u * u).astype(BF16), wd_ref[0, f * FF_TILE:(f + 1) * FF_TILE, :])
    ms = jnp.mean(x * x, axis=-1, keepdims=True)
    hp = (x * lax.rsqrt(ms + EPS) * gp_ref[0]).astype(BF16)
    gate = jax.nn.sigmoid(_dot(hp, wpg_ref[0]))
    o_ref[...] = x + gate * _dot(p_ref[0].astype(BF16), wpp_ref[0])


def _tail_call(x2, h, ya, yb, yc, p3, wg, wa, wb, wc, wo, g_mlp, w_up, w_down, g_ple, w_pg, w_pp,
               layer):
    n, d = x2.shape
    pd = p3.shape[2]
    tm = TAIL_TILE
    row = lambda cols: pl.BlockSpec((tm, cols), lambda i: (i, 0))
    full = lambda w: pl.BlockSpec((1,) + w.shape[1:], lambda i: (layer, 0, 0),
                                  pipeline_mode=pl.Buffered(1))
    return pl.pallas_call(
        _tail_kernel,
        grid=(n // tm,),
        in_specs=[row(d), row(d), row(ya.shape[1]), row(yb.shape[1]), row(yc.shape[1]),
                  pl.BlockSpec((1, tm, pd), lambda i: (layer, i, 0)),
                  full(wg), full(wa), full(wb), full(wc), full(wo),
                  full(g_mlp), full(w_up), full(w_down), full(g_ple), full(w_pg), full(w_pp)],
        out_specs=row(d),
        out_shape=jax.ShapeDtypeStruct((n, d), F32),
        compiler_params=_params("parallel"),
    )(x2, h, ya, yb, yc, p3, wg, wa, wb, wc, wo, g_mlp, w_up, w_down, g_ple, w_pg, w_pp)


def _tile_gain(g, reps, scale=1.0):
    return jnp.tile(g.astype(F32), (1, reps)) * scale


def kernel(x, p, positions, attn_norm, w_in, a_q_norm, a_k_norm, a_lambda, a_subln,
           c_q_norm, c_k_norm, idx_k_norm, w_br_a, w_br_b, w_br_c, w_out,
           mlp_norm, w_up, w_down, ple_norm, w_ple_gate, w_ple_proj):
    b, s, d = x.shape
    depth = w_in.shape[0]
    n = b * s
    topk = min(TOPK_MAX, s // 4)
    t = min(ATTN_TILE, s)
    scale = HEAD_DIM ** -0.5
    qscale = scale * LOG2E

    inv = ROPE_THETA ** (-jnp.arange(0, HEAD_DIM, 2, dtype=F32) / HEAD_DIM)
    ang = positions.astype(F32)[..., None] * inv
    cos, sin = jnp.cos(ang), jnp.sin(ang)
    cos_t = jnp.concatenate([cos, cos, cos, cos], axis=-1).reshape(n, LANES)
    sin_t = jnp.concatenate([-sin, sin, -sin, sin], axis=-1).reshape(n, LANES)

    qa, ka, va = 0, 512, 1024
    qb, kb, vb = 1536, 2048, 2560
    qc, kc, vc = 3072, 3584, 4096
    qi = 4608
    gl = 4932
    col = lambda a, width=GROUP: w_in[:, :, a:a + width]
    w_qk = jnp.concatenate([col(qa), col(ka), col(qc), col(kc)], axis=-1).astype(BF16)
    w_bqk = jnp.concatenate([col(qb) * qscale, col(kb)], axis=-1).astype(BF16)
    w_vt = jnp.concatenate([col(va), col(vb), col(vc)], axis=-1).astype(BF16)
    w_idx = jnp.pad(col(qi, gl - qi), ((0, 0), (0, 0), (0, IDX_COLS - (gl - qi)))).astype(BF16)
    w_gl = col(gl, N_BRANCHES * d).astype(BF16)
    heads = GROUP // HEAD_DIM
    gain_qk = jnp.concatenate([_tile_gain(a_q_norm, heads, qscale), _tile_gain(a_k_norm, heads),
                               _tile_gain(c_q_norm, heads, qscale), _tile_gain(c_k_norm, heads)],
                              axis=-1)[:, None, :]
    gain_idx = jnp.pad(idx_k_norm.astype(F32), ((0, 0), (0, LANES - IDX_DIM)))[:, None, :]
    scale_idx = jnp.concatenate([jnp.full((IDX_Q,), IDX_DIM ** -0.5, F32), jnp.ones((IDX_DIM,), F32),
                                 jnp.full((IDX_HEADS,), IDX_HEADS ** -0.5, F32),
                                 jnp.zeros((LANES - IDX_DIM - IDX_HEADS,), F32)])[None, :]
    gid = jnp.arange(GROUP) // HEAD_DIM
    gmat = (gid[:, None] == gid[None, :]).astype(BF16)
    kt = min(KEY_TILE, s)
    ar = jnp.arange(kt)
    later_mat = jnp.concatenate([(ar[None, :] > ar[:, None]).astype(BF16),
                                 jnp.ones((ONES_ROWS, kt), BF16)], axis=0)
    before_mat = (ar[None, :] < ar[:, None]).astype(BF16)

    wa, wb, wc, wo = (w.astype(BF16) for w in (w_br_a, w_br_b, w_br_c, w_out))
    wu, wd, wpg, wpp = (w.astype(BF16) for w in (w_up, w_down, w_ple_gate, w_ple_proj))
    g_attn, g_mlp, g_ple = (g.astype(F32)[:, None, :] for g in (attn_norm, mlp_norm, ple_norm))
    subln = a_subln.astype(F32)[:, None, :]
    lam_p = a_lambda.astype(F32)
    p3 = p.reshape(depth, n, p.shape[-1])

    x2 = x.reshape(n, d)
    for i in range(depth):
        lam_init = 0.8 - 0.6 * math.exp(-0.3 * i)
        h, qk, bqk, vt, idx = _front_call(x2, g_attn, w_qk, w_bqk, w_vt, w_idx, gain_qk, gain_idx,
                                          scale_idx, cos_t, sin_t, gmat, i, kt)
        qk, bqk, idx = (a.reshape(b, s, -1) for a in (qk, bqk, idx))
        vt = vt.reshape(b, s // kt, -1, kt)
        ya, yb = _attn_ab_call(qk, bqk, vt, lam_p, subln, later_mat, i, lam_init, t)
        bias = _dsa_mask_call(idx, before_mat, topk, t)
        yc = _attn_c_call(qk, vt, bias)
        x2 = _tail_call(x2, h, ya.reshape(n, -1), yb.reshape(n, -1), yc.reshape(n, -1), p3,
                        w_gl, wa, wb, wc, wo, g_mlp, wu, wd, g_ple, wpg, wpp, i)
    return x2.reshape(b, s, d)
```

```python
import functools
import math

import jax
import jax.numpy as jnp
from jax import lax
from jax.experimental import pallas as pl
from jax.experimental.pallas import tpu as pltpu

F32 = jnp.float32
BF16 = jnp.bfloat16

HEAD_DIM = 64
CHUNK = 64
A_HEADS = 4
B_HEADS = 8
C_HEADS = 8
IDX_HEADS = 4
IDX_DIM = 64
TOPK_MAX = 256
N_BRANCHES = 3
ROPE_THETA = 10000.0
EPS = 1e-6

LANES = 128
SUBLANES = 8
HALF_HEAD = HEAD_DIM // 2
NEG = -1e30
INT_MIN = -(2 ** 31)
LOG2E = 1.4426950408889634
V7X_VMEM_BYTES = 64 * 1024 * 1024
VMEM_LIMIT = V7X_VMEM_BYTES * 7 // 8

ROW_TILE = 512
ATTN_TILE = 256
KEY_TILE = 128
FF_TILE = 1024
TAIL_TILE = 512
GROUP = 512


def _params(*sem):
    return pltpu.CompilerParams(dimension_semantics=sem, vmem_limit_bytes=VMEM_LIMIT)


def _dot(a, b):
    return jnp.dot(a, b, preferred_element_type=F32)


def _swap_halves(y):
    lane = lax.broadcasted_iota(jnp.int32, y.shape, 1)
    return jnp.where((lane & HALF_HEAD) != 0, pltpu.roll(y, HALF_HEAD, 1),
                     pltpu.roll(y, LANES - HALF_HEAD, 1))


def _rope_slab(y, cos, sin):
    return y * cos + _swap_halves(y) * sin


IDX_Q = IDX_HEADS * IDX_DIM
IDX_COLS = IDX_Q + LANES


def _front_kernel(x_ref, g_ref, wqk_ref, wbqk_ref, wvt_ref, widx_ref, gqk_ref, gidx_ref, sidx_ref,
                  cos_ref, sin_ref, gmat_ref, h_ref, qk_ref, bqk_ref, vt_ref, idx_ref):
    x = x_ref[...]
    ms = jnp.mean(x * x, axis=-1, keepdims=True)
    h = (x * lax.rsqrt(ms + EPS) * g_ref[0]).astype(BF16)
    h_ref[...] = h
    cos = cos_ref[...]
    sin = sin_ref[...]

    gmat = gmat_ref[...]
    for j in range(qk_ref.shape[1] // GROUP):
        cols = slice(j * GROUP, (j + 1) * GROUP)
        t = _dot(h, wqk_ref[0, :, cols])
        ms = _dot((t * t).astype(BF16), gmat) * (1.0 / HEAD_DIM)
        y = t * lax.rsqrt(ms + EPS) * gqk_ref[0, :, cols]
        for s in range(GROUP // LANES):
            sl = slice(s * LANES, (s + 1) * LANES)
            qk_ref[:, j * GROUP + s * LANES:j * GROUP + (s + 1) * LANES] = (
                _rope_slab(y[:, sl], cos, sin).astype(qk_ref.dtype))

    for j in range(bqk_ref.shape[1] // GROUP):
        cols = slice(j * GROUP, (j + 1) * GROUP)
        bqk_ref[:, cols] = _dot(h, wbqk_ref[0, :, cols]).astype(bqk_ref.dtype)

    tt = vt_ref.shape[2]
    for j in range(vt_ref.shape[1] // GROUP):
        cols = slice(j * GROUP, (j + 1) * GROUP)
        t = _dot(h, wvt_ref[0, :, cols])
        for r in range(vt_ref.shape[0]):
            vt_ref[r, cols, :] = t[r * tt:(r + 1) * tt, :].T.astype(vt_ref.dtype)

    t = _dot(h, widx_ref[0])
    scale = sidx_ref[...]
    for s in range(IDX_Q // LANES):
        sl = slice(s * LANES, (s + 1) * LANES)
        idx_ref[:, sl] = _rope_slab(t[:, sl], cos, sin) * scale[:, sl]
    sl = slice(IDX_Q, IDX_COLS)
    tk = t[:, sl]
    lane = lax.broadcasted_iota(jnp.int32, tk.shape, 1)
    is_k = lane < IDX_DIM
    ms = jnp.sum(jnp.where(is_k, tk * tk, 0.0), axis=-1, keepdims=True) * (1.0 / IDX_DIM)
    yk = tk * lax.rsqrt(ms + EPS) * gidx_ref[0]
    idx_ref[:, sl] = jnp.where(is_k, _rope_slab(yk, cos, sin), tk) * scale[:, sl]


def _front_call(x2, g_attn, w_qk, w_bqk, w_vt, w_idx, gain_qk, gain_idx, scale_idx, cos_t, sin_t,
                gmat, layer, tt):
    n, d = x2.shape
    tm = ROW_TILE
    row = lambda cols: pl.BlockSpec((tm, cols), lambda i: (i, 0))
    full = lambda w: pl.BlockSpec((1,) + w.shape[1:], lambda i: (layer, 0, 0))
    fixed = lambda a: pl.BlockSpec(a.shape, lambda i: (0, 0))
    cqk, cbqk, cvt = w_qk.shape[2], w_bqk.shape[2], w_vt.shape[2]
    return pl.pallas_call(
        _front_kernel,
        grid=(n // tm,),
        in_specs=[row(d), full(g_attn), full(w_qk), full(w_bqk), full(w_vt), full(w_idx),
                  full(gain_qk), full(gain_idx), fixed(scale_idx), row(LANES), row(LANES),
                  fixed(gmat)],
        out_specs=[row(d), row(cqk), row(cbqk),
                   pl.BlockSpec((tm // tt, cvt, tt), lambda i: (i, 0, 0)), row(IDX_COLS)],
        out_shape=[jax.ShapeDtypeStruct((n, d), BF16), jax.ShapeDtypeStruct((n, cqk), BF16),
                   jax.ShapeDtypeStruct((n, cbqk), BF16),
                   jax.ShapeDtypeStruct((n // tt, cvt, tt), BF16),
                   jax.ShapeDtypeStruct((n, IDX_COLS), F32)],
        compiler_params=_params("parallel"),
    )(x2, g_attn, w_qk, w_bqk, w_vt, w_idx, gain_qk, gain_idx, scale_idx, cos_t, sin_t, gmat)


def _load_qt(q_ref, qz_ref):
    qt = q_ref[0].astype(F32).T
    rows = lax.broadcasted_iota(jnp.int32, (LANES, qt.shape[1]), 0)
    for s in range(qt.shape[0] // LANES):
        slab = qt[s * LANES:(s + 1) * LANES, :]
        qz_ref[2 * s] = jnp.where(rows < HEAD_DIM, slab, 0.0).astype(BF16)
        qz_ref[2 * s + 1] = jnp.where(rows >= HEAD_DIM, slab, 0.0).astype(BF16)


ONES_ROWS = 16


def _softmax_step(n, s, vt, m_ref, l_ref, acc_ref):
    m_old = m_ref[n]
    m_new = jnp.maximum(m_old, jnp.max(s, axis=0, keepdims=True))
    p = jnp.exp2(s - m_new).astype(BF16)
    alpha = jnp.exp2(m_old - m_new)
    f = vt.shape[0]
    pv = _dot(jnp.concatenate([vt, jnp.ones((ONES_ROWS, vt.shape[1]), BF16)], axis=0), p)
    l_ref[n] = alpha * l_ref[n] + pv[f:f + 1]
    acc_ref[n] = alpha * acc_ref[n] + pv[:f]
    m_ref[n] = m_new


def _tile_ids(t):
    key = lax.broadcasted_iota(jnp.int32, (KEY_TILE, t), 0)
    qry = lax.broadcasted_iota(jnp.int32, (KEY_TILE, t), 1)
    return key, qry


def _key_rows(u):
    return pl.ds(pl.multiple_of(u * KEY_TILE, KEY_TILE), KEY_TILE)


def _in_pairs(n, one, carry):
    carry = lax.fori_loop(0, n // 2, lambda p, c: one(2 * p + 1, one(2 * p, c)), carry)
    return lax.cond(n % 2 == 1, lambda c: one(n - 1, c), lambda c: c, carry)


def _in_fours(n, one, carry):
    def four(p, c):
        for j in range(4):
            c = one(4 * p + j, c)
        return c
    carry = lax.fori_loop(0, n // 4, four, carry)
    return lax.fori_loop((n // 4) * 4, n, one, carry)


def _interleave(*gens):
    gens = list(gens)
    while gens:
        for gen in list(gens):
            try:
                next(gen)
            except StopIteration:
                gens.remove(gen)


def _attn_ab_kernel(lam_ref, aq_ref, ak_ref, avt_ref, g_ref, bq_ref, bk_ref, bvt_ref, lmat_ref,
                    ya_ref, yb_ref, aqz_ref, m_ref, l_ref, aacc_ref, bqz_ref, r_ref, bacc_ref, *,
                    lam_init):
    t = aq_ref.shape[1]
    i = pl.program_id(1)
    lp = lam_ref[0]
    lam = (jnp.exp(jnp.sum(lp[0:1] * lp[1:2], axis=-1, keepdims=True))
           - jnp.exp(jnp.sum(lp[2:3] * lp[3:4], axis=-1, keepdims=True)) + lam_init)
    key, qry = _tile_ids(t)
    sub = t // KEY_TILE
    lmat = lmat_ref[...]
    _load_qt(aq_ref, aqz_ref)
    _load_qt(bq_ref, bqz_ref)
    m_ref[...] = jnp.full(m_ref.shape, NEG, F32)
    l_ref[...] = jnp.zeros(l_ref.shape, F32)
    aacc_ref[...] = jnp.zeros(aacc_ref.shape, F32)
    r_ref[...] = jnp.zeros(r_ref.shape, F32)
    bacc_ref[...] = jnp.zeros(bacc_ref.shape, F32)

    def a_group(g, diag):
        scores = []
        for r in range(sub):
            for n in range(2 * A_HEADS):
                kb = ak_ref[0, _key_rows(g * sub + r), (n // 2) * LANES:(n // 2 + 1) * LANES]
                scores.append(_dot(kb, aqz_ref[n]))
        yield
        for r in range(sub):
            for n in range(2 * A_HEADS):
                s = scores[r * 2 * A_HEADS + n]
                if diag:
                    s = jnp.where(((key + r * KEY_TILE) // CHUNK) <= (qry // CHUNK), s, NEG)
                vt = avt_ref[0, g * sub + r, (n // 2) * LANES:(n // 2 + 1) * LANES, :]
                _softmax_step(n, s, vt, m_ref, l_ref, aacc_ref)
                yield

    def b_group(g, diag):
        order = [(r, n) for r in reversed(range(sub)) for n in range(B_HEADS)]
        strict = {r: (key + r * KEY_TILE) < qry for r in range(sub)} if diag else None
        logits = {}
        for r, n in order:
            kb = bk_ref[0, _key_rows(g * sub + r), (n // 2) * LANES:(n // 2 + 1) * LANES]
            logits[r, n] = _dot(kb, bqz_ref[n])
        yield
        log_betas, laters, sums = {}, {}, {}
        for r, n in order:
            z = logits[r, n]
            neg_abs = pltpu.bitcast(pltpu.bitcast(z, jnp.int32) | jnp.int32(INT_MIN), F32)
            soft = jnp.log2(1.0 + jnp.exp2(neg_abs))
            log_beta = jnp.minimum(z, 0.0) - soft
            log_om = log_beta - z
            if diag:
                log_om = jnp.where(strict[r], log_om, 0.0)
            later = _dot(lmat, log_om.astype(BF16))
            laters[r, n] = later[:KEY_TILE]
            log_betas[r, n] = log_beta
            sums[r, n] = later[KEY_TILE:KEY_TILE + 1]
            yield
        for r, n in order:
            a = jnp.exp2(log_betas[r, n] + laters[r, n])
            if diag:
                a = jnp.where(strict[r], a, 0.0)
            vt = bvt_ref[0, g * sub + r, n * HEAD_DIM:(n + 1) * HEAD_DIM, :]
            visited = r_ref[n]
            bacc_ref[n] = bacc_ref[n] + _dot(vt, a.astype(BF16)) * jnp.exp2(visited)
            r_ref[n] = visited + sums[r, n]
            yield

    def body(n, carry):
        _interleave(a_group(n, False), b_group(i - 1 - n, False))
        return carry

    _interleave(a_group(i, True), b_group(i, True))
    _in_pairs(i, body, 0)

    gain = g_ref[0] * (1.0 - lam_init)
    for h in range(A_HEADS):
        o = aacc_ref[2 * h] / l_ref[2 * h] - lam * (aacc_ref[2 * h + 1] / l_ref[2 * h + 1])
        ms = jnp.mean(o * o, axis=0, keepdims=True)
        y = (o * lax.rsqrt(ms + EPS)).T * gain
        ya_ref[0, :, h * LANES:(h + 1) * LANES] = y.astype(ya_ref.dtype)
    yb_ref[0] = bacc_ref[...].reshape(B_HEADS * HEAD_DIM, t).T.astype(yb_ref.dtype)


def _attn_ab_call(qk, bqk, vt, lam_p, subln, lmat, layer, lam_init, t):
    b, s, _ = qk.shape
    nt, _, kt = vt.shape[1:]
    ns = 2 * A_HEADS
    qspec = pl.BlockSpec((1, t, GROUP), lambda bi, i: (bi, i, 0))
    kspec = pl.BlockSpec((1, s, GROUP), lambda bi, i: (bi, 0, 1))
    vspec = lambda f: pl.BlockSpec((1, nt, GROUP, kt), lambda bi, i: (bi, 0, f, 0))
    return pl.pallas_call(
        functools.partial(_attn_ab_kernel, lam_init=lam_init),
        grid=(b, s // t),
        in_specs=[pl.BlockSpec((1, 4, HEAD_DIM), lambda bi, i: (layer, 0, 0)),
                  qspec, kspec, vspec(0),
                  pl.BlockSpec((1, 1, LANES), lambda bi, i: (layer, 0, 0)),
                  qspec, kspec, vspec(1),
                  pl.BlockSpec(lmat.shape, lambda bi, i: (0, 0))],
        out_specs=[qspec, qspec],
        out_shape=[jax.ShapeDtypeStruct((b, s, GROUP), BF16)] * 2,
        scratch_shapes=[pltpu.VMEM((ns, LANES, t), BF16), pltpu.VMEM((ns, 1, t), F32),
                        pltpu.VMEM((ns, 1, t), F32), pltpu.VMEM((ns, LANES, t), F32),
                        pltpu.VMEM((B_HEADS, LANES, t), BF16), pltpu.VMEM((B_HEADS, 1, t), F32),
                        pltpu.VMEM((B_HEADS, HEAD_DIM, t), F32)],
        compiler_params=_params("parallel", "arbitrary"),
    )(lam_p, qk, qk, vt, subln, bqk, bqk, vt, lmat)


def _fold_rows(x):
    parts = [x[r * SUBLANES:(r + 1) * SUBLANES] for r in range(x.shape[0] // SUBLANES)]
    while len(parts) > 1:
        parts = [a + b for a, b in zip(parts[0::2], parts[1::2])]
    return parts[0]


def _dsa_mask_kernel(q_ref, kw_ref, wq_ref, lt_ref, o_ref, key_ref, qz_ref, *, topk):
    t = q_ref.shape[1]
    nj = key_ref.shape[0]
    i = pl.program_id(1)
    nch = (i + 1) * (t // KEY_TILE)
    qt = q_ref[0].T
    pad = jnp.zeros((LANES - IDX_DIM, t), F32)
    for h in range(IDX_HEADS):
        qz_ref[h] = jnp.concatenate([qt[h * IDX_DIM:(h + 1) * IDX_DIM], pad], axis=0).astype(BF16)
    wt = wq_ref[0].T
    wh = [wt[IDX_DIM + h:IDX_DIM + h + 1, :] for h in range(IDX_HEADS)]
    key0, qry0 = _tile_ids(t)
    sub = t // KEY_TILE

    def chunk_ok(r):
        return ((key0 + r * KEY_TILE) // CHUNK) <= (qry0 // CHUNK)

    def fill(g, diag):
        for r in range(sub):
            c = g * sub + r
            kb = kw_ref[0, _key_rows(c), :].astype(BF16)
            score = jnp.zeros((KEY_TILE, t), F32)
            for h in range(IDX_HEADS):
                score = score + wh[h] * jnp.maximum(_dot(kb, qz_ref[h]), 0.0)
            if diag:
                score = jnp.where(chunk_ok(r), score, -jnp.inf)
            bits = pltpu.bitcast(score, jnp.int32)
            okey = jnp.where(bits < 0, bits ^ jnp.int32(0x7FFFFFFF), bits)
            key_ref[c] = jnp.where(score == 0.0, 0, okey)

    def fill_one(g, carry):
        fill(g, False)
        return carry

    _in_fours(i, fill_one, 0)
    fill(i, True)

    def count_at_or_above(cand):
        def one(g, acc):
            for r in range(sub):
                acc = acc + _fold_rows(jnp.where(key_ref[g * sub + r] >= cand, 1.0, 0.0))
            return acc
        acc = _in_fours(i + 1, one, jnp.zeros((SUBLANES, t), F32))
        return jnp.sum(acc, axis=0, keepdims=True)

    def bit_body(n, carry):
        lo, above = carry
        cand = lo + jnp.left_shift(jnp.int32(1), 31 - n)
        cnt = count_at_or_above(cand)
        take = cnt >= topk
        return jnp.where(take, cand, lo), jnp.where(take, above, cnt)

    thr, above = lax.fori_loop(
        0, 32, bit_body, (jnp.full((1, t), INT_MIN, jnp.int32), jnp.zeros((1, t), F32)))
    need = topk - above
    lt = lt_ref[...]

    def emit(g, run, diag):
        for r in range(sub):
            c = g * sub + r
            okey = key_ref[c]
            eq = okey == thr
            eqf = jnp.where(eq, 1.0, 0.0)
            before = _dot(lt, eqf.astype(BF16)) + run
            sel = (okey > thr) | (eq & (before < need))
            if diag:
                sel = sel & chunk_ok(r)
            o_ref[0, 0, c] = jnp.where(sel, 0.0, NEG).astype(o_ref.dtype)
            run = run + jnp.sum(eqf, axis=0, keepdims=True)
        return run

    run = _in_fours(i, lambda g, run: emit(g, run, False), jnp.zeros((1, t), F32))
    emit(i, run, True)

    def blank(c, carry):
        o_ref[0, 0, c] = jnp.full((KEY_TILE, t), NEG, o_ref.dtype)
        return carry

    lax.fori_loop(nch, nj, blank, 0)


def _dsa_mask_call(idx, lt, topk, t):
    b, s, _ = idx.shape
    kt = lt.shape[0]
    assert t >= topk
    nt, nk = s // t, s // kt
    return pl.pallas_call(
        functools.partial(_dsa_mask_kernel, topk=topk),
        grid=(b, nt),
        in_specs=[pl.BlockSpec((1, t, IDX_Q), lambda bi, i: (bi, i, 0)),
                  pl.BlockSpec((1, s, LANES), lambda bi, i: (bi, 0, IDX_Q // LANES)),
                  pl.BlockSpec((1, t, LANES), lambda bi, i: (bi, i, IDX_Q // LANES)),
                  pl.BlockSpec((kt, kt), lambda bi, i: (0, 0))],
        out_specs=pl.BlockSpec((1, 1, nk, kt, t), lambda bi, i: (bi, i, 0, 0, 0)),
        out_shape=jax.ShapeDtypeStruct((b, nt, nk, kt, t), BF16),
        scratch_shapes=[pltpu.VMEM((nk, kt, t), jnp.int32), pltpu.VMEM((IDX_HEADS, LANES, t), BF16)],
        compiler_params=_params("parallel", "arbitrary"),
    )(idx, idx, idx, lt)


def _attn_c_kernel(q_ref, k_ref, vt_ref, b_ref, o_ref, qz_ref, m_ref, l_ref, acc_ref):
    t = q_ref.shape[1]
    i = pl.program_id(1)
    _load_qt(q_ref, qz_ref)
    m_ref[...] = jnp.full(m_ref.shape, NEG, F32)
    l_ref[...] = jnp.zeros(l_ref.shape, F32)
    acc_ref[...] = jnp.zeros(acc_ref.shape, F32)

    key, qry = _tile_ids(KEY_TILE)
    ident = jnp.where(key == qry, 1.0, 0.0).astype(BF16)

    sub = t // KEY_TILE

    def body(g, carry):
        scores = []
        for r in range(sub):
            u = g * sub + r
            bias = b_ref[0, 0, u]
            for n in range(C_HEADS):
                kb = k_ref[0, _key_rows(u), (n // 2) * LANES:(n // 2 + 1) * LANES]
                lhs = jnp.concatenate([kb, ident], axis=1)
                rhs = jnp.concatenate([qz_ref[n], bias], axis=0)
                scores.append(_dot(lhs, rhs))
        for r in range(sub):
            for n in range(C_HEADS):
                vt = vt_ref[0, g * sub + r, n * HEAD_DIM:(n + 1) * HEAD_DIM, :]
                _softmax_step(n, scores[r * C_HEADS + n], vt, m_ref, l_ref, acc_ref)
        return carry

    _in_fours(i + 1, body, 0)
    o = acc_ref[...] / l_ref[...]
    o_ref[0] = o.reshape(C_HEADS * HEAD_DIM, t).T.astype(o_ref.dtype)


def _attn_c_call(qk, vt, bias):
    b, s, _ = qk.shape
    nk, _, kt = vt.shape[1:]
    nt, t = bias.shape[1], bias.shape[4]
    return pl.pallas_call(
        _attn_c_kernel,
        grid=(b, nt),
        in_specs=[pl.BlockSpec((1, t, GROUP), lambda bi, i: (bi, i, 2)),
                  pl.BlockSpec((1, s, GROUP), lambda bi, i: (bi, 0, 3)),
                  pl.BlockSpec((1, nk, GROUP, kt), lambda bi, i: (bi, 0, 2, 0)),
                  pl.BlockSpec((1, 1, nk, kt, t), lambda bi, i: (bi, i, 0, 0, 0))],
        out_specs=pl.BlockSpec((1, t, GROUP), lambda bi, i: (bi, i, 0)),
        out_shape=jax.ShapeDtypeStruct((b, s, GROUP), BF16),
        scratch_shapes=[pltpu.VMEM((C_HEADS, LANES, t), BF16), pltpu.VMEM((C_HEADS, 1, t), F32),
                        pltpu.VMEM((C_HEADS, 1, t), F32), pltpu.VMEM((C_HEADS, HEAD_DIM, t), F32)],
        compiler_params=_params("parallel", "arbitrary"),
    )(qk, qk, vt, bias)


def _dsa_kernel(iq_ref, kw_ref, wq_ref, lt_ref, q_ref, k_ref, vt_ref, o_ref,
                bias_ref, key_ref, iqz_ref, qz_ref, m_ref, l_ref, acc_ref, *, topk):
    _dsa_mask_kernel(iq_ref, kw_ref, wq_ref, lt_ref, bias_ref, key_ref, iqz_ref, topk=topk)
    _attn_c_kernel(q_ref, k_ref, vt_ref, bias_ref, o_ref, qz_ref, m_ref, l_ref, acc_ref)


def _dsa_call(idx, lt, qk, vt, topk, t):
    b, s, _ = idx.shape
    nk, _, kt = vt.shape[1:]
    assert t >= topk and kt == lt.shape[0]
    nt = s // t
    return pl.pallas_call(
        functools.partial(_dsa_kernel, topk=topk),
        grid=(b, nt),
        in_specs=[pl.BlockSpec((1, t, IDX_Q), lambda bi, i: (bi, i, 0)),
                  pl.BlockSpec((1, s, LANES), lambda bi, i: (bi, 0, IDX_Q // LANES)),
                  pl.BlockSpec((1, t, LANES), lambda bi, i: (bi, i, IDX_Q // LANES)),
                  pl.BlockSpec((kt, kt), lambda bi, i: (0, 0)),
                  pl.BlockSpec((1, t, GROUP), lambda bi, i: (bi, i, 2)),
                  pl.BlockSpec((1, s, GROUP), lambda bi, i: (bi, 0, 3)),
                  pl.BlockSpec((1, nk, GROUP, kt), lambda bi, i: (bi, 0, 2, 0))],
        out_specs=pl.BlockSpec((1, t, GROUP), lambda bi, i: (bi, i, 0)),
        out_shape=jax.ShapeDtypeStruct((b, s, GROUP), BF16),
        scratch_shapes=[pltpu.VMEM((1, 1, nk, kt, t), BF16), pltpu.VMEM((nk, kt, t), jnp.int32),
                        pltpu.VMEM((IDX_HEADS, LANES, t), BF16),
                        pltpu.VMEM((C_HEADS, LANES, t), BF16), pltpu.VMEM((C_HEADS, 1, t), F32),
                        pltpu.VMEM((C_HEADS, 1, t), F32), pltpu.VMEM((C_HEADS, HEAD_DIM, t), F32)],
        compiler_params=_params("parallel", "arbitrary"),
    )(idx, idx, idx, lt, qk, qk, vt)


def _tail_kernel(x_ref, h_ref, ya_ref, yb_ref, yc_ref, p_ref, wg_ref, wa_ref, wb_ref, wc_ref, wo_ref,
                 gm_ref, wu_ref, wd_ref, gp_ref, wpg_ref, wpp_ref, o_ref):
    d = x_ref.shape[1]
    h = h_ref[...]
    merged = jnp.zeros(x_ref.shape, F32)
    for n, (y_ref, w_ref) in enumerate(((ya_ref, wa_ref), (yb_ref, wb_ref), (yc_ref, wc_ref))):
        gate = jax.nn.sigmoid(_dot(h, wg_ref[0, :, n * d:(n + 1) * d]))
        merged = merged + gate * _dot(y_ref[...], w_ref[0])
    x = x_ref[...] + _dot(merged.astype(BF16), wo_ref[0])
    ms = jnp.mean(x * x, axis=-1, keepdims=True)
    hm = (x * lax.rsqrt(ms + EPS) * gm_ref[0]).astype(BF16)
    for f in range(wu_ref.shape[2] // FF_TILE):
        u = jnp.maximum(_dot(hm, wu_ref[0, :, f * FF_TILE:(f + 1) * FF_TILE]), 0.0)
        x = x + _dot((u * u).astype(BF16), wd_ref[0, f * FF_TILE:(f + 1) * FF_TILE, :])
    ms = jnp.mean(x * x, axis=-1, keepdims=True)
    hp = (x * lax.rsqrt(ms + EPS) * gp_ref[0]).astype(BF16)
    gate = jax.nn.sigmoid(_dot(hp, wpg_ref[0]))
    o_ref[...] = x + gate * _dot(p_ref[0].astype(BF16), wpp_ref[0])


def _tail_call(x2, h, ya, yb, yc, p3, wg, wa, wb, wc, wo, g_mlp, w_up, w_down, g_ple, w_pg, w_pp,
               layer):
    n, d = x2.shape
    pd = p3.shape[2]
    tm = TAIL_TILE
    row = lambda cols: pl.BlockSpec((tm, cols), lambda i: (i, 0))
    full = lambda w: pl.BlockSpec((1,) + w.shape[1:], lambda i: (layer, 0, 0),
                                  pipeline_mode=pl.Buffered(1))
    return pl.pallas_call(
        _tail_kernel,
        grid=(n // tm,),
        in_specs=[row(d), row(d), row(ya.shape[1]), row(yb.shape[1]), row(yc.shape[1]),
                  pl.BlockSpec((1, tm, pd), lambda i: (layer, i, 0)),
                  full(wg), full(wa), full(wb), full(wc), full(wo),
                  full(g_mlp), full(w_up), full(w_down), full(g_ple), full(w_pg), full(w_pp)],
        out_specs=row(d),
        out_shape=jax.ShapeDtypeStruct((n, d), F32),
        compiler_params=_params("parallel"),
    )(x2, h, ya, yb, yc, p3, wg, wa, wb, wc, wo, g_mlp, w_up, w_down, g_ple, w_pg, w_pp)


def _tile_gain(g, reps, scale=1.0):
    return jnp.tile(g.astype(F32), (1, reps)) * scale


def kernel(x, p, positions, attn_norm, w_in, a_q_norm, a_k_norm, a_lambda, a_subln,
           c_q_norm, c_k_norm, idx_k_norm, w_br_a, w_br_b, w_br_c, w_out,
           mlp_norm, w_up, w_down, ple_norm, w_ple_gate, w_ple_proj):
    b, s, d = x.shape
    depth = w_in.shape[0]
    n = b * s
    topk = min(TOPK_MAX, s // 4)
    t = min(ATTN_TILE, s)
    scale = HEAD_DIM ** -0.5
    qscale = scale * LOG2E

    inv = ROPE_THETA ** (-jnp.arange(0, HEAD_DIM, 2, dtype=F32) / HEAD_DIM)
    ang = positions.astype(F32)[..., None] * inv
    cos, sin = jnp.cos(ang), jnp.sin(ang)
    cos_t = jnp.concatenate([cos, cos, cos, cos], axis=-1).reshape(n, LANES)
    sin_t = jnp.concatenate([-sin, sin, -sin, sin], axis=-1).reshape(n, LANES)

    qa, ka, va = 0, 512, 1024
    qb, kb, vb = 1536, 2048, 2560
    qc, kc, vc = 3072, 3584, 4096
    qi = 4608
    gl = 4932
    col = lambda a, width=GROUP: w_in[:, :, a:a + width]
    w_qk = jnp.concatenate([col(qa), col(ka), col(qc), col(kc)], axis=-1).astype(BF16)
    w_bqk = jnp.concatenate([col(qb) * qscale, col(kb)], axis=-1).astype(BF16)
    w_vt = jnp.concatenate([col(va), col(vb), col(vc)], axis=-1).astype(BF16)
    w_idx = jnp.pad(col(qi, gl - qi), ((0, 0), (0, 0), (0, IDX_COLS - (gl - qi)))).astype(BF16)
    w_gl = col(gl, N_BRANCHES * d).astype(BF16)
    heads = GROUP // HEAD_DIM
    gain_qk = jnp.concatenate([_tile_gain(a_q_norm, heads, qscale), _tile_gain(a_k_norm, heads),
                               _tile_gain(c_q_norm, heads, qscale), _tile_gain(c_k_norm, heads)],
                              axis=-1)[:, None, :]
    gain_idx = jnp.pad(idx_k_norm.astype(F32), ((0, 0), (0, LANES - IDX_DIM)))[:, None, :]
    scale_idx = jnp.concatenate([jnp.full((IDX_Q,), IDX_DIM ** -0.5, F32), jnp.ones((IDX_DIM,), F32),
                                 jnp.full((IDX_HEADS,), IDX_HEADS ** -0.5, F32),
                                 jnp.zeros((LANES - IDX_DIM - IDX_HEADS,), F32)])[None, :]
    gid = jnp.arange(GROUP) // HEAD_DIM
    gmat = (gid[:, None] == gid[None, :]).astype(BF16)
    kt = min(KEY_TILE, s)
    ar = jnp.arange(kt)
    later_mat = jnp.concatenate([(ar[None, :] > ar[:, None]).astype(BF16),
                                 jnp.ones((ONES_ROWS, kt), BF16)], axis=0)
    before_mat = (ar[None, :] < ar[:, None]).astype(BF16)

    wa, wb, wc, wo = (w.astype(BF16) for w in (w_br_a, w_br_b, w_br_c, w_out))
    wu, wd, wpg, wpp = (w.astype(BF16) for w in (w_up, w_down, w_ple_gate, w_ple_proj))
    g_attn, g_mlp, g_ple = (g.astype(F32)[:, None, :] for g in (attn_norm, mlp_norm, ple_norm))
    subln = a_subln.astype(F32)[:, None, :]
    lam_p = a_lambda.astype(F32)
    p3 = p.reshape(depth, n, p.shape[-1])

    x2 = x.reshape(n, d)
    for i in range(depth):
        lam_init = 0.8 - 0.6 * math.exp(-0.3 * i)
        h, qk, bqk, vt, idx = _front_call(x2, g_attn, w_qk, w_bqk, w_vt, w_idx, gain_qk, gain_idx,
                                          scale_idx, cos_t, sin_t, gmat, i, kt)
        qk, bqk, idx = (a.reshape(b, s, -1) for a in (qk, bqk, idx))
        vt = vt.reshape(b, s // kt, -1, kt)
        ya, yb = _attn_ab_call(qk, bqk, vt, lam_p, subln, later_mat, i, lam_init, t)
        yc = _dsa_call(idx, before_mat, qk, vt, topk, t)
        x2 = _tail_call(x2, h, ya.reshape(n, -1), yb.reshape(n, -1), yc.reshape(n, -1), p3,
                        w_gl, wa, wb, wc, wo, g_mlp, wu, wd, g_ple, wpg, wpp, i)
    return x2.reshape(b, s, d)
```

```python
import functools
import math

import jax
import jax.numpy as jnp
from jax import lax
from jax.experimental import pallas as pl
from jax.experimental.pallas import tpu as pltpu

F32 = jnp.float32
BF16 = jnp.bfloat16

HEAD_DIM = 64
CHUNK = 64
A_HEADS = 4
B_HEADS = 8
C_HEADS = 8
IDX_HEADS = 4
IDX_DIM = 64
TOPK_MAX = 256
N_BRANCHES = 3
ROPE_THETA = 10000.0
EPS = 1e-6

LANES = 128
SUBLANES = 8
HALF_HEAD = HEAD_DIM // 2
NEG = -1e30
INT_MIN = -(2 ** 31)
LOG2E = 1.4426950408889634
V7X_VMEM_BYTES = 64 * 1024 * 1024
VMEM_LIMIT = V7X_VMEM_BYTES * 7 // 8

ROW_TILE = 512
ATTN_TILE = 256
KEY_TILE = 128
FF_TILE = 1024
TAIL_TILE = 512
GROUP = 512


def _params(*sem):
    return pltpu.CompilerParams(dimension_semantics=sem, vmem_limit_bytes=VMEM_LIMIT)


def _dot(a, b):
    return jnp.dot(a, b, preferred_element_type=F32)


def _swap_halves(y):
    lane = lax.broadcasted_iota(jnp.int32, y.shape, 1)
    return jnp.where((lane & HALF_HEAD) != 0, pltpu.roll(y, HALF_HEAD, 1),
                     pltpu.roll(y, LANES - HALF_HEAD, 1))


def _rope_slab(y, cos, sin):
    return y * cos + _swap_halves(y) * sin


IDX_Q = IDX_HEADS * IDX_DIM
IDX_COLS = IDX_Q + LANES


def _front_kernel(x_ref, g_ref, wqk_ref, wbqk_ref, wvt_ref, widx_ref, gqk_ref, gidx_ref, sidx_ref,
                  cos_ref, sin_ref, gmat_ref, h_ref, qk_ref, bqk_ref, vt_ref, idx_ref):
    x = x_ref[...]
    ms = jnp.mean(x * x, axis=-1, keepdims=True)
    h = (x * lax.rsqrt(ms + EPS) * g_ref[0]).astype(BF16)
    h_ref[...] = h
    cos = cos_ref[...]
    sin = sin_ref[...]

    gmat = gmat_ref[...]
    for j in range(qk_ref.shape[1] // GROUP):
        cols = slice(j * GROUP, (j + 1) * GROUP)
        t = _dot(h, wqk_ref[0, :, cols])
        ms = _dot((t * t).astype(BF16), gmat) * (1.0 / HEAD_DIM)
        y = t * lax.rsqrt(ms + EPS) * gqk_ref[0, :, cols]
        for s in range(GROUP // LANES):
            sl = slice(s * LANES, (s + 1) * LANES)
            qk_ref[:, j * GROUP + s * LANES:j * GROUP + (s + 1) * LANES] = (
                _rope_slab(y[:, sl], cos, sin).astype(qk_ref.dtype))

    for j in range(bqk_ref.shape[1] // GROUP):
        cols = slice(j * GROUP, (j + 1) * GROUP)
        bqk_ref[:, cols] = _dot(h, wbqk_ref[0, :, cols]).astype(bqk_ref.dtype)

    tt = vt_ref.shape[2]
    for j in range(vt_ref.shape[1] // GROUP):
        cols = slice(j * GROUP, (j + 1) * GROUP)
        t = _dot(h, wvt_ref[0, :, cols])
        for r in range(vt_ref.shape[0]):
            vt_ref[r, cols, :] = t[r * tt:(r + 1) * tt, :].T.astype(vt_ref.dtype)

    t = _dot(h, widx_ref[0])
    scale = sidx_ref[...]
    for s in range(IDX_Q // LANES):
        sl = slice(s * LANES, (s + 1) * LANES)
        idx_ref[:, sl] = _rope_slab(t[:, sl], cos, sin) * scale[:, sl]
    sl = slice(IDX_Q, IDX_COLS)
    tk = t[:, sl]
    lane = lax.broadcasted_iota(jnp.int32, tk.shape, 1)
    is_k = lane < IDX_DIM
    ms = jnp.sum(jnp.where(is_k, tk * tk, 0.0), axis=-1, keepdims=True) * (1.0 / IDX_DIM)
    yk = tk * lax.rsqrt(ms + EPS) * gidx_ref[0]
    idx_ref[:, sl] = jnp.where(is_k, _rope_slab(yk, cos, sin), tk) * scale[:, sl]


def _front_call(x2, g_attn, w_qk, w_bqk, w_vt, w_idx, gain_qk, gain_idx, scale_idx, cos_t, sin_t,
                gmat, layer, tt):
    n, d = x2.shape
    tm = ROW_TILE
    row = lambda cols: pl.BlockSpec((tm, cols), lambda i: (i, 0))
    full = lambda w: pl.BlockSpec((1,) + w.shape[1:], lambda i: (layer, 0, 0))
    fixed = lambda a: pl.BlockSpec(a.shape, lambda i: (0, 0))
    cqk, cbqk, cvt = w_qk.shape[2], w_bqk.shape[2], w_vt.shape[2]
    return pl.pallas_call(
        _front_kernel,
        grid=(n // tm,),
        in_specs=[row(d), full(g_attn), full(w_qk), full(w_bqk), full(w_vt), full(w_idx),
                  full(gain_qk), full(gain_idx), fixed(scale_idx), row(LANES), row(LANES),
                  fixed(gmat)],
        out_specs=[row(d), row(cqk), row(cbqk),
                   pl.BlockSpec((tm // tt, cvt, tt), lambda i: (i, 0, 0)), row(IDX_COLS)],
        out_shape=[jax.ShapeDtypeStruct((n, d), BF16), jax.ShapeDtypeStruct((n, cqk), BF16),
                   jax.ShapeDtypeStruct((n, cbqk), BF16),
                   jax.ShapeDtypeStruct((n // tt, cvt, tt), BF16),
                   jax.ShapeDtypeStruct((n, IDX_COLS), F32)],
        compiler_params=_params("parallel"),
    )(x2, g_attn, w_qk, w_bqk, w_vt, w_idx, gain_qk, gain_idx, scale_idx, cos_t, sin_t, gmat)


def _load_qt(q_ref, qz_ref):
    qt = q_ref[0].astype(F32).T
    rows = lax.broadcasted_iota(jnp.int32, (LANES, qt.shape[1]), 0)
    for s in range(qt.shape[0] // LANES):
        slab = qt[s * LANES:(s + 1) * LANES, :]
        qz_ref[2 * s] = jnp.where(rows < HEAD_DIM, slab, 0.0).astype(BF16)
        qz_ref[2 * s + 1] = jnp.where(rows >= HEAD_DIM, slab, 0.0).astype(BF16)


ONES_ROWS = 16


def _softmax_step(n, s, vt, m_ref, l_ref, acc_ref):
    m_old = m_ref[n]
    m_new = jnp.maximum(m_old, jnp.max(s, axis=0, keepdims=True))
    p = jnp.exp2(s - m_new).astype(BF16)
    alpha = jnp.exp2(m_old - m_new)
    f = vt.shape[0]
    pv = _dot(jnp.concatenate([vt, jnp.ones((ONES_ROWS, vt.shape[1]), BF16)], axis=0), p)
    l_ref[n] = alpha * l_ref[n] + pv[f:f + 1]
    acc_ref[n] = alpha * acc_ref[n] + pv[:f]
    m_ref[n] = m_new


def _tile_ids(t):
    key = lax.broadcasted_iota(jnp.int32, (KEY_TILE, t), 0)
    qry = lax.broadcasted_iota(jnp.int32, (KEY_TILE, t), 1)
    return key, qry


def _key_rows(u):
    return pl.ds(pl.multiple_of(u * KEY_TILE, KEY_TILE), KEY_TILE)


def _in_pairs(n, one, carry):
    carry = lax.fori_loop(0, n // 2, lambda p, c: one(2 * p + 1, one(2 * p, c)), carry)
    return lax.cond(n % 2 == 1, lambda c: one(n - 1, c), lambda c: c, carry)


def _in_fours(n, one, carry):
    def four(p, c):
        for j in range(4):
            c = one(4 * p + j, c)
        return c
    carry = lax.fori_loop(0, n // 4, four, carry)
    return lax.fori_loop((n // 4) * 4, n, one, carry)


def _interleave(*gens):
    gens = list(gens)
    while gens:
        for gen in list(gens):
            try:
                next(gen)
            except StopIteration:
                gens.remove(gen)


def _attn_ab_kernel(lam_ref, aq_ref, ak_ref, avt_ref, g_ref, bq_ref, bk_ref, bvt_ref, lmat_ref,
                    ya_ref, yb_ref, aqz_ref, m_ref, l_ref, aacc_ref, bqz_ref, r_ref, bacc_ref, *,
                    lam_init):
    t = aq_ref.shape[1]
    i = pl.program_id(1)
    lp = lam_ref[0]
    lam = (jnp.exp(jnp.sum(lp[0:1] * lp[1:2], axis=-1, keepdims=True))
           - jnp.exp(jnp.sum(lp[2:3] * lp[3:4], axis=-1, keepdims=True)) + lam_init)
    key, qry = _tile_ids(t)
    sub = t // KEY_TILE
    lmat = lmat_ref[...]
    _load_qt(aq_ref, aqz_ref)
    _load_qt(bq_ref, bqz_ref)
    m_ref[...] = jnp.full(m_ref.shape, NEG, F32)
    l_ref[...] = jnp.zeros(l_ref.shape, F32)
    aacc_ref[...] = jnp.zeros(aacc_ref.shape, F32)
    r_ref[...] = jnp.zeros(r_ref.shape, F32)
    bacc_ref[...] = jnp.zeros(bacc_ref.shape, F32)

    def a_group(g, diag):
        scores = []
        for r in range(sub):
            for n in range(2 * A_HEADS):
                kb = ak_ref[0, _key_rows(g * sub + r), (n // 2) * LANES:(n // 2 + 1) * LANES]
                scores.append(_dot(kb, aqz_ref[n]))
        yield
        for r in range(sub):
            for n in range(2 * A_HEADS):
                s = scores[r * 2 * A_HEADS + n]
                if diag:
                    s = jnp.where(((key + r * KEY_TILE) // CHUNK) <= (qry // CHUNK), s, NEG)
                vt = avt_ref[0, g * sub + r, (n // 2) * LANES:(n // 2 + 1) * LANES, :]
                _softmax_step(n, s, vt, m_ref, l_ref, aacc_ref)
                yield

    def b_group(g, diag):
        order = [(r, n) for r in reversed(range(sub)) for n in range(B_HEADS)]
        strict = {r: (key + r * KEY_TILE) < qry for r in range(sub)} if diag else None
        logits = {}
        for r, n in order:
            kb = bk_ref[0, _key_rows(g * sub + r), (n // 2) * LANES:(n // 2 + 1) * LANES]
            logits[r, n] = _dot(kb, bqz_ref[n])
        yield
        log_betas, laters, sums = {}, {}, {}
        for r, n in order:
            z = logits[r, n]
            neg_abs = pltpu.bitcast(pltpu.bitcast(z, jnp.int32) | jnp.int32(INT_MIN), F32)
            soft = jnp.log2(1.0 + jnp.exp2(neg_abs))
            log_beta = jnp.minimum(z, 0.0) - soft
            log_om = log_beta - z
            if diag:
                log_om = jnp.where(strict[r], log_om, 0.0)
            later = _dot(lmat, log_om.astype(BF16))
            laters[r, n] = later[:KEY_TILE]
            log_betas[r, n] = log_beta
            sums[r, n] = later[KEY_TILE:KEY_TILE + 1]
            yield
        for r, n in order:
            a = jnp.exp2(log_betas[r, n] + laters[r, n])
            if diag:
                a = jnp.where(strict[r], a, 0.0)
            vt = bvt_ref[0, g * sub + r, n * HEAD_DIM:(n + 1) * HEAD_DIM, :]
            visited = r_ref[n]
            bacc_ref[n] = bacc_ref[n] + _dot(vt, a.astype(BF16)) * jnp.exp2(visited)
            r_ref[n] = visited + sums[r, n]
            yield

    def body(n, carry):
        _interleave(a_group(n, False), b_group(i - 1 - n, False))
        return carry

    _interleave(a_group(i, True), b_group(i, True))
    _in_pairs(i, body, 0)

    gain = g_ref[0] * (1.0 - lam_init)
    for h in range(A_HEADS):
        o = aacc_ref[2 * h] / l_ref[2 * h] - lam * (aacc_ref[2 * h + 1] / l_ref[2 * h + 1])
        ms = jnp.mean(o * o, axis=0, keepdims=True)
        y = (o * lax.rsqrt(ms + EPS)).T * gain
        ya_ref[0, :, h * LANES:(h + 1) * LANES] = y.astype(ya_ref.dtype)
    yb_ref[0] = bacc_ref[...].reshape(B_HEADS * HEAD_DIM, t).T.astype(yb_ref.dtype)


def _attn_ab_call(qk, bqk, vt, lam_p, subln, lmat, layer, lam_init, t):
    b, s, _ = qk.shape
    nt, _, kt = vt.shape[1:]
    ns = 2 * A_HEADS
    qspec = pl.BlockSpec((1, t, GROUP), lambda bi, i: (bi, i, 0))
    kspec = pl.BlockSpec((1, s, GROUP), lambda bi, i: (bi, 0, 1))
    vspec = lambda f: pl.BlockSpec((1, nt, GROUP, kt), lambda bi, i: (bi, 0, f, 0))
    return pl.pallas_call(
        functools.partial(_attn_ab_kernel, lam_init=lam_init),
        grid=(b, s // t),
        in_specs=[pl.BlockSpec((1, 4, HEAD_DIM), lambda bi, i: (layer, 0, 0)),
                  qspec, kspec, vspec(0),
                  pl.BlockSpec((1, 1, LANES), lambda bi, i: (layer, 0, 0)),
                  qspec, kspec, vspec(1),
                  pl.BlockSpec(lmat.shape, lambda bi, i: (0, 0))],
        out_specs=[qspec, qspec],
        out_shape=[jax.ShapeDtypeStruct((b, s, GROUP), BF16)] * 2,
        scratch_shapes=[pltpu.VMEM((ns, LANES, t), BF16), pltpu.VMEM((ns, 1, t), F32),
                        pltpu.VMEM((ns, 1, t), F32), pltpu.VMEM((ns, LANES, t), F32),
                        pltpu.VMEM((B_HEADS, LANES, t), BF16), pltpu.VMEM((B_HEADS, 1, t), F32),
                        pltpu.VMEM((B_HEADS, HEAD_DIM, t), F32)],
        compiler_params=_params("parallel", "arbitrary"),
    )(lam_p, qk, qk, vt, subln, bqk, bqk, vt, lmat)


def _fold_rows(x):
    parts = [x[r * SUBLANES:(r + 1) * SUBLANES] for r in range(x.shape[0] // SUBLANES)]
    while len(parts) > 1:
        parts = [a + b for a, b in zip(parts[0::2], parts[1::2])]
    return parts[0]


def _dsa_mask_kernel(q_ref, kw_ref, wq_ref, lt_ref, o_ref, key_ref, qz_ref, *, topk):
    t = q_ref.shape[1]
    nj = key_ref.shape[0]
    i = pl.program_id(1)
    nch = (i + 1) * (t // KEY_TILE)
    qt = q_ref[0].T
    pad = jnp.zeros((LANES - IDX_DIM, t), F32)
    for h in range(IDX_HEADS):
        qz_ref[h] = jnp.concatenate([qt[h * IDX_DIM:(h + 1) * IDX_DIM], pad], axis=0).astype(BF16)
    wt = wq_ref[0].T
    wh = [wt[IDX_DIM + h:IDX_DIM + h + 1, :] for h in range(IDX_HEADS)]
    key0, qry0 = _tile_ids(t)
    sub = t // KEY_TILE

    def chunk_ok(r):
        return ((key0 + r * KEY_TILE) // CHUNK) <= (qry0 // CHUNK)

    def fill(g, diag):
        for r in range(sub):
            c = g * sub + r
            kb = kw_ref[0, _key_rows(c), :].astype(BF16)
            score = jnp.zeros((KEY_TILE, t), F32)
            for h in range(IDX_HEADS):
                score = score + wh[h] * jnp.maximum(_dot(kb, qz_ref[h]), 0.0)
            if diag:
                score = jnp.where(chunk_ok(r), score, -jnp.inf)
            bits = pltpu.bitcast(score, jnp.int32)
            okey = jnp.where(bits < 0, bits ^ jnp.int32(0x7FFFFFFF), bits)
            key_ref[c] = jnp.where(score == 0.0, 0, okey)

    def fill_one(g, carry):
        fill(g, False)
        return carry

    _in_fours(i, fill_one, 0)
    fill(i, True)

    def count_at_or_above(cand):
        def one(g, acc):
            for r in range(sub):
                acc = acc + _fold_rows(jnp.where(key_ref[g * sub + r] >= cand, 1.0, 0.0))
            return acc
        acc = _in_fours(i + 1, one, jnp.zeros((SUBLANES, t), F32))
        return jnp.sum(acc, axis=0, keepdims=True)

    def bit_body(n, carry):
        lo, above = carry
        cand = lo + jnp.left_shift(jnp.int32(1), 31 - n)
        cnt = count_at_or_above(cand)
        take = cnt >= topk
        return jnp.where(take, cand, lo), jnp.where(take, above, cnt)

    thr, above = lax.fori_loop(
        0, 32, bit_body, (jnp.full((1, t), INT_MIN, jnp.int32), jnp.zeros((1, t), F32)))
    need = topk - above
    lt = lt_ref[...]

    def emit(g, run, diag):
        for r in range(sub):
            c = g * sub + r
            okey = key_ref[c]
            eq = okey == thr
            eqf = jnp.where(eq, 1.0, 0.0)
            before = _dot(lt, eqf.astype(BF16)) + run
            sel = (okey > thr) | (eq & (before < need))
            if diag:
                sel = sel & chunk_ok(r)
            o_ref[0, 0, c] = jnp.where(sel, 0.0, NEG).astype(o_ref.dtype)
            run = run + jnp.sum(eqf, axis=0, keepdims=True)
        return run

    run = _in_fours(i, lambda g, run: emit(g, run, False), jnp.zeros((1, t), F32))
    emit(i, run, True)

    def blank(c, carry):
        o_ref[0, 0, c] = jnp.full((KEY_TILE, t), NEG, o_ref.dtype)
        return carry

    lax.fori_loop(nch, nj, blank, 0)


def _attn_c_kernel(q_ref, k_ref, vt_ref, b_ref, o_ref, qz_ref, m_ref, l_ref, acc_ref):
    t = q_ref.shape[1]
    i = pl.program_id(1)
    _load_qt(q_ref, qz_ref)
    m_ref[...] = jnp.full(m_ref.shape, NEG, F32)
    l_ref[...] = jnp.zeros(l_ref.shape, F32)
    acc_ref[...] = jnp.zeros(acc_ref.shape, F32)

    key, qry = _tile_ids(KEY_TILE)
    ident = jnp.where(key == qry, 1.0, 0.0).astype(BF16)

    sub = t // KEY_TILE

    def body(g, carry):
        scores = []
        for r in range(sub):
            u = g * sub + r
            bias = b_ref[0, 0, u]
            for n in range(C_HEADS):
                kb = k_ref[0, _key_rows(u), (n // 2) * LANES:(n // 2 + 1) * LANES]
                lhs = jnp.concatenate([kb, ident], axis=1)
                rhs = jnp.concatenate([qz_ref[n], bias], axis=0)
                scores.append(_dot(lhs, rhs))
        for r in range(sub):
            for n in range(C_HEADS):
                vt = vt_ref[0, g * sub + r, n * HEAD_DIM:(n + 1) * HEAD_DIM, :]
                _softmax_step(n, scores[r * C_HEADS + n], vt, m_ref, l_ref, acc_ref)
        return carry

    _in_fours(i + 1, body, 0)
    o = acc_ref[...] / l_ref[...]
    o_ref[0] = o.reshape(C_HEADS * HEAD_DIM, t).T.astype(o_ref.dtype)


def _dsa_kernel(iq_ref, kw_ref, wq_ref, lt_ref, q_ref, k_ref, vt_ref, o_ref,
                bias_ref, key_ref, iqz_ref, qz_ref, m_ref, l_ref, acc_ref, *, topk):
    _dsa_mask_kernel(iq_ref, kw_ref, wq_ref, lt_ref, bias_ref, key_ref, iqz_ref, topk=topk)
    _attn_c_kernel(q_ref, k_ref, vt_ref, bias_ref, o_ref, qz_ref, m_ref, l_ref, acc_ref)


def _dsa_call(idx, lt, qk, vt, topk, t):
    b, s, _ = idx.shape
    nk, _, kt = vt.shape[1:]
    assert t >= topk and kt == lt.shape[0]
    nt = s // t
    return pl.pallas_call(
        functools.partial(_dsa_kernel, topk=topk),
        grid=(b, nt),
        in_specs=[pl.BlockSpec((1, t, IDX_Q), lambda bi, i: (bi, i, 0)),
                  pl.BlockSpec((1, s, LANES), lambda bi, i: (bi, 0, IDX_Q // LANES)),
                  pl.BlockSpec((1, t, LANES), lambda bi, i: (bi, i, IDX_Q // LANES)),
                  pl.BlockSpec((kt, kt), lambda bi, i: (0, 0)),
                  pl.BlockSpec((1, t, GROUP), lambda bi, i: (bi, i, 2)),
                  pl.BlockSpec((1, s, GROUP), lambda bi, i: (bi, 0, 3)),
                  pl.BlockSpec((1, nk, GROUP, kt), lambda bi, i: (bi, 0, 2, 0))],
        out_specs=pl.BlockSpec((1, t, GROUP), lambda bi, i: (bi, i, 0)),
        out_shape=jax.ShapeDtypeStruct((b, s, GROUP), BF16),
        scratch_shapes=[pltpu.VMEM((1, 1, nk, kt, t), BF16), pltpu.VMEM((nk, kt, t), jnp.int32),
                        pltpu.VMEM((IDX_HEADS, LANES, t), BF16),
                        pltpu.VMEM((C_HEADS, LANES, t), BF16), pltpu.VMEM((C_HEADS, 1, t), F32),
                        pltpu.VMEM((C_HEADS, 1, t), F32), pltpu.VMEM((C_HEADS, HEAD_DIM, t), F32)],
        compiler_params=_params("parallel", "arbitrary"),
    )(idx, idx, idx, lt, qk, qk, vt)


def _tail_kernel(x_ref, h_ref, ya_ref, yb_ref, yc_ref, p_ref, wg_ref, wa_ref, wb_ref, wc_ref, wo_ref,
                 gm_ref, wu_ref, wd_ref, gp_ref, wpg_ref, wpp_ref, o_ref):
    d = x_ref.shape[1]
    h = h_ref[...]
    merged = jnp.zeros(x_ref.shape, F32)
    for n, (y_ref, w_ref) in enumerate(((ya_ref, wa_ref), (yb_ref, wb_ref), (yc_ref, wc_ref))):
        gate = jax.nn.sigmoid(_dot(h, wg_ref[0, :, n * d:(n + 1) * d]))
        merged = merged + gate * _dot(y_ref[...], w_ref[0])
    x = x_ref[...] + _dot(merged.astype(BF16), wo_ref[0])
    ms = jnp.mean(x * x, axis=-1, keepdims=True)
    hm = (x * lax.rsqrt(ms + EPS) * gm_ref[0]).astype(BF16)
    for f in range(wu_ref.shape[2] // FF_TILE):
        u = jnp.maximum(_dot(hm, wu_ref[0, :, f * FF_TILE:(f + 1) * FF_TILE]), 0.0)
        x = x + _dot((u * u).astype(BF16), wd_ref[0, f * FF_TILE:(f + 1) * FF_TILE, :])
    ms = jnp.mean(x * x, axis=-1, keepdims=True)
    hp = (x * lax.rsqrt(ms + EPS) * gp_ref[0]).astype(BF16)
    gate = jax.nn.sigmoid(_dot(hp, wpg_ref[0]))
    o_ref[...] = x + gate * _dot(p_ref[0].astype(BF16), wpp_ref[0])


def _tail_call(x2, h, ya, yb, yc, p3, wg, wa, wb, wc, wo, g_mlp, w_up, w_down, g_ple, w_pg, w_pp,
               layer):
    n, d = x2.shape
    pd = p3.shape[2]
    tm = TAIL_TILE
    row = lambda cols: pl.BlockSpec((tm, cols), lambda i: (i, 0))
    full = lambda w: pl.BlockSpec((1,) + w.shape[1:], lambda i: (layer, 0, 0),
                                  pipeline_mode=pl.Buffered(1))
    return pl.pallas_call(
        _tail_kernel,
        grid=(n // tm,),
        in_specs=[row(d), row(d), row(ya.shape[1]), row(yb.shape[1]), row(yc.shape[1]),
                  pl.BlockSpec((1, tm, pd), lambda i: (layer, i, 0)),
                  full(wg), full(wa), full(wb), full(wc), full(wo),
                  full(g_mlp), full(w_up), full(w_down), full(g_ple), full(w_pg), full(w_pp)],
        out_specs=row(d),
        out_shape=jax.ShapeDtypeStruct((n, d), F32),
        compiler_params=_params("parallel"),
    )(x2, h, ya, yb, yc, p3, wg, wa, wb, wc, wo, g_mlp, w_up, w_down, g_ple, w_pg, w_pp)


def _tile_gain(g, reps, scale=1.0):
    return jnp.tile(g.astype(F32), (1, reps)) * scale


def kernel(x, p, positions, attn_norm, w_in, a_q_norm, a_k_norm, a_lambda, a_subln,
           c_q_norm, c_k_norm, idx_k_norm, w_br_a, w_br_b, w_br_c, w_out,
           mlp_norm, w_up, w_down, ple_norm, w_ple_gate, w_ple_proj):
    b, s, d = x.shape
    depth = w_in.shape[0]
    n = b * s
    topk = min(TOPK_MAX, s // 4)
    t = min(ATTN_TILE, s)
    scale = HEAD_DIM ** -0.5
    qscale = scale * LOG2E

    inv = ROPE_THETA ** (-jnp.arange(0, HEAD_DIM, 2, dtype=F32) / HEAD_DIM)
    ang = positions.astype(F32)[..., None] * inv
    cos, sin = jnp.cos(ang), jnp.sin(ang)
    cos_t = jnp.concatenate([cos, cos, cos, cos], axis=-1).reshape(n, LANES)
    sin_t = jnp.concatenate([-sin, sin, -sin, sin], axis=-1).reshape(n, LANES)

    qa, ka, va = 0, 512, 1024
    qb, kb, vb = 1536, 2048, 2560
    qc, kc, vc = 3072, 3584, 4096
    qi = 4608
    gl = 4932
    col = lambda a, width=GROUP: w_in[:, :, a:a + width]
    w_qk = jnp.concatenate([col(qa), col(ka), col(qc), col(kc)], axis=-1).astype(BF16)
    w_bqk = jnp.concatenate([col(qb) * qscale, col(kb)], axis=-1).astype(BF16)
    w_vt = jnp.concatenate([col(va), col(vb), col(vc)], axis=-1).astype(BF16)
    w_idx = jnp.pad(col(qi, gl - qi), ((0, 0), (0, 0), (0, IDX_COLS - (gl - qi)))).astype(BF16)
    w_gl = col(gl, N_BRANCHES * d).astype(BF16)
    heads = GROUP // HEAD_DIM
    gain_qk = jnp.concatenate([_tile_gain(a_q_norm, heads, qscale), _tile_gain(a_k_norm, heads),
                               _tile_gain(c_q_norm, heads, qscale), _tile_gain(c_k_norm, heads)],
                              axis=-1)[:, None, :]
    gain_idx = jnp.pad(idx_k_norm.astype(F32), ((0, 0), (0, LANES - IDX_DIM)))[:, None, :]
    scale_idx = jnp.concatenate([jnp.full((IDX_Q,), IDX_DIM ** -0.5, F32), jnp.ones((IDX_DIM,), F32),
                                 jnp.full((IDX_HEADS,), IDX_HEADS ** -0.5, F32),
                                 jnp.zeros((LANES - IDX_DIM - IDX_HEADS,), F32)])[None, :]
    gid = jnp.arange(GROUP) // HEAD_DIM
    gmat = (gid[:, None] == gid[None, :]).astype(BF16)
    kt = min(KEY_TILE, s)
    ar = jnp.arange(kt)
    later_mat = jnp.concatenate([(ar[None, :] > ar[:, None]).astype(BF16),
                                 jnp.ones((ONES_ROWS, kt), BF16)], axis=0)
    before_mat = (ar[None, :] < ar[:, None]).astype(BF16)

    wa, wb, wc, wo = (w.astype(BF16) for w in (w_br_a, w_br_b, w_br_c, w_out))
    wu, wd, wpg, wpp = (w.astype(BF16) for w in (w_up, w_down, w_ple_gate, w_ple_proj))
    g_attn, g_mlp, g_ple = (g.astype(F32)[:, None, :] for g in (attn_norm, mlp_norm, ple_norm))
    subln = a_subln.astype(F32)[:, None, :]
    lam_p = a_lambda.astype(F32)
    p3 = p.reshape(depth, n, p.shape[-1])

    x2 = x.reshape(n, d)
    for i in range(depth):
        lam_init = 0.8 - 0.6 * math.exp(-0.3 * i)
        h, qk, bqk, vt, idx = _front_call(x2, g_attn, w_qk, w_bqk, w_vt, w_idx, gain_qk, gain_idx,
                                          scale_idx, cos_t, sin_t, gmat, i, kt)
        qk, bqk, idx = (a.reshape(b, s, -1) for a in (qk, bqk, idx))
        vt = vt.reshape(b, s // kt, -1, kt)
        ya, yb = _attn_ab_call(qk, bqk, vt, lam_p, subln, later_mat, i, lam_init, t)
        yc = _dsa_call(idx, before_mat, qk, vt, topk, t)
        x2 = _tail_call(x2, h, ya.reshape(n, -1), yb.reshape(n, -1), yc.reshape(n, -1), p3,
                        w_gl, wa, wb, wc, wo, g_mlp, wu, wd, g_ple, wpg, wpp, i)
    return x2.reshape(b, s, d)
```
